```python
import math
import jax, jax.numpy as jnp
from jax import lax
import numpy as np

D_MODEL = 2048
BATCH = 8
SEQ = 2048
DEPTH = 1

D_MIX = D_MODEL
D_ATTN = D_MIX // 2
D_SSM = D_MIX - D_ATTN
HEAD_DIM = 128
N_HEADS = D_ATTN // HEAD_DIM
N_KV = 2
HEADS_PER_KV = N_HEADS // N_KV
D_KV = N_KV * HEAD_DIM
CMP_LEN = 32
CMP_STRIDE = 16
CMP_HIDDEN = HEAD_DIM
SEL_BLOCK = 64
N_SELECT = 16
WINDOW = 512
WIN_Q_BLOCK = 128
SEL_Q_BLOCK = 32
N_BUCKETS = 32
MAX_DISTANCE = 128
SSM_GROUP = 16
SSM_GROUPS = D_SSM // SSM_GROUP
SSM_STATE = 64
D_FF = 5632
EPS = 1e-6
N_IN = D_ATTN + 6 * D_KV + 3 * N_HEADS + D_SSM
NEG = -1e30

kernel_name = "hymba_nsa_s5_macaron_block"


def rmsnorm(x, g):
    xf = x.astype(jnp.float32)
    y = xf * lax.rsqrt(jnp.mean(xf * xf, axis=-1, keepdims=True) + EPS)
    return (y * g.astype(jnp.float32)).astype(x.dtype)


def swiglu(x, w1, w3, w2):
    return (jax.nn.silu(x @ w1) * (x @ w3)) @ w2


def t5_bucket(dist):
    n = jnp.maximum(dist, 0)
    max_exact = N_BUCKETS // 2
    nf = jnp.maximum(n, 1).astype(jnp.float32)
    large = max_exact + (jnp.log(nf / max_exact) / math.log(MAX_DISTANCE / max_exact)
                         * (N_BUCKETS - max_exact)).astype(jnp.int32)
    large = jnp.minimum(large, N_BUCKETS - 1)
    return jnp.where(n < max_exact, n, large)


def masked_softmax(s, mask):
    s = jnp.where(mask, s, NEG)
    m = jnp.max(s, axis=-1, keepdims=True)
    p = jnp.exp(s - m) * mask
    return p / jnp.maximum(jnp.sum(p, axis=-1, keepdims=True), 1e-30)


def compress_blocks(k, pe, w1, b1, w2):
    B, T, G, dh = k.shape
    nc = (T - CMP_LEN) // CMP_STRIDE + 1
    idx = jnp.arange(nc)[:, None] * CMP_STRIDE + jnp.arange(CMP_LEN)[None, :]
    blk = k[:, idx] + pe[None, None, :, None, :]
    blk = blk.transpose(0, 1, 3, 2, 4).reshape(B, nc, G, CMP_LEN * dh)
    return jax.nn.gelu(blk @ w1 + b1) @ w2


def nsa_mixer(q, kc, vc, ks, vs, kw, vw, gate_logits, rel_bias,
              pe_k, w1_k, b1_k, w2_k, pe_v, w1_v, b1_v, w2_v):
    B, T, _ = q.shape
    G, Hg, dh = N_KV, HEADS_PER_KV, HEAD_DIM
    q = q.reshape(B, T, G, Hg, dh) * (dh ** -0.5)
    kc, vc, ks, vs, kw, vw = [a.reshape(B, T, G, dh) for a in (kc, vc, ks, vs, kw, vw)]
    t = jnp.arange(T)
    table_g = rel_bias.reshape(N_BUCKETS, G, Hg)

    kcb = compress_blocks(kc, pe_k, w1_k, b1_k, w2_k)
    vcb = compress_blocks(vc, pe_v, w1_v, b1_v, w2_v)
    nc = kcb.shape[1]
    c_start = jnp.arange(nc) * CMP_STRIDE
    c_end = c_start + CMP_LEN - 1
    dist_c = t[:, None] - c_end[None, :]
    s_c = jnp.einsum('btghd,bcgd->bghtc', q, kcb).astype(jnp.float32)
    s_c = s_c + table_g[t5_bucket(dist_c)].transpose(2, 3, 0, 1).astype(jnp.float32)
    p_cmp = masked_softmax(s_c, dist_c >= 0)
    o_cmp = jnp.einsum('bghtc,bcgd->btghd', p_cmp.astype(vcb.dtype), vcb)

    ns = T // SEL_BLOCK
    n_sel = min(N_SELECT, ns)
    j_start = jnp.arange(ns) * SEL_BLOCK
    overlap = jnp.clip(jnp.minimum(c_start[:, None] + CMP_LEN, j_start[None, :] + SEL_BLOCK)
                       - jnp.maximum(c_start[:, None], j_start[None, :]), 0, None)
    overlap = overlap.astype(jnp.float32) / CMP_LEN
    imp = jnp.einsum('bghtc,cj->bgtj', p_cmp, overlap)
    cur = t // SEL_BLOCK
    jj = jnp.arange(ns)
    forced = (jj[None, :] == 0) | (jj[None, :] == cur[:, None]) | (jj[None, :] == cur[:, None] - 1)
    causal_blk = j_start[None, :] <= t[:, None]
    imp = jnp.where(forced, 1e6, jnp.where(causal_blk, imp, -1e9))
    _, sel_idx = lax.top_k(imp, n_sel)

    ks_b = ks.reshape(B, ns, SEL_BLOCK, G, dh).transpose(0, 3, 1, 2, 4)
    vs_b = vs.reshape(B, ns, SEL_BLOCK, G, dh).transpose(0, 3, 1, 2, 4)
    nq = T // SEL_Q_BLOCK
    q_ch = q.reshape(B, nq, SEL_Q_BLOCK, G, Hg, dh).transpose(1, 0, 2, 3, 4, 5)
    idx_ch = sel_idx.reshape(B, G, nq, SEL_Q_BLOCK, n_sel).transpose(2, 0, 1, 3, 4)
    t_ch = t.reshape(nq, SEL_Q_BLOCK)
    bi = jnp.arange(B)[:, None, None, None]
    gi = jnp.arange(G)[None, :, None, None]
    g_b = jnp.arange(G)[None, :, None, None, None]
    s_off = jnp.arange(SEL_BLOCK)

    def sel_block(args):
        qc, ic, tc = args
        kg = ks_b[bi, gi, ic]
        vg = vs_b[bi, gi, ic]
        s = jnp.einsum('bqghd,bgqnsd->bghqns', qc, kg).astype(jnp.float32)
        kpos = ic[..., None] * SEL_BLOCK + s_off
        dist = tc[None, None, :, None, None] - kpos
        bias = table_g[t5_bucket(dist), g_b].transpose(0, 1, 5, 2, 3, 4)
        s = (s + bias.astype(jnp.float32)).reshape(B, G, Hg, SEL_Q_BLOCK, n_sel * SEL_BLOCK)
        mask = (dist >= 0).reshape(B, G, 1, SEL_Q_BLOCK, n_sel * SEL_BLOCK)
        p = masked_softmax(s, mask).reshape(B, G, Hg, SEL_Q_BLOCK, n_sel, SEL_BLOCK)
        return jnp.einsum('bghqns,bgqnsd->bqghd', p.astype(vg.dtype), vg)

    o_sel = lax.map(sel_block, (q_ch, idx_ch, t_ch))
    o_sel = o_sel.transpose(1, 0, 2, 3, 4, 5).reshape(B, T, G, Hg, dh)

    nb = T // WIN_Q_BLOCK
    span = WIN_Q_BLOCK + WINDOW
    kw_pad = jnp.pad(kw, ((0, 0), (WINDOW, 0), (0, 0), (0, 0)))
    vw_pad = jnp.pad(vw, ((0, 0), (WINDOW, 0), (0, 0), (0, 0)))
    kidx = jnp.arange(nb)[:, None] * WIN_Q_BLOCK + jnp.arange(span)[None, :]
    kwb = kw_pad[:, kidx]
    vwb = vw_pad[:, kidx]
    qb = q.reshape(B, nb, WIN_Q_BLOCK, G, Hg, dh)
    s_w = jnp.einsum('bnqghd,bnkgd->bnghqk', qb, kwb).astype(jnp.float32)
    qpos = jnp.arange(nb)[:, None] * WIN_Q_BLOCK + jnp.arange(WIN_Q_BLOCK)[None, :]
    kpos = kidx - WINDOW
    dist_w = qpos[:, :, None] - kpos[:, None, :]
    mask_w = (dist_w >= 0) & (dist_w < WINDOW) & (kpos[:, None, :] >= 0)
    bias_w = table_g[t5_bucket(dist_w)].transpose(0, 3, 4, 1, 2)
    s_w = s_w + bias_w[None].astype(jnp.float32)
    p_w = masked_softmax(s_w, mask_w[:, None, None])
    o_win = jnp.einsum('bnghqk,bnkgd->bnqghd', p_w.astype(vwb.dtype), vwb).reshape(B, T, G, Hg, dh)

    g = jax.nn.sigmoid(gate_logits.astype(jnp.float32)).reshape(B, T, G, Hg, 3).astype(q.dtype)
    o = g[..., 0:1] * o_cmp + g[..., 1:2] * o_sel + g[..., 2:3] * o_win
    return o.reshape(B, T, D_ATTN)


def s5_mixer(u, lam_re, lam_im, log_step, b_re, b_im, c_re, c_im, d_skip, w_glu, b_glu):
    B, T, _ = u.shape
    f32 = jnp.float32
    uf = u.astype(f32).reshape(B, T, SSM_GROUPS, SSM_GROUP)
    step = jnp.exp(log_step.astype(f32))[:, None]
    lre, lim = lam_re.astype(f32), lam_im.astype(f32)
    mag = jnp.exp(lre * step)
    ab_re, ab_im = mag * jnp.cos(lim * step), mag * jnp.sin(lim * step)
    nr, ni = ab_re - 1.0, ab_im
    den = lre * lre + lim * lim
    f_re, f_im = (nr * lre + ni * lim) / den, (ni * lre - nr * lim) / den
    br, bim = b_re.astype(f32), b_im.astype(f32)
    bb_re = f_re[..., None] * br - f_im[..., None] * bim
    bb_im = f_re[..., None] * bim + f_im[..., None] * br
    bu_re = jnp.einsum('gph,btgh->btgp', bb_re, uf)
    bu_im = jnp.einsum('gph,btgh->btgp', bb_im, uf)
    a_re = jnp.broadcast_to(ab_re, (1, T, SSM_GROUPS, SSM_STATE))
    a_im = jnp.broadcast_to(ab_im, (1, T, SSM_GROUPS, SSM_STATE))

    def combine(e1, e2):
        a1r, a1i, b1r, b1i = e1
        a2r, a2i, b2r, b2i = e2
        return (a2r * a1r - a2i * a1i, a2r * a1i + a2i * a1r,
                a2r * b1r - a2i * b1i + b2r, a2r * b1i + a2i * b1r + b2i)

    _, _, xr, xi = lax.associative_scan(combine, (a_re, a_im, bu_re, bu_im), axis=1)
    y = (jnp.einsum('ghp,btgp->btgh', c_re.astype(f32), xr)
         - jnp.einsum('ghp,btgp->btgh', c_im.astype(f32), xi)
         + d_skip.astype(f32).reshape(SSM_GROUPS, SSM_GROUP) * uf)
    y = y.reshape(B, T, D_SSM).astype(u.dtype)
    h = jax.nn.gelu(y)
    return h * jax.nn.sigmoid(h @ w_glu + b_glu)


def setup_inputs(seed: int = 0) -> dict:
    key = jax.random.key(seed)
    ks = iter(jax.random.split(key, 40))
    f32 = jnp.float32

    def nrm(shape, scale):
        return jax.random.normal(next(ks), shape, f32) * scale

    def gain(shape):
        return 1.0 + nrm(shape, 0.02)

    L = DEPTH
    P = SSM_STATE
    lam_im0 = math.pi * jnp.arange(P, dtype=f32)
    return {
        "x": nrm((BATCH, SEQ, D_MODEL), 1.0),
        "ffn1_norm": gain((L, D_MODEL)),
        "ffn1_w1": nrm((L, D_MODEL, D_FF), D_MODEL ** -0.5),
        "ffn1_w3": nrm((L, D_MODEL, D_FF), D_MODEL ** -0.5),
        "ffn1_w2": nrm((L, D_FF, D_MODEL), D_FF ** -0.5),
        "mix_norm": gain((L, D_MODEL)),
        "w_in": nrm((L, D_MODEL, N_IN), D_MODEL ** -0.5),
        "cmp_pe_k": nrm((L, CMP_LEN, HEAD_DIM), 0.1),
        "cmp_w1_k": nrm((L, CMP_LEN * HEAD_DIM, CMP_HIDDEN), (CMP_LEN * HEAD_DIM) ** -0.5),
        "cmp_b1_k": nrm((L, CMP_HIDDEN), 0.01),
        "cmp_w2_k": nrm((L, CMP_HIDDEN, HEAD_DIM), CMP_HIDDEN ** -0.5),
        "cmp_pe_v": nrm((L, CMP_LEN, HEAD_DIM), 0.1),
        "cmp_w1_v": nrm((L, CMP_LEN * HEAD_DIM, CMP_HIDDEN), (CMP_LEN * HEAD_DIM) ** -0.5),
        "cmp_b1_v": nrm((L, CMP_HIDDEN), 0.01),
        "cmp_w2_v": nrm((L, CMP_HIDDEN, HEAD_DIM), CMP_HIDDEN ** -0.5),
        "rel_bias": nrm((N_BUCKETS, N_HEADS), 0.5),
        "ssm_lam_re": -0.5 + nrm((L, SSM_GROUPS, P), 0.01),
        "ssm_lam_im": lam_im0[None, None, :] + nrm((L, SSM_GROUPS, P), 0.01),
        "ssm_log_step": jax.random.uniform(next(ks), (L, SSM_GROUPS), f32,
                                           math.log(1e-3), math.log(1e-1)),
        "ssm_b_re": nrm((L, SSM_GROUPS, P, SSM_GROUP), (2.0 * SSM_GROUP) ** -0.5),
        "ssm_b_im": nrm((L, SSM_GROUPS, P, SSM_GROUP), (2.0 * SSM_GROUP) ** -0.5),
        "ssm_c_re": nrm((L, SSM_GROUPS, SSM_GROUP, P), (2.0 * P) ** -0.5),
        "ssm_c_im": nrm((L, SSM_GROUPS, SSM_GROUP, P), (2.0 * P) ** -0.5),
        "ssm_d": nrm((L, D_SSM), 1.0),
        "glu_w": nrm((L, D_SSM, D_SSM), D_SSM ** -0.5),
        "glu_b": nrm((L, D_SSM), 0.01),
        "w_out": nrm((L, D_MIX, D_MODEL), D_MIX ** -0.5),
        "ffn2_norm": gain((L, D_MODEL)),
        "ffn2_w1": nrm((L, D_MODEL, D_FF), D_MODEL ** -0.5),
        "ffn2_w3": nrm((L, D_MODEL, D_FF), D_MODEL ** -0.5),
        "ffn2_w2": nrm((L, D_FF, D_MODEL), D_FF ** -0.5),
        "final_norm": gain((D_MODEL,)),
    }


def reference(x, ffn1_norm, ffn1_w1, ffn1_w3, ffn1_w2, mix_norm, w_in,
              cmp_pe_k, cmp_w1_k, cmp_b1_k, cmp_w2_k, cmp_pe_v, cmp_w1_v, cmp_b1_v, cmp_w2_v,
              rel_bias, ssm_lam_re, ssm_lam_im, ssm_log_step, ssm_b_re, ssm_b_im,
              ssm_c_re, ssm_c_im, ssm_d, glu_w, glu_b, w_out,
              ffn2_norm, ffn2_w1, ffn2_w3, ffn2_w2, final_norm):
    splits = np.cumsum([D_ATTN, D_KV, D_KV, D_KV, D_KV, D_KV, D_KV, 3 * N_HEADS])
    h = x
    for l in range(DEPTH):
        h = h + 0.5 * swiglu(rmsnorm(h, ffn1_norm[l]), ffn1_w1[l], ffn1_w3[l], ffn1_w2[l])
        proj = rmsnorm(h, mix_norm[l]) @ w_in[l]
        q, kc, vc, ksl, vsl, kw, vw, gates, u_ssm = jnp.split(proj, splits, axis=-1)
        a = nsa_mixer(q, kc, vc, ksl, vsl, kw, vw, gates, rel_bias,
                      cmp_pe_k[l], cmp_w1_k[l], cmp_b1_k[l], cmp_w2_k[l],
                      cmp_pe_v[l], cmp_w1_v[l], cmp_b1_v[l], cmp_w2_v[l])
        s = s5_mixer(u_ssm, ssm_lam_re[l], ssm_lam_im[l], ssm_log_step[l], ssm_b_re[l], ssm_b_im[l],
                     ssm_c_re[l], ssm_c_im[l], ssm_d[l], glu_w[l], glu_b[l])
        h = h + jnp.concatenate([a, s], axis=-1) @ w_out[l]
        h = h + 0.5 * swiglu(rmsnorm(h, ffn2_norm[l]), ffn2_w1[l], ffn2_w3[l], ffn2_w2[l])
    return rmsnorm(h, final_norm)
```

```python
import functools
import math

import jax
import jax.numpy as jnp
from jax import lax
from jax.experimental import pallas as pl
from jax.experimental.pallas import tpu as pltpu

F32 = jnp.float32
BF16 = jnp.bfloat16

HEAD_DIM = 128
N_KV = 2
HEADS_PER_KV = 4
N_HEADS = N_KV * HEADS_PER_KV
D_ATTN = N_HEADS * HEAD_DIM
D_KV = N_KV * HEAD_DIM
CMP_LEN = 32
CMP_STRIDE = 16
SEL_BLOCK = 64
N_SELECT = 16
WINDOW = 512
N_BUCKETS = 32
MAX_DISTANCE = 128
SSM_GROUP = 16
SSM_STATE = 64
EPS = 1e-6
NEG = -1e30

LANES = 128
VMEM_LIMIT_BYTES = 56 * 1024 * 1024
TQ = 128
ROWS4 = HEADS_PER_KV * TQ
SSM_CHUNK = 16
SSM_PAIR = 2


def _cparams(sem):
    return pltpu.CompilerParams(dimension_semantics=sem, vmem_limit_bytes=VMEM_LIMIT_BYTES)


def _ffn_body(x_ref, g_ref, w1_ref, w3_ref, w2_ref, *rest, final):
    if final:
        fg_ref, o_ref, xn_ref, acc_ref = rest
    else:
        o_ref, xn_ref, acc_ref = rest
    j = pl.program_id(1)
    nj = pl.num_programs(1)

    @pl.when(j == 0)
    def _():
        x = x_ref[...]
        ms = jnp.mean(x * x, axis=-1, keepdims=True)
        xn_ref[...] = (x * lax.rsqrt(ms + EPS) * g_ref[...]).astype(BF16)

    xn = xn_ref[...]
    a = jnp.dot(xn, w1_ref[...], preferred_element_type=F32)
    b = jnp.dot(xn, w3_ref[...], preferred_element_type=F32)
    gated = (a * jax.nn.sigmoid(a)) * b
    contrib = jnp.dot(gated.astype(BF16), w2_ref[...], preferred_element_type=F32)

    @pl.when(j == 0)
    def _():
        acc_ref[...] = contrib

    @pl.when(j > 0)
    def _():
        acc_ref[...] += contrib

    @pl.when(j == nj - 1)
    def _():
        h = x_ref[...] + 0.5 * acc_ref[...]
        if final:
            ms = jnp.mean(h * h, axis=-1, keepdims=True)
            h = h * lax.rsqrt(ms + EPS) * fg_ref[...]
        o_ref[...] = h


def _ffn(x, gain, w1, w3, w2, final_gain=None, *, tm=512, tf=512):
    n, d = x.shape
    dff = w1.shape[1]
    final = final_gain is not None
    in_specs = [
        pl.BlockSpec((tm, d), lambda i, j: (i, 0)),
        pl.BlockSpec((1, d), lambda i, j: (0, 0)),
        pl.BlockSpec((d, tf), lambda i, j: (0, j)),
        pl.BlockSpec((d, tf), lambda i, j: (0, j)),
        pl.BlockSpec((tf, d), lambda i, j: (j, 0)),
    ]
    args = [x, gain.reshape(1, d), w1, w3, w2]
    if final:
        in_specs.append(pl.BlockSpec((1, d), lambda i, j: (0, 0)))
        args.append(final_gain.reshape(1, d))
    return pl.pallas_call(
        functools.partial(_ffn_body, final=final),
        grid=(n // tm, dff // tf),
        in_specs=in_specs,
        out_specs=pl.BlockSpec((tm, d), lambda i, j: (i, 0)),
        out_shape=jax.ShapeDtypeStruct((n, d), F32),
        scratch_shapes=[pltpu.VMEM((tm, d), BF16), pltpu.VMEM((tm, d), F32)],
        compiler_params=_cparams(("parallel", "arbitrary")),
        name="ffn_final" if final else "ffn",
    )(*args)


def _inproj_body(x_ref, g_ref, w_ref, q_ref, kv_ref, u_ref, gate_ref):
    x = x_ref[...]
    ms = jnp.mean(x * x, axis=-1, keepdims=True)
    xn = (x * lax.rsqrt(ms + EPS) * g_ref[...]).astype(BF16)
    nq, nkv, nu = q_ref.shape[1], kv_ref.shape[1], u_ref.shape[1]
    q = jnp.dot(xn, w_ref[:, 0:nq], preferred_element_type=F32)
    q_ref[...] = (q * (HEAD_DIM ** -0.5)).astype(BF16)
    kv_ref[...] = jnp.dot(xn, w_ref[:, nq:nq + nkv], preferred_element_type=F32).astype(BF16)
    u_ref[...] = jnp.dot(xn, w_ref[:, nq + nkv:nq + nkv + nu], preferred_element_type=F32)
    gl = jnp.dot(xn, w_ref[:, nq + nkv + nu:], preferred_element_type=F32)
    gate_ref[...] = jax.nn.sigmoid(gl)


def _inproj(h, gain, w_perm, d_ssm, *, tm=512):
    n, d = h.shape
    nq, nkv, ng = D_ATTN, 6 * D_KV, N_KV * LANES
    ncol = w_perm.shape[1]
    assert ncol == nq + nkv + d_ssm + ng
    return pl.pallas_call(
        _inproj_body,
        grid=(n // tm,),
        in_specs=[
            pl.BlockSpec((tm, d), lambda i: (i, 0)),
            pl.BlockSpec((1, d), lambda i: (0, 0)),
            pl.BlockSpec((d, ncol), lambda i: (0, 0)),
        ],
        out_specs=[
            pl.BlockSpec((tm, nq), lambda i: (i, 0)),
            pl.BlockSpec((tm, nkv), lambda i: (i, 0)),
            pl.BlockSpec((tm, d_ssm), lambda i: (i, 0)),
            pl.BlockSpec((tm, ng), lambda i: (i, 0)),
        ],
        out_shape=[
            jax.ShapeDtypeStruct((n, nq), BF16),
            jax.ShapeDtypeStruct((n, nkv), BF16),
            jax.ShapeDtypeStruct((n, d_ssm), F32),
            jax.ShapeDtypeStruct((n, ng), F32),
        ],
        compiler_params=_cparams(("parallel",)),
        name="inproj",
    )(h, gain.reshape(1, d), w_perm)


def _compress_body(x_ref, pe_ref, w1_ref, b1_ref, w2_ref, o_ref):
    w1 = w1_ref[...]
    c0 = jnp.dot(pe_ref[...], w1, preferred_element_type=F32)[0:1, :] + b1_ref[...]
    hid = jnp.dot(x_ref[...], w1, preferred_element_type=F32) + c0
    hid = jax.nn.gelu(hid, approximate=True)
    o_ref[...] = jnp.dot(hid.astype(BF16), w2_ref[...], preferred_element_type=F32).astype(BF16)


def _compress(xblk, pe, w1, b1, w2, *, tm=512):
    _, r, kdim = xblk.shape
    hid, dh = w1.shape[2], w2.shape[2]
    tm = min(tm, r)
    return pl.pallas_call(
        _compress_body,
        grid=(2, r // tm),
        in_specs=[
            pl.BlockSpec((None, tm, kdim), lambda s, i: (s, i, 0)),
            pl.BlockSpec((None, 8, kdim), lambda s, i: (s, 0, 0)),
            pl.BlockSpec((None, kdim, hid), lambda s, i: (s, 0, 0)),
            pl.BlockSpec((None, 1, hid), lambda s, i: (s, 0, 0)),
            pl.BlockSpec((None, hid, dh), lambda s, i: (s, 0, 0)),
        ],
        out_specs=pl.BlockSpec((None, tm, dh), lambda s, i: (s, i, 0)),
        out_shape=jax.ShapeDtypeStruct((2, r, dh), BF16),
        compiler_params=_cparams(("parallel", "parallel")),
        name="compress",
    )(xblk, pe, w1, b1, w2)


def _nt_dot(a, b):
    return lax.dot_general(a, b, (((1,), (1,)), ((), ())), preferred_element_type=F32)


def _nsa_body(q_ref, ks_ref, vs_ref, kw_ref, vw_ref, kcb_ref, vcb_ref, gate_ref, biasc_ref, dt_ref, wb_ref,
              ov_ref, e3_ref, o_ref, s_ref, m_ref, l_ref, acc_ref):
    i = pl.program_id(2)
    t0 = i * TQ
    q = q_ref[...]
    q4 = jnp.concatenate([q[:, h * HEAD_DIM:(h + 1) * HEAD_DIM] for h in range(HEADS_PER_KV)], axis=0)

    sc = _nt_dot(q4, kcb_ref[...]) + biasc_ref[...]
    row_t = t0 + (lax.broadcasted_iota(jnp.int32, (ROWS4, LANES), 0) & (TQ - 1))
    col_c = lax.broadcasted_iota(jnp.int32, (ROWS4, LANES), 1)
    valid_c = (row_t - col_c * CMP_STRIDE - (CMP_LEN - 1) >= 0) & (col_c < LANES - 1)
    mc = jnp.max(sc, axis=-1, keepdims=True)
    pc = jnp.where(valid_c, jnp.exp(sc - mc), 0.0)
    pc = pc / jnp.maximum(jnp.sum(pc, axis=-1, keepdims=True), 1e-30)
    pcb = pc.astype(BF16)
    o_cmp = jnp.dot(pcb, vcb_ref[...], preferred_element_type=F32)
    pimp = jnp.dot(pcb, ov_ref[...], preferred_element_type=F32)
    imp = pimp[0:TQ] + pimp[TQ:2 * TQ] + pimp[2 * TQ:3 * TQ] + pimp[3 * TQ:4 * TQ]

    tq_pos = t0 + lax.broadcasted_iota(jnp.int32, (TQ, LANES), 0)
    jb = lax.broadcasted_iota(jnp.int32, (TQ, LANES), 1)
    cur = lax.shift_right_logical(tq_pos, int(math.log2(SEL_BLOCK)))
    forced = (jb == 0) | (jb == cur) | (jb == cur - 1)
    causal_blk = jb * SEL_BLOCK <= tq_pos
    impm = jnp.where(forced, 1e6, jnp.where(causal_blk, imp, -1e9))
    ns = e3_ref.shape[0] * (LANES // SEL_BLOCK)
    cnt = jnp.zeros((TQ, LANES), F32)
    for jp in range(ns):
        col = impm[:, jp:jp + 1]
        beats = (col > impm) | ((col == impm) & (jb > jp))
        cnt = cnt + jnp.where(beats, 1.0, 0.0)
    sel = jnp.where((cnt < min(N_SELECT, ns)) & (jb < ns), 1.0, 0.0).astype(BF16)

    m_ref[...] = jnp.full((ROWS4, LANES), -3e38, F32)

    def score_tile(kt, _):
        k_t = ks_ref[pl.ds(pl.multiple_of(kt * LANES, LANES), LANES), :]
        msk = jnp.dot(sel, e3_ref[kt], preferred_element_type=F32)
        neg = (msk - 1.0) * 1e30
        neg4 = jnp.concatenate([neg] * HEADS_PER_KV, axis=0)
        bias = dt_ref[jnp.clip(kt - i + 2, 0, 2)]
        s = _nt_dot(q4, k_t) + bias + neg4
        s_ref[kt] = s
        m_ref[...] = jnp.maximum(m_ref[...], s)
        return 0

    lax.fori_loop(0, i + 1, score_tile, 0)
    m_ref[...] = jnp.broadcast_to(jnp.max(m_ref[...], axis=-1, keepdims=True), (ROWS4, LANES))
    l_ref[...] = jnp.zeros((ROWS4, LANES), F32)
    acc_ref[...] = jnp.zeros((ROWS4, HEAD_DIM), F32)

    def pv_tile(kt, _):
        p = jnp.exp(s_ref[kt] - m_ref[...])
        l_ref[...] += p
        v_t = vs_ref[pl.ds(pl.multiple_of(kt * LANES, LANES), LANES), :]
        acc_ref[...] += jnp.dot(p.astype(BF16), v_t, preferred_element_type=F32)
        return 0

    lax.fori_loop(0, i + 1, pv_tile, 0)
    o_sel = acc_ref[...] / jnp.maximum(jnp.sum(l_ref[...], axis=-1, keepdims=True), 1e-30)

    span = wb_ref.shape[1]
    w0 = pl.multiple_of(t0, LANES)
    sw = _nt_dot(q4, kw_ref[pl.ds(w0, span), :]) + wb_ref[...]
    kpos = t0 - WINDOW + lax.broadcasted_iota(jnp.int32, (ROWS4, span), 1)
    sw = jnp.where(kpos >= 0, sw, NEG)
    mw = jnp.max(sw, axis=-1, keepdims=True)
    pw = jnp.exp(sw - mw)
    lw = jnp.sum(pw, axis=-1, keepdims=True)
    o_win = jnp.dot(pw.astype(BF16), vw_ref[pl.ds(w0, span), :], preferred_element_type=F32)
    o_win = o_win / jnp.maximum(lw, 1e-30)

    gt = gate_ref[...]
    for h in range(HEADS_PER_KV):
        r = slice(h * TQ, (h + 1) * TQ)
        o_h = (gt[:, 3 * h:3 * h + 1] * o_cmp[r] + gt[:, 3 * h + 1:3 * h + 2] * o_sel[r]
               + gt[:, 3 * h + 2:3 * h + 3] * o_win[r])
        o_ref[:, h * HEAD_DIM:(h + 1) * HEAD_DIM] = o_h.astype(BF16)


def _nsa(q, kv, kwp, vwp, cb, gates, biasc, dtiles, wb, ov, e3):
    b, t, _ = q.shape
    nt = t // TQ
    tp = kwp.shape[1]
    span = wb.shape[2]
    gw = HEADS_PER_KV * HEAD_DIM
    ksel0, vsel0 = 2 * N_KV, 3 * N_KV
    return pl.pallas_call(
        _nsa_body,
        grid=(b, N_KV, nt),
        in_specs=[
            pl.BlockSpec((None, TQ, gw), lambda bi, g, i: (bi, i, g)),
            pl.BlockSpec((None, t, HEAD_DIM), lambda bi, g, i: (bi, 0, ksel0 + g)),
            pl.BlockSpec((None, t, HEAD_DIM), lambda bi, g, i: (bi, 0, vsel0 + g)),
            pl.BlockSpec((None, tp, HEAD_DIM), lambda bi, g, i: (bi, 0, g)),
            pl.BlockSpec((None, tp, HEAD_DIM), lambda bi, g, i: (bi, 0, g)),
            pl.BlockSpec((None, LANES, HEAD_DIM), lambda bi, g, i: (0, bi * N_KV + g, 0)),
            pl.BlockSpec((None, LANES, HEAD_DIM), lambda bi, g, i: (1, bi * N_KV + g, 0)),
            pl.BlockSpec((None, TQ, LANES), lambda bi, g, i: (bi, i, g)),
            pl.BlockSpec((None, None, ROWS4, LANES), lambda bi, g, i: (g, i, 0, 0)),
            pl.BlockSpec((None, 3, ROWS4, LANES), lambda bi, g, i: (g, 0, 0, 0)),
            pl.BlockSpec((None, ROWS4, span), lambda bi, g, i: (g, 0, 0)),
            pl.BlockSpec((LANES, LANES), lambda bi, g, i: (0, 0)),
            pl.BlockSpec((nt, LANES, LANES), lambda bi, g, i: (0, 0, 0)),
        ],
        out_specs=pl.BlockSpec((None, TQ, gw), lambda bi, g, i: (bi, i, g)),
        out_shape=jax.ShapeDtypeStruct((b, t, N_KV * gw), BF16),
        scratch_shapes=[
            pltpu.VMEM((nt, ROWS4, LANES), F32),
            pltpu.VMEM((ROWS4, LANES), F32),
            pltpu.VMEM((ROWS4, LANES), F32),
            pltpu.VMEM((ROWS4, HEAD_DIM), F32),
        ],
        compiler_params=_cparams(("parallel", "parallel", "arbitrary")),
        name="nsa",
    )(q, kv, kv, kwp, vwp, cb, cb, gates, biasc, dtiles, wb, ov, e3)


def _s5_body(u_ref, toep_ref, win_ref, wout_ref, al_ref, y_ref, inj_ref, xp_ref, *, nb):
    u = u_ref[...]
    half = al_ref.shape[1] // 2
    inj_ref[...] = jnp.dot(u, win_ref[...], preferred_element_type=F32)
    ar = jnp.broadcast_to(al_ref[0:1, 0:half], (nb, half))
    ai = jnp.broadcast_to(al_ref[0:1, half:], (nb, half))
    nchunk = u.shape[0] // nb

    def step(c, carry):
        xr, xi = carry
        r0 = pl.multiple_of(c * nb, nb)
        xp_ref[pl.ds(r0, nb), 0:half] = xr
        xp_ref[pl.ds(r0, nb), half:] = xi
        ir = inj_ref[pl.ds(r0, nb), 0:half]
        ii = inj_ref[pl.ds(r0, nb), half:]
        return ar * xr - ai * xi + ir, ar * xi + ai * xr + ii

    z = jnp.zeros((nb, half), F32)
    lax.fori_loop(0, nchunk, step, (z, z))
    y = jnp.dot(u, toep_ref[...], preferred_element_type=F32)
    y = y + jnp.dot(xp_ref[...].astype(BF16), wout_ref[...], preferred_element_type=F32)
    y_ref[...] = y


def _s5(u_t, toep, win, wout, al, nb):
    npair, rows, kdim = u_t.shape
    sdim = win.shape[2]
    return pl.pallas_call(
        functools.partial(_s5_body, nb=nb),
        grid=(npair,),
        in_specs=[
            pl.BlockSpec((None, rows, kdim), lambda p: (p, 0, 0)),
            pl.BlockSpec((None, kdim, kdim), lambda p: (p, 0, 0)),
            pl.BlockSpec((None, kdim, sdim), lambda p: (p, 0, 0)),
            pl.BlockSpec((None, sdim, kdim), lambda p: (p, 0, 0)),
            pl.BlockSpec((None, 8, sdim), lambda p: (p, 0, 0)),
        ],
        out_specs=pl.BlockSpec((None, rows, kdim), lambda p: (p, 0, 0)),
        out_shape=jax.ShapeDtypeStruct((npair, rows, kdim), F32),
        scratch_shapes=[pltpu.VMEM((rows, sdim), F32), pltpu.VMEM((rows, sdim), F32)],
        compiler_params=_cparams(("parallel",)),
        name="s5",
    )(u_t, toep, win, wout, al)


def _glu_out_body(h_ref, a_ref, y_ref, u_ref, d_ref, wg_ref, bg_ref, woa_ref, wos_ref, o_ref):
    y = y_ref[...] + d_ref[...] * u_ref[...]
    hg = jax.nn.gelu(y, approximate=True)
    z = jnp.dot(hg.astype(BF16), wg_ref[...], preferred_element_type=F32) + bg_ref[...]
    s = hg * jax.nn.sigmoid(z)
    mix = jnp.dot(a_ref[...], woa_ref[...], preferred_element_type=F32)
    mix = mix + jnp.dot(s.astype(BF16), wos_ref[...], preferred_element_type=F32)
    o_ref[...] = h_ref[...] + mix


def _glu_out(h, a, y, u, d, wg, bg, woa, wos, *, tm=512):
    n, dm = h.shape
    da, ds = a.shape[1], y.shape[1]
    row = lambda w: pl.BlockSpec((tm, w), lambda i: (i, 0))
    full = lambda r, c: pl.BlockSpec((r, c), lambda i: (0, 0))
    return pl.pallas_call(
        _glu_out_body,
        grid=(n // tm,),
        in_specs=[row(dm), row(da), row(ds), row(ds), full(1, ds), full(ds, ds), full(1, ds),
                  full(da, dm), full(ds, dm)],
        out_specs=row(dm),
        out_shape=jax.ShapeDtypeStruct((n, dm), F32),
        compiler_params=_cparams(("parallel",)),
        name="glu_out",
    )(h, a, y, u, d.reshape(1, ds), wg, bg.reshape(1, ds), woa, wos)


def _t5_bucket(dist):
    n = jnp.maximum(dist, 0)
    max_exact = N_BUCKETS // 2
    nf = jnp.maximum(n, 1).astype(F32)
    large = max_exact + (jnp.log(nf / max_exact) / math.log(MAX_DISTANCE / max_exact)
                         * (N_BUCKETS - max_exact)).astype(jnp.int32)
    large = jnp.minimum(large, N_BUCKETS - 1)
    return jnp.where(n < max_exact, n, large)


def _bias_tables(rel_bias, t):
    nt = t // TQ
    span = TQ + WINDOW
    table = rel_bias.astype(F32).reshape(N_BUCKETS, N_KV, HEADS_PER_KV)

    def lookup(dist, valid):
        bias = jnp.transpose(table[_t5_bucket(dist)], (2, 3, 0, 1))
        return jnp.where(valid[None, None], bias, NEG)

    tt = jnp.arange(t)[:, None]
    cc = jnp.arange(LANES)[None, :]
    dist_c = tt - (cc * CMP_STRIDE + CMP_LEN - 1)
    nc = (t - CMP_LEN) // CMP_STRIDE + 1
    bc = lookup(dist_c, (dist_c >= 0) & (cc < nc))
    bc = bc.reshape(N_KV, HEADS_PER_KV, nt, TQ, LANES).transpose(0, 2, 1, 3, 4).reshape(N_KV, nt, ROWS4, LANES)
    a = jnp.arange(TQ)[:, None]
    k = jnp.arange(TQ)[None, :]
    always = jnp.ones((TQ, TQ), bool)
    far = lookup(jnp.full((TQ, TQ), 2 * TQ), always)
    sub = lookup(TQ + a - k, always)
    diag = lookup(a - k, a - k >= 0)
    dt = jnp.stack([far, sub, diag], axis=1).reshape(N_KV, 3, ROWS4, TQ)
    m = jnp.arange(span)[None, :]
    dist_w = a + WINDOW - m
    wb = lookup(dist_w, (dist_w >= 0) & (dist_w < WINDOW)).reshape(N_KV, ROWS4, span)
    return bc, dt, wb


def _sel_tables(t):
    ns = t // SEL_BLOCK
    nc_pad = LANES
    c_start = jnp.arange(nc_pad) * CMP_STRIDE
    j_start = jnp.arange(LANES) * SEL_BLOCK
    ov = jnp.clip(jnp.minimum(c_start[:, None] + CMP_LEN, j_start[None, :] + SEL_BLOCK)
                  - jnp.maximum(c_start[:, None], j_start[None, :]), 0, None).astype(F32) / CMP_LEN
    ov = jnp.where(jnp.arange(LANES)[None, :] < ns, ov, 0.0).astype(BF16)
    kpos = jnp.arange(t)
    e = (kpos[None, :] // SEL_BLOCK == jnp.arange(LANES)[:, None]).astype(BF16)
    e3 = e.reshape(LANES, t // LANES, LANES).transpose(1, 0, 2)
    return ov, e3


def _s5_tables(lam_re, lam_im, log_step, b_re, b_im, c_re, c_im):
    hi = lax.Precision.HIGHEST
    ng, p = lam_re.shape
    hch = b_re.shape[2]
    L = SSM_CHUNK
    step = jnp.exp(log_step.astype(F32))[:, None]
    lre, lim = lam_re.astype(F32), lam_im.astype(F32)
    mag = jnp.exp(lre * step)
    ab_re, ab_im = mag * jnp.cos(lim * step), mag * jnp.sin(lim * step)
    nr, ni = ab_re - 1.0, ab_im
    den = lre * lre + lim * lim
    f_re, f_im = (nr * lre + ni * lim) / den, (ni * lre - nr * lim) / den
    br, bim = b_re.astype(F32), b_im.astype(F32)
    bb_re = f_re[..., None] * br - f_im[..., None] * bim
    bb_im = f_re[..., None] * bim + f_im[..., None] * br
    cr, ci = c_re.astype(F32), c_im.astype(F32)
    pr, pi = [jnp.ones_like(ab_re)], [jnp.zeros_like(ab_re)]
    for _ in range(L):
        pr, pi = pr + [pr[-1] * ab_re - pi[-1] * ab_im], pi + [pr[-1] * ab_im + pi[-1] * ab_re]
    pw_re, pw_im = jnp.stack(pr, 0), jnp.stack(pi, 0)
    cp_re = cr[None] * pw_re[:, :, None, :] - ci[None] * pw_im[:, :, None, :]
    cp_im = -(cr[None] * pw_im[:, :, None, :] + ci[None] * pw_re[:, :, None, :])
    kern = (jnp.einsum('tghp,gpk->gthk', cp_re[:L], bb_re, precision=hi)
            + jnp.einsum('tghp,gpk->gthk', cp_im[:L], bb_im, precision=hi))
    kern = jnp.concatenate([kern, jnp.zeros((ng, 1, hch, hch), F32)], axis=1)
    s_i, t_i = jnp.arange(L)[:, None], jnp.arange(L)[None, :]
    tau = jnp.where(t_i >= s_i, t_i - s_i, L)
    toep = kern[:, tau]
    toep = toep.transpose(0, 1, 4, 2, 3).reshape(ng, L * hch, L * hch)
    wr = pw_re[L - 1 - jnp.arange(L)]
    wi = pw_im[L - 1 - jnp.arange(L)]
    win_re = (wr[..., None] * bb_re[None] - wi[..., None] * bb_im[None])
    win_im = (wr[..., None] * bb_im[None] + wi[..., None] * bb_re[None])
    win_re = win_re.transpose(1, 0, 3, 2).reshape(ng, L * hch, p)
    win_im = win_im.transpose(1, 0, 3, 2).reshape(ng, L * hch, p)
    wo_re = cp_re[1:].transpose(1, 3, 0, 2).reshape(ng, p, L * hch)
    wo_im = cp_im[1:].transpose(1, 3, 0, 2).reshape(ng, p, L * hch)

    npair = ng // SSM_PAIR

    def blockdiag(m):
        m = m.reshape(npair, SSM_PAIR, *m.shape[1:])
        z = jnp.zeros_like(m[:, 0])
        return jnp.concatenate([jnp.concatenate([m[:, 0], z], axis=2), jnp.concatenate([z, m[:, 1]], axis=2)], axis=1)

    toep_p = blockdiag(toep)
    win_p = jnp.concatenate([blockdiag(win_re), blockdiag(win_im)], axis=2)
    wout_p = jnp.concatenate([blockdiag(wo_re), blockdiag(wo_im)], axis=1)
    al = jnp.concatenate([pw_re[L].reshape(npair, SSM_PAIR * p), pw_im[L].reshape(npair, SSM_PAIR * p)], axis=1)
    al = jnp.broadcast_to(al[:, None, :], (npair, 8, 2 * SSM_PAIR * p))
    return toep_p.astype(BF16), win_p.astype(BF16), wout_p.astype(BF16), al


def _mixers(h1, mix_norm, w_in, cmp_k, cmp_v, rel_bias, ssm, ssm_d, glu_w, glu_b, w_out, b, t):
    n, d = h1.shape
    d_ssm = glu_w.shape[0]
    ng = d_ssm // SSM_GROUP
    o_g, o_u = D_ATTN + 6 * D_KV, D_ATTN + 6 * D_KV + 3 * N_HEADS
    gcols = 3 * HEADS_PER_KV
    wg = [jnp.pad(w_in[:, o_g + g * gcols:o_g + (g + 1) * gcols], ((0, 0), (0, LANES - gcols))) for g in range(N_KV)]
    w_perm = jnp.concatenate([w_in[:, :o_g], w_in[:, o_u:]] + wg, axis=1).astype(BF16)
    q, kv, u, gates = _inproj(h1, mix_norm, w_perm, d_ssm)

    nch = t // CMP_STRIDE

    def blocks(cols):
        kf = kv[:, cols * D_KV:(cols + 1) * D_KV].reshape(b, nch, CMP_STRIDE, N_KV, HEAD_DIM)
        kf = kf.transpose(0, 3, 1, 2, 4).reshape(b, N_KV, nch, CMP_STRIDE * HEAD_DIM)
        nxt = jnp.concatenate([kf[:, :, 1:], jnp.zeros_like(kf[:, :, :1])], axis=2)
        return jnp.concatenate([kf, nxt], axis=-1).reshape(b * N_KV * nch, CMP_LEN * HEAD_DIM)

    assert nch == LANES, "compressed-block axis is laid out on one 128-lane tile"
    xblk = jnp.stack([blocks(0), blocks(1)], axis=0)
    pe = jnp.stack([jnp.broadcast_to(c[0].reshape(1, -1), (8, CMP_LEN * HEAD_DIM)) for c in (cmp_k, cmp_v)], 0)
    w1 = jnp.stack([cmp_k[1], cmp_v[1]], 0).astype(BF16)
    b1 = jnp.stack([cmp_k[2].reshape(1, -1), cmp_v[2].reshape(1, -1)], 0).astype(F32)
    w2 = jnp.stack([cmp_k[3], cmp_v[3]], 0).astype(BF16)
    cb = _compress(xblk, pe.astype(BF16), w1, b1, w2)

    kv3 = kv.reshape(b, t, 6 * D_KV)
    kwp = jnp.pad(kv3[:, :, 4 * D_KV:5 * D_KV], ((0, 0), (WINDOW, 0), (0, 0)))
    vwp = jnp.pad(kv3[:, :, 5 * D_KV:6 * D_KV], ((0, 0), (WINDOW, 0), (0, 0)))
    biasc, dtiles, wb = _bias_tables(rel_bias, t)
    ov, e3 = _sel_tables(t)
    a = _nsa(q.reshape(b, t, D_ATTN), kv3, kwp, vwp, cb, gates.reshape(b, t, N_KV * LANES),
             biasc, dtiles, wb, ov, e3)

    L, npair, nchunk = SSM_CHUNK, ng // SSM_PAIR, t // SSM_CHUNK
    toep, win, wout, al = _s5_tables(*ssm)
    u_t = u.reshape(b, nchunk, L, npair, SSM_PAIR, SSM_GROUP).transpose(3, 1, 0, 4, 2, 5)
    u_t = u_t.reshape(npair, nchunk * b, SSM_PAIR * L * SSM_GROUP).astype(BF16)
    y_t = _s5(u_t, toep, win, wout, al, b)
    y = y_t.reshape(npair, nchunk, b, SSM_PAIR, L, SSM_GROUP).transpose(2, 1, 4, 0, 3, 5).reshape(n, d_ssm)

    wo = w_out.astype(BF16)
    return _glu_out(h1, a.reshape(n, D_ATTN), y, u, ssm_d, glu_w.astype(BF16), glu_b, wo[:D_ATTN], wo[D_ATTN:])


def kernel(x, ffn1_norm, ffn1_w1, ffn1_w3, ffn1_w2, mix_norm, w_in, cmp_pe_k, cmp_w1_k, cmp_b1_k, cmp_w2_k,
           cmp_pe_v, cmp_w1_v, cmp_b1_v, cmp_w2_v, rel_bias, ssm_lam_re, ssm_lam_im, ssm_log_step, ssm_b_re,
           ssm_b_im, ssm_c_re, ssm_c_im, ssm_d, glu_w, glu_b, w_out, ffn2_norm, ffn2_w1, ffn2_w3, ffn2_w2,
           final_norm):
    b, t, d = x.shape
    depth = ffn1_w1.shape[0]
    h = x.reshape(b * t, d)
    for l in range(depth):
        last = l == depth - 1
        h = _ffn(h, ffn1_norm[l], ffn1_w1[l].astype(BF16), ffn1_w3[l].astype(BF16), ffn1_w2[l].astype(BF16))
        ssm = (ssm_lam_re[l], ssm_lam_im[l], ssm_log_step[l], ssm_b_re[l], ssm_b_im[l], ssm_c_re[l], ssm_c_im[l])
        h = _mixers(h, mix_norm[l], w_in[l],
                    (cmp_pe_k[l], cmp_w1_k[l], cmp_b1_k[l], cmp_w2_k[l]),
                    (cmp_pe_v[l], cmp_w1_v[l], cmp_b1_v[l], cmp_w2_v[l]),
                    rel_bias, ssm, ssm_d[l], glu_w[l], glu_b[l], w_out[l], b, t)
        h = _ffn(h, ffn2_norm[l], ffn2_w1[l].astype(BF16), ffn2_w3[l].astype(BF16), ffn2_w2[l].astype(BF16),
                 final_gain=final_norm if last else None)
    return h.reshape(b, t, d)
```

```python
import functools
import math

import jax
import jax.numpy as jnp
from jax import lax
from jax.experimental import pallas as pl
from jax.experimental.pallas import tpu as pltpu

F32 = jnp.float32
BF16 = jnp.bfloat16

HEAD_DIM = 128
N_KV = 2
HEADS_PER_KV = 4
N_HEADS = N_KV * HEADS_PER_KV
D_ATTN = N_HEADS * HEAD_DIM
D_KV = N_KV * HEAD_DIM
CMP_LEN = 32
CMP_STRIDE = 16
SEL_BLOCK = 64
N_SELECT = 16
WINDOW = 512
N_BUCKETS = 32
MAX_DISTANCE = 128
SSM_GROUP = 16
SSM_STATE = 64
EPS = 1e-6
NEG = -1e30

LANES = 128
VMEM_LIMIT_BYTES = 56 * 1024 * 1024
TQ = 128
ROWS4 = HEADS_PER_KV * TQ
SSM_CHUNK = 16


def _cparams(sem):
    return pltpu.CompilerParams(dimension_semantics=sem, vmem_limit_bytes=VMEM_LIMIT_BYTES)


def _ffn_body(x_ref, g_ref, w1_ref, w3_ref, w2_ref, *rest, final):
    if final:
        fg_ref, o_ref, xn_ref, acc_ref = rest
    else:
        o_ref, xn_ref, acc_ref = rest
    j = pl.program_id(1)
    nj = pl.num_programs(1)

    @pl.when(j == 0)
    def _():
        x = x_ref[...]
        ms = jnp.mean(x * x, axis=-1, keepdims=True)
        xn_ref[...] = (x * lax.rsqrt(ms + EPS) * g_ref[...]).astype(BF16)

    xn = xn_ref[...]
    a = jnp.dot(xn, w1_ref[...], preferred_element_type=F32)
    b = jnp.dot(xn, w3_ref[...], preferred_element_type=F32)
    gated = (a * jax.nn.sigmoid(a)) * b
    contrib = jnp.dot(gated.astype(BF16), w2_ref[...], preferred_element_type=F32)

    @pl.when(j == 0)
    def _():
        acc_ref[...] = contrib

    @pl.when(j > 0)
    def _():
        acc_ref[...] += contrib

    @pl.when(j == nj - 1)
    def _():
        h = x_ref[...] + 0.5 * acc_ref[...]
        if final:
            ms = jnp.mean(h * h, axis=-1, keepdims=True)
            h = h * lax.rsqrt(ms + EPS) * fg_ref[...]
        o_ref[...] = h


def _ffn(x, gain, w1, w3, w2, final_gain=None, *, tm=512, tf=512):
    n, d = x.shape
    dff = w1.shape[1]
    final = final_gain is not None
    in_specs = [
        pl.BlockSpec((tm, d), lambda i, j: (i, 0)),
        pl.BlockSpec((1, d), lambda i, j: (0, 0)),
        pl.BlockSpec((d, tf), lambda i, j: (0, j)),
        pl.BlockSpec((d, tf), lambda i, j: (0, j)),
        pl.BlockSpec((tf, d), lambda i, j: (j, 0)),
    ]
    args = [x, gain.reshape(1, d), w1, w3, w2]
    if final:
        in_specs.append(pl.BlockSpec((1, d), lambda i, j: (0, 0)))
        args.append(final_gain.reshape(1, d))
    return pl.pallas_call(
        functools.partial(_ffn_body, final=final),
        grid=(n // tm, dff // tf),
        in_specs=in_specs,
        out_specs=pl.BlockSpec((tm, d), lambda i, j: (i, 0)),
        out_shape=jax.ShapeDtypeStruct((n, d), F32),
        scratch_shapes=[pltpu.VMEM((tm, d), BF16), pltpu.VMEM((tm, d), F32)],
        compiler_params=_cparams(("parallel", "arbitrary")),
        name="ffn_final" if final else "ffn",
    )(*args)


def _inproj_body(x_ref, g_ref, w_ref, q_ref, kvs_ref, kcv_ref, u_ref, gate_ref):
    i = pl.program_id(1)

    @pl.when(i == 0)
    def _():
        kvs_ref[...] = jnp.zeros(kvs_ref.shape, kvs_ref.dtype)
        kcv_ref[...] = jnp.zeros(kcv_ref.shape, kcv_ref.dtype)

    @pl.when(i > 0)
    def _():
        x = x_ref[...]
        ms = jnp.mean(x * x, axis=-1, keepdims=True)
        xn = (x * lax.rsqrt(ms + EPS) * g_ref[...]).astype(BF16)
        c0 = 0
        c1 = q_ref.shape[1]
        q = jnp.dot(xn, w_ref[:, c0:c1], preferred_element_type=F32)
        q_ref[...] = (q * (HEAD_DIM ** -0.5)).astype(BF16)
        c0, c1 = c1, c1 + kcv_ref.shape[1]
        kcv_ref[...] = jnp.dot(xn, w_ref[:, c0:c1], preferred_element_type=F32)
        c0, c1 = c1, c1 + kvs_ref.shape[1]
        kvs_ref[...] = jnp.dot(xn, w_ref[:, c0:c1], preferred_element_type=F32).astype(BF16)
        c0, c1 = c1, c1 + u_ref.shape[1]
        u_ref[...] = jnp.dot(xn, w_ref[:, c0:c1], preferred_element_type=F32)
        gate_ref[...] = jax.nn.sigmoid(jnp.dot(xn, w_ref[:, c1:], preferred_element_type=F32))


def _inproj(h, gain, w_perm, d_ssm):
    b, t, d = h.shape
    tm = WINDOW
    nt = t // tm
    nq, ncv, nvs, ng = D_ATTN, 2 * D_KV, 4 * D_KV, N_KV * LANES
    ncol = w_perm.shape[1]
    assert ncol == nq + ncv + nvs + d_ssm + ng and t % tm == 0
    data = lambda bi, i: (bi, jnp.maximum(i - 1, 0), 0)
    return pl.pallas_call(
        _inproj_body,
        grid=(b, nt + 1),
        in_specs=[
            pl.BlockSpec((None, tm, d), data),
            pl.BlockSpec((1, d), lambda bi, i: (0, 0)),
            pl.BlockSpec((d, ncol), lambda bi, i: (0, 0)),
        ],
        out_specs=[
            pl.BlockSpec((None, tm, nq), data),
            pl.BlockSpec((None, tm, nvs), lambda bi, i: (bi, i, 0)),
            pl.BlockSpec((None, tm, ncv), lambda bi, i: (bi, jnp.where(i == 0, nt, i - 1), 0)),
            pl.BlockSpec((None, tm, d_ssm), data),
            pl.BlockSpec((None, tm, ng), data),
        ],
        out_shape=[
            jax.ShapeDtypeStruct((b, t, nq), BF16),
            jax.ShapeDtypeStruct((b, t + tm, nvs), BF16),
            jax.ShapeDtypeStruct((b, t + tm, ncv), F32),
            jax.ShapeDtypeStruct((b, t, d_ssm), F32),
            jax.ShapeDtypeStruct((b, t, ng), F32),
        ],
        compiler_params=_cparams(("parallel", "arbitrary")),
        name="inproj",
    )(h, gain.reshape(1, d), w_perm)


def _compress_body(x_ref, pe_ref, w1_ref, b1_ref, w2_ref, o_ref):
    nb, nblk = o_ref.shape[0], o_ref.shape[1]
    acc = jnp.broadcast_to(b1_ref[...], (nb * nblk, b1_ref.shape[1])).astype(F32)
    for l in range(CMP_LEN):
        xl = x_ref[:, pl.ds(l, nblk, stride=CMP_STRIDE), :] + pe_ref[l:l + 1, :]
        acc = acc + jnp.dot(xl.reshape(nb * nblk, xl.shape[2]).astype(BF16), w1_ref[l], preferred_element_type=F32)
    hid = jax.nn.gelu(acc, approximate=True)
    out = jnp.dot(hid.astype(BF16), w2_ref[...], preferred_element_type=F32)
    o_ref[...] = out.reshape(o_ref.shape).astype(BF16)


def _compress(kcv, pe, w1, b1, w2, nblk, *, nb=4):
    b, tpad, _ = kcv.shape
    nb = min(nb, b)
    hid, dh = w1.shape[3], w2.shape[2]
    assert (nblk - 1) * CMP_STRIDE + CMP_LEN <= tpad
    return pl.pallas_call(
        _compress_body,
        grid=(2 * N_KV, b // nb),
        in_specs=[
            pl.BlockSpec((nb, tpad, dh), lambda s, i: (i, 0, s)),
            pl.BlockSpec((None, CMP_LEN, dh), lambda s, i: (s // N_KV, 0, 0)),
            pl.BlockSpec((None, CMP_LEN, dh, hid), lambda s, i: (s // N_KV, 0, 0, 0)),
            pl.BlockSpec((None, 1, hid), lambda s, i: (s // N_KV, 0, 0)),
            pl.BlockSpec((None, hid, dh), lambda s, i: (s // N_KV, 0, 0)),
        ],
        out_specs=pl.BlockSpec((None, None, nb, nblk, dh), lambda s, i: (s // N_KV, s % N_KV, i, 0, 0)),
        out_shape=jax.ShapeDtypeStruct((2, N_KV, b, nblk, dh), BF16),
        compiler_params=_cparams(("parallel", "parallel")),
        name="compress",
    )(kcv, pe, w1, b1, w2)


def _nt_dot(a, b):
    return lax.dot_general(a, b, (((1,), (1,)), ((), ())), preferred_element_type=F32)


def _nsa_body(q_ref, ks_ref, vs_ref, kw_ref, vw_ref, kcb_ref, vcb_ref, gate_ref, biasc_ref, dt_ref, wb_ref,
              ov_ref, e3_ref, o_ref, s_ref, m_ref, l_ref, acc_ref):
    i = pl.program_id(2)
    t0 = i * TQ
    q = q_ref[...]
    q4 = jnp.concatenate([q[:, h * HEAD_DIM:(h + 1) * HEAD_DIM] for h in range(HEADS_PER_KV)], axis=0)

    sc = _nt_dot(q4, kcb_ref[...]) + biasc_ref[...]
    row_t = t0 + (lax.broadcasted_iota(jnp.int32, (ROWS4, LANES), 0) & (TQ - 1))
    col_c = lax.broadcasted_iota(jnp.int32, (ROWS4, LANES), 1)
    valid_c = (row_t - col_c * CMP_STRIDE - (CMP_LEN - 1) >= 0) & (col_c < LANES - 1)
    mc = jnp.max(sc, axis=-1, keepdims=True)
    pc = jnp.where(valid_c, jnp.exp(sc - mc), 0.0)
    pc = pc / jnp.maximum(jnp.sum(pc, axis=-1, keepdims=True), 1e-30)
    pcb = pc.astype(BF16)
    o_cmp = jnp.dot(pcb, vcb_ref[...], preferred_element_type=F32)
    pimp = jnp.dot(pcb, ov_ref[...], preferred_element_type=F32)
    imp = pimp[0:TQ] + pimp[TQ:2 * TQ] + pimp[2 * TQ:3 * TQ] + pimp[3 * TQ:4 * TQ]

    tq_pos = t0 + lax.broadcasted_iota(jnp.int32, (TQ, LANES), 0)
    jb = lax.broadcasted_iota(jnp.int32, (TQ, LANES), 1)
    cur = lax.shift_right_logical(tq_pos, int(math.log2(SEL_BLOCK)))
    forced = (jb == 0) | (jb == cur) | (jb == cur - 1)
    causal_blk = jb * SEL_BLOCK <= tq_pos
    impm = jnp.where(forced, 1e6, jnp.where(causal_blk, imp, -1e9))
    ns = e3_ref.shape[0] * (LANES // SEL_BLOCK)
    cnt = jnp.zeros((TQ, LANES), F32)
    for jp in range(ns):
        col = impm[:, jp:jp + 1]
        beats = (col > impm) | ((col == impm) & (jb > jp))
        cnt = cnt + jnp.where(beats, 1.0, 0.0)
    sel = jnp.where((cnt < min(N_SELECT, ns)) & (jb < ns), 1.0, 0.0).astype(BF16)

    m_ref[...] = jnp.full((ROWS4, LANES), -3e38, F32)

    def score_tile(kt, _):
        k_t = ks_ref[pl.ds(pl.multiple_of(WINDOW + kt * LANES, LANES), LANES), :]
        msk = jnp.dot(sel, e3_ref[kt], preferred_element_type=F32)
        neg = (msk - 1.0) * 1e30
        neg4 = jnp.concatenate([neg] * HEADS_PER_KV, axis=0)
        bias = dt_ref[jnp.clip(kt - i + 2, 0, 2)]
        s = _nt_dot(q4, k_t) + bias + neg4
        s_ref[kt] = s
        m_ref[...] = jnp.maximum(m_ref[...], s)
        return 0

    lax.fori_loop(0, i + 1, score_tile, 0)
    m_ref[...] = jnp.broadcast_to(jnp.max(m_ref[...], axis=-1, keepdims=True), (ROWS4, LANES))
    l_ref[...] = jnp.zeros((ROWS4, LANES), F32)
    acc_ref[...] = jnp.zeros((ROWS4, HEAD_DIM), F32)

    def pv_tile(kt, _):
        p = jnp.exp(s_ref[kt] - m_ref[...])
        l_ref[...] += p
        v_t = vs_ref[pl.ds(pl.multiple_of(WINDOW + kt * LANES, LANES), LANES), :]
        acc_ref[...] += jnp.dot(p.astype(BF16), v_t, preferred_element_type=F32)
        return 0

    lax.fori_loop(0, i + 1, pv_tile, 0)
    o_sel = acc_ref[...] / jnp.maximum(jnp.sum(l_ref[...], axis=-1, keepdims=True), 1e-30)

    span = wb_ref.shape[1]
    w0 = pl.multiple_of(t0, LANES)
    sw = _nt_dot(q4, kw_ref[pl.ds(w0, span), :]) + wb_ref[...]
    kpos = t0 - WINDOW + lax.broadcasted_iota(jnp.int32, (ROWS4, span), 1)
    sw = jnp.where(kpos >= 0, sw, NEG)
    mw = jnp.max(sw, axis=-1, keepdims=True)
    pw = jnp.exp(sw - mw)
    lw = jnp.sum(pw, axis=-1, keepdims=True)
    o_win = jnp.dot(pw.astype(BF16), vw_ref[pl.ds(w0, span), :], preferred_element_type=F32)
    o_win = o_win / jnp.maximum(lw, 1e-30)

    gt = gate_ref[...]
    for h in range(HEADS_PER_KV):
        r = slice(h * TQ, (h + 1) * TQ)
        o_h = (gt[:, 3 * h:3 * h + 1] * o_cmp[r] + gt[:, 3 * h + 1:3 * h + 2] * o_sel[r]
               + gt[:, 3 * h + 2:3 * h + 3] * o_win[r])
        o_ref[:, h * HEAD_DIM:(h + 1) * HEAD_DIM] = o_h.astype(BF16)


def _nsa(q, kvs, cb, gates, biasc, dtiles, wb, ov, e3):
    b, t, _ = q.shape
    nt = t // TQ
    tp = kvs.shape[1]
    span = wb.shape[2]
    gw = HEADS_PER_KV * HEAD_DIM
    assert tp == t + WINDOW and span == TQ + WINDOW

    def stream(j):
        return pl.BlockSpec((None, tp, HEAD_DIM), lambda bi, g, i: (bi, 0, j * N_KV + g))

    return pl.pallas_call(
        _nsa_body,
        grid=(b, N_KV, nt),
        in_specs=[
            pl.BlockSpec((None, TQ, gw), lambda bi, g, i: (bi, i, g)),
            stream(0), stream(1), stream(2), stream(3),
            pl.BlockSpec((None, None, None, LANES, HEAD_DIM), lambda bi, g, i: (0, g, bi, 0, 0)),
            pl.BlockSpec((None, None, None, LANES, HEAD_DIM), lambda bi, g, i: (1, g, bi, 0, 0)),
            pl.BlockSpec((None, TQ, LANES), lambda bi, g, i: (bi, i, g)),
            pl.BlockSpec((None, None, ROWS4, LANES), lambda bi, g, i: (g, i, 0, 0)),
            pl.BlockSpec((None, 3, ROWS4, LANES), lambda bi, g, i: (g, 0, 0, 0)),
            pl.BlockSpec((None, ROWS4, span), lambda bi, g, i: (g, 0, 0)),
            pl.BlockSpec((LANES, LANES), lambda bi, g, i: (0, 0)),
            pl.BlockSpec((nt, LANES, LANES), lambda bi, g, i: (0, 0, 0)),
        ],
        out_specs=pl.BlockSpec((None, TQ, gw), lambda bi, g, i: (bi, i, g)),
        out_shape=jax.ShapeDtypeStruct((b, t, N_KV * gw), BF16),
        scratch_shapes=[
            pltpu.VMEM((nt, ROWS4, LANES), F32),
            pltpu.VMEM((ROWS4, LANES), F32),
            pltpu.VMEM((ROWS4, LANES), F32),
            pltpu.VMEM((ROWS4, HEAD_DIM), F32),
        ],
        compiler_params=_cparams(("parallel", "parallel", "arbitrary")),
        name="nsa",
    )(q, kvs, kvs, kvs, kvs, cb, cb, gates, biasc, dtiles, wb, ov, e3)


def _s5_body(u_ref, grev_ref, win_ref, wout_ref, al_ref, d_ref, y_ref, ucat_ref, inj_ref, xp_ref):
    nb, t, lanes = u_ref.shape
    L = SSM_CHUNK
    nchunk = t // L
    rows = nb * nchunk
    sdim = al_ref.shape[1] // 2

    def u_at(s):
        return u_ref[:, pl.ds(s, nchunk, stride=L), :].reshape(rows, lanes)

    for s in range(L):
        ucat_ref[:, s * lanes:(s + 1) * lanes] = u_at(s).astype(BF16)
    inj = jnp.dot(ucat_ref[...], win_ref[...], preferred_element_type=F32)
    npl = sdim // lanes
    for k in range(2 * npl):
        inj_ref[k] = inj[:, k * lanes:(k + 1) * lanes]
    ar = [jnp.broadcast_to(al_ref[0:1, k * lanes:(k + 1) * lanes], (nb, lanes)) for k in range(npl)]
    ai = [jnp.broadcast_to(al_ref[0:1, sdim + k * lanes:sdim + (k + 1) * lanes], (nb, lanes)) for k in range(npl)]

    def step(c, carry):
        xr, xi = carry
        rsel = pl.ds(c, nb, stride=nchunk)
        nr, ni = [], []
        for k in range(npl):
            xp_ref[k, rsel, :] = xr[k]
            xp_ref[npl + k, rsel, :] = xi[k]
            nr.append(ar[k] * xr[k] - ai[k] * xi[k] + inj_ref[k, rsel, :])
            ni.append(ar[k] * xi[k] + ai[k] * xr[k] + inj_ref[npl + k, rsel, :])
        return tuple(nr), tuple(ni)

    z = tuple(jnp.zeros((nb, lanes), F32) for _ in range(npl))
    lax.fori_loop(0, nchunk, step, (z, z))
    xp = jnp.concatenate([xp_ref[k] for k in range(2 * npl)], axis=1).astype(BF16)
    d = d_ref[...]
    for tp in range(0, L, 2):
        res = jnp.dot(ucat_ref[:, 0:(tp + 2) * lanes], grev_ref[(L - 1 - tp) * lanes:(L + 1) * lanes, :],
                      preferred_element_type=F32)
        res = res + jnp.dot(xp, wout_ref[:, tp * lanes:(tp + 2) * lanes], preferred_element_type=F32)
        for k in range(2):
            y = res[:, k * lanes:(k + 1) * lanes] + d * u_at(tp + k)
            y_ref[:, pl.ds(tp + k, nchunk, stride=L), :] = y.reshape(nb, nchunk, lanes)


def _s5(u, grev, win, wout, al, d, *, nb=4):
    b, t, dch = u.shape
    nb = min(nb, b)
    nblk = dch // LANES
    rows = nb * (t // SSM_CHUNK)
    kcat, sdim2 = win.shape[1], win.shape[2]
    return pl.pallas_call(
        _s5_body,
        grid=(nblk, b // nb),
        in_specs=[
            pl.BlockSpec((nb, t, LANES), lambda j, i: (i, 0, j)),
            pl.BlockSpec((None,) + grev.shape[1:], lambda j, i: (j, 0, 0)),
            pl.BlockSpec((None, kcat, sdim2), lambda j, i: (j, 0, 0)),
            pl.BlockSpec((None, sdim2, kcat), lambda j, i: (j, 0, 0)),
            pl.BlockSpec((None, 8, sdim2), lambda j, i: (j, 0, 0)),
            pl.BlockSpec((1, LANES), lambda j, i: (0, j)),
        ],
        out_specs=pl.BlockSpec((nb, t, LANES), lambda j, i: (i, 0, j)),
        out_shape=jax.ShapeDtypeStruct((b, t, dch), F32),
        scratch_shapes=[pltpu.VMEM((rows, kcat), BF16), pltpu.VMEM((sdim2 // LANES, rows, LANES), F32),
                        pltpu.VMEM((sdim2 // LANES, rows, LANES), F32)],
        compiler_params=_cparams(("parallel", "arbitrary")),
        name="s5",
    )(u, grev, win, wout, al, d.reshape(1, dch))


def _glu_out_body(h_ref, a_ref, y_ref, wg_ref, bg_ref, woa_ref, wos_ref, o_ref):
    hg = jax.nn.gelu(y_ref[...], approximate=True)
    z = jnp.dot(hg.astype(BF16), wg_ref[...], preferred_element_type=F32) + bg_ref[...]
    s = hg * jax.nn.sigmoid(z)
    mix = jnp.dot(a_ref[...], woa_ref[...], preferred_element_type=F32)
    mix = mix + jnp.dot(s.astype(BF16), wos_ref[...], preferred_element_type=F32)
    o_ref[...] = h_ref[...] + mix


def _glu_out(h, a, y, wg, bg, woa, wos, *, tm=512):
    n, dm = h.shape
    da, ds = a.shape[1], y.shape[1]
    row = lambda w: pl.BlockSpec((tm, w), lambda i: (i, 0))
    full = lambda r, c: pl.BlockSpec((r, c), lambda i: (0, 0))
    return pl.pallas_call(
        _glu_out_body,
        grid=(n // tm,),
        in_specs=[row(dm), row(da), row(ds), full(ds, ds), full(1, ds), full(da, dm), full(ds, dm)],
        out_specs=row(dm),
        out_shape=jax.ShapeDtypeStruct((n, dm), F32),
        compiler_params=_cparams(("parallel",)),
        name="glu_out",
    )(h, a, y, wg, bg.reshape(1, ds), woa, wos)


def _t5_bucket(dist):
    n = jnp.maximum(dist, 0)
    max_exact = N_BUCKETS // 2
    nf = jnp.maximum(n, 1).astype(F32)
    large = max_exact + (jnp.log(nf / max_exact) / math.log(MAX_DISTANCE / max_exact)
                         * (N_BUCKETS - max_exact)).astype(jnp.int32)
    large = jnp.minimum(large, N_BUCKETS - 1)
    return jnp.where(n < max_exact, n, large)


def _bias_tables(rel_bias, t):
    nt = t // TQ
    span = TQ + WINDOW
    table = rel_bias.astype(F32).reshape(N_BUCKETS, N_KV, HEADS_PER_KV, 1, 1)
    buckets = _t5_bucket(jnp.arange(2 * MAX_DISTANCE))
    first = [jnp.sum((buckets < k).astype(jnp.int32)) for k in range(N_BUCKETS)]

    def lookup(dist, valid):
        bias = jnp.broadcast_to(table[0], (N_KV, HEADS_PER_KV) + dist.shape)
        for k in range(1, N_BUCKETS):
            bias = jnp.where((dist >= first[k])[None, None], table[k], bias)
        return jnp.where(valid[None, None], bias, NEG)

    tt = jnp.arange(t)[:, None]
    cc = jnp.arange(LANES)[None, :]
    dist_c = tt - (cc * CMP_STRIDE + CMP_LEN - 1)
    nc = (t - CMP_LEN) // CMP_STRIDE + 1
    bc = lookup(dist_c, (dist_c >= 0) & (cc < nc))
    bc = bc.reshape(N_KV, HEADS_PER_KV, nt, TQ, LANES).transpose(0, 2, 1, 3, 4).reshape(N_KV, nt, ROWS4, LANES)
    a = jnp.arange(TQ)[:, None]
    k = jnp.arange(TQ)[None, :]
    always = jnp.ones((TQ, TQ), bool)
    far = lookup(jnp.full((TQ, TQ), 2 * TQ), always)
    sub = lookup(TQ + a - k, always)
    diag = lookup(a - k, a - k >= 0)
    dt = jnp.stack([far, sub, diag], axis=1).reshape(N_KV, 3, ROWS4, TQ)
    m = jnp.arange(span)[None, :]
    dist_w = a + WINDOW - m
    wb = lookup(dist_w, (dist_w >= 0) & (dist_w < WINDOW)).reshape(N_KV, ROWS4, span)
    return bc, dt, wb


def _sel_tables(t):
    ns = t // SEL_BLOCK
    nc_pad = LANES
    c_start = jnp.arange(nc_pad) * CMP_STRIDE
    j_start = jnp.arange(LANES) * SEL_BLOCK
    ov = jnp.clip(jnp.minimum(c_start[:, None] + CMP_LEN, j_start[None, :] + SEL_BLOCK)
                  - jnp.maximum(c_start[:, None], j_start[None, :]), 0, None).astype(F32) / CMP_LEN
    ov = jnp.where(jnp.arange(LANES)[None, :] < ns, ov, 0.0).astype(BF16)
    kpos = jnp.arange(t)
    e = (kpos[None, :] // SEL_BLOCK == jnp.arange(LANES)[:, None]).astype(BF16)
    e3 = e.reshape(LANES, t // LANES, LANES).transpose(1, 0, 2)
    return ov, e3


def _s5_tables(lam_re, lam_im, log_step, b_re, b_im, c_re, c_im):
    hi = lax.Precision.HIGHEST
    ng, p = lam_re.shape
    hch = b_re.shape[2]
    L = SSM_CHUNK
    step = jnp.exp(log_step.astype(F32))[:, None]
    lre, lim = lam_re.astype(F32), lam_im.astype(F32)
    mag = jnp.exp(lre * step)
    ab_re, ab_im = mag * jnp.cos(lim * step), mag * jnp.sin(lim * step)
    nr, ni = ab_re - 1.0, ab_im
    den = lre * lre + lim * lim
    f_re, f_im = (nr * lre + ni * lim) / den, (ni * lre - nr * lim) / den
    br, bim = b_re.astype(F32), b_im.astype(F32)
    bb_re = f_re[..., None] * br - f_im[..., None] * bim
    bb_im = f_re[..., None] * bim + f_im[..., None] * br
    cr, ci = c_re.astype(F32), c_im.astype(F32)
    pr, pi = [jnp.ones_like(ab_re)], [jnp.zeros_like(ab_re)]
    for _ in range(L):
        pr, pi = pr + [pr[-1] * ab_re - pi[-1] * ab_im], pi + [pr[-1] * ab_im + pi[-1] * ab_re]
    pw_re, pw_im = jnp.stack(pr, 0), jnp.stack(pi, 0)
    cp_re = cr[None] * pw_re[:, :, None, :] - ci[None] * pw_im[:, :, None, :]
    cp_im = -(cr[None] * pw_im[:, :, None, :] + ci[None] * pw_re[:, :, None, :])
    kern = (jnp.einsum('tghp,gpk->gthk', cp_re[:L], bb_re, precision=hi)
            + jnp.einsum('tghp,gpk->gthk', cp_im[:L], bb_im, precision=hi))
    gpb = LANES // hch
    nblk = ng // gpb
    eye = jnp.eye(gpb, dtype=F32)
    kblk = jnp.einsum('jgthk,gc->jtgkch', kern.reshape(nblk, gpb, L, hch, hch), eye).reshape(nblk, L, LANES, LANES)
    zero = jnp.zeros((nblk, 1, LANES, LANES), F32)
    kpad = jnp.concatenate([zero, kblk, zero], axis=1)
    gmat = jnp.concatenate([kpad[:, :L + 1], kpad[:, 1:]], axis=3)
    grev = gmat[:, ::-1].reshape(nblk, (L + 1) * LANES, 2 * LANES)
    wr = pw_re[L - 1 - jnp.arange(L)]
    wi = pw_im[L - 1 - jnp.arange(L)]
    win_re = (wr[..., None] * bb_re[None] - wi[..., None] * bb_im[None])
    win_im = (wr[..., None] * bb_im[None] + wi[..., None] * bb_re[None])

    def inj_op(m):
        m = jnp.einsum('sjgpk,gc->jsgkcp', m.reshape(L, nblk, gpb, p, hch), eye)
        return m.reshape(nblk, L * LANES, gpb * p)

    win = jnp.concatenate([inj_op(win_re), inj_op(win_im)], axis=2)

    def read_op(m):
        m = jnp.einsum('tjghp,gc->jgptch', m.reshape(L, nblk, gpb, hch, p), eye)
        return m.reshape(nblk, gpb * p, L * LANES)

    wout = jnp.concatenate([read_op(cp_re[1:]), read_op(cp_im[1:])], axis=1)
    al = jnp.concatenate([pw_re[L].reshape(nblk, gpb * p), pw_im[L].reshape(nblk, gpb * p)], axis=1)
    al = jnp.broadcast_to(al[:, None, :], (nblk, 8, 2 * gpb * p))
    return grev.astype(BF16), win.astype(BF16), wout.astype(BF16), al


def _mixers(h1, mix_norm, w_in, cmp_k, cmp_v, rel_bias, ssm, ssm_d, glu_w, glu_b, w_out, b, t):
    n, d = h1.shape
    d_ssm = glu_w.shape[0]
    o_g, o_u = D_ATTN + 6 * D_KV, D_ATTN + 6 * D_KV + 3 * N_HEADS
    gcols = 3 * HEADS_PER_KV
    wg = [jnp.pad(w_in[:, o_g + g * gcols:o_g + (g + 1) * gcols], ((0, 0), (0, LANES - gcols))) for g in range(N_KV)]
    w_perm = jnp.concatenate([w_in[:, :o_g], w_in[:, o_u:]] + wg, axis=1).astype(BF16)
    q, kvs, kcv, u, gates = _inproj(h1.reshape(b, t, d), mix_norm, w_perm, d_ssm)

    nblk = t // CMP_STRIDE
    assert nblk == LANES, "compressed-block axis is laid out on one 128-lane tile"
    pe = jnp.stack([cmp_k[0], cmp_v[0]], 0).astype(F32)
    w1 = jnp.stack([cmp_k[1], cmp_v[1]], 0).astype(BF16).reshape(2, CMP_LEN, HEAD_DIM, -1)
    b1 = jnp.stack([cmp_k[2].reshape(1, -1), cmp_v[2].reshape(1, -1)], 0).astype(F32)
    w2 = jnp.stack([cmp_k[3], cmp_v[3]], 0).astype(BF16)
    cb = _compress(kcv, pe, w1, b1, w2, nblk)

    biasc, dtiles, wb = _bias_tables(rel_bias, t)
    ov, e3 = _sel_tables(t)
    a = _nsa(q, kvs, cb, gates, biasc, dtiles, wb, ov, e3)

    grev, win, wout, al = _s5_tables(*ssm)
    y = _s5(u, grev, win, wout, al, ssm_d)

    wo = w_out.astype(BF16)
    return _glu_out(h1, a.reshape(n, D_ATTN), y.reshape(n, d_ssm), glu_w.astype(BF16), glu_b, wo[:D_ATTN], wo[D_ATTN:])


def kernel(x, ffn1_norm, ffn1_w1, ffn1_w3, ffn1_w2, mix_norm, w_in, cmp_pe_k, cmp_w1_k, cmp_b1_k, cmp_w2_k,
           cmp_pe_v, cmp_w1_v, cmp_b1_v, cmp_w2_v, rel_bias, ssm_lam_re, ssm_lam_im, ssm_log_step, ssm_b_re,
           ssm_b_im, ssm_c_re, ssm_c_im, ssm_d, glu_w, glu_b, w_out, ffn2_norm, ffn2_w1, ffn2_w3, ffn2_w2,
           final_norm):
    b, t, d = x.shape
    depth = ffn1_w1.shape[0]
    h = x.reshape(b * t, d)
    for l in range(depth):
        last = l == depth - 1
        h = _ffn(h, ffn1_norm[l], ffn1_w1[l].astype(BF16), ffn1_w3[l].astype(BF16), ffn1_w2[l].astype(BF16))
        ssm = (ssm_lam_re[l], ssm_lam_im[l], ssm_log_step[l], ssm_b_re[l], ssm_b_im[l], ssm_c_re[l], ssm_c_im[l])
        h = _mixers(h, mix_norm[l], w_in[l],
                    (cmp_pe_k[l], cmp_w1_k[l], cmp_b1_k[l], cmp_w2_k[l]),
                    (cmp_pe_v[l], cmp_w1_v[l], cmp_b1_v[l], cmp_w2_v[l]),
                    rel_bias, ssm, ssm_d[l], glu_w[l], glu_b[l], w_out[l], b, t)
        h = _ffn(h, ffn2_norm[l], ffn2_w1[l].astype(BF16), ffn2_w3[l].astype(BF16), ffn2_w2[l].astype(BF16),
                 final_gain=final_norm if last else None)
    return h.reshape(b, t, d)
```

```python
import functools
import math

import jax
import jax.numpy as jnp
from jax import lax
from jax.experimental import pallas as pl
from jax.experimental.pallas import tpu as pltpu

F32 = jnp.float32
BF16 = jnp.bfloat16

HEAD_DIM = 128
N_KV = 2
HEADS_PER_KV = 4
N_HEADS = N_KV * HEADS_PER_KV
D_ATTN = N_HEADS * HEAD_DIM
D_KV = N_KV * HEAD_DIM
CMP_LEN = 32
CMP_STRIDE = 16
SEL_BLOCK = 64
N_SELECT = 16
WINDOW = 512
N_BUCKETS = 32
MAX_DISTANCE = 128
SSM_GROUP = 16
SSM_STATE = 64
EPS = 1e-6
NEG = -1e30

LANES = 128
VMEM_LIMIT_BYTES = 56 * 1024 * 1024
TQ = 128
ROWS4 = HEADS_PER_KV * TQ
SEL_CHUNK = 512
SSM_CHUNK = 16


def _cparams(sem):
    return pltpu.CompilerParams(dimension_semantics=sem, vmem_limit_bytes=VMEM_LIMIT_BYTES)


def _ffn_body(x_ref, g_ref, w1_ref, w3_ref, w2_ref, *rest, final):
    if final:
        fg_ref, o_ref, xn_ref, acc_ref = rest
    else:
        o_ref, xn_ref, acc_ref = rest
    j = pl.program_id(1)
    nj = pl.num_programs(1)

    @pl.when(j == 0)
    def _():
        x = x_ref[...]
        ms = jnp.mean(x * x, axis=-1, keepdims=True)
        xn_ref[...] = (x * lax.rsqrt(ms + EPS) * g_ref[...]).astype(BF16)
        acc_ref[...] = jnp.zeros(acc_ref.shape, F32)

    xn = xn_ref[...]
    a = jnp.dot(xn, w1_ref[...], preferred_element_type=F32)
    b = jnp.dot(xn, w3_ref[...], preferred_element_type=F32)
    gated = (a * jax.nn.sigmoid(a)) * b
    acc_ref[...] += jnp.dot(gated.astype(BF16), w2_ref[...], preferred_element_type=F32)

    @pl.when(j == nj - 1)
    def _():
        h = x_ref[...] + 0.5 * acc_ref[...]
        if final:
            ms = jnp.mean(h * h, axis=-1, keepdims=True)
            h = h * lax.rsqrt(ms + EPS) * fg_ref[...]
        o_ref[...] = h


def _ffn(x, gain, w1, w3, w2, final_gain=None, *, tm=512, tf=512):
    n, d = x.shape
    dff = w1.shape[1]
    final = final_gain is not None
    in_specs = [
        pl.BlockSpec((tm, d), lambda i, j: (i, 0)),
        pl.BlockSpec((1, d), lambda i, j: (0, 0)),
        pl.BlockSpec((d, tf), lambda i, j: (0, j)),
        pl.BlockSpec((d, tf), lambda i, j: (0, j)),
        pl.BlockSpec((tf, d), lambda i, j: (j, 0)),
    ]
    args = [x, gain.reshape(1, d), w1, w3, w2]
    if final:
        in_specs.append(pl.BlockSpec((1, d), lambda i, j: (0, 0)))
        args.append(final_gain.reshape(1, d))
    return pl.pallas_call(
        functools.partial(_ffn_body, final=final),
        grid=(n // tm, dff // tf),
        in_specs=in_specs,
        out_specs=pl.BlockSpec((tm, d), lambda i, j: (i, 0)),
        out_shape=jax.ShapeDtypeStruct((n, d), F32),
        scratch_shapes=[pltpu.VMEM((tm, d), BF16), pltpu.VMEM((tm, d), F32)],
        compiler_params=_cparams(("parallel", "arbitrary")),
        name="ffn_final" if final else "ffn",
    )(*args)


def _inproj_body(x_ref, g_ref, w_ref, q_ref, kvs_ref, kcv_ref, u_ref, gate_ref):
    i = pl.program_id(1)

    @pl.when(i == 0)
    def _():
        kvs_ref[...] = jnp.zeros(kvs_ref.shape, kvs_ref.dtype)
        kcv_ref[...] = jnp.zeros(kcv_ref.shape, kcv_ref.dtype)

    @pl.when(i > 0)
    def _():
        x = x_ref[...]
        ms = jnp.mean(x * x, axis=-1, keepdims=True)
        xn = (x * lax.rsqrt(ms + EPS) * g_ref[...]).astype(BF16)
        c0 = 0
        c1 = q_ref.shape[1]
        q = jnp.dot(xn, w_ref[:, c0:c1], preferred_element_type=F32)
        q_ref[...] = (q * (HEAD_DIM ** -0.5)).astype(BF16)
        c0, c1 = c1, c1 + kcv_ref.shape[1]
        kcv_ref[...] = jnp.dot(xn, w_ref[:, c0:c1], preferred_element_type=F32)
        c0, c1 = c1, c1 + kvs_ref.shape[1]
        kvs_ref[...] = jnp.dot(xn, w_ref[:, c0:c1], preferred_element_type=F32).astype(BF16)
        c0, c1 = c1, c1 + u_ref.shape[1]
        u_ref[...] = jnp.dot(xn, w_ref[:, c0:c1], preferred_element_type=F32)
        gate_ref[...] = jax.nn.sigmoid(jnp.dot(xn, w_ref[:, c1:], preferred_element_type=F32))


def _inproj(h, gain, w_perm, d_ssm):
    b, t, d = h.shape
    tm = WINDOW
    nt = t // tm
    nq, ncv, nvs, ng = D_ATTN, 2 * D_KV, 4 * D_KV, N_KV * LANES
    ncol = w_perm.shape[1]
    assert ncol == nq + ncv + nvs + d_ssm + ng and t % tm == 0
    data = lambda bi, i: (bi, jnp.maximum(i - 1, 0), 0)
    return pl.pallas_call(
        _inproj_body,
        grid=(b, nt + 1),
        in_specs=[
            pl.BlockSpec((None, tm, d), data),
            pl.BlockSpec((1, d), lambda bi, i: (0, 0)),
            pl.BlockSpec((d, ncol), lambda bi, i: (0, 0)),
        ],
        out_specs=[
            pl.BlockSpec((None, tm, nq), data),
            pl.BlockSpec((None, tm, nvs), lambda bi, i: (bi, i, 0)),
            pl.BlockSpec((None, tm, ncv), lambda bi, i: (bi, jnp.where(i == 0, nt, i - 1), 0)),
            pl.BlockSpec((None, tm, d_ssm), data),
            pl.BlockSpec((None, tm, ng), data),
        ],
        out_shape=[
            jax.ShapeDtypeStruct((b, t, nq), BF16),
            jax.ShapeDtypeStruct((b, t + tm, nvs), BF16),
            jax.ShapeDtypeStruct((b, t + tm, ncv), F32),
            jax.ShapeDtypeStruct((b, t, d_ssm), F32),
            jax.ShapeDtypeStruct((b, t, ng), F32),
        ],
        compiler_params=_cparams(("parallel", "arbitrary")),
        name="inproj",
    )(h, gain.reshape(1, d), w_perm)


def _compress_body(x_ref, pe_ref, w1_ref, b1_ref, w2_ref, o_ref):
    nb, nblk = o_ref.shape[0], o_ref.shape[1]
    acc = jnp.broadcast_to(b1_ref[...], (nb * nblk, b1_ref.shape[1])).astype(F32)
    for l in range(CMP_LEN):
        xl = x_ref[:, pl.ds(l, nblk, stride=CMP_STRIDE), :] + pe_ref[l:l + 1, :]
        acc = acc + jnp.dot(xl.reshape(nb * nblk, xl.shape[2]).astype(BF16), w1_ref[l], preferred_element_type=F32)
    hid = jax.nn.gelu(acc, approximate=True)
    out = jnp.dot(hid.astype(BF16), w2_ref[...], preferred_element_type=F32)
    o_ref[...] = out.reshape(o_ref.shape).astype(BF16)


def _compress(kcv, pe, w1, b1, w2, nblk, *, nb=4):
    b, tpad, _ = kcv.shape
    nb = min(nb, b)
    hid, dh = w1.shape[3], w2.shape[2]
    assert (nblk - 1) * CMP_STRIDE + CMP_LEN <= tpad
    return pl.pallas_call(
        _compress_body,
        grid=(2 * N_KV, b // nb),
        in_specs=[
            pl.BlockSpec((nb, tpad, dh), lambda s, i: (i, 0, s)),
            pl.BlockSpec((None, CMP_LEN, dh), lambda s, i: (s // N_KV, 0, 0)),
            pl.BlockSpec((None, CMP_LEN, dh, hid), lambda s, i: (s // N_KV, 0, 0, 0)),
            pl.BlockSpec((None, 1, hid), lambda s, i: (s // N_KV, 0, 0)),
            pl.BlockSpec((None, hid, dh), lambda s, i: (s // N_KV, 0, 0)),
        ],
        out_specs=pl.BlockSpec((None, None, nb, nblk, dh), lambda s, i: (s // N_KV, s % N_KV, i, 0, 0)),
        out_shape=jax.ShapeDtypeStruct((2, N_KV, b, nblk, dh), BF16),
        compiler_params=_cparams(("parallel", "parallel")),
        name="compress",
    )(kcv, pe, w1, b1, w2)


def _nt_dot(a, b):
    return lax.dot_general(a, b, (((1,), (1,)), ((), ())), preferred_element_type=F32)


def _nsa_body(q_ref, ks_ref, vs_ref, kw_ref, vw_ref, kcb_ref, vcb_ref, gate_ref, biasc_ref, dt_ref, wb_ref,
              ovt_ref, et_ref, o_ref, kse_ref, s_ref, m_ref, l_ref, acc_ref):
    i = pl.program_id(2)
    t0 = i * TQ
    t = et_ref.shape[0]
    ns = t // SEL_BLOCK
    nsel = min(N_SELECT, ns)

    @pl.when(i == 0)
    def _():
        kse_ref[:, 0:HEAD_DIM] = ks_ref[WINDOW:WINDOW + t, :]
        kse_ref[:, HEAD_DIM:] = et_ref[...]

    q = q_ref[...]
    q4 = jnp.concatenate([q[:, h * HEAD_DIM:(h + 1) * HEAD_DIM] for h in range(HEADS_PER_KV)], axis=0)

    sc = _nt_dot(q4, kcb_ref[...]) + biasc_ref[...]
    row_t = t0 + (lax.broadcasted_iota(jnp.int32, (ROWS4, LANES), 0) & (TQ - 1))
    col_c = lax.broadcasted_iota(jnp.int32, (ROWS4, LANES), 1)
    valid_c = (row_t - col_c * CMP_STRIDE - (CMP_LEN - 1) >= 0) & (col_c < LANES - 1)
    mc = jnp.max(sc, axis=-1, keepdims=True)
    pc = jnp.where(valid_c, jnp.exp(sc - mc), 0.0)
    pc = pc / jnp.maximum(jnp.sum(pc, axis=-1, keepdims=True), 1e-30)
    pcb = pc.astype(BF16)
    o_cmp = jnp.dot(pcb, vcb_ref[...], preferred_element_type=F32)
    pimp = _nt_dot(ovt_ref[...], pcb)
    imp = pimp[0:ns, 0:TQ]
    for h in range(1, HEADS_PER_KV):
        imp = imp + pimp[0:ns, h * TQ:(h + 1) * TQ]

    jrow = lax.broadcasted_iota(jnp.int32, (ns, TQ), 0)
    tpos = t0 + lax.broadcasted_iota(jnp.int32, (ns, TQ), 1)
    cur = lax.shift_right_logical(tpos, int(math.log2(SEL_BLOCK)))
    forced = (jrow == 0) | (jrow == cur) | (jrow == cur - 1)
    impm = jnp.where(forced, 1e6, jnp.where(jrow * SEL_BLOCK <= tpos, imp, -1e9))
    nslab = ns // 8
    slabs = [impm[8 * v:8 * v + 8] for v in range(nslab)]
    cnts = [jnp.zeros((8, TQ), F32) for _ in range(nslab)]
    sub = lax.broadcasted_iota(jnp.int32, (8, TQ), 0)
    for jp in range(ns):
        v0, r0 = divmod(jp, 8)
        row = jnp.broadcast_to(slabs[v0][r0:r0 + 1, :], (8, TQ))
        for v in range(nslab):
            if v > v0:
                beats = row >= slabs[v]
            elif v < v0:
                beats = row > slabs[v]
            else:
                beats = (row > slabs[v]) | ((row == slabs[v]) & (sub > r0))
            cnts[v] = cnts[v] + jnp.where(beats, 1.0, 0.0)
    negt = [jnp.where(c < nsel, 0.0, NEG) for c in cnts]
    negt = jnp.concatenate(negt + [jnp.zeros((LANES - ns, TQ), F32)], axis=0)
    neg = negt.T.astype(BF16)
    q4s = jnp.concatenate([q4, jnp.concatenate([neg] * HEADS_PER_KV, axis=0)], axis=1)

    span = wb_ref.shape[1]
    w0 = pl.multiple_of(t0, LANES)
    kpos = t0 - WINDOW + lax.broadcasted_iota(jnp.int32, (1, span), 1)
    sw = _nt_dot(q4, kw_ref[pl.ds(w0, span), :]) + wb_ref[...] + jnp.where(kpos >= 0, 0.0, NEG)
    mw = jnp.max(sw, axis=-1, keepdims=True)
    pw = jnp.exp(sw - mw)
    lw = jnp.sum(pw, axis=-1, keepdims=True)
    o_win = jnp.dot(pw.astype(BF16), vw_ref[pl.ds(w0, span), :], preferred_element_type=F32)
    o_win = o_win / jnp.maximum(lw, 1e-30)

    nsub = SEL_CHUNK // LANES
    nchunks = i // nsub + 1
    m_ref[...] = jnp.full((ROWS4, LANES), -3e38, F32)

    def score_chunk(ck, _):
        kb = pl.multiple_of(ck * SEL_CHUNK, SEL_CHUNK)
        s = _nt_dot(q4s, kse_ref[pl.ds(kb, SEL_CHUNK), :])
        mloc = None
        for j in range(nsub):
            sj = s[:, j * LANES:(j + 1) * LANES] + dt_ref[jnp.clip(ck * nsub + j - i + 2, 0, 3)]
            s_ref[ck, :, j * LANES:(j + 1) * LANES] = sj
            mloc = sj if mloc is None else jnp.maximum(mloc, sj)
        m_ref[...] = jnp.maximum(m_ref[...], mloc)
        return 0

    lax.fori_loop(0, nchunks, score_chunk, 0)
    m_ref[...] = jnp.broadcast_to(jnp.max(m_ref[...], axis=-1, keepdims=True), (ROWS4, LANES))
    l_ref[...] = jnp.zeros((ROWS4, LANES), F32)
    acc_ref[...] = jnp.zeros((ROWS4, HEAD_DIM), F32)

    def pv_chunk(ck, _):
        kb = pl.multiple_of(WINDOW + ck * SEL_CHUNK, SEL_CHUNK)
        m = m_ref[...]
        ps = [jnp.exp(s_ref[ck, :, j * LANES:(j + 1) * LANES] - m) for j in range(nsub)]
        lsum = ps[0]
        for pj in ps[1:]:
            lsum = lsum + pj
        l_ref[...] += lsum
        p = jnp.concatenate([pj.astype(BF16) for pj in ps], axis=1)
        acc_ref[...] += jnp.dot(p, vs_ref[pl.ds(kb, SEL_CHUNK), :], preferred_element_type=F32)
        return 0

    lax.fori_loop(0, nchunks, pv_chunk, 0)
    o_sel = acc_ref[...] / jnp.maximum(jnp.sum(l_ref[...], axis=-1, keepdims=True), 1e-30)

    gt = gate_ref[...]
    for h in range(HEADS_PER_KV):
        r = slice(h * TQ, (h + 1) * TQ)
        o_h = (gt[:, 3 * h:3 * h + 1] * o_cmp[r] + gt[:, 3 * h + 1:3 * h + 2] * o_sel[r]
               + gt[:, 3 * h + 2:3 * h + 3] * o_win[r])
        o_ref[:, h * HEAD_DIM:(h + 1) * HEAD_DIM] = o_h.astype(BF16)


def _nsa(q, kvs, cb, gates, biasc, dtiles, wb, ovt, et):
    b, t, _ = q.shape
    nt = t // TQ
    tp = kvs.shape[1]
    span = wb.shape[2]
    gw = HEADS_PER_KV * HEAD_DIM
    assert tp == t + WINDOW and span == TQ + WINDOW and t % SEL_CHUNK == 0 and (t // SEL_BLOCK) % 8 == 0

    def stream(j):
        return pl.BlockSpec((None, tp, HEAD_DIM), lambda bi, g, i: (bi, 0, j * N_KV + g))

    return pl.pallas_call(
        _nsa_body,
        grid=(b, N_KV, nt),
        in_specs=[
            pl.BlockSpec((None, TQ, gw), lambda bi, g, i: (bi, i, g)),
            stream(0), stream(1), stream(2), stream(3),
            pl.BlockSpec((None, None, None, LANES, HEAD_DIM), lambda bi, g, i: (0, g, bi, 0, 0)),
            pl.BlockSpec((None, None, None, LANES, HEAD_DIM), lambda bi, g, i: (1, g, bi, 0, 0)),
            pl.BlockSpec((None, TQ, LANES), lambda bi, g, i: (bi, i, g)),
            pl.BlockSpec((None, None, ROWS4, LANES), lambda bi, g, i: (g, i, 0, 0)),
            pl.BlockSpec((None, 4, ROWS4, LANES), lambda bi, g, i: (g, 0, 0, 0)),
            pl.BlockSpec((None, ROWS4, span), lambda bi, g, i: (g, 0, 0)),
            pl.BlockSpec((LANES, LANES), lambda bi, g, i: (0, 0)),
            pl.BlockSpec((t, LANES), lambda bi, g, i: (0, 0)),
        ],
        out_specs=pl.BlockSpec((None, TQ, gw), lambda bi, g, i: (bi, i, g)),
        out_shape=jax.ShapeDtypeStruct((b, t, N_KV * gw), BF16),
        scratch_shapes=[
            pltpu.VMEM((t, 2 * HEAD_DIM), BF16),
            pltpu.VMEM((t // SEL_CHUNK, ROWS4, SEL_CHUNK), F32),
            pltpu.VMEM((ROWS4, LANES), F32),
            pltpu.VMEM((ROWS4, LANES), F32),
            pltpu.VMEM((ROWS4, HEAD_DIM), F32),
        ],
        compiler_params=_cparams(("parallel", "parallel", "arbitrary")),
        name="nsa",
    )(q, kvs, kvs, kvs, kvs, cb, cb, gates, biasc, dtiles, wb, ovt, et)


def _s5_body(u_ref, grev_ref, win_ref, wout_ref, al_ref, d_ref, y_ref, ucat_ref, inj_ref, xp_ref):
    nb, t, lanes = u_ref.shape
    L = SSM_CHUNK
    nchunk = t // L
    rows = nb * nchunk
    sdim = al_ref.shape[1] // 2

    def u_at(s):
        return u_ref[:, pl.ds(s, nchunk, stride=L), :].reshape(rows, lanes)

    for s in range(L):
        ucat_ref[:, s * lanes:(s + 1) * lanes] = u_at(s).astype(BF16)
    inj = jnp.dot(ucat_ref[...], win_ref[...], preferred_element_type=F32)
    npl = sdim // lanes
    for k in range(2 * npl):
        inj_ref[k] = inj[:, k * lanes:(k + 1) * lanes]
    ar = [jnp.broadcast_to(al_ref[0:1, k * lanes:(k + 1) * lanes], (nb, lanes)) for k in range(npl)]
    ai = [jnp.broadcast_to(al_ref[0:1, sdim + k * lanes:sdim + (k + 1) * lanes], (nb, lanes)) for k in range(npl)]

    def step(c, carry):
        xr, xi = carry
        rsel = pl.ds(c, nb, stride=nchunk)
        nr, ni = [], []
        for k in range(npl):
            xp_ref[k, rsel, :] = xr[k]
            xp_ref[npl + k, rsel, :] = xi[k]
            nr.append(ar[k] * xr[k] - ai[k] * xi[k] + inj_ref[k, rsel, :])
            ni.append(ar[k] * xi[k] + ai[k] * xr[k] + inj_ref[npl + k, rsel, :])
        return tuple(nr), tuple(ni)

    z = tuple(jnp.zeros((nb, lanes), F32) for _ in range(npl))
    lax.fori_loop(0, nchunk, step, (z, z))
    xp = jnp.concatenate([xp_ref[k] for k in range(2 * npl)], axis=1).astype(BF16)
    d = d_ref[...]
    for tp in range(0, L, 2):
        res = jnp.dot(ucat_ref[:, 0:(tp + 2) * lanes], grev_ref[(L - 1 - tp) * lanes:(L + 1) * lanes, :],
                      preferred_element_type=F32)
        res = res + jnp.dot(xp, wout_ref[:, tp * lanes:(tp + 2) * lanes], preferred_element_type=F32)
        for k in range(2):
            y = res[:, k * lanes:(k + 1) * lanes] + d * u_at(tp + k)
            y_ref[:, pl.ds(tp + k, nchunk, stride=L), :] = y.reshape(nb, nchunk, lanes)


def _s5(u, grev, win, wout, al, d, *, nb=4):
    b, t, dch = u.shape
    nb = min(nb, b)
    nblk = dch // LANES
    rows = nb * (t // SSM_CHUNK)
    kcat, sdim2 = win.shape[1], win.shape[2]
    return pl.pallas_call(
        _s5_body,
        grid=(nblk, b // nb),
        in_specs=[
            pl.BlockSpec((nb, t, LANES), lambda j, i: (i, 0, j)),
            pl.BlockSpec((None,) + grev.shape[1:], lambda j, i: (j, 0, 0)),
            pl.BlockSpec((None, kcat, sdim2), lambda j, i: (j, 0, 0)),
            pl.BlockSpec((None, sdim2, kcat), lambda j, i: (j, 0, 0)),
            pl.BlockSpec((None, 8, sdim2), lambda j, i: (j, 0, 0)),
            pl.BlockSpec((1, LANES), lambda j, i: (0, j)),
        ],
        out_specs=pl.BlockSpec((nb, t, LANES), lambda j, i: (i, 0, j)),
        out_shape=jax.ShapeDtypeStruct((b, t, dch), F32),
        scratch_shapes=[pltpu.VMEM((rows, kcat), BF16), pltpu.VMEM((sdim2 // LANES, rows, LANES), F32),
                        pltpu.VMEM((sdim2 // LANES, rows, LANES), F32)],
        compiler_params=_cparams(("parallel", "arbitrary")),
        name="s5",
    )(u, grev, win, wout, al, d.reshape(1, dch))


def _glu_out_body(h_ref, a_ref, y_ref, wg_ref, bg_ref, woa_ref, wos_ref, o_ref):
    hg = jax.nn.gelu(y_ref[...], approximate=True)
    z = jnp.dot(hg.astype(BF16), wg_ref[...], preferred_element_type=F32) + bg_ref[...]
    s = hg * jax.nn.sigmoid(z)
    mix = jnp.dot(a_ref[...], woa_ref[...], preferred_element_type=F32)
    mix = mix + jnp.dot(s.astype(BF16), wos_ref[...], preferred_element_type=F32)
    o_ref[...] = h_ref[...] + mix


def _glu_out(h, a, y, wg, bg, woa, wos, *, tm=512):
    n, dm = h.shape
    da, ds = a.shape[1], y.shape[1]
    row = lambda w: pl.BlockSpec((tm, w), lambda i: (i, 0))
    full = lambda r, c: pl.BlockSpec((r, c), lambda i: (0, 0))
    return pl.pallas_call(
        _glu_out_body,
        grid=(n // tm,),
        in_specs=[row(dm), row(da), row(ds), full(ds, ds), full(1, ds), full(da, dm), full(ds, dm)],
        out_specs=row(dm),
        out_shape=jax.ShapeDtypeStruct((n, dm), F32),
        compiler_params=_cparams(("parallel",)),
        name="glu_out",
    )(h, a, y, wg, bg.reshape(1, ds), woa, wos)


def _t5_bucket(dist):
    n = jnp.maximum(dist, 0)
    max_exact = N_BUCKETS // 2
    nf = jnp.maximum(n, 1).astype(F32)
    large = max_exact + (jnp.log(nf / max_exact) / math.log(MAX_DISTANCE / max_exact)
                         * (N_BUCKETS - max_exact)).astype(jnp.int32)
    large = jnp.minimum(large, N_BUCKETS - 1)
    return jnp.where(n < max_exact, n, large)


def _bias_tables(rel_bias, t):
    nt = t // TQ
    span = TQ + WINDOW
    table = rel_bias.astype(F32).reshape(N_BUCKETS, N_KV, HEADS_PER_KV, 1, 1)
    buckets = _t5_bucket(jnp.arange(2 * MAX_DISTANCE))
    first = [jnp.sum((buckets < k).astype(jnp.int32)) for k in range(N_BUCKETS)]

    def lookup(dist, valid):
        bias = jnp.broadcast_to(table[0], (N_KV, HEADS_PER_KV) + dist.shape)
        for k in range(1, N_BUCKETS):
            bias = jnp.where((dist >= first[k])[None, None], table[k], bias)
        return jnp.where(valid[None, None], bias, NEG)

    tt = jnp.arange(t)[:, None]
    cc = jnp.arange(LANES)[None, :]
    dist_c = tt - (cc * CMP_STRIDE + CMP_LEN - 1)
    nc = (t - CMP_LEN) // CMP_STRIDE + 1
    bc = lookup(dist_c, (dist_c >= 0) & (cc < nc))
    bc = bc.reshape(N_KV, HEADS_PER_KV, nt, TQ, LANES).transpose(0, 2, 1, 3, 4).reshape(N_KV, nt, ROWS4, LANES)
    a = jnp.arange(TQ)[:, None]
    k = jnp.arange(TQ)[None, :]
    always = jnp.ones((TQ, TQ), bool)
    far = lookup(jnp.full((TQ, TQ), 2 * TQ), always)
    sub = lookup(TQ + a - k, always)
    diag = lookup(a - k, a - k >= 0)
    beyond = jnp.full_like(far, NEG)
    dt = jnp.stack([far, sub, diag, beyond], axis=1).reshape(N_KV, 4, ROWS4, TQ)
    m = jnp.arange(span)[None, :]
    dist_w = a + WINDOW - m
    wb = lookup(dist_w, (dist_w >= 0) & (dist_w < WINDOW)).reshape(N_KV, ROWS4, span)
    return bc, dt, wb


def _sel_tables(t):
    ns = t // SEL_BLOCK
    nc_pad = LANES
    c_start = jnp.arange(nc_pad) * CMP_STRIDE
    j_start = jnp.arange(LANES) * SEL_BLOCK
    ov = jnp.clip(jnp.minimum(c_start[:, None] + CMP_LEN, j_start[None, :] + SEL_BLOCK)
                  - jnp.maximum(c_start[:, None], j_start[None, :]), 0, None).astype(F32) / CMP_LEN
    ov = jnp.where(jnp.arange(LANES)[None, :] < ns, ov, 0.0)
    et = jnp.arange(t)[:, None] // SEL_BLOCK == jnp.arange(LANES)[None, :]
    return ov.T.astype(BF16), et.astype(BF16)


def _s5_tables(lam_re, lam_im, log_step, b_re, b_im, c_re, c_im):
    hi = lax.Precision.HIGHEST
    ng, p = lam_re.shape
    hch = b_re.shape[2]
    L = SSM_CHUNK
    step = jnp.exp(log_step.astype(F32))[:, None]
    lre, lim = lam_re.astype(F32), lam_im.astype(F32)
    mag = jnp.exp(lre * step)
    ab_re, ab_im = mag * jnp.cos(lim * step), mag * jnp.sin(lim * step)
    nr, ni = ab_re - 1.0, ab_im
    den = lre * lre + lim * lim
    f_re, f_im = (nr * lre + ni * lim) / den, (ni * lre - nr * lim) / den
    br, bim = b_re.astype(F32), b_im.astype(F32)
    bb_re = f_re[..., None] * br - f_im[..., None] * bim
    bb_im = f_re[..., None] * bim + f_im[..., None] * br
    cr, ci = c_re.astype(F32), c_im.astype(F32)
    pr, pi = [jnp.ones_like(ab_re)], [jnp.zeros_like(ab_re)]
    for _ in range(L):
        pr, pi = pr + [pr[-1] * ab_re - pi[-1] * ab_im], pi + [pr[-1] * ab_im + pi[-1] * ab_re]
    pw_re, pw_im = jnp.stack(pr, 0), jnp.stack(pi, 0)
    cp_re = cr[None] * pw_re[:, :, None, :] - ci[None] * pw_im[:, :, None, :]
    cp_im = -(cr[None] * pw_im[:, :, None, :] + ci[None] * pw_re[:, :, None, :])
    kern = (jnp.einsum('tghp,gpk->gthk', cp_re[:L], bb_re, precision=hi)
            + jnp.einsum('tghp,gpk->gthk', cp_im[:L], bb_im, precision=hi))
    gpb = LANES // hch
    nblk = ng // gpb
    eye = jnp.eye(gpb, dtype=F32)
    kblk = jnp.einsum('jgthk,gc->jtgkch', kern.reshape(nblk, gpb, L, hch, hch), eye).reshape(nblk, L, LANES, LANES)
    zero = jnp.zeros((nblk, 1, LANES, LANES), F32)
    kpad = jnp.concatenate([zero, kblk, zero], axis=1)
    gmat = jnp.concatenate([kpad[:, :L + 1], kpad[:, 1:]], axis=3)
    grev = gmat[:, ::-1].reshape(nblk, (L + 1) * LANES, 2 * LANES)
    wr = pw_re[L - 1 - jnp.arange(L)]
    wi = pw_im[L - 1 - jnp.arange(L)]
    win_re = (wr[..., None] * bb_re[None] - wi[..., None] * bb_im[None])
    win_im = (wr[..., None] * bb_im[None] + wi[..., None] * bb_re[None])

    def inj_op(m):
        m = jnp.einsum('sjgpk,gc->jsgkcp', m.reshape(L, nblk, gpb, p, hch), eye)
        return m.reshape(nblk, L * LANES, gpb * p)

    win = jnp.concatenate([inj_op(win_re), inj_op(win_im)], axis=2)

    def read_op(m):
        m = jnp.einsum('tjghp,gc->jgptch', m.reshape(L, nblk, gpb, hch, p), eye)
        return m.reshape(nblk, gpb * p, L * LANES)

    wout = jnp.concatenate([read_op(cp_re[1:]), read_op(cp_im[1:])], axis=1)
    al = jnp.concatenate([pw_re[L].reshape(nblk, gpb * p), pw_im[L].reshape(nblk, gpb * p)], axis=1)
    al = jnp.broadcast_to(al[:, None, :], (nblk, 8, 2 * gpb * p))
    return grev.astype(BF16), win.astype(BF16), wout.astype(BF16), al


def _mixers(h1, mix_norm, w_in, cmp_k, cmp_v, rel_bias, ssm, ssm_d, glu_w, glu_b, w_out, b, t):
    n, d = h1.shape
    d_ssm = glu_w.shape[0]
    o_g, o_u = D_ATTN + 6 * D_KV, D_ATTN + 6 * D_KV + 3 * N_HEADS
    gcols = 3 * HEADS_PER_KV
    wg = [jnp.pad(w_in[:, o_g + g * gcols:o_g + (g + 1) * gcols], ((0, 0), (0, LANES - gcols))) for g in range(N_KV)]
    w_perm = jnp.concatenate([w_in[:, :o_g], w_in[:, o_u:]] + wg, axis=1).astype(BF16)
    q, kvs, kcv, u, gates = _inproj(h1.reshape(b, t, d), mix_norm, w_perm, d_ssm)

    nblk = t // CMP_STRIDE
    assert nblk == LANES, "compressed-block axis is laid out on one 128-lane tile"
    pe = jnp.stack([cmp_k[0], cmp_v[0]], 0).astype(F32)
    w1 = jnp.stack([cmp_k[1], cmp_v[1]], 0).astype(BF16).reshape(2, CMP_LEN, HEAD_DIM, -1)
    b1 = jnp.stack([cmp_k[2].reshape(1, -1), cmp_v[2].reshape(1, -1)], 0).astype(F32)
    w2 = jnp.stack([cmp_k[3], cmp_v[3]], 0).astype(BF16)
    cb = _compress(kcv, pe, w1, b1, w2, nblk)

    biasc, dtiles, wb = _bias_tables(rel_bias, t)
    ovt, et = _sel_tables(t)
    a = _nsa(q, kvs, cb, gates, biasc, dtiles, wb, ovt, et)

    grev, win, wout, al = _s5_tables(*ssm)
    y = _s5(u, grev, win, wout, al, ssm_d)

    wo = w_out.astype(BF16)
    return _glu_out(h1, a.reshape(n, D_ATTN), y.reshape(n, d_ssm), glu_w.astype(BF16), glu_b, wo[:D_ATTN], wo[D_ATTN:])


def kernel(x, ffn1_norm, ffn1_w1, ffn1_w3, ffn1_w2, mix_norm, w_in, cmp_pe_k, cmp_w1_k, cmp_b1_k, cmp_w2_k,
           cmp_pe_v, cmp_w1_v, cmp_b1_v, cmp_w2_v, rel_bias, ssm_lam_re, ssm_lam_im, ssm_log_step, ssm_b_re,
           ssm_b_im, ssm_c_re, ssm_c_im, ssm_d, glu_w, glu_b, w_out, ffn2_norm, ffn2_w1, ffn2_w3, ffn2_w2,
           final_norm):
    b, t, d = x.shape
    depth = ffn1_w1.shape[0]
    h = x.reshape(b * t, d)
    for l in range(depth):
        last = l == depth - 1
        h = _ffn(h, ffn1_norm[l], ffn1_w1[l].astype(BF16), ffn1_w3[l].astype(BF16), ffn1_w2[l].astype(BF16))
        ssm = (ssm_lam_re[l], ssm_lam_im[l], ssm_log_step[l], ssm_b_re[l], ssm_b_im[l], ssm_c_re[l], ssm_c_im[l])
        h = _mixers(h, mix_norm[l], w_in[l],
                    (cmp_pe_k[l], cmp_w1_k[l], cmp_b1_k[l], cmp_w2_k[l]),
                    (cmp_pe_v[l], cmp_w1_v[l], cmp_b1_v[l], cmp_w2_v[l]),
                    rel_bias, ssm, ssm_d[l], glu_w[l], glu_b[l], w_out[l], b, t)
        h = _ffn(h, ffn2_norm[l], ffn2_w1[l].astype(BF16), ffn2_w3[l].astype(BF16), ffn2_w2[l].astype(BF16),
                 final_gain=final_norm if last else None)
    return h.reshape(b, t, d)
```

```python
import functools
import math

import jax
import jax.numpy as jnp
from jax import lax
from jax.experimental import pallas as pl
from jax.experimental.pallas import tpu as pltpu

F32 = jnp.float32
BF16 = jnp.bfloat16

HEAD_DIM = 128
N_KV = 2
HEADS_PER_KV = 4
N_HEADS = N_KV * HEADS_PER_KV
D_ATTN = N_HEADS * HEAD_DIM
D_KV = N_KV * HEAD_DIM
CMP_LEN = 32
CMP_STRIDE = 16
SEL_BLOCK = 64
N_SELECT = 16
WINDOW = 512
N_BUCKETS = 32
MAX_DISTANCE = 128
SSM_GROUP = 16
SSM_STATE = 64
EPS = 1e-6
NEG = -1e30

LANES = 128
VMEM_LIMIT_BYTES = 56 * 1024 * 1024
TQ = 128
ROWS4 = HEADS_PER_KV * TQ
SEL_CHUNK = 512
SSM_CHUNK = 16


def _cparams(sem):
    return pltpu.CompilerParams(dimension_semantics=sem, vmem_limit_bytes=VMEM_LIMIT_BYTES)


def _ffn_body(x_ref, g_ref, w1_ref, w3_ref, w2_ref, *rest, final):
    if final:
        fg_ref, o_ref, xn_ref, acc_ref = rest
    else:
        o_ref, xn_ref, acc_ref = rest
    j = pl.program_id(1)
    nj = pl.num_programs(1)

    @pl.when(j == 0)
    def _():
        x = x_ref[...]
        ms = jnp.mean(x * x, axis=-1, keepdims=True)
        xn_ref[...] = (x * lax.rsqrt(ms + EPS) * g_ref[...]).astype(BF16)
        acc_ref[...] = jnp.zeros(acc_ref.shape, F32)

    xn = xn_ref[...]
    a = jnp.dot(xn, w1_ref[...], preferred_element_type=F32)
    b = jnp.dot(xn, w3_ref[...], preferred_element_type=F32)
    gated = (a * jax.nn.sigmoid(a)) * b
    acc_ref[...] += jnp.dot(gated.astype(BF16), w2_ref[...], preferred_element_type=F32)

    @pl.when(j == nj - 1)
    def _():
        h = x_ref[...] + 0.5 * acc_ref[...]
        if final:
            ms = jnp.mean(h * h, axis=-1, keepdims=True)
            h = h * lax.rsqrt(ms + EPS) * fg_ref[...]
        o_ref[...] = h


def _ffn(x, gain, w1, w3, w2, final_gain=None, *, tm=512, tf=512):
    n, d = x.shape
    dff = w1.shape[1]
    final = final_gain is not None
    in_specs = [
        pl.BlockSpec((tm, d), lambda i, j: (i, 0)),
        pl.BlockSpec((1, d), lambda i, j: (0, 0)),
        pl.BlockSpec((d, tf), lambda i, j: (0, j)),
        pl.BlockSpec((d, tf), lambda i, j: (0, j)),
        pl.BlockSpec((tf, d), lambda i, j: (j, 0)),
    ]
    args = [x, gain.reshape(1, d), w1, w3, w2]
    if final:
        in_specs.append(pl.BlockSpec((1, d), lambda i, j: (0, 0)))
        args.append(final_gain.reshape(1, d))
    return pl.pallas_call(
        functools.partial(_ffn_body, final=final),
        grid=(n // tm, dff // tf),
        in_specs=in_specs,
        out_specs=pl.BlockSpec((tm, d), lambda i, j: (i, 0)),
        out_shape=jax.ShapeDtypeStruct((n, d), F32),
        scratch_shapes=[pltpu.VMEM((tm, d), BF16), pltpu.VMEM((tm, d), F32)],
        compiler_params=_cparams(("parallel", "arbitrary")),
        name="ffn_final" if final else "ffn",
    )(*args)


def _inproj_body(x_ref, g_ref, w_ref, q_ref, kvs_ref, kcv_ref, u_ref, gate_ref):
    i = pl.program_id(1)

    @pl.when(i == 0)
    def _():
        kvs_ref[...] = jnp.zeros(kvs_ref.shape, kvs_ref.dtype)
        kcv_ref[...] = jnp.zeros(kcv_ref.shape, kcv_ref.dtype)

    @pl.when(i > 0)
    def _():
        x = x_ref[...]
        ms = jnp.mean(x * x, axis=-1, keepdims=True)
        xn = (x * lax.rsqrt(ms + EPS) * g_ref[...]).astype(BF16)
        c0 = 0
        c1 = q_ref.shape[1]
        q = jnp.dot(xn, w_ref[:, c0:c1], preferred_element_type=F32)
        q_ref[...] = (q * (HEAD_DIM ** -0.5)).astype(BF16)
        c0, c1 = c1, c1 + kcv_ref.shape[1]
        kcv_ref[...] = jnp.dot(xn, w_ref[:, c0:c1], preferred_element_type=F32)
        c0, c1 = c1, c1 + kvs_ref.shape[1]
        kvs_ref[...] = jnp.dot(xn, w_ref[:, c0:c1], preferred_element_type=F32).astype(BF16)
        c0, c1 = c1, c1 + u_ref.shape[1]
        u_ref[...] = jnp.dot(xn, w_ref[:, c0:c1], preferred_element_type=F32)
        gate_ref[...] = jax.nn.sigmoid(jnp.dot(xn, w_ref[:, c1:], preferred_element_type=F32))


def _inproj(h, gain, w_perm, d_ssm):
    b, t, d = h.shape
    tm = WINDOW
    nt = t // tm
    nq, ncv, nvs, ng = D_ATTN, 2 * D_KV, 4 * D_KV, N_KV * LANES
    ncol = w_perm.shape[1]
    assert ncol == nq + ncv + nvs + d_ssm + ng and t % tm == 0
    data = lambda bi, i: (bi, jnp.maximum(i - 1, 0), 0)
    return pl.pallas_call(
        _inproj_body,
        grid=(b, nt + 1),
        in_specs=[
            pl.BlockSpec((None, tm, d), data),
            pl.BlockSpec((1, d), lambda bi, i: (0, 0)),
            pl.BlockSpec((d, ncol), lambda bi, i: (0, 0)),
        ],
        out_specs=[
            pl.BlockSpec((None, tm, nq), data),
            pl.BlockSpec((None, tm, nvs), lambda bi, i: (bi, i, 0)),
            pl.BlockSpec((None, tm, ncv), lambda bi, i: (bi, jnp.where(i == 0, nt, i - 1), 0)),
            pl.BlockSpec((None, tm, d_ssm), data),
            pl.BlockSpec((None, tm, ng), data),
        ],
        out_shape=[
            jax.ShapeDtypeStruct((b, t, nq), BF16),
            jax.ShapeDtypeStruct((b, t + tm, nvs), BF16),
            jax.ShapeDtypeStruct((b, t + tm, ncv), F32),
            jax.ShapeDtypeStruct((b, t, d_ssm), F32),
            jax.ShapeDtypeStruct((b, t, ng), F32),
        ],
        compiler_params=_cparams(("parallel", "arbitrary")),
        name="inproj",
    )(h, gain.reshape(1, d), w_perm)


def _compress_body(x_ref, pe_ref, w1_ref, b1_ref, w2_ref, o_ref):
    nb, nblk = o_ref.shape[0], o_ref.shape[1]
    acc = jnp.broadcast_to(b1_ref[...], (nb * nblk, b1_ref.shape[1])).astype(F32)
    for l in range(CMP_LEN):
        xl = x_ref[:, pl.ds(l, nblk, stride=CMP_STRIDE), :] + pe_ref[l:l + 1, :]
        acc = acc + jnp.dot(xl.reshape(nb * nblk, xl.shape[2]).astype(BF16), w1_ref[l], preferred_element_type=F32)
    hid = jax.nn.gelu(acc, approximate=True)
    out = jnp.dot(hid.astype(BF16), w2_ref[...], preferred_element_type=F32)
    o_ref[...] = out.reshape(o_ref.shape).astype(BF16)


def _compress(kcv, pe, w1, b1, w2, nblk, *, nb=4):
    b, tpad, _ = kcv.shape
    nb = min(nb, b)
    hid, dh = w1.shape[3], w2.shape[2]
    assert (nblk - 1) * CMP_STRIDE + CMP_LEN <= tpad
    return pl.pallas_call(
        _compress_body,
        grid=(2 * N_KV, b // nb),
        in_specs=[
            pl.BlockSpec((nb, tpad, dh), lambda s, i: (i, 0, s)),
            pl.BlockSpec((None, CMP_LEN, dh), lambda s, i: (s // N_KV, 0, 0)),
            pl.BlockSpec((None, CMP_LEN, dh, hid), lambda s, i: (s // N_KV, 0, 0, 0)),
            pl.BlockSpec((None, 1, hid), lambda s, i: (s // N_KV, 0, 0)),
            pl.BlockSpec((None, hid, dh), lambda s, i: (s // N_KV, 0, 0)),
        ],
        out_specs=pl.BlockSpec((None, None, nb, nblk, dh), lambda s, i: (s // N_KV, s % N_KV, i, 0, 0)),
        out_shape=jax.ShapeDtypeStruct((2, N_KV, b, nblk, dh), BF16),
        compiler_params=_cparams(("parallel", "parallel")),
        name="compress",
    )(kcv, pe, w1, b1, w2)


def _nt_dot(a, b):
    return lax.dot_general(a, b, (((1,), (1,)), ((), ())), preferred_element_type=F32)


def _nsa_body(q_ref, ks_ref, vs_ref, kw_ref, vw_ref, kcb_ref, vcb_ref, gate_ref, biasc_ref, dt_ref, wb_ref,
              ovt_ref, et_ref, o_ref, kse_ref, s_ref, m_ref, l_ref, acc_ref):
    i = pl.program_id(1)
    t0 = i * TQ
    t = et_ref.shape[0]
    ns = t // SEL_BLOCK
    nsel = min(N_SELECT, ns)
    groups = range(N_KV)
    gl = lambda g: slice(g * HEAD_DIM, (g + 1) * HEAD_DIM)

    @pl.when(i == 0)
    def _():
        for g in groups:
            kse_ref[g, :, 0:HEAD_DIM] = ks_ref[WINDOW:WINDOW + t, gl(g)]
            kse_ref[g, :, HEAD_DIM:] = et_ref[...]

    q = q_ref[...]
    w0 = pl.multiple_of(t0, LANES)
    span = wb_ref.shape[2]

    def front(g):
        q4 = jnp.concatenate([q[:, (g * HEADS_PER_KV + h) * HEAD_DIM:(g * HEADS_PER_KV + h + 1) * HEAD_DIM]
                              for h in range(HEADS_PER_KV)], axis=0)
        sc = _nt_dot(q4, kcb_ref[g]) + biasc_ref[g]
        row_t = t0 + (lax.broadcasted_iota(jnp.int32, (ROWS4, LANES), 0) & (TQ - 1))
        col_c = lax.broadcasted_iota(jnp.int32, (ROWS4, LANES), 1)
        valid_c = (row_t - col_c * CMP_STRIDE - (CMP_LEN - 1) >= 0) & (col_c < LANES - 1)
        mc = jnp.max(sc, axis=-1, keepdims=True)
        pc = jnp.where(valid_c, jnp.exp(sc - mc), 0.0)
        pc = pc / jnp.maximum(jnp.sum(pc, axis=-1, keepdims=True), 1e-30)
        pcb = pc.astype(BF16)
        o_cmp = jnp.dot(pcb, vcb_ref[g], preferred_element_type=F32)
        pimp = _nt_dot(ovt_ref[...], pcb)
        imp = pimp[0:ns, 0:TQ]
        for h in range(1, HEADS_PER_KV):
            imp = imp + pimp[0:ns, h * TQ:(h + 1) * TQ]

        jrow = lax.broadcasted_iota(jnp.int32, (ns, TQ), 0)
        tpos = t0 + lax.broadcasted_iota(jnp.int32, (ns, TQ), 1)
        cur = lax.shift_right_logical(tpos, int(math.log2(SEL_BLOCK)))
        forced = (jrow == 0) | (jrow == cur) | (jrow == cur - 1)
        impm = jnp.where(forced, 1e6, jnp.where(jrow * SEL_BLOCK <= tpos, imp, -1e9))
        nslab = ns // 8
        slabs = [impm[8 * v:8 * v + 8] for v in range(nslab)]
        cnts = [jnp.zeros((8, TQ), F32) for _ in range(nslab)]
        sub = lax.broadcasted_iota(jnp.int32, (8, TQ), 0)
        for jp in range(ns):
            v0, r0 = divmod(jp, 8)
            row = jnp.broadcast_to(slabs[v0][r0:r0 + 1, :], (8, TQ))
            for v in range(nslab):
                if v > v0:
                    beats = row >= slabs[v]
                elif v < v0:
                    beats = row > slabs[v]
                else:
                    beats = (row > slabs[v]) | ((row == slabs[v]) & (sub > r0))
                cnts[v] = cnts[v] + jnp.where(beats, 1.0, 0.0)
        negt = [jnp.where(c < nsel, 0.0, NEG) for c in cnts]
        negt = jnp.concatenate(negt + [jnp.zeros((LANES - ns, TQ), F32)], axis=0)
        neg = negt.T.astype(BF16)
        q4s = jnp.concatenate([q4, jnp.concatenate([neg] * HEADS_PER_KV, axis=0)], axis=1)

        kpos = t0 - WINDOW + lax.broadcasted_iota(jnp.int32, (1, span), 1)
        sw = _nt_dot(q4, kw_ref[pl.ds(w0, span), gl(g)]) + wb_ref[g] + jnp.where(kpos >= 0, 0.0, NEG)
        mw = jnp.max(sw, axis=-1, keepdims=True)
        pw = jnp.exp(sw - mw)
        lw = jnp.sum(pw, axis=-1, keepdims=True)
        o_win = jnp.dot(pw.astype(BF16), vw_ref[pl.ds(w0, span), gl(g)], preferred_element_type=F32)
        return q4s, o_cmp, o_win / jnp.maximum(lw, 1e-30)

    fronts = [front(g) for g in groups]

    nsub = SEL_CHUNK // LANES
    nchunks = i // nsub + 1
    m_ref[...] = jnp.full(m_ref.shape, -3e38, F32)

    def score_chunk(ck, _):
        kb = pl.multiple_of(ck * SEL_CHUNK, SEL_CHUNK)
        for g in groups:
            s = _nt_dot(fronts[g][0], kse_ref[g, pl.ds(kb, SEL_CHUNK), :])
            mloc = None
            for j in range(nsub):
                sj = s[:, j * LANES:(j + 1) * LANES] + dt_ref[g, jnp.clip(ck * nsub + j - i + 2, 0, 3)]
                s_ref[g, ck, :, j * LANES:(j + 1) * LANES] = sj
                mloc = sj if mloc is None else jnp.maximum(mloc, sj)
            m_ref[g] = jnp.maximum(m_ref[g], mloc)
        return 0

    lax.fori_loop(0, nchunks, score_chunk, 0)
    for g in groups:
        m_ref[g] = jnp.broadcast_to(jnp.max(m_ref[g], axis=-1, keepdims=True), (ROWS4, LANES))
    l_ref[...] = jnp.zeros(l_ref.shape, F32)
    acc_ref[...] = jnp.zeros(acc_ref.shape, F32)

    def pv_chunk(ck, _):
        kb = pl.multiple_of(WINDOW + ck * SEL_CHUNK, SEL_CHUNK)
        for g in groups:
            m = m_ref[g]
            ps = [jnp.exp(s_ref[g, ck, :, j * LANES:(j + 1) * LANES] - m) for j in range(nsub)]
            lsum = ps[0]
            for pj in ps[1:]:
                lsum = lsum + pj
            l_ref[g] += lsum
            p = jnp.concatenate([pj.astype(BF16) for pj in ps], axis=1)
            acc_ref[g] += jnp.dot(p, vs_ref[pl.ds(kb, SEL_CHUNK), gl(g)], preferred_element_type=F32)
        return 0

    lax.fori_loop(0, nchunks, pv_chunk, 0)

    for g in groups:
        _, o_cmp, o_win = fronts[g]
        o_sel = acc_ref[g] / jnp.maximum(jnp.sum(l_ref[g], axis=-1, keepdims=True), 1e-30)
        gt = gate_ref[:, g * LANES:(g + 1) * LANES]
        for h in range(HEADS_PER_KV):
            r = slice(h * TQ, (h + 1) * TQ)
            o_h = (gt[:, 3 * h:3 * h + 1] * o_cmp[r] + gt[:, 3 * h + 1:3 * h + 2] * o_sel[r]
                   + gt[:, 3 * h + 2:3 * h + 3] * o_win[r])
            c0 = (g * HEADS_PER_KV + h) * HEAD_DIM
            o_ref[:, c0:c0 + HEAD_DIM] = o_h.astype(BF16)


def _nsa(q, kvs, cb, gates, biasc, dtiles, wb, ovt, et):
    b, t, _ = q.shape
    nt = t // TQ
    tp = kvs.shape[1]
    span = wb.shape[2]
    gw = HEADS_PER_KV * HEAD_DIM
    assert tp == t + WINDOW and span == TQ + WINDOW and t % SEL_CHUNK == 0 and (t // SEL_BLOCK) % 8 == 0

    def stream(j):
        return pl.BlockSpec((None, tp, N_KV * HEAD_DIM), lambda bi, i: (bi, 0, j))

    return pl.pallas_call(
        _nsa_body,
        grid=(b, nt),
        in_specs=[
            pl.BlockSpec((None, TQ, N_KV * gw), lambda bi, i: (bi, i, 0)),
            stream(0), stream(1), stream(2), stream(3),
            pl.BlockSpec((None, N_KV, None, LANES, HEAD_DIM), lambda bi, i: (0, 0, bi, 0, 0)),
            pl.BlockSpec((None, N_KV, None, LANES, HEAD_DIM), lambda bi, i: (1, 0, bi, 0, 0)),
            pl.BlockSpec((None, TQ, N_KV * LANES), lambda bi, i: (bi, i, 0)),
            pl.BlockSpec((N_KV, None, ROWS4, LANES), lambda bi, i: (0, i, 0, 0)),
            pl.BlockSpec((N_KV, 4, ROWS4, LANES), lambda bi, i: (0, 0, 0, 0)),
            pl.BlockSpec((N_KV, ROWS4, span), lambda bi, i: (0, 0, 0)),
            pl.BlockSpec((LANES, LANES), lambda bi, i: (0, 0)),
            pl.BlockSpec((t, LANES), lambda bi, i: (0, 0)),
        ],
        out_specs=pl.BlockSpec((None, TQ, N_KV * gw), lambda bi, i: (bi, i, 0)),
        out_shape=jax.ShapeDtypeStruct((b, t, N_KV * gw), BF16),
        scratch_shapes=[
            pltpu.VMEM((N_KV, t, 2 * HEAD_DIM), BF16),
            pltpu.VMEM((N_KV, t // SEL_CHUNK, ROWS4, SEL_CHUNK), F32),
            pltpu.VMEM((N_KV, ROWS4, LANES), F32),
            pltpu.VMEM((N_KV, ROWS4, LANES), F32),
            pltpu.VMEM((N_KV, ROWS4, HEAD_DIM), F32),
        ],
        compiler_params=_cparams(("parallel", "arbitrary")),
        name="nsa",
    )(q, kvs, kvs, kvs, kvs, cb, cb, gates, biasc, dtiles, wb, ovt, et)


def _s5_body(u_ref, grev_ref, win_ref, wout_ref, al_ref, d_ref, y_ref, ucat_ref, inj_ref, xp_ref):
    nb, t, lanes = u_ref.shape
    L = SSM_CHUNK
    nchunk = t // L
    rows = nb * nchunk
    sdim = al_ref.shape[1] // 2

    def u_at(s):
        return u_ref[:, pl.ds(s, nchunk, stride=L), :].reshape(rows, lanes)

    for s in range(L):
        ucat_ref[:, s * lanes:(s + 1) * lanes] = u_at(s).astype(BF16)
    inj = jnp.dot(ucat_ref[...], win_ref[...], preferred_element_type=F32)
    npl = sdim // lanes
    for k in range(2 * npl):
        inj_ref[k] = inj[:, k * lanes:(k + 1) * lanes]
    ar = [jnp.broadcast_to(al_ref[0:1, k * lanes:(k + 1) * lanes], (nb, lanes)) for k in range(npl)]
    ai = [jnp.broadcast_to(al_ref[0:1, sdim + k * lanes:sdim + (k + 1) * lanes], (nb, lanes)) for k in range(npl)]

    def step(c, carry):
        xr, xi = carry
        rsel = pl.ds(c, nb, stride=nchunk)
        nr, ni = [], []
        for k in range(npl):
            xp_ref[k, rsel, :] = xr[k]
            xp_ref[npl + k, rsel, :] = xi[k]
            nr.append(ar[k] * xr[k] - ai[k] * xi[k] + inj_ref[k, rsel, :])
            ni.append(ar[k] * xi[k] + ai[k] * xr[k] + inj_ref[npl + k, rsel, :])
        return tuple(nr), tuple(ni)

    z = tuple(jnp.zeros((nb, lanes), F32) for _ in range(npl))
    lax.fori_loop(0, nchunk, step, (z, z))
    xp = jnp.concatenate([xp_ref[k] for k in range(2 * npl)], axis=1).astype(BF16)
    d = d_ref[...]
    for tp in range(0, L, 2):
        res = jnp.dot(ucat_ref[:, 0:(tp + 2) * lanes], grev_ref[(L - 1 - tp) * lanes:(L + 1) * lanes, :],
                      preferred_element_type=F32)
        res = res + jnp.dot(xp, wout_ref[:, tp * lanes:(tp + 2) * lanes], preferred_element_type=F32)
        for k in range(2):
            y = res[:, k * lanes:(k + 1) * lanes] + d * u_at(tp + k)
            y_ref[:, pl.ds(tp + k, nchunk, stride=L), :] = y.reshape(nb, nchunk, lanes)


def _s5(u, grev, win, wout, al, d, *, nb=4):
    b, t, dch = u.shape
    nb = min(nb, b)
    nblk = dch // LANES
    rows = nb * (t // SSM_CHUNK)
    kcat, sdim2 = win.shape[1], win.shape[2]
    return pl.pallas_call(
        _s5_body,
        grid=(nblk, b // nb),
        in_specs=[
            pl.BlockSpec((nb, t, LANES), lambda j, i: (i, 0, j)),
            pl.BlockSpec((None,) + grev.shape[1:], lambda j, i: (j, 0, 0)),
            pl.BlockSpec((None, kcat, sdim2), lambda j, i: (j, 0, 0)),
            pl.BlockSpec((None, sdim2, kcat), lambda j, i: (j, 0, 0)),
            pl.BlockSpec((None, 8, sdim2), lambda j, i: (j, 0, 0)),
            pl.BlockSpec((1, LANES), lambda j, i: (0, j)),
        ],
        out_specs=pl.BlockSpec((nb, t, LANES), lambda j, i: (i, 0, j)),
        out_shape=jax.ShapeDtypeStruct((b, t, dch), F32),
        scratch_shapes=[pltpu.VMEM((rows, kcat), BF16), pltpu.VMEM((sdim2 // LANES, rows, LANES), F32),
                        pltpu.VMEM((sdim2 // LANES, rows, LANES), F32)],
        compiler_params=_cparams(("parallel", "arbitrary")),
        name="s5",
    )(u, grev, win, wout, al, d.reshape(1, dch))


def _glu_out_body(h_ref, a_ref, y_ref, wg_ref, bg_ref, woa_ref, wos_ref, o_ref):
    hg = jax.nn.gelu(y_ref[...], approximate=True)
    z = jnp.dot(hg.astype(BF16), wg_ref[...], preferred_element_type=F32) + bg_ref[...]
    s = hg * jax.nn.sigmoid(z)
    mix = jnp.dot(a_ref[...], woa_ref[...], preferred_element_type=F32)
    mix = mix + jnp.dot(s.astype(BF16), wos_ref[...], preferred_element_type=F32)
    o_ref[...] = h_ref[...] + mix


def _glu_out(h, a, y, wg, bg, woa, wos, *, tm=512):
    n, dm = h.shape
    da, ds = a.shape[1], y.shape[1]
    row = lambda w: pl.BlockSpec((tm, w), lambda i: (i, 0))
    full = lambda r, c: pl.BlockSpec((r, c), lambda i: (0, 0))
    return pl.pallas_call(
        _glu_out_body,
        grid=(n // tm,),
        in_specs=[row(dm), row(da), row(ds), full(ds, ds), full(1, ds), full(da, dm), full(ds, dm)],
        out_specs=row(dm),
        out_shape=jax.ShapeDtypeStruct((n, dm), F32),
        compiler_params=_cparams(("parallel",)),
        name="glu_out",
    )(h, a, y, wg, bg.reshape(1, ds), woa, wos)


def _t5_bucket(dist):
    n = jnp.maximum(dist, 0)
    max_exact = N_BUCKETS // 2
    nf = jnp.maximum(n, 1).astype(F32)
    large = max_exact + (jnp.log(nf / max_exact) / math.log(MAX_DISTANCE / max_exact)
                         * (N_BUCKETS - max_exact)).astype(jnp.int32)
    large = jnp.minimum(large, N_BUCKETS - 1)
    return jnp.where(n < max_exact, n, large)


def _bias_tables(rel_bias, t):
    nt = t // TQ
    span = TQ + WINDOW
    table = rel_bias.astype(F32).reshape(N_BUCKETS, N_KV, HEADS_PER_KV, 1, 1)
    buckets = _t5_bucket(jnp.arange(2 * MAX_DISTANCE))
    first = [jnp.sum((buckets < k).astype(jnp.int32)) for k in range(N_BUCKETS)]

    def lookup(dist, valid):
        bias = jnp.broadcast_to(table[0], (N_KV, HEADS_PER_KV) + dist.shape)
        for k in range(1, N_BUCKETS):
            bias = jnp.where((dist >= first[k])[None, None], table[k], bias)
        return jnp.where(valid[None, None], bias, NEG)

    tt = jnp.arange(t)[:, None]
    cc = jnp.arange(LANES)[None, :]
    dist_c = tt - (cc * CMP_STRIDE + CMP_LEN - 1)
    nc = (t - CMP_LEN) // CMP_STRIDE + 1
    bc = lookup(dist_c, (dist_c >= 0) & (cc < nc))
    bc = bc.reshape(N_KV, HEADS_PER_KV, nt, TQ, LANES).transpose(0, 2, 1, 3, 4).reshape(N_KV, nt, ROWS4, LANES)
    a = jnp.arange(TQ)[:, None]
    k = jnp.arange(TQ)[None, :]
    always = jnp.ones((TQ, TQ), bool)
    far = lookup(jnp.full((TQ, TQ), 2 * TQ), always)
    sub = lookup(TQ + a - k, always)
    diag = lookup(a - k, a - k >= 0)
    beyond = jnp.full_like(far, NEG)
    dt = jnp.stack([far, sub, diag, beyond], axis=1).reshape(N_KV, 4, ROWS4, TQ)
    m = jnp.arange(span)[None, :]
    dist_w = a + WINDOW - m
    wb = lookup(dist_w, (dist_w >= 0) & (dist_w < WINDOW)).reshape(N_KV, ROWS4, span)
    return bc, dt, wb


def _sel_tables(t):
    ns = t // SEL_BLOCK
    nc_pad = LANES
    c_start = jnp.arange(nc_pad) * CMP_STRIDE
    j_start = jnp.arange(LANES) * SEL_BLOCK
    ov = jnp.clip(jnp.minimum(c_start[:, None] + CMP_LEN, j_start[None, :] + SEL_BLOCK)
                  - jnp.maximum(c_start[:, None], j_start[None, :]), 0, None).astype(F32) / CMP_LEN
    ov = jnp.where(jnp.arange(LANES)[None, :] < ns, ov, 0.0)
    et = jnp.arange(t)[:, None] // SEL_BLOCK == jnp.arange(LANES)[None, :]
    return ov.T.astype(BF16), et.astype(BF16)


def _s5_tables(lam_re, lam_im, log_step, b_re, b_im, c_re, c_im):
    hi = lax.Precision.HIGHEST
    ng, p = lam_re.shape
    hch = b_re.shape[2]
    L = SSM_CHUNK
    step = jnp.exp(log_step.astype(F32))[:, None]
    lre, lim = lam_re.astype(F32), lam_im.astype(F32)
    mag = jnp.exp(lre * step)
    ab_re, ab_im = mag * jnp.cos(lim * step), mag * jnp.sin(lim * step)
    nr, ni = ab_re - 1.0, ab_im
    den = lre * lre + lim * lim
    f_re, f_im = (nr * lre + ni * lim) / den, (ni * lre - nr * lim) / den
    br, bim = b_re.astype(F32), b_im.astype(F32)
    bb_re = f_re[..., None] * br - f_im[..., None] * bim
    bb_im = f_re[..., None] * bim + f_im[..., None] * br
    cr, ci = c_re.astype(F32), c_im.astype(F32)
    pr, pi = [jnp.ones_like(ab_re)], [jnp.zeros_like(ab_re)]
    for _ in range(L):
        pr, pi = pr + [pr[-1] * ab_re - pi[-1] * ab_im], pi + [pr[-1] * ab_im + pi[-1] * ab_re]
    pw_re, pw_im = jnp.stack(pr, 0), jnp.stack(pi, 0)
    cp_re = cr[None] * pw_re[:, :, None, :] - ci[None] * pw_im[:, :, None, :]
    cp_im = -(cr[None] * pw_im[:, :, None, :] + ci[None] * pw_re[:, :, None, :])
    kern = (jnp.einsum('tghp,gpk->gtkh', cp_re[:L], bb_re, precision=hi)
            + jnp.einsum('tghp,gpk->gtkh', cp_im[:L], bb_im, precision=hi))
    gpb = LANES // hch
    nblk = ng // gpb

    def replicate(compact, src):
        rep = (jnp.arange(compact.shape[-1])[:, None] == src[None, :]).astype(BF16)
        return jnp.einsum('jrw,wc->jrc', compact.astype(BF16), rep, preferred_element_type=F32)

    def same_group(x, row_group, col_group):
        return jnp.where((row_group[:, None] == col_group[None, :])[None], x, 0.0)

    lane = jnp.arange(LANES)
    kc = kern.reshape(nblk, gpb, L, hch, hch).transpose(0, 2, 1, 3, 4).reshape(nblk, L * LANES, hch)
    kblk = same_group(replicate(kc, lane % hch), jnp.arange(L * LANES) % LANES // hch, lane // hch)
    kblk = kblk.reshape(nblk, L, LANES, LANES)
    zero = jnp.zeros((nblk, 1, LANES, LANES), F32)
    kpad = jnp.concatenate([zero, kblk, zero], axis=1)
    gmat = jnp.concatenate([kpad[:, :L + 1], kpad[:, 1:]], axis=3)
    grev = gmat[:, ::-1].reshape(nblk, (L + 1) * LANES, 2 * LANES)
    wr = pw_re[L - 1 - jnp.arange(L)]
    wi = pw_im[L - 1 - jnp.arange(L)]
    win_re = (wr[..., None] * bb_re[None] - wi[..., None] * bb_im[None])
    win_im = (wr[..., None] * bb_im[None] + wi[..., None] * bb_re[None])

    state = jnp.arange(gpb * p)

    def inj_op(m):
        c = m.reshape(L, nblk, gpb, p, hch).transpose(1, 0, 2, 4, 3).reshape(nblk, L * LANES, p)
        return same_group(replicate(c, state % p), jnp.arange(L * LANES) % LANES // hch, state // p)

    win = jnp.concatenate([inj_op(win_re), inj_op(win_im)], axis=2)

    def read_op(m):
        c = m.reshape(L, nblk, gpb, hch, p).transpose(1, 2, 4, 0, 3).reshape(nblk, gpb * p, L * hch)
        col = jnp.arange(L * LANES)
        return same_group(replicate(c, col // LANES * hch + col % hch), state // p, col % LANES // hch)

    wout = jnp.concatenate([read_op(cp_re[1:]), read_op(cp_im[1:])], axis=1)
    al = jnp.concatenate([pw_re[L].reshape(nblk, gpb * p), pw_im[L].reshape(nblk, gpb * p)], axis=1)
    al = jnp.broadcast_to(al[:, None, :], (nblk, 8, 2 * gpb * p))
    return grev.astype(BF16), win.astype(BF16), wout.astype(BF16), al


def _mixers(h1, mix_norm, w_in, cmp_k, cmp_v, rel_bias, ssm, ssm_d, glu_w, glu_b, w_out, b, t):
    n, d = h1.shape
    d_ssm = glu_w.shape[0]
    o_g, o_u = D_ATTN + 6 * D_KV, D_ATTN + 6 * D_KV + 3 * N_HEADS
    gcols = 3 * HEADS_PER_KV
    wg = [jnp.pad(w_in[:, o_g + g * gcols:o_g + (g + 1) * gcols], ((0, 0), (0, LANES - gcols))) for g in range(N_KV)]
    w_perm = jnp.concatenate([w_in[:, :o_g], w_in[:, o_u:]] + wg, axis=1).astype(BF16)
    q, kvs, kcv, u, gates = _inproj(h1.reshape(b, t, d), mix_norm, w_perm, d_ssm)

    nblk = t // CMP_STRIDE
    assert nblk == LANES, "compressed-block axis is laid out on one 128-lane tile"
    pe = jnp.stack([cmp_k[0], cmp_v[0]], 0).astype(F32)
    w1 = jnp.stack([cmp_k[1], cmp_v[1]], 0).astype(BF16).reshape(2, CMP_LEN, HEAD_DIM, -1)
    b1 = jnp.stack([cmp_k[2].reshape(1, -1), cmp_v[2].reshape(1, -1)], 0).astype(F32)
    w2 = jnp.stack([cmp_k[3], cmp_v[3]], 0).astype(BF16)
    cb = _compress(kcv, pe, w1, b1, w2, nblk)

    biasc, dtiles, wb = _bias_tables(rel_bias, t)
    ovt, et = _sel_tables(t)
    a = _nsa(q, kvs, cb, gates, biasc, dtiles, wb, ovt, et)

    grev, win, wout, al = _s5_tables(*ssm)
    y = _s5(u, grev, win, wout, al, ssm_d)

    wo = w_out.astype(BF16)
    return _glu_out(h1, a.reshape(n, D_ATTN), y.reshape(n, d_ssm), glu_w.astype(BF16), glu_b, wo[:D_ATTN], wo[D_ATTN:])


def kernel(x, ffn1_norm, ffn1_w1, ffn1_w3, ffn1_w2, mix_norm, w_in, cmp_pe_k, cmp_w1_k, cmp_b1_k, cmp_w2_k,
           cmp_pe_v, cmp_w1_v, cmp_b1_v, cmp_w2_v, rel_bias, ssm_lam_re, ssm_lam_im, ssm_log_step, ssm_b_re,
           ssm_b_im, ssm_c_re, ssm_c_im, ssm_d, glu_w, glu_b, w_out, ffn2_norm, ffn2_w1, ffn2_w3, ffn2_w2,
           final_norm):
    b, t, d = x.shape
    depth = ffn1_w1.shape[0]
    h = x.reshape(b * t, d)
    for l in range(depth):
        last = l == depth - 1
        h = _ffn(h, ffn1_norm[l], ffn1_w1[l].astype(BF16), ffn1_w3[l].astype(BF16), ffn1_w2[l].astype(BF16))
        ssm = (ssm_lam_re[l], ssm_lam_im[l], ssm_log_step[l], ssm_b_re[l], ssm_b_im[l], ssm_c_re[l], ssm_c_im[l])
        h = _mixers(h, mix_norm[l], w_in[l],
                    (cmp_pe_k[l], cmp_w1_k[l], cmp_b1_k[l], cmp_w2_k[l]),
                    (cmp_pe_v[l], cmp_w1_v[l], cmp_b1_v[l], cmp_w2_v[l]),
                    rel_bias, ssm, ssm_d[l], glu_w[l], glu_b[l], w_out[l], b, t)
        h = _ffn(h, ffn2_norm[l], ffn2_w1[l].astype(BF16), ffn2_w3[l].astype(BF16), ffn2_w2[l].astype(BF16),
                 final_gain=final_norm if last else None)
    return h.reshape(b, t, d)
```

```python
import functools
import math

import jax
import jax.numpy as jnp
from jax import lax
from jax.experimental import pallas as pl
from jax.experimental.pallas import tpu as pltpu

F32 = jnp.float32
BF16 = jnp.bfloat16

HEAD_DIM = 128
N_KV = 2
HEADS_PER_KV = 4
N_HEADS = N_KV * HEADS_PER_KV
D_ATTN = N_HEADS * HEAD_DIM
D_KV = N_KV * HEAD_DIM
CMP_LEN = 32
CMP_STRIDE = 16
SEL_BLOCK = 64
N_SELECT = 16
WINDOW = 512
N_BUCKETS = 32
MAX_DISTANCE = 128
SSM_GROUP = 16
SSM_STATE = 64
EPS = 1e-6
NEG = -1e30

LANES = 128
VMEM_LIMIT_BYTES = 56 * 1024 * 1024
TQ = 128
ROWS4 = HEADS_PER_KV * TQ
SEL_CHUNK = 512
SSM_CHUNK = 16


def _cparams(sem):
    return pltpu.CompilerParams(dimension_semantics=sem, vmem_limit_bytes=VMEM_LIMIT_BYTES)


def _ffn_body(x_ref, g_ref, w1_ref, w3_ref, w2_ref, *rest, final):
    if final:
        fg_ref, o_ref, xn_ref, acc_ref = rest
    else:
        o_ref, xn_ref, acc_ref = rest
    j = pl.program_id(1)
    nj = pl.num_programs(1)

    @pl.when(j == 0)
    def _():
        x = x_ref[...]
        ms = jnp.mean(x * x, axis=-1, keepdims=True)
        xn_ref[...] = (x * lax.rsqrt(ms + EPS) * g_ref[...]).astype(BF16)
        acc_ref[...] = jnp.zeros(acc_ref.shape, F32)

    xn = xn_ref[...]
    a = jnp.dot(xn, w1_ref[...], preferred_element_type=F32)
    b = jnp.dot(xn, w3_ref[...], preferred_element_type=F32)
    gated = (a * jax.nn.sigmoid(a)) * b
    acc_ref[...] += jnp.dot(gated.astype(BF16), w2_ref[...], preferred_element_type=F32)

    @pl.when(j == nj - 1)
    def _():
        h = x_ref[...] + 0.5 * acc_ref[...]
        if final:
            ms = jnp.mean(h * h, axis=-1, keepdims=True)
            h = h * lax.rsqrt(ms + EPS) * fg_ref[...]
        o_ref[...] = h


def _ffn(x, gain, w1, w3, w2, final_gain=None, *, tm=512, tf=512):
    n, d = x.shape
    dff = w1.shape[1]
    final = final_gain is not None
    in_specs = [
        pl.BlockSpec((tm, d), lambda i, j: (i, 0)),
        pl.BlockSpec((1, d), lambda i, j: (0, 0)),
        pl.BlockSpec((d, tf), lambda i, j: (0, j)),
        pl.BlockSpec((d, tf), lambda i, j: (0, j)),
        pl.BlockSpec((tf, d), lambda i, j: (j, 0)),
    ]
    args = [x, gain.reshape(1, d), w1, w3, w2]
    if final:
        in_specs.append(pl.BlockSpec((1, d), lambda i, j: (0, 0)))
        args.append(final_gain.reshape(1, d))
    return pl.pallas_call(
        functools.partial(_ffn_body, final=final),
        grid=(n // tm, dff // tf),
        in_specs=in_specs,
        out_specs=pl.BlockSpec((tm, d), lambda i, j: (i, 0)),
        out_shape=jax.ShapeDtypeStruct((n, d), F32),
        scratch_shapes=[pltpu.VMEM((tm, d), BF16), pltpu.VMEM((tm, d), F32)],
        compiler_params=_cparams(("parallel", "arbitrary")),
        name="ffn_final" if final else "ffn",
    )(*args)


def _inproj_body(x_ref, g_ref, w_ref, q_ref, kvs_ref, kcv_ref, u_ref, gate_ref):
    i = pl.program_id(1)

    @pl.when(i == 0)
    def _():
        kvs_ref[...] = jnp.zeros(kvs_ref.shape, kvs_ref.dtype)
        kcv_ref[...] = jnp.zeros(kcv_ref.shape, kcv_ref.dtype)

    @pl.when(i > 0)
    def _():
        x = x_ref[...]
        ms = jnp.mean(x * x, axis=-1, keepdims=True)
        xn = (x * lax.rsqrt(ms + EPS) * g_ref[...]).astype(BF16)
        c0 = 0
        c1 = q_ref.shape[1]
        q = jnp.dot(xn, w_ref[:, c0:c1], preferred_element_type=F32)
        q_ref[...] = (q * (HEAD_DIM ** -0.5)).astype(BF16)
        c0, c1 = c1, c1 + kcv_ref.shape[1]
        kcv_ref[...] = jnp.dot(xn, w_ref[:, c0:c1], preferred_element_type=F32)
        c0, c1 = c1, c1 + kvs_ref.shape[1]
        kvs_ref[...] = jnp.dot(xn, w_ref[:, c0:c1], preferred_element_type=F32).astype(BF16)
        c0, c1 = c1, c1 + u_ref.shape[1]
        u_ref[...] = jnp.dot(xn, w_ref[:, c0:c1], preferred_element_type=F32)
        gate_ref[...] = jax.nn.sigmoid(jnp.dot(xn, w_ref[:, c1:], preferred_element_type=F32))


def _inproj(h, gain, w_perm, d_ssm):
    b, t, d = h.shape
    tm = WINDOW
    nt = t // tm
    nq, ncv, nvs, ng = D_ATTN, 2 * D_KV, 4 * D_KV, N_KV * LANES
    ncol = w_perm.shape[1]
    assert ncol == nq + ncv + nvs + d_ssm + ng and t % tm == 0
    data = lambda bi, i: (bi, jnp.maximum(i - 1, 0), 0)
    return pl.pallas_call(
        _inproj_body,
        grid=(b, nt + 1),
        in_specs=[
            pl.BlockSpec((None, tm, d), data),
            pl.BlockSpec((1, d), lambda bi, i: (0, 0)),
            pl.BlockSpec((d, ncol), lambda bi, i: (0, 0)),
        ],
        out_specs=[
            pl.BlockSpec((None, tm, nq), data),
            pl.BlockSpec((None, tm, nvs), lambda bi, i: (bi, i, 0)),
            pl.BlockSpec((None, tm, ncv), lambda bi, i: (bi, jnp.where(i == 0, nt, i - 1), 0)),
            pl.BlockSpec((None, tm, d_ssm), data),
            pl.BlockSpec((None, tm, ng), data),
        ],
        out_shape=[
            jax.ShapeDtypeStruct((b, t, nq), BF16),
            jax.ShapeDtypeStruct((b, t + tm, nvs), BF16),
            jax.ShapeDtypeStruct((b, t + tm, ncv), F32),
            jax.ShapeDtypeStruct((b, t, d_ssm), F32),
            jax.ShapeDtypeStruct((b, t, ng), F32),
        ],
        compiler_params=_cparams(("parallel", "arbitrary")),
        name="inproj",
    )(h, gain.reshape(1, d), w_perm)


def _compress_body(x_ref, pe_ref, w1_ref, b1_ref, w2_ref, o_ref):
    nb, nblk = o_ref.shape[0], o_ref.shape[1]
    acc = jnp.broadcast_to(b1_ref[...], (nb * nblk, b1_ref.shape[1])).astype(F32)
    for l in range(CMP_LEN):
        xl = x_ref[:, pl.ds(l, nblk, stride=CMP_STRIDE), :] + pe_ref[l:l + 1, :]
        acc = acc + jnp.dot(xl.reshape(nb * nblk, xl.shape[2]).astype(BF16), w1_ref[l], preferred_element_type=F32)
    hid = jax.nn.gelu(acc, approximate=True)
    out = jnp.dot(hid.astype(BF16), w2_ref[...], preferred_element_type=F32)
    o_ref[...] = out.reshape(o_ref.shape).astype(BF16)


def _compress(kcv, pe, w1, b1, w2, nblk, *, nb=4):
    b, tpad, _ = kcv.shape
    nb = min(nb, b)
    hid, dh = w1.shape[3], w2.shape[2]
    assert (nblk - 1) * CMP_STRIDE + CMP_LEN <= tpad
    return pl.pallas_call(
        _compress_body,
        grid=(2 * N_KV, b // nb),
        in_specs=[
            pl.BlockSpec((nb, tpad, dh), lambda s, i: (i, 0, s)),
            pl.BlockSpec((None, CMP_LEN, dh), lambda s, i: (s // N_KV, 0, 0)),
            pl.BlockSpec((None, CMP_LEN, dh, hid), lambda s, i: (s // N_KV, 0, 0, 0)),
            pl.BlockSpec((None, 1, hid), lambda s, i: (s // N_KV, 0, 0)),
            pl.BlockSpec((None, hid, dh), lambda s, i: (s // N_KV, 0, 0)),
        ],
        out_specs=pl.BlockSpec((None, None, nb, nblk, dh), lambda s, i: (s // N_KV, s % N_KV, i, 0, 0)),
        out_shape=jax.ShapeDtypeStruct((2, N_KV, b, nblk, dh), BF16),
        compiler_params=_cparams(("parallel", "parallel")),
        name="compress",
    )(kcv, pe, w1, b1, w2)


def _nt_dot(a, b):
    return lax.dot_general(a, b, (((1,), (1,)), ((), ())), preferred_element_type=F32)


def _nsa_body(q_ref, ks_ref, vs_ref, kw_ref, vw_ref, kcb_ref, vcb_ref, gate_ref, biasc_ref, dt_ref, wb_ref,
              ovt_ref, et_ref, o_ref, kse_ref, s_ref, m_ref, l_ref, acc_ref):
    i = pl.program_id(1)
    t0 = i * TQ
    t = et_ref.shape[0]
    ns = t // SEL_BLOCK
    nsel = min(N_SELECT, ns)
    groups = range(N_KV)
    gl = lambda g: slice(g * HEAD_DIM, (g + 1) * HEAD_DIM)

    @pl.when(i == 0)
    def _():
        for g in groups:
            kse_ref[g, :, 0:HEAD_DIM] = ks_ref[WINDOW:WINDOW + t, gl(g)]
            kse_ref[g, :, HEAD_DIM:] = et_ref[...]

    q = q_ref[...]
    w0 = pl.multiple_of(t0, LANES)
    span = wb_ref.shape[2]

    def front(g):
        q4 = jnp.concatenate([q[:, (g * HEADS_PER_KV + h) * HEAD_DIM:(g * HEADS_PER_KV + h + 1) * HEAD_DIM]
                              for h in range(HEADS_PER_KV)], axis=0)
        sc = _nt_dot(q4, kcb_ref[g]) + biasc_ref[g]
        row_t = t0 + (lax.broadcasted_iota(jnp.int32, (ROWS4, LANES), 0) & (TQ - 1))
        col_c = lax.broadcasted_iota(jnp.int32, (ROWS4, LANES), 1)
        valid_c = (row_t - col_c * CMP_STRIDE - (CMP_LEN - 1) >= 0) & (col_c < LANES - 1)
        mc = jnp.max(sc, axis=-1, keepdims=True)
        pc = jnp.where(valid_c, jnp.exp(sc - mc), 0.0)
        pc = pc / jnp.maximum(jnp.sum(pc, axis=-1, keepdims=True), 1e-30)
        pcb = pc.astype(BF16)
        o_cmp = jnp.dot(pcb, vcb_ref[g], preferred_element_type=F32)
        pimp = _nt_dot(ovt_ref[...], pcb)
        imp = pimp[0:ns, 0:TQ]
        for h in range(1, HEADS_PER_KV):
            imp = imp + pimp[0:ns, h * TQ:(h + 1) * TQ]

        jrow = lax.broadcasted_iota(jnp.int32, (ns, TQ), 0)
        tpos = t0 + lax.broadcasted_iota(jnp.int32, (ns, TQ), 1)
        cur = lax.shift_right_logical(tpos, int(math.log2(SEL_BLOCK)))
        forced = (jrow == 0) | (jrow == cur) | (jrow == cur - 1)
        impm = jnp.where(forced, 1e6, jnp.where(jrow * SEL_BLOCK <= tpos, imp, -1e9))
        nslab = ns // 8
        slabs = [impm[8 * v:8 * v + 8] for v in range(nslab)]
        cnts = [jnp.zeros((8, TQ), F32) for _ in range(nslab)]
        sub = lax.broadcasted_iota(jnp.int32, (8, TQ), 0)
        for jp in range(ns):
            v0, r0 = divmod(jp, 8)
            row = jnp.broadcast_to(slabs[v0][r0:r0 + 1, :], (8, TQ))
            for v in range(nslab):
                if v > v0:
                    beats = row >= slabs[v]
                elif v < v0:
                    beats = row > slabs[v]
                else:
                    beats = (row > slabs[v]) | ((row == slabs[v]) & (sub > r0))
                cnts[v] = cnts[v] + jnp.where(beats, 1.0, 0.0)
        negt = [jnp.where(c < nsel, 0.0, NEG) for c in cnts]
        negt = jnp.concatenate(negt + [jnp.zeros((LANES - ns, TQ), F32)], axis=0)
        neg = negt.T.astype(BF16)
        q4s = jnp.concatenate([q4, jnp.concatenate([neg] * HEADS_PER_KV, axis=0)], axis=1)

        kpos = t0 - WINDOW + lax.broadcasted_iota(jnp.int32, (1, span), 1)
        sw = _nt_dot(q4, kw_ref[pl.ds(w0, span), gl(g)]) + wb_ref[g] + jnp.where(kpos >= 0, 0.0, NEG)
        mw = jnp.max(sw, axis=-1, keepdims=True)
        pw = jnp.exp(sw - mw)
        lw = jnp.sum(pw, axis=-1, keepdims=True)
        o_win = jnp.dot(pw.astype(BF16), vw_ref[pl.ds(w0, span), gl(g)], preferred_element_type=F32)
        return q4s, o_cmp, o_win / jnp.maximum(lw, 1e-30)

    fronts = [front(g) for g in groups]

    nsub = SEL_CHUNK // LANES
    nchunks = i // nsub + 1
    m_ref[...] = jnp.full(m_ref.shape, -3e38, F32)

    def score_chunk(ck, _):
        kb = pl.multiple_of(ck * SEL_CHUNK, SEL_CHUNK)
        for g in groups:
            s = _nt_dot(fronts[g][0], kse_ref[g, pl.ds(kb, SEL_CHUNK), :])
            mloc = None
            for j in range(nsub):
                sj = s[:, j * LANES:(j + 1) * LANES] + dt_ref[g, jnp.clip(ck * nsub + j - i + 2, 0, 3)]
                s_ref[g, ck, :, j * LANES:(j + 1) * LANES] = sj
                mloc = sj if mloc is None else jnp.maximum(mloc, sj)
            m_ref[g] = jnp.maximum(m_ref[g], mloc)
        return 0

    lax.fori_loop(0, nchunks, score_chunk, 0)
    for g in groups:
        m_ref[g] = jnp.broadcast_to(jnp.max(m_ref[g], axis=-1, keepdims=True), (ROWS4, LANES))
    l_ref[...] = jnp.zeros(l_ref.shape, F32)
    acc_ref[...] = jnp.zeros(acc_ref.shape, F32)

    def pv_chunk(ck, _):
        kb = pl.multiple_of(WINDOW + ck * SEL_CHUNK, SEL_CHUNK)
        for g in groups:
            m = m_ref[g]
            ps = [jnp.exp(s_ref[g, ck, :, j * LANES:(j + 1) * LANES] - m) for j in range(nsub)]
            lsum = ps[0]
            for pj in ps[1:]:
                lsum = lsum + pj
            l_ref[g] += lsum
            p = jnp.concatenate([pj.astype(BF16) for pj in ps], axis=1)
            acc_ref[g] += jnp.dot(p, vs_ref[pl.ds(kb, SEL_CHUNK), gl(g)], preferred_element_type=F32)
        return 0

    lax.fori_loop(0, nchunks, pv_chunk, 0)

    for g in groups:
        _, o_cmp, o_win = fronts[g]
        o_sel = acc_ref[g] / jnp.maximum(jnp.sum(l_ref[g], axis=-1, keepdims=True), 1e-30)
        gt = gate_ref[:, g * LANES:(g + 1) * LANES]
        for h in range(HEADS_PER_KV):
            r = slice(h * TQ, (h + 1) * TQ)
            o_h = (gt[:, 3 * h:3 * h + 1] * o_cmp[r] + gt[:, 3 * h + 1:3 * h + 2] * o_sel[r]
                   + gt[:, 3 * h + 2:3 * h + 3] * o_win[r])
            c0 = (g * HEADS_PER_KV + h) * HEAD_DIM
            o_ref[:, c0:c0 + HEAD_DIM] = o_h.astype(BF16)


def _nsa(q, kvs, cb, gates, biasc, dtiles, wb, ovt, et):
    b, t, _ = q.shape
    nt = t // TQ
    tp = kvs.shape[1]
    span = wb.shape[2]
    gw = HEADS_PER_KV * HEAD_DIM
    assert tp == t + WINDOW and span == TQ + WINDOW and t % SEL_CHUNK == 0 and (t // SEL_BLOCK) % 8 == 0

    def stream(j):
        return pl.BlockSpec((None, tp, N_KV * HEAD_DIM), lambda bi, i: (bi, 0, j))

    return pl.pallas_call(
        _nsa_body,
        grid=(b, nt),
        in_specs=[
            pl.BlockSpec((None, TQ, N_KV * gw), lambda bi, i: (bi, i, 0)),
            stream(0), stream(1), stream(2), stream(3),
            pl.BlockSpec((None, N_KV, None, LANES, HEAD_DIM), lambda bi, i: (0, 0, bi, 0, 0)),
            pl.BlockSpec((None, N_KV, None, LANES, HEAD_DIM), lambda bi, i: (1, 0, bi, 0, 0)),
            pl.BlockSpec((None, TQ, N_KV * LANES), lambda bi, i: (bi, i, 0)),
            pl.BlockSpec((N_KV, None, ROWS4, LANES), lambda bi, i: (0, i, 0, 0)),
            pl.BlockSpec((N_KV, 4, ROWS4, LANES), lambda bi, i: (0, 0, 0, 0)),
            pl.BlockSpec((N_KV, ROWS4, span), lambda bi, i: (0, 0, 0)),
            pl.BlockSpec((LANES, LANES), lambda bi, i: (0, 0)),
            pl.BlockSpec((t, LANES), lambda bi, i: (0, 0)),
        ],
        out_specs=pl.BlockSpec((None, TQ, N_KV * gw), lambda bi, i: (bi, i, 0)),
        out_shape=jax.ShapeDtypeStruct((b, t, N_KV * gw), BF16),
        scratch_shapes=[
            pltpu.VMEM((N_KV, t, 2 * HEAD_DIM), BF16),
            pltpu.VMEM((N_KV, t // SEL_CHUNK, ROWS4, SEL_CHUNK), F32),
            pltpu.VMEM((N_KV, ROWS4, LANES), F32),
            pltpu.VMEM((N_KV, ROWS4, LANES), F32),
            pltpu.VMEM((N_KV, ROWS4, HEAD_DIM), F32),
        ],
        compiler_params=_cparams(("parallel", "arbitrary")),
        name="nsa",
    )(q, kvs, kvs, kvs, kvs, cb, cb, gates, biasc, dtiles, wb, ovt, et)


def _s5_body(u_ref, kc_ref, winc_ref, woutc_ref, rep_k_ref, rep_in_ref, rep_out_ref, m_k_ref, m_in_ref, m_out_ref,
             al_ref, d_ref, y_ref, ucat_ref, inj_ref, xp_ref, grev_ref, win_ref, wout_ref):
    nb, t, lanes = u_ref.shape
    L = SSM_CHUNK
    nchunk = t // L
    rows = nb * nchunk
    sdim = al_ref.shape[1] // 2

    @pl.when(pl.program_id(1) == 0)
    def _():
        m_k, m_in, m_out = m_k_ref[...], m_in_ref[...], m_out_ref[...]
        kx = jnp.dot(kc_ref[...], rep_k_ref[...], preferred_element_type=F32)
        zero = jnp.zeros((lanes, lanes), BF16)

        def kblk(tau):
            return (kx[tau * lanes:(tau + 1) * lanes] * m_k).astype(BF16) if 0 <= tau < L else zero

        for r in range(L + 1):
            grev_ref[r * lanes:(r + 1) * lanes, 0:lanes] = kblk(L - r - 1)
            grev_ref[r * lanes:(r + 1) * lanes, lanes:] = kblk(L - r)
        for half in range(2):
            for s in range(L):
                x = jnp.dot(winc_ref[half, s * lanes:(s + 1) * lanes, :], rep_in_ref[...], preferred_element_type=F32)
                win_ref[s * lanes:(s + 1) * lanes, half * sdim:(half + 1) * sdim] = (x * m_in).astype(BF16)
            for tt in range(L):
                x = jnp.dot(woutc_ref[half], rep_out_ref[:, tt * lanes:(tt + 1) * lanes], preferred_element_type=F32)
                wout_ref[half * sdim:(half + 1) * sdim, tt * lanes:(tt + 1) * lanes] = (x * m_out).astype(BF16)

    def u_at(s):
        return u_ref[:, pl.ds(s, nchunk, stride=L), :].reshape(rows, lanes)

    for s in range(L):
        ucat_ref[:, s * lanes:(s + 1) * lanes] = u_at(s).astype(BF16)
    inj = jnp.dot(ucat_ref[...], win_ref[...], preferred_element_type=F32)
    npl = sdim // lanes
    for k in range(2 * npl):
        inj_ref[k] = inj[:, k * lanes:(k + 1) * lanes]
    ar = [jnp.broadcast_to(al_ref[0:1, k * lanes:(k + 1) * lanes], (nb, lanes)) for k in range(npl)]
    ai = [jnp.broadcast_to(al_ref[0:1, sdim + k * lanes:sdim + (k + 1) * lanes], (nb, lanes)) for k in range(npl)]

    def step(c, carry):
        xr, xi = carry
        rsel = pl.ds(c, nb, stride=nchunk)
        nr, ni = [], []
        for k in range(npl):
            xp_ref[k, rsel, :] = xr[k]
            xp_ref[npl + k, rsel, :] = xi[k]
            nr.append(ar[k] * xr[k] - ai[k] * xi[k] + inj_ref[k, rsel, :])
            ni.append(ar[k] * xi[k] + ai[k] * xr[k] + inj_ref[npl + k, rsel, :])
        return tuple(nr), tuple(ni)

    z = tuple(jnp.zeros((nb, lanes), F32) for _ in range(npl))
    lax.fori_loop(0, nchunk, step, (z, z))
    xp = jnp.concatenate([xp_ref[k] for k in range(2 * npl)], axis=1).astype(BF16)
    d = d_ref[...]
    for tp in range(0, L, 2):
        res = jnp.dot(ucat_ref[:, 0:(tp + 2) * lanes], grev_ref[(L - 1 - tp) * lanes:(L + 1) * lanes, :],
                      preferred_element_type=F32)
        res = res + jnp.dot(xp, wout_ref[:, tp * lanes:(tp + 2) * lanes], preferred_element_type=F32)
        for k in range(2):
            y = res[:, k * lanes:(k + 1) * lanes] + d * u_at(tp + k)
            y_ref[:, pl.ds(tp + k, nchunk, stride=L), :] = y.reshape(nb, nchunk, lanes)


def _s5(u, kc, winc, woutc, al, d, *, nb=4):
    b, t, dch = u.shape
    nb = min(nb, b)
    nblk = dch // LANES
    L = SSM_CHUNK
    rows = nb * (t // L)
    hch, p = kc.shape[2], winc.shape[3]
    gpb = LANES // hch
    sdim = gpb * p
    kcat = L * LANES
    lane, col, st = jnp.arange(LANES), jnp.arange(kcat), jnp.arange(sdim)
    rep_k = (jnp.arange(hch)[:, None] == (lane % hch)[None, :]).astype(BF16)
    rep_in = (jnp.arange(p)[:, None] == (st % p)[None, :]).astype(BF16)
    rep_out = (jnp.arange(L * hch)[:, None] == (col // LANES * hch + col % hch)[None, :]).astype(BF16)
    m_k = ((lane // hch)[:, None] == (lane // hch)[None, :]).astype(F32)
    m_in = ((lane // hch)[:, None] == (st // p)[None, :]).astype(F32)
    m_out = ((st // p)[:, None] == (lane // hch)[None, :]).astype(F32)
    const = lambda a: pl.BlockSpec(a.shape, lambda j, i: (0,) * a.ndim)
    return pl.pallas_call(
        _s5_body,
        grid=(nblk, b // nb),
        in_specs=[
            pl.BlockSpec((nb, t, LANES), lambda j, i: (i, 0, j)),
            pl.BlockSpec((None, kcat, hch), lambda j, i: (j, 0, 0)),
            pl.BlockSpec((None, 2, kcat, p), lambda j, i: (j, 0, 0, 0)),
            pl.BlockSpec((None, 2, sdim, L * hch), lambda j, i: (j, 0, 0, 0)),
            const(rep_k), const(rep_in), const(rep_out), const(m_k), const(m_in), const(m_out),
            pl.BlockSpec((None, 8, 2 * sdim), lambda j, i: (j, 0, 0)),
            pl.BlockSpec((1, LANES), lambda j, i: (0, j)),
        ],
        out_specs=pl.BlockSpec((nb, t, LANES), lambda j, i: (i, 0, j)),
        out_shape=jax.ShapeDtypeStruct((b, t, dch), F32),
        scratch_shapes=[pltpu.VMEM((rows, kcat), BF16), pltpu.VMEM((2 * sdim // LANES, rows, LANES), F32),
                        pltpu.VMEM((2 * sdim // LANES, rows, LANES), F32),
                        pltpu.VMEM(((L + 1) * LANES, 2 * LANES), BF16), pltpu.VMEM((kcat, 2 * sdim), BF16),
                        pltpu.VMEM((2 * sdim, kcat), BF16)],
        compiler_params=_cparams(("parallel", "arbitrary")),
        name="s5",
    )(u, kc, winc, woutc, rep_k, rep_in, rep_out, m_k, m_in, m_out, al, d.reshape(1, dch))


def _glu_out_body(h_ref, a_ref, y_ref, wg_ref, bg_ref, woa_ref, wos_ref, o_ref):
    hg = jax.nn.gelu(y_ref[...], approximate=True)
    z = jnp.dot(hg.astype(BF16), wg_ref[...], preferred_element_type=F32) + bg_ref[...]
    s = hg * jax.nn.sigmoid(z)
    mix = jnp.dot(a_ref[...], woa_ref[...], preferred_element_type=F32)
    mix = mix + jnp.dot(s.astype(BF16), wos_ref[...], preferred_element_type=F32)
    o_ref[...] = h_ref[...] + mix


def _glu_out(h, a, y, wg, bg, woa, wos, *, tm=512):
    n, dm = h.shape
    da, ds = a.shape[1], y.shape[1]
    row = lambda w: pl.BlockSpec((tm, w), lambda i: (i, 0))
    full = lambda r, c: pl.BlockSpec((r, c), lambda i: (0, 0))
    return pl.pallas_call(
        _glu_out_body,
        grid=(n // tm,),
        in_specs=[row(dm), row(da), row(ds), full(ds, ds), full(1, ds), full(da, dm), full(ds, dm)],
        out_specs=row(dm),
        out_shape=jax.ShapeDtypeStruct((n, dm), F32),
        compiler_params=_cparams(("parallel",)),
        name="glu_out",
    )(h, a, y, wg, bg.reshape(1, ds), woa, wos)


def _t5_bucket(dist):
    n = jnp.maximum(dist, 0)
    max_exact = N_BUCKETS // 2
    nf = jnp.maximum(n, 1).astype(F32)
    large = max_exact + (jnp.log(nf / max_exact) / math.log(MAX_DISTANCE / max_exact)
                         * (N_BUCKETS - max_exact)).astype(jnp.int32)
    large = jnp.minimum(large, N_BUCKETS - 1)
    return jnp.where(n < max_exact, n, large)


def _t5_body(first_ref, table_ref, bc_ref, dt_ref, wb_ref, *, nc):
    g = pl.program_id(0)
    nt = bc_ref.shape[0]
    span = wb_ref.shape[1]
    a = lax.broadcasted_iota(jnp.int32, (TQ, LANES), 0)
    c = lax.broadcasted_iota(jnp.int32, (TQ, LANES), 1)
    heads = [g * HEADS_PER_KV + h for h in range(HEADS_PER_KV)]

    def lookup(dist, valid):
        bias = [jnp.full((TQ, LANES), table_ref[0, hd], F32) for hd in heads]
        for k in range(1, N_BUCKETS):
            ge = dist >= first_ref[k]
            bias = [jnp.where(ge, table_ref[k, hd], bv) for hd, bv in zip(heads, bias)]
        return bias if valid is None else [jnp.where(valid, bv, NEG) for bv in bias]

    def put(ref_at, tiles):
        for h, tile in enumerate(tiles):
            ref_at(slice(h * TQ, (h + 1) * TQ), tile)

    def put_dt(v):
        def f(rows, tile):
            dt_ref[v, rows, :] = tile
        return f

    put(put_dt(0), lookup(jnp.full((TQ, LANES), 2 * TQ, jnp.int32), None))
    put(put_dt(1), lookup(TQ + a - c, None))
    put(put_dt(2), lookup(a - c, a - c >= 0))
    dt_ref[3] = jnp.full((ROWS4, LANES), NEG, F32)
    for j in range(span // LANES):
        dist = a + WINDOW - (c + j * LANES)

        def put_wb(rows, tile, j=j):
            wb_ref[rows, j * LANES:(j + 1) * LANES] = tile

        put(put_wb, lookup(dist, (dist >= 0) & (dist < WINDOW)))

    def cmp_tile(i, _):
        dist = i * TQ + a - (c * CMP_STRIDE + CMP_LEN - 1)

        def put_bc(rows, tile):
            bc_ref[i, rows, :] = tile

        put(put_bc, lookup(dist, (dist >= 0) & (c < nc)))
        return 0

    lax.fori_loop(0, nt, cmp_tile, 0)


def _bias_tables(rel_bias, t):
    nt = t // TQ
    span = TQ + WINDOW
    nc = (t - CMP_LEN) // CMP_STRIDE + 1
    buckets = _t5_bucket(jnp.arange(2 * MAX_DISTANCE))
    first = jnp.sum((buckets[None, :] < jnp.arange(N_BUCKETS)[:, None]).astype(jnp.int32), axis=1)
    smem = pl.BlockSpec(memory_space=pltpu.SMEM)
    return pl.pallas_call(
        functools.partial(_t5_body, nc=nc),
        grid=(N_KV,),
        in_specs=[smem, smem],
        out_specs=[
            pl.BlockSpec((None, nt, ROWS4, LANES), lambda g: (g, 0, 0, 0)),
            pl.BlockSpec((None, 4, ROWS4, LANES), lambda g: (g, 0, 0, 0)),
            pl.BlockSpec((None, ROWS4, span), lambda g: (g, 0, 0)),
        ],
        out_shape=[
            jax.ShapeDtypeStruct((N_KV, nt, ROWS4, LANES), F32),
            jax.ShapeDtypeStruct((N_KV, 4, ROWS4, LANES), F32),
            jax.ShapeDtypeStruct((N_KV, ROWS4, span), F32),
        ],
        compiler_params=_cparams(("parallel",)),
        name="t5_tables",
    )(first, rel_bias.astype(F32))


def _sel_tables(t):
    ns = t // SEL_BLOCK
    nc_pad = LANES
    c_start = jnp.arange(nc_pad) * CMP_STRIDE
    j_start = jnp.arange(LANES) * SEL_BLOCK
    ov = jnp.clip(jnp.minimum(c_start[:, None] + CMP_LEN, j_start[None, :] + SEL_BLOCK)
                  - jnp.maximum(c_start[:, None], j_start[None, :]), 0, None).astype(F32) / CMP_LEN
    ov = jnp.where(jnp.arange(LANES)[None, :] < ns, ov, 0.0)
    et = jnp.arange(t)[:, None] // SEL_BLOCK == jnp.arange(LANES)[None, :]
    return ov.T.astype(BF16), et.astype(BF16)


def _s5_tables(lam_re, lam_im, log_step, b_re, b_im, c_re, c_im):
    hi = lax.Precision.HIGHEST
    ng, p = lam_re.shape
    hch = b_re.shape[2]
    L = SSM_CHUNK
    step = jnp.exp(log_step.astype(F32))[:, None]
    lre, lim = lam_re.astype(F32), lam_im.astype(F32)
    mag = jnp.exp(lre * step)
    ab_re, ab_im = mag * jnp.cos(lim * step), mag * jnp.sin(lim * step)
    nr, ni = ab_re - 1.0, ab_im
    den = lre * lre + lim * lim
    f_re, f_im = (nr * lre + ni * lim) / den, (ni * lre - nr * lim) / den
    br, bim = b_re.astype(F32), b_im.astype(F32)
    bb_re = f_re[..., None] * br - f_im[..., None] * bim
    bb_im = f_re[..., None] * bim + f_im[..., None] * br
    cr, ci = c_re.astype(F32), c_im.astype(F32)
    pr, pi = [jnp.ones_like(ab_re)], [jnp.zeros_like(ab_re)]
    for _ in range(L):
        pr, pi = pr + [pr[-1] * ab_re - pi[-1] * ab_im], pi + [pr[-1] * ab_im + pi[-1] * ab_re]
    pw_re, pw_im = jnp.stack(pr, 0), jnp.stack(pi, 0)
    cp_re = cr[None] * pw_re[:, :, None, :] - ci[None] * pw_im[:, :, None, :]
    cp_im = -(cr[None] * pw_im[:, :, None, :] + ci[None] * pw_re[:, :, None, :])
    kern = (jnp.einsum('tghp,gpk->gtkh', cp_re[:L], bb_re, precision=hi)
            + jnp.einsum('tghp,gpk->gtkh', cp_im[:L], bb_im, precision=hi))
    gpb = LANES // hch
    nblk = ng // gpb
    kc = kern.reshape(nblk, gpb, L, hch, hch).transpose(0, 2, 1, 3, 4).reshape(nblk, L * LANES, hch)
    wr = pw_re[L - 1 - jnp.arange(L)]
    wi = pw_im[L - 1 - jnp.arange(L)]
    win_re = (wr[..., None] * bb_re[None] - wi[..., None] * bb_im[None])
    win_im = (wr[..., None] * bb_im[None] + wi[..., None] * bb_re[None])
    inj_op = lambda m: m.reshape(L, nblk, gpb, p, hch).transpose(1, 0, 2, 4, 3).reshape(nblk, L * LANES, p)
    winc = jnp.stack([inj_op(win_re), inj_op(win_im)], axis=1)
    read_op = lambda m: m.reshape(L, nblk, gpb, hch, p).transpose(1, 2, 4, 0, 3).reshape(nblk, gpb * p, L * hch)
    woutc = jnp.stack([read_op(cp_re[1:]), read_op(cp_im[1:])], axis=1)
    al = jnp.concatenate([pw_re[L].reshape(nblk, gpb * p), pw_im[L].reshape(nblk, gpb * p)], axis=1)
    al = jnp.broadcast_to(al[:, None, :], (nblk, 8, 2 * gpb * p))
    return kc.astype(BF16), winc.astype(BF16), woutc.astype(BF16), al


def _mixers(h1, mix_norm, w_in, cmp_k, cmp_v, rel_bias, ssm, ssm_d, glu_w, glu_b, w_out, b, t):
    n, d = h1.shape
    d_ssm = glu_w.shape[0]
    o_g, o_u = D_ATTN + 6 * D_KV, D_ATTN + 6 * D_KV + 3 * N_HEADS
    gcols = 3 * HEADS_PER_KV
    wg = [jnp.pad(w_in[:, o_g + g * gcols:o_g + (g + 1) * gcols], ((0, 0), (0, LANES - gcols))) for g in range(N_KV)]
    w_perm = jnp.concatenate([w_in[:, :o_g], w_in[:, o_u:]] + wg, axis=1).astype(BF16)
    q, kvs, kcv, u, gates = _inproj(h1.reshape(b, t, d), mix_norm, w_perm, d_ssm)

    nblk = t // CMP_STRIDE
    assert nblk == LANES, "compressed-block axis is laid out on one 128-lane tile"
    pe = jnp.stack([cmp_k[0], cmp_v[0]], 0).astype(F32)
    w1 = jnp.stack([cmp_k[1], cmp_v[1]], 0).astype(BF16).reshape(2, CMP_LEN, HEAD_DIM, -1)
    b1 = jnp.stack([cmp_k[2].reshape(1, -1), cmp_v[2].reshape(1, -1)], 0).astype(F32)
    w2 = jnp.stack([cmp_k[3], cmp_v[3]], 0).astype(BF16)
    cb = _compress(kcv, pe, w1, b1, w2, nblk)

    biasc, dtiles, wb = _bias_tables(rel_bias, t)
    ovt, et = _sel_tables(t)
    a = _nsa(q, kvs, cb, gates, biasc, dtiles, wb, ovt, et)

    kc, winc, woutc, al = _s5_tables(*ssm)
    y = _s5(u, kc, winc, woutc, al, ssm_d)

    wo = w_out.astype(BF16)
    return _glu_out(h1, a.reshape(n, D_ATTN), y.reshape(n, d_ssm), glu_w.astype(BF16), glu_b, wo[:D_ATTN], wo[D_ATTN:])


def kernel(x, ffn1_norm, ffn1_w1, ffn1_w3, ffn1_w2, mix_norm, w_in, cmp_pe_k, cmp_w1_k, cmp_b1_k, cmp_w2_k,
           cmp_pe_v, cmp_w1_v, cmp_b1_v, cmp_w2_v, rel_bias, ssm_lam_re, ssm_lam_im, ssm_log_step, ssm_b_re,
           ssm_b_im, ssm_c_re, ssm_c_im, ssm_d, glu_w, glu_b, w_out, ffn2_norm, ffn2_w1, ffn2_w3, ffn2_w2,
           final_norm):
    b, t, d = x.shape
    depth = ffn1_w1.shape[0]
    h = x.reshape(b * t, d)
    for l in range(depth):
        last = l == depth - 1
        h = _ffn(h, ffn1_norm[l], ffn1_w1[l].astype(BF16), ffn1_w3[l].astype(BF16), ffn1_w2[l].astype(BF16))
        ssm = (ssm_lam_re[l], ssm_lam_im[l], ssm_log_step[l], ssm_b_re[l], ssm_b_im[l], ssm_c_re[l], ssm_c_im[l])
        h = _mixers(h, mix_norm[l], w_in[l],
                    (cmp_pe_k[l], cmp_w1_k[l], cmp_b1_k[l], cmp_w2_k[l]),
                    (cmp_pe_v[l], cmp_w1_v[l], cmp_b1_v[l], cmp_w2_v[l]),
                    rel_bias, ssm, ssm_d[l], glu_w[l], glu_b[l], w_out[l], b, t)
        h = _ffn(h, ffn2_norm[l], ffn2_w1[l].astype(BF16), ffn2_w3[l].astype(BF16), ffn2_w2[l].astype(BF16),
                 final_gain=final_norm if last else None)
    return h.reshape(b, t, d)
```

```python
import functools
import math

import jax
import jax.numpy as jnp
from jax import lax
from jax.experimental import pallas as pl
from jax.experimental.pallas import tpu as pltpu

F32 = jnp.float32
BF16 = jnp.bfloat16

HEAD_DIM = 128
N_KV = 2
HEADS_PER_KV = 4
N_HEADS = N_KV * HEADS_PER_KV
D_ATTN = N_HEADS * HEAD_DIM
D_KV = N_KV * HEAD_DIM
CMP_LEN = 32
CMP_STRIDE = 16
SEL_BLOCK = 64
N_SELECT = 16
WINDOW = 512
N_BUCKETS = 32
MAX_DISTANCE = 128
SSM_GROUP = 16
SSM_STATE = 64
EPS = 1e-6
NEG = -1e30

LANES = 128
VMEM_LIMIT_BYTES = 56 * 1024 * 1024
TQ = 128
ROWS4 = HEADS_PER_KV * TQ
SEL_CHUNK = 512
SSM_CHUNK = 16


def _cparams(sem):
    return pltpu.CompilerParams(dimension_semantics=sem, vmem_limit_bytes=VMEM_LIMIT_BYTES)


def _ffn_body(x_ref, g_ref, w1_ref, w3_ref, w2_ref, *rest, final):
    if final:
        fg_ref, o_ref, xn_ref = rest
    else:
        o_ref, xn_ref = rest
    j = pl.program_id(1)
    nj = pl.num_programs(1)

    @pl.when(j == 0)
    def _():
        x = x_ref[...]
        ms = jnp.mean(x * x, axis=-1, keepdims=True)
        xn_ref[...] = (x * lax.rsqrt(ms + EPS) * g_ref[...]).astype(BF16)
        o_ref[...] = jnp.zeros(o_ref.shape, F32)

    xn = xn_ref[...]
    a = jnp.dot(xn, w1_ref[...], preferred_element_type=F32)
    b = jnp.dot(xn, w3_ref[...], preferred_element_type=F32)
    gated = (a * jax.nn.sigmoid(a)) * b
    o_ref[...] += jnp.dot(gated.astype(BF16), w2_ref[...], preferred_element_type=F32)

    @pl.when(j == nj - 1)
    def _():
        h = x_ref[...] + 0.5 * o_ref[...]
        if final:
            ms = jnp.mean(h * h, axis=-1, keepdims=True)
            h = h * lax.rsqrt(ms + EPS) * fg_ref[...]
        o_ref[...] = h


def _ffn(x, gain, w1, w3, w2, final_gain=None, *, tm=512, tf=512):
    n, d = x.shape
    dff = w1.shape[1]
    final = final_gain is not None
    in_specs = [
        pl.BlockSpec((tm, d), lambda i, j: (i, 0)),
        pl.BlockSpec((1, d), lambda i, j: (0, 0)),
        pl.BlockSpec((d, tf), lambda i, j: (0, j)),
        pl.BlockSpec((d, tf), lambda i, j: (0, j)),
        pl.BlockSpec((tf, d), lambda i, j: (j, 0)),
    ]
    args = [x, gain.reshape(1, d), w1, w3, w2]
    if final:
        in_specs.append(pl.BlockSpec((1, d), lambda i, j: (0, 0)))
        args.append(final_gain.reshape(1, d))
    return pl.pallas_call(
        functools.partial(_ffn_body, final=final),
        grid=(n // tm, dff // tf),
        in_specs=in_specs,
        out_specs=pl.BlockSpec((tm, d), lambda i, j: (i, 0)),
        out_shape=jax.ShapeDtypeStruct((n, d), F32),
        scratch_shapes=[pltpu.VMEM((tm, d), BF16)],
        compiler_params=_cparams(("parallel", "arbitrary")),
        name="ffn_final" if final else "ffn",
    )(*args)


def _inproj_body(x_ref, g_ref, w_ref, q_ref, kvs_ref, kcv_ref, u_ref, gate_ref):
    i = pl.program_id(1)

    @pl.when(i == 0)
    def _():
        kvs_ref[...] = jnp.zeros(kvs_ref.shape, kvs_ref.dtype)
        kcv_ref[...] = jnp.zeros(kcv_ref.shape, kcv_ref.dtype)

    @pl.when(i > 0)
    def _():
        x = x_ref[...]
        ms = jnp.mean(x * x, axis=-1, keepdims=True)
        xn = (x * lax.rsqrt(ms + EPS) * g_ref[...]).astype(BF16)
        c0 = 0
        c1 = q_ref.shape[1]
        q = jnp.dot(xn, w_ref[:, c0:c1], preferred_element_type=F32)
        q_ref[...] = (q * (HEAD_DIM ** -0.5)).astype(BF16)
        c0, c1 = c1, c1 + kcv_ref.shape[1]
        kcv_ref[...] = jnp.dot(xn, w_ref[:, c0:c1], preferred_element_type=F32)
        c0, c1 = c1, c1 + kvs_ref.shape[1]
        kvs_ref[...] = jnp.dot(xn, w_ref[:, c0:c1], preferred_element_type=F32).astype(BF16)
        c0, c1 = c1, c1 + u_ref.shape[1]
        u_ref[...] = jnp.dot(xn, w_ref[:, c0:c1], preferred_element_type=F32)
        gate_ref[...] = jax.nn.sigmoid(jnp.dot(xn, w_ref[:, c1:], preferred_element_type=F32))


def _inproj(h, gain, w_perm, d_ssm):
    b, t, d = h.shape
    tm = WINDOW
    nt = t // tm
    nq, ncv, nvs, ng = D_ATTN, 2 * D_KV, 4 * D_KV, N_KV * LANES
    ncol = w_perm.shape[1]
    assert ncol == nq + ncv + nvs + d_ssm + ng and t % tm == 0
    data = lambda bi, i: (bi, jnp.maximum(i - 1, 0), 0)
    return pl.pallas_call(
        _inproj_body,
        grid=(b, nt + 1),
        in_specs=[
            pl.BlockSpec((None, tm, d), data),
            pl.BlockSpec((1, d), lambda bi, i: (0, 0)),
            pl.BlockSpec((d, ncol), lambda bi, i: (0, 0)),
        ],
        out_specs=[
            pl.BlockSpec((None, tm, nq), data),
            pl.BlockSpec((None, tm, nvs), lambda bi, i: (bi, i, 0)),
            pl.BlockSpec((None, tm, ncv), lambda bi, i: (bi, jnp.where(i == 0, nt, i - 1), 0)),
            pl.BlockSpec((None, tm, d_ssm), data),
            pl.BlockSpec((None, tm, ng), data),
        ],
        out_shape=[
            jax.ShapeDtypeStruct((b, t, nq), BF16),
            jax.ShapeDtypeStruct((b, t + tm, nvs), BF16),
            jax.ShapeDtypeStruct((b, t + tm, ncv), F32),
            jax.ShapeDtypeStruct((b, t, d_ssm), F32),
            jax.ShapeDtypeStruct((b, t, ng), F32),
        ],
        compiler_params=_cparams(("parallel", "arbitrary")),
        name="inproj",
    )(h, gain.reshape(1, d), w_perm)


def _compress_body(x_ref, pe_ref, w1_ref, b1_ref, w2_ref, o_ref):
    nb, nblk = o_ref.shape[0], o_ref.shape[1]
    acc = jnp.broadcast_to(b1_ref[...], (nb * nblk, b1_ref.shape[1])).astype(F32)
    for l in range(CMP_LEN):
        xl = x_ref[:, pl.ds(l, nblk, stride=CMP_STRIDE), :] + pe_ref[l:l + 1, :]
        acc = acc + jnp.dot(xl.reshape(nb * nblk, xl.shape[2]).astype(BF16), w1_ref[l], preferred_element_type=F32)
    hid = jax.nn.gelu(acc, approximate=True)
    out = jnp.dot(hid.astype(BF16), w2_ref[...], preferred_element_type=F32)
    o_ref[...] = out.reshape(o_ref.shape).astype(BF16)


def _compress(kcv, pe, w1, b1, w2, nblk, *, nb=4):
    b, tpad, _ = kcv.shape
    nb = min(nb, b)
    hid, dh = w1.shape[3], w2.shape[2]
    assert (nblk - 1) * CMP_STRIDE + CMP_LEN <= tpad
    return pl.pallas_call(
        _compress_body,
        grid=(2 * N_KV, b // nb),
        in_specs=[
            pl.BlockSpec((nb, tpad, dh), lambda s, i: (i, 0, s)),
            pl.BlockSpec((None, CMP_LEN, dh), lambda s, i: (s // N_KV, 0, 0)),
            pl.BlockSpec((None, CMP_LEN, dh, hid), lambda s, i: (s // N_KV, 0, 0, 0)),
            pl.BlockSpec((None, 1, hid), lambda s, i: (s // N_KV, 0, 0)),
            pl.BlockSpec((None, hid, dh), lambda s, i: (s // N_KV, 0, 0)),
        ],
        out_specs=pl.BlockSpec((None, None, nb, nblk, dh), lambda s, i: (s // N_KV, s % N_KV, i, 0, 0)),
        out_shape=jax.ShapeDtypeStruct((2, N_KV, b, nblk, dh), BF16),
        compiler_params=_cparams(("parallel", "parallel")),
        name="compress",
    )(kcv, pe, w1, b1, w2)


def _nt_dot(a, b):
    return lax.dot_general(a, b, (((1,), (1,)), ((), ())), preferred_element_type=F32)


def _nsa_body(q_ref, ks_ref, vs_ref, kw_ref, vw_ref, kcb_ref, vcb_ref, gate_ref, biasc_ref, dt_ref, wb_ref,
              ovt_ref, et_ref, o_ref, kse_ref, s_ref, m_ref, l_ref, acc_ref):
    i = pl.program_id(1)
    t0 = i * TQ
    t = et_ref.shape[0]
    ns = t // SEL_BLOCK
    nsel = min(N_SELECT, ns)
    groups = range(N_KV)
    gl = lambda g: slice(g * HEAD_DIM, (g + 1) * HEAD_DIM)

    @pl.when(i == 0)
    def _():
        for g in groups:
            kse_ref[g, :, 0:HEAD_DIM] = ks_ref[WINDOW:WINDOW + t, gl(g)]
            kse_ref[g, :, HEAD_DIM:] = et_ref[...]

    q = q_ref[...]
    w0 = pl.multiple_of(t0, LANES)
    span = wb_ref.shape[2]

    q4 = [jnp.concatenate([q[:, (g * HEADS_PER_KV + h) * HEAD_DIM:(g * HEADS_PER_KV + h + 1) * HEAD_DIM]
                           for h in range(HEADS_PER_KV)], axis=0) for g in groups]

    sc = [_nt_dot(q4[g], kcb_ref[g]) + biasc_ref[g] for g in groups]
    row_t = t0 + (lax.broadcasted_iota(jnp.int32, (ROWS4, LANES), 0) & (TQ - 1))
    col_c = lax.broadcasted_iota(jnp.int32, (ROWS4, LANES), 1)
    valid_c = (row_t - col_c * CMP_STRIDE - (CMP_LEN - 1) >= 0) & (col_c < LANES - 1)
    mc = [jnp.max(s, axis=-1, keepdims=True) for s in sc]
    pc = [jnp.where(valid_c, jnp.exp(s - m), 0.0) for s, m in zip(sc, mc)]
    pc = [p / jnp.maximum(jnp.sum(p, axis=-1, keepdims=True), 1e-30) for p in pc]
    pcb = [p.astype(BF16) for p in pc]
    o_cmps = [jnp.dot(pcb[g], vcb_ref[g], preferred_element_type=F32) for g in groups]
    pimp = [_nt_dot(ovt_ref[...], p) for p in pcb]
    imp = [sum(p[0:ns, h * TQ:(h + 1) * TQ] for h in range(1, HEADS_PER_KV)) + p[0:ns, 0:TQ] for p in pimp]

    jrow = lax.broadcasted_iota(jnp.int32, (ns, TQ), 0)
    tpos = t0 + lax.broadcasted_iota(jnp.int32, (ns, TQ), 1)
    cur = lax.shift_right_logical(tpos, int(math.log2(SEL_BLOCK)))
    forced = (jrow == 0) | (jrow == cur) | (jrow == cur - 1)
    impm = [jnp.where(forced, 1e6, jnp.where(jrow * SEL_BLOCK <= tpos, x, -1e9)) for x in imp]
    nslab = ns // 8
    slabs = [[x[8 * v:8 * v + 8] for v in range(nslab)] for x in impm]
    cnts = [[jnp.zeros((8, TQ), F32) for _ in range(nslab)] for _ in groups]
    sub = lax.broadcasted_iota(jnp.int32, (8, TQ), 0)
    for jp in range(ns):
        v0, r0 = divmod(jp, 8)
        for g in groups:
            row = jnp.broadcast_to(slabs[g][v0][r0:r0 + 1, :], (8, TQ))
            for v in range(nslab):
                if v > v0:
                    beats = row >= slabs[g][v]
                elif v < v0:
                    beats = row > slabs[g][v]
                else:
                    beats = (row > slabs[g][v]) | ((row == slabs[g][v]) & (sub > r0))
                cnts[g][v] = cnts[g][v] + jnp.where(beats, 1.0, 0.0)
    q4s = []
    for g in groups:
        negt = [jnp.where(c < nsel, 0.0, NEG) for c in cnts[g]]
        negt = jnp.concatenate(negt + [jnp.zeros((LANES - ns, TQ), F32)], axis=0)
        neg = negt.T.astype(BF16)
        q4s.append(jnp.concatenate([q4[g], jnp.concatenate([neg] * HEADS_PER_KV, axis=0)], axis=1))

    nsub = SEL_CHUNK // LANES
    nchunks = i // nsub + 1
    m_ref[...] = jnp.full(m_ref.shape, -3e38, F32)

    def score_chunk(ck, _):
        kb = pl.multiple_of(ck * SEL_CHUNK, SEL_CHUNK)
        s = [_nt_dot(q4s[g], kse_ref[g, pl.ds(kb, SEL_CHUNK), :]) for g in groups]
        for g in groups:
            mloc = None
            for j in range(nsub):
                sj = s[g][:, j * LANES:(j + 1) * LANES] + dt_ref[g, jnp.clip(ck * nsub + j - i + 2, 0, 3)]
                s_ref[g, ck, :, j * LANES:(j + 1) * LANES] = sj
                mloc = sj if mloc is None else jnp.maximum(mloc, sj)
            m_ref[g] = jnp.maximum(m_ref[g], mloc)
        return 0

    lax.fori_loop(0, nchunks, score_chunk, 0)
    for g in groups:
        m_ref[g] = jnp.broadcast_to(jnp.max(m_ref[g], axis=-1, keepdims=True), (ROWS4, LANES))
    l_ref[...] = jnp.zeros(l_ref.shape, F32)
    acc_ref[...] = jnp.zeros(acc_ref.shape, F32)

    kneg = jnp.where(t0 - WINDOW + lax.broadcasted_iota(jnp.int32, (1, span), 1) >= 0, 0.0, NEG)
    sw = [_nt_dot(q4[g], kw_ref[pl.ds(w0, span), gl(g)]) + wb_ref[g] + kneg for g in groups]
    mw = [jnp.max(s, axis=-1, keepdims=True) for s in sw]
    pw = [jnp.exp(s - m) for s, m in zip(sw, mw)]
    lw = [jnp.sum(p, axis=-1, keepdims=True) for p in pw]
    o_wins = [jnp.dot(pw[g].astype(BF16), vw_ref[pl.ds(w0, span), gl(g)], preferred_element_type=F32) for g in groups]
    o_wins = [o / jnp.maximum(l, 1e-30) for o, l in zip(o_wins, lw)]

    def pv_chunk(ck, _):
        kb = pl.multiple_of(WINDOW + ck * SEL_CHUNK, SEL_CHUNK)
        ps = [[jnp.exp(s_ref[g, ck, :, j * LANES:(j + 1) * LANES] - m_ref[g]) for j in range(nsub)] for g in groups]
        for g in groups:
            lsum = ps[g][0]
            for pj in ps[g][1:]:
                lsum = lsum + pj
            l_ref[g] += lsum
            p = jnp.concatenate([pj.astype(BF16) for pj in ps[g]], axis=1)
            acc_ref[g] += jnp.dot(p, vs_ref[pl.ds(kb, SEL_CHUNK), gl(g)], preferred_element_type=F32)
        return 0

    lax.fori_loop(0, nchunks, pv_chunk, 0)

    ls = [jnp.maximum(jnp.sum(l_ref[g], axis=-1, keepdims=True), 1e-30) for g in groups]
    o_sels = [acc_ref[g] / ls[g] for g in groups]
    gts = [gate_ref[:, g * LANES:(g + 1) * LANES] for g in groups]
    for h in range(HEADS_PER_KV):
        r = slice(h * TQ, (h + 1) * TQ)
        for g in groups:
            gt = gts[g]
            o_h = (gt[:, 3 * h:3 * h + 1] * o_cmps[g][r] + gt[:, 3 * h + 1:3 * h + 2] * o_sels[g][r]
                   + gt[:, 3 * h + 2:3 * h + 3] * o_wins[g][r])
            c0 = (g * HEADS_PER_KV + h) * HEAD_DIM
            o_ref[:, c0:c0 + HEAD_DIM] = o_h.astype(BF16)


def _nsa(q, kvs, cb, gates, biasc, dtiles, wb, ovt, et):
    b, t, _ = q.shape
    nt = t // TQ
    tp = kvs.shape[1]
    span = wb.shape[2]
    gw = HEADS_PER_KV * HEAD_DIM
    assert tp == t + WINDOW and span == TQ + WINDOW and t % SEL_CHUNK == 0 and (t // SEL_BLOCK) % 8 == 0

    def stream(j):
        return pl.BlockSpec((None, tp, N_KV * HEAD_DIM), lambda bi, i: (bi, 0, j))

    return pl.pallas_call(
        _nsa_body,
        grid=(b, nt),
        in_specs=[
            pl.BlockSpec((None, TQ, N_KV * gw), lambda bi, i: (bi, i, 0)),
            stream(0), stream(1), stream(2), stream(3),
            pl.BlockSpec((None, N_KV, None, LANES, HEAD_DIM), lambda bi, i: (0, 0, bi, 0, 0)),
            pl.BlockSpec((None, N_KV, None, LANES, HEAD_DIM), lambda bi, i: (1, 0, bi, 0, 0)),
            pl.BlockSpec((None, TQ, N_KV * LANES), lambda bi, i: (bi, i, 0)),
            pl.BlockSpec((N_KV, None, ROWS4, LANES), lambda bi, i: (0, i, 0, 0)),
            pl.BlockSpec((N_KV, 4, ROWS4, LANES), lambda bi, i: (0, 0, 0, 0)),
            pl.BlockSpec((N_KV, ROWS4, span), lambda bi, i: (0, 0, 0)),
            pl.BlockSpec((LANES, LANES), lambda bi, i: (0, 0)),
            pl.BlockSpec((t, LANES), lambda bi, i: (0, 0)),
        ],
        out_specs=pl.BlockSpec((None, TQ, N_KV * gw), lambda bi, i: (bi, i, 0)),
        out_shape=jax.ShapeDtypeStruct((b, t, N_KV * gw), BF16),
        scratch_shapes=[
            pltpu.VMEM((N_KV, t, 2 * HEAD_DIM), BF16),
            pltpu.VMEM((N_KV, t // SEL_CHUNK, ROWS4, SEL_CHUNK), F32),
            pltpu.VMEM((N_KV, ROWS4, LANES), F32),
            pltpu.VMEM((N_KV, ROWS4, LANES), F32),
            pltpu.VMEM((N_KV, ROWS4, HEAD_DIM), F32),
        ],
        compiler_params=_cparams(("parallel", "arbitrary")),
        name="nsa",
    )(q, kvs, kvs, kvs, kvs, cb, cb, gates, biasc, dtiles, wb, ovt, et)


def _s5_body(u_ref, kc_ref, winc_ref, woutc_ref, rep_k_ref, rep_in_ref, rep_out_ref, m_k_ref, m_in_ref, m_out_ref,
             al_ref, d_ref, y_ref, ucat_ref, inj_ref, xp_ref, grev_ref, win_ref, wout_ref):
    nb, t, lanes = u_ref.shape
    L = SSM_CHUNK
    nchunk = t // L
    rows = nb * nchunk
    sdim = al_ref.shape[1] // 2

    @pl.when(pl.program_id(1) == 0)
    def _():
        m_k, m_in, m_out = m_k_ref[...], m_in_ref[...], m_out_ref[...]
        kx = jnp.dot(kc_ref[...], rep_k_ref[...], preferred_element_type=F32)
        zero = jnp.zeros((lanes, lanes), BF16)

        def kblk(tau):
            return (kx[tau * lanes:(tau + 1) * lanes] * m_k).astype(BF16) if 0 <= tau < L else zero

        for r in range(L + 1):
            grev_ref[r * lanes:(r + 1) * lanes, 0:lanes] = kblk(L - r - 1)
            grev_ref[r * lanes:(r + 1) * lanes, lanes:] = kblk(L - r)
        for half in range(2):
            for s in range(L):
                x = jnp.dot(winc_ref[half, s * lanes:(s + 1) * lanes, :], rep_in_ref[...], preferred_element_type=F32)
                win_ref[s * lanes:(s + 1) * lanes, half * sdim:(half + 1) * sdim] = (x * m_in).astype(BF16)
            for tt in range(L):
                x = jnp.dot(woutc_ref[half], rep_out_ref[:, tt * lanes:(tt + 1) * lanes], preferred_element_type=F32)
                wout_ref[half * sdim:(half + 1) * sdim, tt * lanes:(tt + 1) * lanes] = (x * m_out).astype(BF16)

    def u_at(s):
        return u_ref[:, pl.ds(s, nchunk, stride=L), :].reshape(rows, lanes)

    for s in range(L):
        ucat_ref[:, s * lanes:(s + 1) * lanes] = u_at(s).astype(BF16)
    inj = jnp.dot(ucat_ref[...], win_ref[...], preferred_element_type=F32)
    npl = sdim // lanes
    for k in range(2 * npl):
        inj_ref[k] = inj[:, k * lanes:(k + 1) * lanes]
    ar = [jnp.broadcast_to(al_ref[0:1, k * lanes:(k + 1) * lanes], (nb, lanes)) for k in range(npl)]
    ai = [jnp.broadcast_to(al_ref[0:1, sdim + k * lanes:sdim + (k + 1) * lanes], (nb, lanes)) for k in range(npl)]

    def step(c, carry):
        xr, xi = carry
        rsel = pl.ds(c, nb, stride=nchunk)
        nr, ni = [], []
        for k in range(npl):
            xp_ref[k, rsel, :] = xr[k]
            xp_ref[npl + k, rsel, :] = xi[k]
            nr.append(ar[k] * xr[k] - ai[k] * xi[k] + inj_ref[k, rsel, :])
            ni.append(ar[k] * xi[k] + ai[k] * xr[k] + inj_ref[npl + k, rsel, :])
        return tuple(nr), tuple(ni)

    z = tuple(jnp.zeros((nb, lanes), F32) for _ in range(npl))
    lax.fori_loop(0, nchunk, step, (z, z), unroll=4)
    xp = jnp.concatenate([xp_ref[k] for k in range(2 * npl)], axis=1).astype(BF16)
    d = d_ref[...]
    for tp in range(0, L, 2):
        res = jnp.dot(ucat_ref[:, 0:(tp + 2) * lanes], grev_ref[(L - 1 - tp) * lanes:(L + 1) * lanes, :],
                      preferred_element_type=F32)
        res = res + jnp.dot(xp, wout_ref[:, tp * lanes:(tp + 2) * lanes], preferred_element_type=F32)
        for k in range(2):
            y = res[:, k * lanes:(k + 1) * lanes] + d * u_at(tp + k)
            y_ref[:, pl.ds(tp + k, nchunk, stride=L), :] = y.reshape(nb, nchunk, lanes)


def _s5(u, kc, winc, woutc, al, d, *, nb=4):
    b, t, dch = u.shape
    nb = min(nb, b)
    nblk = dch // LANES
    L = SSM_CHUNK
    rows = nb * (t // L)
    hch, p = kc.shape[2], winc.shape[3]
    gpb = LANES // hch
    sdim = gpb * p
    kcat = L * LANES
    lane, col, st = jnp.arange(LANES), jnp.arange(kcat), jnp.arange(sdim)
    rep_k = (jnp.arange(hch)[:, None] == (lane % hch)[None, :]).astype(BF16)
    rep_in = (jnp.arange(p)[:, None] == (st % p)[None, :]).astype(BF16)
    rep_out = (jnp.arange(L * hch)[:, None] == (col // LANES * hch + col % hch)[None, :]).astype(BF16)
    m_k = ((lane // hch)[:, None] == (lane // hch)[None, :]).astype(F32)
    m_in = ((lane // hch)[:, None] == (st // p)[None, :]).astype(F32)
    m_out = ((st // p)[:, None] == (lane // hch)[None, :]).astype(F32)
    const = lambda a: pl.BlockSpec(a.shape, lambda j, i: (0,) * a.ndim)
    return pl.pallas_call(
        _s5_body,
        grid=(nblk, b // nb),
        in_specs=[
            pl.BlockSpec((nb, t, LANES), lambda j, i: (i, 0, j)),
            pl.BlockSpec((None, kcat, hch), lambda j, i: (j, 0, 0)),
            pl.BlockSpec((None, 2, kcat, p), lambda j, i: (j, 0, 0, 0)),
            pl.BlockSpec((None, 2, sdim, L * hch), lambda j, i: (j, 0, 0, 0)),
            const(rep_k), const(rep_in), const(rep_out), const(m_k), const(m_in), const(m_out),
            pl.BlockSpec((None, 8, 2 * sdim), lambda j, i: (j, 0, 0)),
            pl.BlockSpec((1, LANES), lambda j, i: (0, j)),
        ],
        out_specs=pl.BlockSpec((nb, t, LANES), lambda j, i: (i, 0, j)),
        out_shape=jax.ShapeDtypeStruct((b, t, dch), F32),
        scratch_shapes=[pltpu.VMEM((rows, kcat), BF16), pltpu.VMEM((2 * sdim // LANES, rows, LANES), F32),
                        pltpu.VMEM((2 * sdim // LANES, rows, LANES), F32),
                        pltpu.VMEM(((L + 1) * LANES, 2 * LANES), BF16), pltpu.VMEM((kcat, 2 * sdim), BF16),
                        pltpu.VMEM((2 * sdim, kcat), BF16)],
        compiler_params=_cparams(("parallel", "arbitrary")),
        name="s5",
    )(u, kc, winc, woutc, rep_k, rep_in, rep_out, m_k, m_in, m_out, al, d.reshape(1, dch))


def _glu_out_body(h_ref, a_ref, y_ref, wg_ref, bg_ref, woa_ref, wos_ref, o_ref):
    hg = jax.nn.gelu(y_ref[...], approximate=True)
    z = jnp.dot(hg.astype(BF16), wg_ref[...], preferred_element_type=F32) + bg_ref[...]
    s = hg * jax.nn.sigmoid(z)
    mix = jnp.dot(a_ref[...], woa_ref[...], preferred_element_type=F32)
    mix = mix + jnp.dot(s.astype(BF16), wos_ref[...], preferred_element_type=F32)
    o_ref[...] = h_ref[...] + mix


def _glu_out(h, a, y, wg, bg, woa, wos, *, tm=512):
    n, dm = h.shape
    da, ds = a.shape[1], y.shape[1]
    row = lambda w: pl.BlockSpec((tm, w), lambda i: (i, 0))
    full = lambda r, c: pl.BlockSpec((r, c), lambda i: (0, 0))
    return pl.pallas_call(
        _glu_out_body,
        grid=(n // tm,),
        in_specs=[row(dm), row(da), row(ds), full(ds, ds), full(1, ds), full(da, dm), full(ds, dm)],
        out_specs=row(dm),
        out_shape=jax.ShapeDtypeStruct((n, dm), F32),
        compiler_params=_cparams(("parallel",)),
        name="glu_out",
    )(h, a, y, wg, bg.reshape(1, ds), woa, wos)


def _t5_bucket(dist):
    n = jnp.maximum(dist, 0)
    max_exact = N_BUCKETS // 2
    nf = jnp.maximum(n, 1).astype(F32)
    large = max_exact + (jnp.log(nf / max_exact) / math.log(MAX_DISTANCE / max_exact)
                         * (N_BUCKETS - max_exact)).astype(jnp.int32)
    large = jnp.minimum(large, N_BUCKETS - 1)
    return jnp.where(n < max_exact, n, large)


def _t5_body(first_ref, table_ref, bc_ref, dt_ref, wb_ref, *, nc):
    g = pl.program_id(0)
    nt = bc_ref.shape[0]
    span = wb_ref.shape[1]
    a = lax.broadcasted_iota(jnp.int32, (TQ, LANES), 0)
    c = lax.broadcasted_iota(jnp.int32, (TQ, LANES), 1)
    heads = [g * HEADS_PER_KV + h for h in range(HEADS_PER_KV)]

    def lookup(dist, valid):
        bias = [jnp.full((TQ, LANES), table_ref[0, hd], F32) for hd in heads]
        for k in range(1, N_BUCKETS):
            ge = dist >= first_ref[k]
            bias = [jnp.where(ge, table_ref[k, hd], bv) for hd, bv in zip(heads, bias)]
        return bias if valid is None else [jnp.where(valid, bv, NEG) for bv in bias]

    def put(ref_at, tiles):
        for h, tile in enumerate(tiles):
            ref_at(slice(h * TQ, (h + 1) * TQ), tile)

    def put_dt(v):
        def f(rows, tile):
            dt_ref[v, rows, :] = tile
        return f

    put(put_dt(0), lookup(jnp.full((TQ, LANES), 2 * TQ, jnp.int32), None))
    put(put_dt(1), lookup(TQ + a - c, None))
    put(put_dt(2), lookup(a - c, a - c >= 0))
    dt_ref[3] = jnp.full((ROWS4, LANES), NEG, F32)
    for j in range(span // LANES):
        dist = a + WINDOW - (c + j * LANES)

        def put_wb(rows, tile, j=j):
            wb_ref[rows, j * LANES:(j + 1) * LANES] = tile

        put(put_wb, lookup(dist, (dist >= 0) & (dist < WINDOW)))

    def cmp_tile(i, _):
        dist = i * TQ + a - (c * CMP_STRIDE + CMP_LEN - 1)

        def put_bc(rows, tile):
            bc_ref[i, rows, :] = tile

        put(put_bc, lookup(dist, (dist >= 0) & (c < nc)))
        return 0

    lax.fori_loop(0, nt, cmp_tile, 0)


def _bias_tables(rel_bias, t):
    nt = t // TQ
    span = TQ + WINDOW
    nc = (t - CMP_LEN) // CMP_STRIDE + 1
    buckets = _t5_bucket(jnp.arange(2 * MAX_DISTANCE))
    first = jnp.sum((buckets[None, :] < jnp.arange(N_BUCKETS)[:, None]).astype(jnp.int32), axis=1)
    smem = pl.BlockSpec(memory_space=pltpu.SMEM)
    return pl.pallas_call(
        functools.partial(_t5_body, nc=nc),
        grid=(N_KV,),
        in_specs=[smem, smem],
        out_specs=[
            pl.BlockSpec((None, nt, ROWS4, LANES), lambda g: (g, 0, 0, 0)),
            pl.BlockSpec((None, 4, ROWS4, LANES), lambda g: (g, 0, 0, 0)),
            pl.BlockSpec((None, ROWS4, span), lambda g: (g, 0, 0)),
        ],
        out_shape=[
            jax.ShapeDtypeStruct((N_KV, nt, ROWS4, LANES), F32),
            jax.ShapeDtypeStruct((N_KV, 4, ROWS4, LANES), F32),
            jax.ShapeDtypeStruct((N_KV, ROWS4, span), F32),
        ],
        compiler_params=_cparams(("parallel",)),
        name="t5_tables",
    )(first, rel_bias.astype(F32))


def _sel_tables(t):
    ns = t // SEL_BLOCK
    nc_pad = LANES
    c_start = jnp.arange(nc_pad) * CMP_STRIDE
    j_start = jnp.arange(LANES) * SEL_BLOCK
    ov = jnp.clip(jnp.minimum(c_start[:, None] + CMP_LEN, j_start[None, :] + SEL_BLOCK)
                  - jnp.maximum(c_start[:, None], j_start[None, :]), 0, None).astype(F32) / CMP_LEN
    ov = jnp.where(jnp.arange(LANES)[None, :] < ns, ov, 0.0)
    et = jnp.arange(t)[:, None] // SEL_BLOCK == jnp.arange(LANES)[None, :]
    return ov.T.astype(BF16), et.astype(BF16)


def _s5_tables(lam_re, lam_im, log_step, b_re, b_im, c_re, c_im):
    ng, p = lam_re.shape
    hch = b_re.shape[2]
    L = SSM_CHUNK
    step = jnp.exp(log_step.astype(F32))[:, None]
    lre, lim = lam_re.astype(F32), lam_im.astype(F32)
    mag = jnp.exp(lre * step)
    ab_re, ab_im = mag * jnp.cos(lim * step), mag * jnp.sin(lim * step)
    nr, ni = ab_re - 1.0, ab_im
    den = lre * lre + lim * lim
    f_re, f_im = (nr * lre + ni * lim) / den, (ni * lre - nr * lim) / den
    br, bim = b_re.astype(F32), b_im.astype(F32)
    bb_re = f_re[..., None] * br - f_im[..., None] * bim
    bb_im = f_re[..., None] * bim + f_im[..., None] * br
    cr, ci = c_re.astype(F32), c_im.astype(F32)
    pr, pi = [jnp.ones_like(ab_re)], [jnp.zeros_like(ab_re)]
    for _ in range(L):
        pr, pi = pr + [pr[-1] * ab_re - pi[-1] * ab_im], pi + [pr[-1] * ab_im + pi[-1] * ab_re]
    pw_re, pw_im = jnp.stack(pr, 0), jnp.stack(pi, 0)
    cp_re = cr[None] * pw_re[:, :, None, :] - ci[None] * pw_im[:, :, None, :]
    cp_im = -(cr[None] * pw_im[:, :, None, :] + ci[None] * pw_re[:, :, None, :])
    cpr, cpi = cp_re[:L].transpose(1, 0, 2, 3)[:, :, None], cp_im[:L].transpose(1, 0, 2, 3)[:, :, None]
    bbr, bbi = bb_re.transpose(0, 2, 1)[:, None, :, None], bb_im.transpose(0, 2, 1)[:, None, :, None]
    kern = jnp.sum(cpr * bbr + cpi * bbi, axis=-1)
    gpb = LANES // hch
    nblk = ng // gpb
    kc = kern.reshape(nblk, gpb, L, hch, hch).transpose(0, 2, 1, 3, 4).reshape(nblk, L * LANES, hch)
    wr = pw_re[L - 1 - jnp.arange(L)]
    wi = pw_im[L - 1 - jnp.arange(L)]
    win_re = (wr[..., None] * bb_re[None] - wi[..., None] * bb_im[None])
    win_im = (wr[..., None] * bb_im[None] + wi[..., None] * bb_re[None])
    inj_op = lambda m: m.reshape(L, nblk, gpb, p, hch).transpose(1, 0, 2, 4, 3).reshape(nblk, L * LANES, p)
    winc = jnp.stack([inj_op(win_re), inj_op(win_im)], axis=1)
    read_op = lambda m: m.reshape(L, nblk, gpb, hch, p).transpose(1, 2, 4, 0, 3).reshape(nblk, gpb * p, L * hch)
    woutc = jnp.stack([read_op(cp_re[1:]), read_op(cp_im[1:])], axis=1)
    al = jnp.concatenate([pw_re[L].reshape(nblk, gpb * p), pw_im[L].reshape(nblk, gpb * p)], axis=1)
    al = jnp.broadcast_to(al[:, None, :], (nblk, 8, 2 * gpb * p))
    return kc.astype(BF16), winc.astype(BF16), woutc.astype(BF16), al


def _mixers(h1, mix_norm, w_in, cmp_k, cmp_v, rel_bias, ssm, ssm_d, glu_w, glu_b, w_out, b, t):
    n, d = h1.shape
    d_ssm = glu_w.shape[0]
    o_g, o_u = D_ATTN + 6 * D_KV, D_ATTN + 6 * D_KV + 3 * N_HEADS
    gcols = 3 * HEADS_PER_KV
    wg = [jnp.pad(w_in[:, o_g + g * gcols:o_g + (g + 1) * gcols], ((0, 0), (0, LANES - gcols))) for g in range(N_KV)]
    w_perm = jnp.concatenate([w_in[:, :o_g], w_in[:, o_u:]] + wg, axis=1).astype(BF16)
    q, kvs, kcv, u, gates = _inproj(h1.reshape(b, t, d), mix_norm, w_perm, d_ssm)

    nblk = t // CMP_STRIDE
    assert nblk == LANES, "compressed-block axis is laid out on one 128-lane tile"
    pe = jnp.stack([cmp_k[0], cmp_v[0]], 0).astype(F32)
    w1 = jnp.stack([cmp_k[1], cmp_v[1]], 0).astype(BF16).reshape(2, CMP_LEN, HEAD_DIM, -1)
    b1 = jnp.stack([cmp_k[2].reshape(1, -1), cmp_v[2].reshape(1, -1)], 0).astype(F32)
    w2 = jnp.stack([cmp_k[3], cmp_v[3]], 0).astype(BF16)
    cb = _compress(kcv, pe, w1, b1, w2, nblk)

    biasc, dtiles, wb = _bias_tables(rel_bias, t)
    ovt, et = _sel_tables(t)
    a = _nsa(q, kvs, cb, gates, biasc, dtiles, wb, ovt, et)

    kc, winc, woutc, al = _s5_tables(*ssm)
    y = _s5(u, kc, winc, woutc, al, ssm_d)

    wo = w_out.astype(BF16)
    return _glu_out(h1, a.reshape(n, D_ATTN), y.reshape(n, d_ssm), glu_w.astype(BF16), glu_b, wo[:D_ATTN], wo[D_ATTN:])


def kernel(x, ffn1_norm, ffn1_w1, ffn1_w3, ffn1_w2, mix_norm, w_in, cmp_pe_k, cmp_w1_k, cmp_b1_k, cmp_w2_k,
           cmp_pe_v, cmp_w1_v, cmp_b1_v, cmp_w2_v, rel_bias, ssm_lam_re, ssm_lam_im, ssm_log_step, ssm_b_re,
           ssm_b_im, ssm_c_re, ssm_c_im, ssm_d, glu_w, glu_b, w_out, ffn2_norm, ffn2_w1, ffn2_w3, ffn2_w2,
           final_norm):
    b, t, d = x.shape
    depth = ffn1_w1.shape[0]
    h = x.reshape(b * t, d)
    for l in range(depth):
        last = l == depth - 1
        h = _ffn(h, ffn1_norm[l], ffn1_w1[l].astype(BF16), ffn1_w3[l].astype(BF16), ffn1_w2[l].astype(BF16))
        ssm = (ssm_lam_re[l], ssm_lam_im[l], ssm_log_step[l], ssm_b_re[l], ssm_b_im[l], ssm_c_re[l], ssm_c_im[l])
        h = _mixers(h, mix_norm[l], w_in[l],
                    (cmp_pe_k[l], cmp_w1_k[l], cmp_b1_k[l], cmp_w2_k[l]),
                    (cmp_pe_v[l], cmp_w1_v[l], cmp_b1_v[l], cmp_w2_v[l]),
                    rel_bias, ssm, ssm_d[l], glu_w[l], glu_b[l], w_out[l], b, t)
        h = _ffn(h, ffn2_norm[l], ffn2_w1[l].astype(BF16), ffn2_w3[l].astype(BF16), ffn2_w2[l].astype(BF16),
                 final_gain=final_norm if last else None)
    return h.reshape(b, t, d)
```

```python
import functools
import math

import jax
import jax.numpy as jnp
from jax import lax
from jax.experimental import pallas as pl
from jax.experimental.pallas import tpu as pltpu

F32 = jnp.float32
BF16 = jnp.bfloat16

HEAD_DIM = 128
N_KV = 2
HEADS_PER_KV = 4
N_HEADS = N_KV * HEADS_PER_KV
D_ATTN = N_HEADS * HEAD_DIM
D_KV = N_KV * HEAD_DIM
CMP_LEN = 32
CMP_STRIDE = 16
SEL_BLOCK = 64
N_SELECT = 16
WINDOW = 512
N_BUCKETS = 32
MAX_DISTANCE = 128
SSM_GROUP = 16
SSM_STATE = 64
EPS = 1e-6
NEG = -1e30

LANES = 128
VMEM_LIMIT_BYTES = 56 * 1024 * 1024
NSA_VMEM_LIMIT_BYTES = 60 * 1024 * 1024
TQ = 256
QK = TQ // LANES
ROWS4 = HEADS_PER_KV * TQ
SEL_CHUNK = 512
SSM_CHUNK = 16


def _cparams(sem, vmem_limit_bytes=VMEM_LIMIT_BYTES):
    return pltpu.CompilerParams(dimension_semantics=sem, vmem_limit_bytes=vmem_limit_bytes)


def _ffn_body(x_ref, g_ref, w1_ref, w3_ref, w2_ref, *rest, final):
    if final:
        fg_ref, o_ref, xn_ref = rest
    else:
        o_ref, xn_ref = rest
    j = pl.program_id(1)
    nj = pl.num_programs(1)

    def contribution(xn):
        a = jnp.dot(xn, w1_ref[...], preferred_element_type=F32)
        b = jnp.dot(xn, w3_ref[...], preferred_element_type=F32)
        gated = (a * jax.nn.sigmoid(a)) * b
        return jnp.dot(gated.astype(BF16), w2_ref[...], preferred_element_type=F32)

    @pl.when(j == 0)
    def _():
        x = x_ref[...]
        ms = jnp.mean(x * x, axis=-1, keepdims=True)
        xn = (x * lax.rsqrt(ms + EPS) * g_ref[...]).astype(BF16)
        xn_ref[...] = xn
        o_ref[...] = contribution(xn)

    @pl.when((j > 0) & (j < nj - 1))
    def _():
        o_ref[...] += contribution(xn_ref[...])

    @pl.when(j == nj - 1)
    def _():
        h = x_ref[...] + 0.5 * (o_ref[...] + contribution(xn_ref[...]))
        if final:
            ms = jnp.mean(h * h, axis=-1, keepdims=True)
            h = h * lax.rsqrt(ms + EPS) * fg_ref[...]
        o_ref[...] = h


def _ffn(x, gain, w1, w3, w2, final_gain=None, *, tm=512, tf=512):
    n, d = x.shape
    dff = w1.shape[1]
    final = final_gain is not None
    in_specs = [
        pl.BlockSpec((tm, d), lambda i, j: (i, 0)),
        pl.BlockSpec((1, d), lambda i, j: (0, 0)),
        pl.BlockSpec((d, tf), lambda i, j: (0, j)),
        pl.BlockSpec((d, tf), lambda i, j: (0, j)),
        pl.BlockSpec((tf, d), lambda i, j: (j, 0)),
    ]
    args = [x, gain.reshape(1, d), w1, w3, w2]
    if final:
        in_specs.append(pl.BlockSpec((1, d), lambda i, j: (0, 0)))
        args.append(final_gain.reshape(1, d))
    return pl.pallas_call(
        functools.partial(_ffn_body, final=final),
        grid=(n // tm, dff // tf),
        in_specs=in_specs,
        out_specs=pl.BlockSpec((tm, d), lambda i, j: (i, 0)),
        out_shape=jax.ShapeDtypeStruct((n, d), F32),
        scratch_shapes=[pltpu.VMEM((tm, d), BF16)],
        compiler_params=_cparams(("parallel", "arbitrary")),
        name="ffn_final" if final else "ffn",
    )(*args)


def _inproj_body(x_ref, g_ref, w_ref, q_ref, kvs_ref, kcv_ref, u_ref, gate_ref):
    i = pl.program_id(1)

    @pl.when(i == 0)
    def _():
        kvs_ref[...] = jnp.zeros(kvs_ref.shape, kvs_ref.dtype)
        kcv_ref[...] = jnp.zeros(kcv_ref.shape, kcv_ref.dtype)

    @pl.when(i > 0)
    def _():
        x = x_ref[...]
        ms = jnp.mean(x * x, axis=-1, keepdims=True)
        xn = (x * lax.rsqrt(ms + EPS) * g_ref[...]).astype(BF16)
        c0 = 0
        c1 = q_ref.shape[1]
        q = jnp.dot(xn, w_ref[:, c0:c1], preferred_element_type=F32)
        q_ref[...] = (q * (HEAD_DIM ** -0.5)).astype(BF16)
        c0, c1 = c1, c1 + kcv_ref.shape[1]
        kcv_ref[...] = jnp.dot(xn, w_ref[:, c0:c1], preferred_element_type=F32)
        c0, c1 = c1, c1 + kvs_ref.shape[1]
        kvs_ref[...] = jnp.dot(xn, w_ref[:, c0:c1], preferred_element_type=F32).astype(BF16)
        c0, c1 = c1, c1 + u_ref.shape[1]
        u_ref[...] = jnp.dot(xn, w_ref[:, c0:c1], preferred_element_type=F32)
        gate_ref[...] = jax.nn.sigmoid(jnp.dot(xn, w_ref[:, c1:], preferred_element_type=F32))


def _inproj(h, gain, w_perm, d_ssm):
    b, t, d = h.shape
    tm = WINDOW
    nt = t // tm
    nq, ncv, nvs, ng = D_ATTN, 2 * D_KV, 4 * D_KV, N_KV * LANES
    ncol = w_perm.shape[1]
    assert ncol == nq + ncv + nvs + d_ssm + ng and t % tm == 0
    data = lambda bi, i: (bi, jnp.maximum(i - 1, 0), 0)
    return pl.pallas_call(
        _inproj_body,
        grid=(b, nt + 1),
        in_specs=[
            pl.BlockSpec((None, tm, d), data),
            pl.BlockSpec((1, d), lambda bi, i: (0, 0)),
            pl.BlockSpec((d, ncol), lambda bi, i: (0, 0)),
        ],
        out_specs=[
            pl.BlockSpec((None, tm, nq), data),
            pl.BlockSpec((None, tm, nvs), lambda bi, i: (bi, i, 0)),
            pl.BlockSpec((None, tm, ncv), lambda bi, i: (bi, jnp.where(i == 0, nt, i - 1), 0)),
            pl.BlockSpec((None, tm, d_ssm), data),
            pl.BlockSpec((None, tm, ng), data),
        ],
        out_shape=[
            jax.ShapeDtypeStruct((b, t, nq), BF16),
            jax.ShapeDtypeStruct((b, t + tm, nvs), BF16),
            jax.ShapeDtypeStruct((b, t + tm, ncv), F32),
            jax.ShapeDtypeStruct((b, t, d_ssm), F32),
            jax.ShapeDtypeStruct((b, t, ng), F32),
        ],
        compiler_params=_cparams(("parallel", "arbitrary")),
        name="inproj",
    )(h, gain.reshape(1, d), w_perm)


def _compress_body(x_ref, pe_ref, w1_ref, b1_ref, w2_ref, o_ref):
    nb, nblk = o_ref.shape[0], o_ref.shape[1]
    acc = jnp.broadcast_to(b1_ref[...], (nb * nblk, b1_ref.shape[1])).astype(F32)
    for l in range(CMP_LEN):
        xl = x_ref[:, pl.ds(l, nblk, stride=CMP_STRIDE), :] + pe_ref[l:l + 1, :]
        acc = acc + jnp.dot(xl.reshape(nb * nblk, xl.shape[2]).astype(BF16), w1_ref[l], preferred_element_type=F32)
    hid = jax.nn.gelu(acc, approximate=True)
    out = jnp.dot(hid.astype(BF16), w2_ref[...], preferred_element_type=F32)
    o_ref[...] = out.reshape(o_ref.shape).astype(BF16)


def _compress(kcv, pe, w1, b1, w2, nblk, *, nb=4):
    b, tpad, _ = kcv.shape
    nb = min(nb, b)
    hid, dh = w1.shape[3], w2.shape[2]
    assert (nblk - 1) * CMP_STRIDE + CMP_LEN <= tpad
    return pl.pallas_call(
        _compress_body,
        grid=(2 * N_KV, b // nb),
        in_specs=[
            pl.BlockSpec((nb, tpad, dh), lambda s, i: (i, 0, s)),
            pl.BlockSpec((None, CMP_LEN, dh), lambda s, i: (s // N_KV, 0, 0)),
            pl.BlockSpec((None, CMP_LEN, dh, hid), lambda s, i: (s // N_KV, 0, 0, 0)),
            pl.BlockSpec((None, 1, hid), lambda s, i: (s // N_KV, 0, 0)),
            pl.BlockSpec((None, hid, dh), lambda s, i: (s // N_KV, 0, 0)),
        ],
        out_specs=pl.BlockSpec((None, None, nb, nblk, dh), lambda s, i: (s // N_KV, s % N_KV, i, 0, 0)),
        out_shape=jax.ShapeDtypeStruct((2, N_KV, b, nblk, dh), BF16),
        compiler_params=_cparams(("parallel", "parallel")),
        name="compress",
    )(kcv, pe, w1, b1, w2)


def _nt_dot(a, b):
    return lax.dot_general(a, b, (((1,), (1,)), ((), ())), preferred_element_type=F32)


def _nsa_body(q_ref, ks_ref, vs_ref, kw_ref, vw_ref, kcb_ref, vcb_ref, gate_ref, biasc_ref, dt_ref, wb_ref,
              ovt_ref, et_ref, o_ref, kse_ref, s_ref, m_ref, l_ref, acc_ref):
    i = pl.program_id(1)
    t0 = i * TQ
    t = et_ref.shape[0]
    ns = t // SEL_BLOCK
    nsel = min(N_SELECT, ns)
    groups = range(N_KV)
    gl = lambda g: slice(g * HEAD_DIM, (g + 1) * HEAD_DIM)

    @pl.when(i == 0)
    def _():
        for g in groups:
            kse_ref[g, :, 0:HEAD_DIM] = ks_ref[WINDOW:WINDOW + t, gl(g)]
            kse_ref[g, :, HEAD_DIM:] = et_ref[...]

    q = q_ref[...]
    w0 = pl.multiple_of(t0, LANES)
    span = wb_ref.shape[2]

    q4 = [jnp.concatenate([q[:, (g * HEADS_PER_KV + h) * HEAD_DIM:(g * HEADS_PER_KV + h + 1) * HEAD_DIM]
                           for h in range(HEADS_PER_KV)], axis=0) for g in groups]

    sc = [_nt_dot(q4[g], kcb_ref[g]) + biasc_ref[g] for g in groups]
    row_t = t0 + (lax.broadcasted_iota(jnp.int32, (ROWS4, LANES), 0) & (TQ - 1))
    col_c = lax.broadcasted_iota(jnp.int32, (ROWS4, LANES), 1)
    valid_c = (row_t - col_c * CMP_STRIDE - (CMP_LEN - 1) >= 0) & (col_c < LANES - 1)
    mc = [jnp.max(s, axis=-1, keepdims=True) for s in sc]
    pc = [jnp.where(valid_c, jnp.exp(s - m), 0.0) for s, m in zip(sc, mc)]
    pc = [p / jnp.maximum(jnp.sum(p, axis=-1, keepdims=True), 1e-30) for p in pc]
    pcb = [p.astype(BF16) for p in pc]
    o_cmps = [jnp.dot(pcb[g], vcb_ref[g], preferred_element_type=F32) for g in groups]
    pimp = [_nt_dot(ovt_ref[...], p) for p in pcb]
    imp = [sum(p[0:ns, h * TQ:(h + 1) * TQ] for h in range(1, HEADS_PER_KV)) + p[0:ns, 0:TQ] for p in pimp]

    jrow = lax.broadcasted_iota(jnp.int32, (ns, TQ), 0)
    tpos = t0 + lax.broadcasted_iota(jnp.int32, (ns, TQ), 1)
    cur = lax.shift_right_logical(tpos, int(math.log2(SEL_BLOCK)))
    forced = (jrow == 0) | (jrow == cur) | (jrow == cur - 1)
    impm = [jnp.where(forced, 1e6, jnp.where(jrow * SEL_BLOCK <= tpos, x, -1e9)) for x in imp]
    nslab = ns // 8
    slabs = [[x[8 * v:8 * v + 8] for v in range(nslab)] for x in impm]
    cnts = [[jnp.zeros((8, TQ), F32) for _ in range(nslab)] for _ in groups]
    sub = lax.broadcasted_iota(jnp.int32, (8, TQ), 0)
    for jp in range(ns):
        v0, r0 = divmod(jp, 8)
        for g in groups:
            row = jnp.broadcast_to(slabs[g][v0][r0:r0 + 1, :], (8, TQ))
            for v in range(nslab):
                if v > v0:
                    beats = row >= slabs[g][v]
                elif v < v0:
                    beats = row > slabs[g][v]
                else:
                    beats = (row > slabs[g][v]) | ((row == slabs[g][v]) & (sub > r0))
                cnts[g][v] = cnts[g][v] + jnp.where(beats, 1.0, 0.0)
    q4s = []
    for g in groups:
        negt = [jnp.where(c < nsel, 0.0, NEG) for c in cnts[g]]
        negt = jnp.concatenate(negt + [jnp.zeros((LANES - ns, TQ), F32)], axis=0)
        neg = negt.T.astype(BF16)
        q4s.append(jnp.concatenate([q4[g], jnp.concatenate([neg] * HEADS_PER_KV, axis=0)], axis=1))

    nsub = SEL_CHUNK // LANES
    nchunks = ((i + 1) * QK - 1) // nsub + 1
    m_ref[...] = jnp.full(m_ref.shape, -3e38, F32)

    def score_chunk(ck, _):
        kb = pl.multiple_of(ck * SEL_CHUNK, SEL_CHUNK)
        s = [_nt_dot(q4s[g], kse_ref[g, pl.ds(kb, SEL_CHUNK), :]) for g in groups]
        for g in groups:
            mloc = None
            for j in range(nsub):
                sj = s[g][:, j * LANES:(j + 1) * LANES] + dt_ref[g, jnp.clip(ck * nsub + j - i * QK + 2, 0, QK + 2)]
                s_ref[g, ck, :, j * LANES:(j + 1) * LANES] = sj
                mloc = sj if mloc is None else jnp.maximum(mloc, sj)
            m_ref[g] = jnp.maximum(m_ref[g], mloc)
        return 0

    lax.fori_loop(0, nchunks, score_chunk, 0)
    for g in groups:
        m_ref[g] = jnp.broadcast_to(jnp.max(m_ref[g], axis=-1, keepdims=True), (ROWS4, LANES))
    l_ref[...] = jnp.zeros(l_ref.shape, F32)
    acc_ref[...] = jnp.zeros(acc_ref.shape, F32)

    kneg = jnp.where(t0 - WINDOW + lax.broadcasted_iota(jnp.int32, (1, span), 1) >= 0, 0.0, NEG)
    sw = [_nt_dot(q4[g], kw_ref[pl.ds(w0, span), gl(g)]) + wb_ref[g] + kneg for g in groups]
    mw = [jnp.max(s, axis=-1, keepdims=True) for s in sw]
    pw = [jnp.exp(s - m) for s, m in zip(sw, mw)]
    lw = [jnp.sum(p, axis=-1, keepdims=True) for p in pw]
    o_wins = [jnp.dot(pw[g].astype(BF16), vw_ref[pl.ds(w0, span), gl(g)], preferred_element_type=F32) for g in groups]
    o_wins = [o / jnp.maximum(l, 1e-30) for o, l in zip(o_wins, lw)]

    def pv_chunk(ck, _):
        kb = pl.multiple_of(WINDOW + ck * SEL_CHUNK, SEL_CHUNK)
        ps = [[jnp.exp(s_ref[g, ck, :, j * LANES:(j + 1) * LANES] - m_ref[g]) for j in range(nsub)] for g in groups]
        for g in groups:
            lsum = ps[g][0]
            for pj in ps[g][1:]:
                lsum = lsum + pj
            l_ref[g] += lsum
            p = jnp.concatenate([pj.astype(BF16) for pj in ps[g]], axis=1)
            acc_ref[g] += jnp.dot(p, vs_ref[pl.ds(kb, SEL_CHUNK), gl(g)], preferred_element_type=F32)
        return 0

    lax.fori_loop(0, nchunks, pv_chunk, 0)

    ls = [jnp.maximum(jnp.sum(l_ref[g], axis=-1, keepdims=True), 1e-30) for g in groups]
    o_sels = [acc_ref[g] / ls[g] for g in groups]
    gts = [gate_ref[:, g * LANES:(g + 1) * LANES] for g in groups]
    for h in range(HEADS_PER_KV):
        r = slice(h * TQ, (h + 1) * TQ)
        for g in groups:
            gt = gts[g]
            o_h = (gt[:, 3 * h:3 * h + 1] * o_cmps[g][r] + gt[:, 3 * h + 1:3 * h + 2] * o_sels[g][r]
                   + gt[:, 3 * h + 2:3 * h + 3] * o_wins[g][r])
            c0 = (g * HEADS_PER_KV + h) * HEAD_DIM
            o_ref[:, c0:c0 + HEAD_DIM] = o_h.astype(BF16)


def _nsa(q, kvs, cb, gates, biasc, dtiles, wb, ovt, et):
    b, t, _ = q.shape
    nt = t // TQ
    tp = kvs.shape[1]
    span = wb.shape[2]
    gw = HEADS_PER_KV * HEAD_DIM
    assert tp == t + WINDOW and span == TQ + WINDOW and t % SEL_CHUNK == 0 and (t // SEL_BLOCK) % 8 == 0

    def stream(j):
        return pl.BlockSpec((None, tp, N_KV * HEAD_DIM), lambda bi, i: (bi, 0, j))

    return pl.pallas_call(
        _nsa_body,
        grid=(b, nt),
        in_specs=[
            pl.BlockSpec((None, TQ, N_KV * gw), lambda bi, i: (bi, i, 0)),
            stream(0), stream(1), stream(2), stream(3),
            pl.BlockSpec((None, N_KV, None, LANES, HEAD_DIM), lambda bi, i: (0, 0, bi, 0, 0)),
            pl.BlockSpec((None, N_KV, None, LANES, HEAD_DIM), lambda bi, i: (1, 0, bi, 0, 0)),
            pl.BlockSpec((None, TQ, N_KV * LANES), lambda bi, i: (bi, i, 0)),
            pl.BlockSpec((N_KV, None, ROWS4, LANES), lambda bi, i: (0, i, 0, 0)),
            pl.BlockSpec((N_KV, QK + 3, ROWS4, LANES), lambda bi, i: (0, 0, 0, 0)),
            pl.BlockSpec((N_KV, ROWS4, span), lambda bi, i: (0, 0, 0)),
            pl.BlockSpec((LANES, LANES), lambda bi, i: (0, 0)),
            pl.BlockSpec((t, LANES), lambda bi, i: (0, 0)),
        ],
        out_specs=pl.BlockSpec((None, TQ, N_KV * gw), lambda bi, i: (bi, i, 0)),
        out_shape=jax.ShapeDtypeStruct((b, t, N_KV * gw), BF16),
        scratch_shapes=[
            pltpu.VMEM((N_KV, t, 2 * HEAD_DIM), BF16),
            pltpu.VMEM((N_KV, t // SEL_CHUNK, ROWS4, SEL_CHUNK), F32),
            pltpu.VMEM((N_KV, ROWS4, LANES), F32),
            pltpu.VMEM((N_KV, ROWS4, LANES), F32),
            pltpu.VMEM((N_KV, ROWS4, HEAD_DIM), F32),
        ],
        compiler_params=_cparams(("parallel", "arbitrary"), NSA_VMEM_LIMIT_BYTES),
        name="nsa",
    )(q, kvs, kvs, kvs, kvs, cb, cb, gates, biasc, dtiles, wb, ovt, et)


def _s5_body(u_ref, kc_ref, winc_ref, woutc_ref, rep_k_ref, rep_in_ref, rep_out_ref, m_k_ref, m_in_ref, m_out_ref,
             al_ref, d_ref, y_ref, ucat_ref, inj_ref, xp_ref, grev_ref, win_ref, wout_ref):
    nb, t, lanes = u_ref.shape
    L = SSM_CHUNK
    nchunk = t // L
    rows = nb * nchunk
    sdim = al_ref.shape[1] // 2

    @pl.when(pl.program_id(1) == 0)
    def _():
        m_k, m_in, m_out = m_k_ref[...], m_in_ref[...], m_out_ref[...]
        kx = jnp.dot(kc_ref[...], rep_k_ref[...], preferred_element_type=F32)
        zero = jnp.zeros((lanes, lanes), BF16)

        def kblk(tau):
            return (kx[tau * lanes:(tau + 1) * lanes] * m_k).astype(BF16) if 0 <= tau < L else zero

        for r in range(L + 1):
            grev_ref[r * lanes:(r + 1) * lanes, 0:lanes] = kblk(L - r - 1)
            grev_ref[r * lanes:(r + 1) * lanes, lanes:] = kblk(L - r)
        for half in range(2):
            for s in range(L):
                x = jnp.dot(winc_ref[half, s * lanes:(s + 1) * lanes, :], rep_in_ref[...], preferred_element_type=F32)
                win_ref[s * lanes:(s + 1) * lanes, half * sdim:(half + 1) * sdim] = (x * m_in).astype(BF16)
            for tt in range(L):
                x = jnp.dot(woutc_ref[half], rep_out_ref[:, tt * lanes:(tt + 1) * lanes], preferred_element_type=F32)
                wout_ref[half * sdim:(half + 1) * sdim, tt * lanes:(tt + 1) * lanes] = (x * m_out).astype(BF16)

    def u_at(s):
        return u_ref[:, pl.ds(s, nchunk, stride=L), :].reshape(rows, lanes)

    for s in range(L):
        ucat_ref[:, s * lanes:(s + 1) * lanes] = u_at(s).astype(BF16)
    inj = jnp.dot(ucat_ref[...], win_ref[...], preferred_element_type=F32)
    npl = sdim // lanes
    for k in range(2 * npl):
        inj_ref[k] = inj[:, k * lanes:(k + 1) * lanes]
    ar = [jnp.broadcast_to(al_ref[0:1, k * lanes:(k + 1) * lanes], (nb, lanes)) for k in range(npl)]
    ai = [jnp.broadcast_to(al_ref[0:1, sdim + k * lanes:sdim + (k + 1) * lanes], (nb, lanes)) for k in range(npl)]

    def step(c, carry):
        xr, xi = carry
        rsel = pl.ds(c, nb, stride=nchunk)
        nr, ni = [], []
        for k in range(npl):
            xp_ref[k, rsel, :] = xr[k]
            xp_ref[npl + k, rsel, :] = xi[k]
            nr.append(ar[k] * xr[k] - ai[k] * xi[k] + inj_ref[k, rsel, :])
            ni.append(ar[k] * xi[k] + ai[k] * xr[k] + inj_ref[npl + k, rsel, :])
        return tuple(nr), tuple(ni)

    z = tuple(jnp.zeros((nb, lanes), F32) for _ in range(npl))
    lax.fori_loop(0, nchunk, step, (z, z), unroll=4)
    xp = jnp.concatenate([xp_ref[k] for k in range(2 * npl)], axis=1).astype(BF16)
    d = d_ref[...]
    for tp in range(0, L, 2):
        res = jnp.dot(ucat_ref[:, 0:(tp + 2) * lanes], grev_ref[(L - 1 - tp) * lanes:(L + 1) * lanes, :],
                      preferred_element_type=F32)
        res = res + jnp.dot(xp, wout_ref[:, tp * lanes:(tp + 2) * lanes], preferred_element_type=F32)
        for k in range(2):
            y = res[:, k * lanes:(k + 1) * lanes] + d * u_at(tp + k)
            y_ref[:, pl.ds(tp + k, nchunk, stride=L), :] = y.reshape(nb, nchunk, lanes)


def _s5(u, kc, winc, woutc, al, d, *, nb=4):
    b, t, dch = u.shape
    nb = min(nb, b)
    nblk = dch // LANES
    L = SSM_CHUNK
    rows = nb * (t // L)
    hch, p = kc.shape[2], winc.shape[3]
    gpb = LANES // hch
    sdim = gpb * p
    kcat = L * LANES
    lane, col, st = jnp.arange(LANES), jnp.arange(kcat), jnp.arange(sdim)
    rep_k = (jnp.arange(hch)[:, None] == (lane % hch)[None, :]).astype(BF16)
    rep_in = (jnp.arange(p)[:, None] == (st % p)[None, :]).astype(BF16)
    rep_out = (jnp.arange(L * hch)[:, None] == (col // LANES * hch + col % hch)[None, :]).astype(BF16)
    m_k = ((lane // hch)[:, None] == (lane // hch)[None, :]).astype(F32)
    m_in = ((lane // hch)[:, None] == (st // p)[None, :]).astype(F32)
    m_out = ((st // p)[:, None] == (lane // hch)[None, :]).astype(F32)
    const = lambda a: pl.BlockSpec(a.shape, lambda j, i: (0,) * a.ndim)
    return pl.pallas_call(
        _s5_body,
        grid=(nblk, b // nb),
        in_specs=[
            pl.BlockSpec((nb, t, LANES), lambda j, i: (i, 0, j)),
            pl.BlockSpec((None, kcat, hch), lambda j, i: (j, 0, 0)),
            pl.BlockSpec((None, 2, kcat, p), lambda j, i: (j, 0, 0, 0)),
            pl.BlockSpec((None, 2, sdim, L * hch), lambda j, i: (j, 0, 0, 0)),
            const(rep_k), const(rep_in), const(rep_out), const(m_k), const(m_in), const(m_out),
            pl.BlockSpec((None, 8, 2 * sdim), lambda j, i: (j, 0, 0)),
            pl.BlockSpec((1, LANES), lambda j, i: (0, j)),
        ],
        out_specs=pl.BlockSpec((nb, t, LANES), lambda j, i: (i, 0, j)),
        out_shape=jax.ShapeDtypeStruct((b, t, dch), F32),
        scratch_shapes=[pltpu.VMEM((rows, kcat), BF16), pltpu.VMEM((2 * sdim // LANES, rows, LANES), F32),
                        pltpu.VMEM((2 * sdim // LANES, rows, LANES), F32),
                        pltpu.VMEM(((L + 1) * LANES, 2 * LANES), BF16), pltpu.VMEM((kcat, 2 * sdim), BF16),
                        pltpu.VMEM((2 * sdim, kcat), BF16)],
        compiler_params=_cparams(("parallel", "arbitrary")),
        name="s5",
    )(u, kc, winc, woutc, rep_k, rep_in, rep_out, m_k, m_in, m_out, al, d.reshape(1, dch))


def _glu_out_body(h_ref, a_ref, y_ref, wg_ref, bg_ref, woa_ref, wos_ref, o_ref):
    hg = jax.nn.gelu(y_ref[...], approximate=True)
    z = jnp.dot(hg.astype(BF16), wg_ref[...], preferred_element_type=F32) + bg_ref[...]
    s = hg * jax.nn.sigmoid(z)
    mix = jnp.dot(a_ref[...], woa_ref[...], preferred_element_type=F32)
    mix = mix + jnp.dot(s.astype(BF16), wos_ref[...], preferred_element_type=F32)
    o_ref[...] = h_ref[...] + mix


def _glu_out(h, a, y, wg, bg, woa, wos, *, tm=512):
    n, dm = h.shape
    da, ds = a.shape[1], y.shape[1]
    row = lambda w: pl.BlockSpec((tm, w), lambda i: (i, 0))
    full = lambda r, c: pl.BlockSpec((r, c), lambda i: (0, 0))
    return pl.pallas_call(
        _glu_out_body,
        grid=(n // tm,),
        in_specs=[row(dm), row(da), row(ds), full(ds, ds), full(1, ds), full(da, dm), full(ds, dm)],
        out_specs=row(dm),
        out_shape=jax.ShapeDtypeStruct((n, dm), F32),
        compiler_params=_cparams(("parallel",)),
        name="glu_out",
    )(h, a, y, wg, bg.reshape(1, ds), woa, wos)


def _t5_bucket(dist):
    n = jnp.maximum(dist, 0)
    max_exact = N_BUCKETS // 2
    nf = jnp.maximum(n, 1).astype(F32)
    large = max_exact + (jnp.log(nf / max_exact) / math.log(MAX_DISTANCE / max_exact)
                         * (N_BUCKETS - max_exact)).astype(jnp.int32)
    large = jnp.minimum(large, N_BUCKETS - 1)
    return jnp.where(n < max_exact, n, large)


def _t5_body(first_ref, table_ref, bc_ref, dt_ref, wb_ref, *, nc):
    g = pl.program_id(0)
    nt = bc_ref.shape[0]
    span = wb_ref.shape[1]
    a = lax.broadcasted_iota(jnp.int32, (TQ, LANES), 0)
    c = lax.broadcasted_iota(jnp.int32, (TQ, LANES), 1)
    heads = [g * HEADS_PER_KV + h for h in range(HEADS_PER_KV)]

    def lookup(dist, valid):
        bias = [jnp.full((TQ, LANES), table_ref[0, hd], F32) for hd in heads]
        for k in range(1, N_BUCKETS):
            ge = dist >= first_ref[k]
            bias = [jnp.where(ge, table_ref[k, hd], bv) for hd, bv in zip(heads, bias)]
        return bias if valid is None else [jnp.where(valid, bv, NEG) for bv in bias]

    def put(ref_at, tiles):
        for h, tile in enumerate(tiles):
            ref_at(slice(h * TQ, (h + 1) * TQ), tile)

    def put_dt(v):
        def f(rows, tile):
            dt_ref[v, rows, :] = tile
        return f

    put(put_dt(0), lookup(jnp.full((TQ, LANES), 2 * MAX_DISTANCE, jnp.int32), None))
    for r in range(-1, QK):
        dist = a - r * LANES - c
        put(put_dt(r + 2), lookup(dist, dist >= 0))
    dt_ref[QK + 2] = jnp.full((ROWS4, LANES), NEG, F32)
    for j in range(span // LANES):
        dist = a + WINDOW - (c + j * LANES)

        def put_wb(rows, tile, j=j):
            wb_ref[rows, j * LANES:(j + 1) * LANES] = tile

        put(put_wb, lookup(dist, (dist >= 0) & (dist < WINDOW)))

    def cmp_tile(i, _):
        dist = i * TQ + a - (c * CMP_STRIDE + CMP_LEN - 1)

        def put_bc(rows, tile):
            bc_ref[i, rows, :] = tile

        put(put_bc, lookup(dist, (dist >= 0) & (c < nc)))
        return 0

    lax.fori_loop(0, nt, cmp_tile, 0)


def _bias_tables(rel_bias, t):
    nt = t // TQ
    span = TQ + WINDOW
    nc = (t - CMP_LEN) // CMP_STRIDE + 1
    buckets = _t5_bucket(jnp.arange(2 * MAX_DISTANCE))
    first = jnp.sum((buckets[None, :] < jnp.arange(N_BUCKETS)[:, None]).astype(jnp.int32), axis=1)
    smem = pl.BlockSpec(memory_space=pltpu.SMEM)
    return pl.pallas_call(
        functools.partial(_t5_body, nc=nc),
        grid=(N_KV,),
        in_specs=[smem, smem],
        out_specs=[
            pl.BlockSpec((None, nt, ROWS4, LANES), lambda g: (g, 0, 0, 0)),
            pl.BlockSpec((None, QK + 3, ROWS4, LANES), lambda g: (g, 0, 0, 0)),
            pl.BlockSpec((None, ROWS4, span), lambda g: (g, 0, 0)),
        ],
        out_shape=[
            jax.ShapeDtypeStruct((N_KV, nt, ROWS4, LANES), F32),
            jax.ShapeDtypeStruct((N_KV, QK + 3, ROWS4, LANES), F32),
            jax.ShapeDtypeStruct((N_KV, ROWS4, span), F32),
        ],
        compiler_params=_cparams(("parallel",)),
        name="t5_tables",
    )(first, rel_bias.astype(F32))


def _sel_tables(t):
    ns = t // SEL_BLOCK
    nc_pad = LANES
    c_start = jnp.arange(nc_pad) * CMP_STRIDE
    j_start = jnp.arange(LANES) * SEL_BLOCK
    ov = jnp.clip(jnp.minimum(c_start[:, None] + CMP_LEN, j_start[None, :] + SEL_BLOCK)
                  - jnp.maximum(c_start[:, None], j_start[None, :]), 0, None).astype(F32) / CMP_LEN
    ov = jnp.where(jnp.arange(LANES)[None, :] < ns, ov, 0.0)
    et = jnp.arange(t)[:, None] // SEL_BLOCK == jnp.arange(LANES)[None, :]
    return ov.T.astype(BF16), et.astype(BF16)


def _s5_tables(lam_re, lam_im, log_step, b_re, b_im, c_re, c_im):
    ng, p = lam_re.shape
    hch = b_re.shape[2]
    L = SSM_CHUNK
    step = jnp.exp(log_step.astype(F32))[:, None]
    lre, lim = lam_re.astype(F32), lam_im.astype(F32)
    mag = jnp.exp(lre * step)
    ab_re, ab_im = mag * jnp.cos(lim * step), mag * jnp.sin(lim * step)
    nr, ni = ab_re - 1.0, ab_im
    den = lre * lre + lim * lim
    f_re, f_im = (nr * lre + ni * lim) / den, (ni * lre - nr * lim) / den
    br, bim = b_re.astype(F32), b_im.astype(F32)
    bb_re = f_re[..., None] * br - f_im[..., None] * bim
    bb_im = f_re[..., None] * bim + f_im[..., None] * br
    cr, ci = c_re.astype(F32), c_im.astype(F32)
    pr, pi = [jnp.ones_like(ab_re)], [jnp.zeros_like(ab_re)]
    for _ in range(L):
        pr, pi = pr + [pr[-1] * ab_re - pi[-1] * ab_im], pi + [pr[-1] * ab_im + pi[-1] * ab_re]
    pw_re, pw_im = jnp.stack(pr, 0), jnp.stack(pi, 0)
    cp_re = cr[None] * pw_re[:, :, None, :] - ci[None] * pw_im[:, :, None, :]
    cp_im = -(cr[None] * pw_im[:, :, None, :] + ci[None] * pw_re[:, :, None, :])
    cpr, cpi = cp_re[:L].transpose(1, 0, 2, 3)[:, :, None], cp_im[:L].transpose(1, 0, 2, 3)[:, :, None]
    bbr, bbi = bb_re.transpose(0, 2, 1)[:, None, :, None], bb_im.transpose(0, 2, 1)[:, None, :, None]
    kern = jnp.sum(cpr * bbr + cpi * bbi, axis=-1)
    gpb = LANES // hch
    nblk = ng // gpb
    kc = kern.reshape(nblk, gpb, L, hch, hch).transpose(0, 2, 1, 3, 4).reshape(nblk, L * LANES, hch)
    wr = pw_re[L - 1 - jnp.arange(L)]
    wi = pw_im[L - 1 - jnp.arange(L)]
    win_re = (wr[..., None] * bb_re[None] - wi[..., None] * bb_im[None])
    win_im = (wr[..., None] * bb_im[None] + wi[..., None] * bb_re[None])
    inj_op = lambda m: m.reshape(L, nblk, gpb, p, hch).transpose(1, 0, 2, 4, 3).reshape(nblk, L * LANES, p)
    winc = jnp.stack([inj_op(win_re), inj_op(win_im)], axis=1)
    read_op = lambda m: m.reshape(L, nblk, gpb, hch, p).transpose(1, 2, 4, 0, 3).reshape(nblk, gpb * p, L * hch)
    woutc = jnp.stack([read_op(cp_re[1:]), read_op(cp_im[1:])], axis=1)
    al = jnp.concatenate([pw_re[L].reshape(nblk, gpb * p), pw_im[L].reshape(nblk, gpb * p)], axis=1)
    al = jnp.broadcast_to(al[:, None, :], (nblk, 8, 2 * gpb * p))
    return kc.astype(BF16), winc.astype(BF16), woutc.astype(BF16), al


def _mixers(h1, mix_norm, w_in, cmp_k, cmp_v, rel_bias, ssm, ssm_d, glu_w, glu_b, w_out, b, t):
    n, d = h1.shape
    d_ssm = glu_w.shape[0]
    o_g, o_u = D_ATTN + 6 * D_KV, D_ATTN + 6 * D_KV + 3 * N_HEADS
    gcols = 3 * HEADS_PER_KV
    wg = [jnp.pad(w_in[:, o_g + g * gcols:o_g + (g + 1) * gcols], ((0, 0), (0, LANES - gcols))) for g in range(N_KV)]
    w_perm = jnp.concatenate([w_in[:, :o_g], w_in[:, o_u:]] + wg, axis=1).astype(BF16)
    q, kvs, kcv, u, gates = _inproj(h1.reshape(b, t, d), mix_norm, w_perm, d_ssm)

    nblk = t // CMP_STRIDE
    assert nblk == LANES, "compressed-block axis is laid out on one 128-lane tile"
    pe = jnp.stack([cmp_k[0], cmp_v[0]], 0).astype(F32)
    w1 = jnp.stack([cmp_k[1], cmp_v[1]], 0).astype(BF16).reshape(2, CMP_LEN, HEAD_DIM, -1)
    b1 = jnp.stack([cmp_k[2].reshape(1, -1), cmp_v[2].reshape(1, -1)], 0).astype(F32)
    w2 = jnp.stack([cmp_k[3], cmp_v[3]], 0).astype(BF16)
    cb = _compress(kcv, pe, w1, b1, w2, nblk)

    biasc, dtiles, wb = _bias_tables(rel_bias, t)
    ovt, et = _sel_tables(t)
    a = _nsa(q, kvs, cb, gates, biasc, dtiles, wb, ovt, et)

    kc, winc, woutc, al = _s5_tables(*ssm)
    y = _s5(u, kc, winc, woutc, al, ssm_d)

    wo = w_out.astype(BF16)
    return _glu_out(h1, a.reshape(n, D_ATTN), y.reshape(n, d_ssm), glu_w.astype(BF16), glu_b, wo[:D_ATTN], wo[D_ATTN:])


def kernel(x, ffn1_norm, ffn1_w1, ffn1_w3, ffn1_w2, mix_norm, w_in, cmp_pe_k, cmp_w1_k, cmp_b1_k, cmp_w2_k,
           cmp_pe_v, cmp_w1_v, cmp_b1_v, cmp_w2_v, rel_bias, ssm_lam_re, ssm_lam_im, ssm_log_step, ssm_b_re,
           ssm_b_im, ssm_c_re, ssm_c_im, ssm_d, glu_w, glu_b, w_out, ffn2_norm, ffn2_w1, ffn2_w3, ffn2_w2,
           final_norm):
    b, t, d = x.shape
    depth = ffn1_w1.shape[0]
    h = x.reshape(b * t, d)
    for l in range(depth):
        last = l == depth - 1
        h = _ffn(h, ffn1_norm[l], ffn1_w1[l].astype(BF16), ffn1_w3[l].astype(BF16), ffn1_w2[l].astype(BF16))
        ssm = (ssm_lam_re[l], ssm_lam_im[l], ssm_log_step[l], ssm_b_re[l], ssm_b_im[l], ssm_c_re[l], ssm_c_im[l])
        h = _mixers(h, mix_norm[l], w_in[l],
                    (cmp_pe_k[l], cmp_w1_k[l], cmp_b1_k[l], cmp_w2_k[l]),
                    (cmp_pe_v[l], cmp_w1_v[l], cmp_b1_v[l], cmp_w2_v[l]),
                    rel_bias, ssm, ssm_d[l], glu_w[l], glu_b[l], w_out[l], b, t)
        h = _ffn(h, ffn2_norm[l], ffn2_w1[l].astype(BF16), ffn2_w3[l].astype(BF16), ffn2_w2[l].astype(BF16),
                 final_gain=final_norm if last else None)
    return h.reshape(b, t, d)
```

```python
import functools
import math

import jax
import jax.numpy as jnp
from jax import lax
from jax.experimental import pallas as pl
from jax.experimental.pallas import tpu as pltpu

F32 = jnp.float32
BF16 = jnp.bfloat16

HEAD_DIM = 128
N_KV = 2
HEADS_PER_KV = 4
N_HEADS = N_KV * HEADS_PER_KV
D_ATTN = N_HEADS * HEAD_DIM
D_KV = N_KV * HEAD_DIM
CMP_LEN = 32
CMP_STRIDE = 16
SEL_BLOCK = 64
N_SELECT = 16
WINDOW = 512
N_BUCKETS = 32
MAX_DISTANCE = 128
SSM_GROUP = 16
SSM_STATE = 64
EPS = 1e-6
NEG = -1e30

LANES = 128
VMEM_LIMIT_BYTES = 56 * 1024 * 1024
NSA_VMEM_LIMIT_BYTES = 60 * 1024 * 1024
TQ = 256
QK = TQ // LANES
ROWS4 = HEADS_PER_KV * TQ
SEL_CHUNK = 512
SSM_CHUNK = 16


def _cparams(sem, vmem_limit_bytes=VMEM_LIMIT_BYTES):
    return pltpu.CompilerParams(dimension_semantics=sem, vmem_limit_bytes=vmem_limit_bytes)


def _ffn_body(x_ref, g_ref, w1_ref, w3_ref, w2_ref, *rest, final, last_cols):
    if final:
        fg_ref, o_ref, xn_ref = rest
    else:
        o_ref, xn_ref = rest
    j = pl.program_id(1)
    nj = pl.num_programs(1)

    def contribution(xn, cols):
        a = jnp.dot(xn, w1_ref[:, 0:cols], preferred_element_type=F32)
        b = jnp.dot(xn, w3_ref[:, 0:cols], preferred_element_type=F32)
        gated = (a * jax.nn.sigmoid(a)) * b
        return jnp.dot(gated.astype(BF16), w2_ref[0:cols, :], preferred_element_type=F32)

    tf = w1_ref.shape[1]

    @pl.when(j == 0)
    def _():
        x = x_ref[...]
        ms = jnp.mean(x * x, axis=-1, keepdims=True)
        xn = (x * lax.rsqrt(ms + EPS) * g_ref[...]).astype(BF16)
        xn_ref[...] = xn
        o_ref[...] = contribution(xn, tf)

    @pl.when((j > 0) & (j < nj - 1))
    def _():
        o_ref[...] += contribution(xn_ref[...], tf)

    @pl.when(j == nj - 1)
    def _():
        h = x_ref[...] + 0.5 * (o_ref[...] + contribution(xn_ref[...], last_cols))
        if final:
            ms = jnp.mean(h * h, axis=-1, keepdims=True)
            h = h * lax.rsqrt(ms + EPS) * fg_ref[...]
        o_ref[...] = h


def _ffn(x, gain, w1, w3, w2, final_gain=None, *, tm=512, tf=1024):
    n, d = x.shape
    dff = w1.shape[1]
    nj = pl.cdiv(dff, tf)
    last_cols = dff - (nj - 1) * tf
    assert nj >= 2 and last_cols % LANES == 0
    final = final_gain is not None
    in_specs = [
        pl.BlockSpec((tm, d), lambda i, j: (i, 0)),
        pl.BlockSpec((1, d), lambda i, j: (0, 0)),
        pl.BlockSpec((d, tf), lambda i, j: (0, j)),
        pl.BlockSpec((d, tf), lambda i, j: (0, j)),
        pl.BlockSpec((tf, d), lambda i, j: (j, 0)),
    ]
    args = [x, gain.reshape(1, d), w1, w3, w2]
    if final:
        in_specs.append(pl.BlockSpec((1, d), lambda i, j: (0, 0)))
        args.append(final_gain.reshape(1, d))
    return pl.pallas_call(
        functools.partial(_ffn_body, final=final, last_cols=last_cols),
        grid=(n // tm, nj),
        in_specs=in_specs,
        out_specs=pl.BlockSpec((tm, d), lambda i, j: (i, 0)),
        out_shape=jax.ShapeDtypeStruct((n, d), F32),
        scratch_shapes=[pltpu.VMEM((tm, d), BF16)],
        compiler_params=_cparams(("parallel", "arbitrary")),
        name="ffn_final" if final else "ffn",
    )(*args)


def _inproj_body(x_ref, g_ref, w_ref, q_ref, kvs_ref, kcv_ref, u_ref, gate_ref):
    i = pl.program_id(1)

    @pl.when(i == 0)
    def _():
        kvs_ref[...] = jnp.zeros(kvs_ref.shape, kvs_ref.dtype)
        kcv_ref[...] = jnp.zeros(kcv_ref.shape, kcv_ref.dtype)

    @pl.when(i > 0)
    def _():
        x = x_ref[...]
        ms = jnp.mean(x * x, axis=-1, keepdims=True)
        xn = (x * lax.rsqrt(ms + EPS) * g_ref[...]).astype(BF16)
        c0 = 0
        c1 = q_ref.shape[1]
        q = jnp.dot(xn, w_ref[:, c0:c1], preferred_element_type=F32)
        q_ref[...] = (q * (HEAD_DIM ** -0.5)).astype(BF16)
        c0, c1 = c1, c1 + kcv_ref.shape[1]
        kcv_ref[...] = jnp.dot(xn, w_ref[:, c0:c1], preferred_element_type=F32)
        c0, c1 = c1, c1 + kvs_ref.shape[1]
        kvs_ref[...] = jnp.dot(xn, w_ref[:, c0:c1], preferred_element_type=F32).astype(BF16)
        c0, c1 = c1, c1 + u_ref.shape[1]
        u_ref[...] = jnp.dot(xn, w_ref[:, c0:c1], preferred_element_type=F32)
        gate_ref[...] = jax.nn.sigmoid(jnp.dot(xn, w_ref[:, c1:], preferred_element_type=F32))


def _inproj(h, gain, w_perm, d_ssm):
    b, t, d = h.shape
    tm = WINDOW
    nt = t // tm
    nq, ncv, nvs, ng = D_ATTN, 2 * D_KV, 4 * D_KV, N_KV * LANES
    ncol = w_perm.shape[1]
    assert ncol == nq + ncv + nvs + d_ssm + ng and t % tm == 0
    data = lambda bi, i: (bi, jnp.maximum(i - 1, 0), 0)
    return pl.pallas_call(
        _inproj_body,
        grid=(b, nt + 1),
        in_specs=[
            pl.BlockSpec((None, tm, d), data),
            pl.BlockSpec((1, d), lambda bi, i: (0, 0)),
            pl.BlockSpec((d, ncol), lambda bi, i: (0, 0)),
        ],
        out_specs=[
            pl.BlockSpec((None, tm, nq), data),
            pl.BlockSpec((None, tm, nvs), lambda bi, i: (bi, i, 0)),
            pl.BlockSpec((None, tm, ncv), lambda bi, i: (bi, jnp.where(i == 0, nt, i - 1), 0)),
            pl.BlockSpec((None, tm, d_ssm), data),
            pl.BlockSpec((None, tm, ng), data),
        ],
        out_shape=[
            jax.ShapeDtypeStruct((b, t, nq), BF16),
            jax.ShapeDtypeStruct((b, t + tm, nvs), BF16),
            jax.ShapeDtypeStruct((b, t + tm, ncv), F32),
            jax.ShapeDtypeStruct((b, t, d_ssm), F32),
            jax.ShapeDtypeStruct((b, t, ng), F32),
        ],
        compiler_params=_cparams(("parallel", "arbitrary")),
        name="inproj",
    )(h, gain.reshape(1, d), w_perm)


def _compress_body(x_ref, pe_ref, w1_ref, b1_ref, w2_ref, o_ref):
    nb, nblk = o_ref.shape[0], o_ref.shape[1]
    acc = jnp.broadcast_to(b1_ref[...], (nb * nblk, b1_ref.shape[1])).astype(F32)
    for l in range(CMP_LEN):
        xl = x_ref[:, pl.ds(l, nblk, stride=CMP_STRIDE), :] + pe_ref[l:l + 1, :]
        acc = acc + jnp.dot(xl.reshape(nb * nblk, xl.shape[2]).astype(BF16), w1_ref[l], preferred_element_type=F32)
    hid = jax.nn.gelu(acc, approximate=True)
    out = jnp.dot(hid.astype(BF16), w2_ref[...], preferred_element_type=F32)
    o_ref[...] = out.reshape(o_ref.shape).astype(BF16)


def _compress(kcv, pe, w1, b1, w2, nblk, *, nb=4):
    b, tpad, _ = kcv.shape
    nb = min(nb, b)
    hid, dh = w1.shape[3], w2.shape[2]
    assert (nblk - 1) * CMP_STRIDE + CMP_LEN <= tpad
    return pl.pallas_call(
        _compress_body,
        grid=(2 * N_KV, b // nb),
        in_specs=[
            pl.BlockSpec((nb, tpad, dh), lambda s, i: (i, 0, s)),
            pl.BlockSpec((None, CMP_LEN, dh), lambda s, i: (s // N_KV, 0, 0)),
            pl.BlockSpec((None, CMP_LEN, dh, hid), lambda s, i: (s // N_KV, 0, 0, 0)),
            pl.BlockSpec((None, 1, hid), lambda s, i: (s // N_KV, 0, 0)),
            pl.BlockSpec((None, hid, dh), lambda s, i: (s // N_KV, 0, 0)),
        ],
        out_specs=pl.BlockSpec((None, None, nb, nblk, dh), lambda s, i: (s // N_KV, s % N_KV, i, 0, 0)),
        out_shape=jax.ShapeDtypeStruct((2, N_KV, b, nblk, dh), BF16),
        compiler_params=_cparams(("parallel", "parallel")),
        name="compress",
    )(kcv, pe, w1, b1, w2)


def _nt_dot(a, b):
    return lax.dot_general(a, b, (((1,), (1,)), ((), ())), preferred_element_type=F32)


def _nsa_body(q_ref, ks_ref, vs_ref, kw_ref, vw_ref, kcb_ref, vcb_ref, gate_ref, biasc_ref, dt_ref, wb_ref,
              ovt_ref, et_ref, o_ref, kse_ref, s_ref, m_ref, l_ref, acc_ref):
    i = pl.program_id(1)
    t0 = i * TQ
    t = et_ref.shape[0]
    ns = t // SEL_BLOCK
    nsel = min(N_SELECT, ns)
    groups = range(N_KV)
    gl = lambda g: slice(g * HEAD_DIM, (g + 1) * HEAD_DIM)

    @pl.when(i == 0)
    def _():
        for g in groups:
            kse_ref[g, :, 0:HEAD_DIM] = ks_ref[WINDOW:WINDOW + t, gl(g)]
            kse_ref[g, :, HEAD_DIM:] = et_ref[...]

    q = q_ref[...]
    w0 = pl.multiple_of(t0, LANES)
    span = wb_ref.shape[2]

    q4 = [jnp.concatenate([q[:, (g * HEADS_PER_KV + h) * HEAD_DIM:(g * HEADS_PER_KV + h + 1) * HEAD_DIM]
                           for h in range(HEADS_PER_KV)], axis=0) for g in groups]

    sc = [_nt_dot(q4[g], kcb_ref[g]) + biasc_ref[g] for g in groups]
    row_t = t0 + (lax.broadcasted_iota(jnp.int32, (ROWS4, LANES), 0) & (TQ - 1))
    col_c = lax.broadcasted_iota(jnp.int32, (ROWS4, LANES), 1)
    valid_c = (row_t - col_c * CMP_STRIDE - (CMP_LEN - 1) >= 0) & (col_c < LANES - 1)
    mc = [jnp.max(s, axis=-1, keepdims=True) for s in sc]
    pc = [jnp.where(valid_c, jnp.exp(s - m), 0.0) for s, m in zip(sc, mc)]
    pc = [p / jnp.maximum(jnp.sum(p, axis=-1, keepdims=True), 1e-30) for p in pc]
    pcb = [p.astype(BF16) for p in pc]
    o_cmps = [jnp.dot(pcb[g], vcb_ref[g], preferred_element_type=F32) for g in groups]
    pimp = [_nt_dot(ovt_ref[...], p) for p in pcb]
    imp = [sum(p[0:ns, h * TQ:(h + 1) * TQ] for h in range(1, HEADS_PER_KV)) + p[0:ns, 0:TQ] for p in pimp]

    jrow = lax.broadcasted_iota(jnp.int32, (ns, TQ), 0)
    tpos = t0 + lax.broadcasted_iota(jnp.int32, (ns, TQ), 1)
    cur = lax.shift_right_logical(tpos, int(math.log2(SEL_BLOCK)))
    forced = (jrow == 0) | (jrow == cur) | (jrow == cur - 1)
    impm = [jnp.where(forced, 1e6, jnp.where(jrow * SEL_BLOCK <= tpos, x, -1e9)) for x in imp]
    nslab = ns // 8
    slabs = [[x[8 * v:8 * v + 8] for v in range(nslab)] for x in impm]
    cnts = [[jnp.zeros((8, TQ), F32) for _ in range(nslab)] for _ in groups]
    sub = lax.broadcasted_iota(jnp.int32, (8, TQ), 0)
    for jp in range(ns):
        v0, r0 = divmod(jp, 8)
        for g in groups:
            row = jnp.broadcast_to(slabs[g][v0][r0:r0 + 1, :], (8, TQ))
            for v in range(nslab):
                if v > v0:
                    beats = row >= slabs[g][v]
                elif v < v0:
                    beats = row > slabs[g][v]
                else:
                    beats = (row > slabs[g][v]) | ((row == slabs[g][v]) & (sub > r0))
                cnts[g][v] = cnts[g][v] + jnp.where(beats, 1.0, 0.0)
    q4s = []
    for g in groups:
        negt = [jnp.where(c < nsel, 0.0, NEG) for c in cnts[g]]
        negt = jnp.concatenate(negt + [jnp.zeros((LANES - ns, TQ), F32)], axis=0)
        neg = negt.T.astype(BF16)
        q4s.append(jnp.concatenate([q4[g], jnp.concatenate([neg] * HEADS_PER_KV, axis=0)], axis=1))

    nsub = SEL_CHUNK // LANES
    nchunks = ((i + 1) * QK - 1) // nsub + 1
    m_ref[...] = jnp.full(m_ref.shape, -3e38, F32)

    def score_chunk(ck, _):
        kb = pl.multiple_of(ck * SEL_CHUNK, SEL_CHUNK)
        s = [_nt_dot(q4s[g], kse_ref[g, pl.ds(kb, SEL_CHUNK), :]) for g in groups]
        for g in groups:
            mloc = None
            for j in range(nsub):
                sj = s[g][:, j * LANES:(j + 1) * LANES] + dt_ref[g, jnp.clip(ck * nsub + j - i * QK + 2, 0, QK + 2)]
                s_ref[g, ck, :, j * LANES:(j + 1) * LANES] = sj
                mloc = sj if mloc is None else jnp.maximum(mloc, sj)
            m_ref[g] = jnp.maximum(m_ref[g], mloc)
        return 0

    lax.fori_loop(0, nchunks, score_chunk, 0)
    for g in groups:
        m_ref[g] = jnp.broadcast_to(jnp.max(m_ref[g], axis=-1, keepdims=True), (ROWS4, LANES))
    l_ref[...] = jnp.zeros(l_ref.shape, F32)
    acc_ref[...] = jnp.zeros(acc_ref.shape, F32)

    kneg = jnp.where(t0 - WINDOW + lax.broadcasted_iota(jnp.int32, (1, span), 1) >= 0, 0.0, NEG)
    sw = [_nt_dot(q4[g], kw_ref[pl.ds(w0, span), gl(g)]) + wb_ref[g] + kneg for g in groups]
    mw = [jnp.max(s, axis=-1, keepdims=True) for s in sw]
    pw = [jnp.exp(s - m) for s, m in zip(sw, mw)]
    lw = [jnp.sum(p, axis=-1, keepdims=True) for p in pw]
    o_wins = [jnp.dot(pw[g].astype(BF16), vw_ref[pl.ds(w0, span), gl(g)], preferred_element_type=F32) for g in groups]
    o_wins = [o / jnp.maximum(l, 1e-30) for o, l in zip(o_wins, lw)]

    def pv_chunk(ck, _):
        kb = pl.multiple_of(WINDOW + ck * SEL_CHUNK, SEL_CHUNK)
        ps = [[jnp.exp(s_ref[g, ck, :, j * LANES:(j + 1) * LANES] - m_ref[g]) for j in range(nsub)] for g in groups]
        for g in groups:
            lsum = ps[g][0]
            for pj in ps[g][1:]:
                lsum = lsum + pj
            l_ref[g] += lsum
            p = jnp.concatenate([pj.astype(BF16) for pj in ps[g]], axis=1)
            acc_ref[g] += jnp.dot(p, vs_ref[pl.ds(kb, SEL_CHUNK), gl(g)], preferred_element_type=F32)
        return 0

    lax.fori_loop(0, nchunks, pv_chunk, 0)

    ls = [jnp.maximum(jnp.sum(l_ref[g], axis=-1, keepdims=True), 1e-30) for g in groups]
    o_sels = [acc_ref[g] / ls[g] for g in groups]
    gts = [gate_ref[:, g * LANES:(g + 1) * LANES] for g in groups]
    for h in range(HEADS_PER_KV):
        r = slice(h * TQ, (h + 1) * TQ)
        for g in groups:
            gt = gts[g]
            o_h = (gt[:, 3 * h:3 * h + 1] * o_cmps[g][r] + gt[:, 3 * h + 1:3 * h + 2] * o_sels[g][r]
                   + gt[:, 3 * h + 2:3 * h + 3] * o_wins[g][r])
            c0 = (g * HEADS_PER_KV + h) * HEAD_DIM
            o_ref[:, c0:c0 + HEAD_DIM] = o_h.astype(BF16)


def _nsa(q, kvs, cb, gates, biasc, dtiles, wb, ovt, et):
    b, t, _ = q.shape
    nt = t // TQ
    tp = kvs.shape[1]
    span = wb.shape[2]
    gw = HEADS_PER_KV * HEAD_DIM
    assert tp == t + WINDOW and span == TQ + WINDOW and t % SEL_CHUNK == 0 and (t // SEL_BLOCK) % 8 == 0

    def stream(j):
        return pl.BlockSpec((None, tp, N_KV * HEAD_DIM), lambda bi, i: (bi, 0, j))

    return pl.pallas_call(
        _nsa_body,
        grid=(b, nt),
        in_specs=[
            pl.BlockSpec((None, TQ, N_KV * gw), lambda bi, i: (bi, i, 0)),
            stream(0), stream(1), stream(2), stream(3),
            pl.BlockSpec((None, N_KV, None, LANES, HEAD_DIM), lambda bi, i: (0, 0, bi, 0, 0)),
            pl.BlockSpec((None, N_KV, None, LANES, HEAD_DIM), lambda bi, i: (1, 0, bi, 0, 0)),
            pl.BlockSpec((None, TQ, N_KV * LANES), lambda bi, i: (bi, i, 0)),
            pl.BlockSpec((N_KV, None, ROWS4, LANES), lambda bi, i: (0, i, 0, 0)),
            pl.BlockSpec((N_KV, QK + 3, ROWS4, LANES), lambda bi, i: (0, 0, 0, 0)),
            pl.BlockSpec((N_KV, ROWS4, span), lambda bi, i: (0, 0, 0)),
            pl.BlockSpec((LANES, LANES), lambda bi, i: (0, 0)),
            pl.BlockSpec((t, LANES), lambda bi, i: (0, 0)),
        ],
        out_specs=pl.BlockSpec((None, TQ, N_KV * gw), lambda bi, i: (bi, i, 0)),
        out_shape=jax.ShapeDtypeStruct((b, t, N_KV * gw), BF16),
        scratch_shapes=[
            pltpu.VMEM((N_KV, t, 2 * HEAD_DIM), BF16),
            pltpu.VMEM((N_KV, t // SEL_CHUNK, ROWS4, SEL_CHUNK), F32),
            pltpu.VMEM((N_KV, ROWS4, LANES), F32),
            pltpu.VMEM((N_KV, ROWS4, LANES), F32),
            pltpu.VMEM((N_KV, ROWS4, HEAD_DIM), F32),
        ],
        compiler_params=_cparams(("parallel", "arbitrary"), NSA_VMEM_LIMIT_BYTES),
        name="nsa",
    )(q, kvs, kvs, kvs, kvs, cb, cb, gates, biasc, dtiles, wb, ovt, et)


def _s5_body(u_ref, c0_ref, winc_ref, woutc_ref, rep_k_ref, rep_in_ref, rep_out_ref, m_in_ref, m_out_ref,
             al_ref, d_ref, y_ref, ucat_ref, inj_ref, xp_ref, grev_ref, win_ref, wout_ref):
    nb, t, lanes = u_ref.shape
    L = SSM_CHUNK
    nchunk = t // L
    rows = nb * nchunk
    sdim = al_ref.shape[1] // 2

    @pl.when(pl.program_id(1) == 0)
    def _():
        m_in, m_out = m_in_ref[...], m_out_ref[...]
        for half in range(2):
            for s in range(L):
                x = jnp.dot(winc_ref[half, s * lanes:(s + 1) * lanes, :], rep_in_ref[...], preferred_element_type=F32)
                win_ref[s * lanes:(s + 1) * lanes, half * sdim:(half + 1) * sdim] = (x * m_in).astype(BF16)
            for tt in range(L):
                x = jnp.dot(woutc_ref[half], rep_out_ref[:, tt * lanes:(tt + 1) * lanes], preferred_element_type=F32)
                wout_ref[half * sdim:(half + 1) * sdim, tt * lanes:(tt + 1) * lanes] = (x * m_out).astype(BF16)
        c0 = [(jnp.dot(c0_ref[half], rep_k_ref[...], preferred_element_type=F32) * m_out).astype(BF16)
              for half in range(2)]
        kx = jnp.dot(win_ref[...], jnp.concatenate(c0, axis=0), preferred_element_type=F32)
        zero = jnp.zeros((lanes, lanes), BF16)

        def kblk(tau):
            s = L - 1 - tau
            return kx[s * lanes:(s + 1) * lanes].astype(BF16) if 0 <= tau < L else zero

        for r in range(L + 1):
            grev_ref[r * lanes:(r + 1) * lanes, 0:lanes] = kblk(L - r - 1)
            grev_ref[r * lanes:(r + 1) * lanes, lanes:] = kblk(L - r)

    def u_at(s):
        return u_ref[:, pl.ds(s, nchunk, stride=L), :].reshape(rows, lanes)

    for s in range(L):
        ucat_ref[:, s * lanes:(s + 1) * lanes] = u_at(s).astype(BF16)
    inj = jnp.dot(ucat_ref[...], win_ref[...], preferred_element_type=F32)
    npl = sdim // lanes
    for k in range(2 * npl):
        inj_ref[k] = inj[:, k * lanes:(k + 1) * lanes]
    ar = [jnp.broadcast_to(al_ref[0:1, k * lanes:(k + 1) * lanes], (nb, lanes)) for k in range(npl)]
    ai = [jnp.broadcast_to(al_ref[0:1, sdim + k * lanes:sdim + (k + 1) * lanes], (nb, lanes)) for k in range(npl)]

    def step(c, carry):
        xr, xi = carry
        rsel = pl.ds(c, nb, stride=nchunk)
        nr, ni = [], []
        for k in range(npl):
            xp_ref[k, rsel, :] = xr[k]
            xp_ref[npl + k, rsel, :] = xi[k]
            nr.append(ar[k] * xr[k] - ai[k] * xi[k] + inj_ref[k, rsel, :])
            ni.append(ar[k] * xi[k] + ai[k] * xr[k] + inj_ref[npl + k, rsel, :])
        return tuple(nr), tuple(ni)

    z = tuple(jnp.zeros((nb, lanes), F32) for _ in range(npl))
    lax.fori_loop(0, nchunk, step, (z, z), unroll=4)
    xp = jnp.concatenate([xp_ref[k] for k in range(2 * npl)], axis=1).astype(BF16)
    d = d_ref[...]
    for tp in range(0, L, 2):
        res = jnp.dot(ucat_ref[:, 0:(tp + 2) * lanes], grev_ref[(L - 1 - tp) * lanes:(L + 1) * lanes, :],
                      preferred_element_type=F32)
        res = res + jnp.dot(xp, wout_ref[:, tp * lanes:(tp + 2) * lanes], preferred_element_type=F32)
        for k in range(2):
            y = res[:, k * lanes:(k + 1) * lanes] + d * u_at(tp + k)
            y_ref[:, pl.ds(tp + k, nchunk, stride=L), :] = y.reshape(nb, nchunk, lanes)


def _s5(u, c0c, winc, woutc, al, d, *, nb=4):
    b, t, dch = u.shape
    nb = min(nb, b)
    nblk = dch // LANES
    L = SSM_CHUNK
    rows = nb * (t // L)
    hch, p = c0c.shape[3], winc.shape[3]
    gpb = LANES // hch
    sdim = gpb * p
    kcat = L * LANES
    lane, col, st = jnp.arange(LANES), jnp.arange(kcat), jnp.arange(sdim)
    rep_k = (jnp.arange(hch)[:, None] == (lane % hch)[None, :]).astype(BF16)
    rep_in = (jnp.arange(p)[:, None] == (st % p)[None, :]).astype(BF16)
    rep_out = (jnp.arange(L * hch)[:, None] == (col // LANES * hch + col % hch)[None, :]).astype(BF16)
    m_in = ((lane // hch)[:, None] == (st // p)[None, :]).astype(F32)
    m_out = ((st // p)[:, None] == (lane // hch)[None, :]).astype(F32)
    const = lambda a: pl.BlockSpec(a.shape, lambda j, i: (0,) * a.ndim)
    return pl.pallas_call(
        _s5_body,
        grid=(nblk, b // nb),
        in_specs=[
            pl.BlockSpec((nb, t, LANES), lambda j, i: (i, 0, j)),
            pl.BlockSpec((None, 2, sdim, hch), lambda j, i: (j, 0, 0, 0)),
            pl.BlockSpec((None, 2, kcat, p), lambda j, i: (j, 0, 0, 0)),
            pl.BlockSpec((None, 2, sdim, L * hch), lambda j, i: (j, 0, 0, 0)),
            const(rep_k), const(rep_in), const(rep_out), const(m_in), const(m_out),
            pl.BlockSpec((None, 8, 2 * sdim), lambda j, i: (j, 0, 0)),
            pl.BlockSpec((1, LANES), lambda j, i: (0, j)),
        ],
        out_specs=pl.BlockSpec((nb, t, LANES), lambda j, i: (i, 0, j)),
        out_shape=jax.ShapeDtypeStruct((b, t, dch), F32),
        scratch_shapes=[pltpu.VMEM((rows, kcat), BF16), pltpu.VMEM((2 * sdim // LANES, rows, LANES), F32),
                        pltpu.VMEM((2 * sdim // LANES, rows, LANES), F32),
                        pltpu.VMEM(((L + 1) * LANES, 2 * LANES), BF16), pltpu.VMEM((kcat, 2 * sdim), BF16),
                        pltpu.VMEM((2 * sdim, kcat), BF16)],
        compiler_params=_cparams(("parallel", "arbitrary")),
        name="s5",
    )(u, c0c, winc, woutc, rep_k, rep_in, rep_out, m_in, m_out, al, d.reshape(1, dch))


def _glu_out_body(h_ref, a_ref, y_ref, wg_ref, bg_ref, woa_ref, wos_ref, o_ref):
    hg = jax.nn.gelu(y_ref[...], approximate=True)
    z = jnp.dot(hg.astype(BF16), wg_ref[...], preferred_element_type=F32) + bg_ref[...]
    s = hg * jax.nn.sigmoid(z)
    mix = jnp.dot(a_ref[...], woa_ref[...], preferred_element_type=F32)
    mix = mix + jnp.dot(s.astype(BF16), wos_ref[...], preferred_element_type=F32)
    o_ref[...] = h_ref[...] + mix


def _glu_out(h, a, y, wg, bg, woa, wos, *, tm=512):
    n, dm = h.shape
    da, ds = a.shape[1], y.shape[1]
    row = lambda w: pl.BlockSpec((tm, w), lambda i: (i, 0))
    full = lambda r, c: pl.BlockSpec((r, c), lambda i: (0, 0))
    return pl.pallas_call(
        _glu_out_body,
        grid=(n // tm,),
        in_specs=[row(dm), row(da), row(ds), full(ds, ds), full(1, ds), full(da, dm), full(ds, dm)],
        out_specs=row(dm),
        out_shape=jax.ShapeDtypeStruct((n, dm), F32),
        compiler_params=_cparams(("parallel",)),
        name="glu_out",
    )(h, a, y, wg, bg.reshape(1, ds), woa, wos)


def _t5_bucket(dist):
    n = jnp.maximum(dist, 0)
    max_exact = N_BUCKETS // 2
    nf = jnp.maximum(n, 1).astype(F32)
    large = max_exact + (jnp.log(nf / max_exact) / math.log(MAX_DISTANCE / max_exact)
                         * (N_BUCKETS - max_exact)).astype(jnp.int32)
    large = jnp.minimum(large, N_BUCKETS - 1)
    return jnp.where(n < max_exact, n, large)


def _t5_body(first_ref, table_ref, bc_ref, dt_ref, wb_ref, *, nc):
    g = pl.program_id(0)
    nt = bc_ref.shape[0]
    span = wb_ref.shape[1]
    a = lax.broadcasted_iota(jnp.int32, (TQ, LANES), 0)
    c = lax.broadcasted_iota(jnp.int32, (TQ, LANES), 1)
    heads = [g * HEADS_PER_KV + h for h in range(HEADS_PER_KV)]

    def lookup(dist, valid):
        bias = [jnp.full((TQ, LANES), table_ref[0, hd], F32) for hd in heads]
        for k in range(1, N_BUCKETS):
            ge = dist >= first_ref[k]
            bias = [jnp.where(ge, table_ref[k, hd], bv) for hd, bv in zip(heads, bias)]
        return bias if valid is None else [jnp.where(valid, bv, NEG) for bv in bias]

    def put(ref_at, tiles):
        for h, tile in enumerate(tiles):
            ref_at(slice(h * TQ, (h + 1) * TQ), tile)

    def put_dt(v):
        def f(rows, tile):
            dt_ref[v, rows, :] = tile
        return f

    put(put_dt(0), lookup(jnp.full((TQ, LANES), 2 * MAX_DISTANCE, jnp.int32), None))
    for r in range(-1, QK):
        dist = a - r * LANES - c
        put(put_dt(r + 2), lookup(dist, dist >= 0))
    dt_ref[QK + 2] = jnp.full((ROWS4, LANES), NEG, F32)
    for j in range(span // LANES):
        dist = a + WINDOW - (c + j * LANES)

        def put_wb(rows, tile, j=j):
            wb_ref[rows, j * LANES:(j + 1) * LANES] = tile

        put(put_wb, lookup(dist, (dist >= 0) & (dist < WINDOW)))

    def cmp_tile(i, _):
        dist = i * TQ + a - (c * CMP_STRIDE + CMP_LEN - 1)

        def put_bc(rows, tile):
            bc_ref[i, rows, :] = tile

        put(put_bc, lookup(dist, (dist >= 0) & (c < nc)))
        return 0

    lax.fori_loop(0, nt, cmp_tile, 0)


def _bias_tables(rel_bias, t):
    nt = t // TQ
    span = TQ + WINDOW
    nc = (t - CMP_LEN) // CMP_STRIDE + 1
    buckets = _t5_bucket(jnp.arange(2 * MAX_DISTANCE))
    first = jnp.sum((buckets[None, :] < jnp.arange(N_BUCKETS)[:, None]).astype(jnp.int32), axis=1)
    smem = pl.BlockSpec(memory_space=pltpu.SMEM)
    return pl.pallas_call(
        functools.partial(_t5_body, nc=nc),
        grid=(N_KV,),
        in_specs=[smem, smem],
        out_specs=[
            pl.BlockSpec((None, nt, ROWS4, LANES), lambda g: (g, 0, 0, 0)),
            pl.BlockSpec((None, QK + 3, ROWS4, LANES), lambda g: (g, 0, 0, 0)),
            pl.BlockSpec((None, ROWS4, span), lambda g: (g, 0, 0)),
        ],
        out_shape=[
            jax.ShapeDtypeStruct((N_KV, nt, ROWS4, LANES), F32),
            jax.ShapeDtypeStruct((N_KV, QK + 3, ROWS4, LANES), F32),
            jax.ShapeDtypeStruct((N_KV, ROWS4, span), F32),
        ],
        compiler_params=_cparams(("parallel",)),
        name="t5_tables",
    )(first, rel_bias.astype(F32))


def _sel_tables(t):
    ns = t // SEL_BLOCK
    nc_pad = LANES
    c_start = jnp.arange(nc_pad) * CMP_STRIDE
    j_start = jnp.arange(LANES) * SEL_BLOCK
    ov = jnp.clip(jnp.minimum(c_start[:, None] + CMP_LEN, j_start[None, :] + SEL_BLOCK)
                  - jnp.maximum(c_start[:, None], j_start[None, :]), 0, None).astype(F32) / CMP_LEN
    ov = jnp.where(jnp.arange(LANES)[None, :] < ns, ov, 0.0)
    et = jnp.arange(t)[:, None] // SEL_BLOCK == jnp.arange(LANES)[None, :]
    return ov.T.astype(BF16), et.astype(BF16)


def _s5_tables(lam_re, lam_im, log_step, b_re, b_im, c_re, c_im):
    ng, p = lam_re.shape
    hch = b_re.shape[2]
    L = SSM_CHUNK
    step = jnp.exp(log_step.astype(F32))[:, None]
    lre, lim = lam_re.astype(F32), lam_im.astype(F32)
    mag = jnp.exp(lre * step)
    ab_re, ab_im = mag * jnp.cos(lim * step), mag * jnp.sin(lim * step)
    nr, ni = ab_re - 1.0, ab_im
    den = lre * lre + lim * lim
    f_re, f_im = (nr * lre + ni * lim) / den, (ni * lre - nr * lim) / den
    br, bim = b_re.astype(F32), b_im.astype(F32)
    bb_re = f_re[..., None] * br - f_im[..., None] * bim
    bb_im = f_re[..., None] * bim + f_im[..., None] * br
    cr, ci = c_re.astype(F32), c_im.astype(F32)
    pr, pi = [jnp.ones_like(ab_re)], [jnp.zeros_like(ab_re)]
    for _ in range(L):
        pr, pi = pr + [pr[-1] * ab_re - pi[-1] * ab_im], pi + [pr[-1] * ab_im + pi[-1] * ab_re]
    pw_re, pw_im = jnp.stack(pr, 0), jnp.stack(pi, 0)
    cp_re = cr[None] * pw_re[:, :, None, :] - ci[None] * pw_im[:, :, None, :]
    cp_im = -(cr[None] * pw_im[:, :, None, :] + ci[None] * pw_re[:, :, None, :])
    gpb = LANES // hch
    nblk = ng // gpb
    wr = pw_re[L - 1 - jnp.arange(L)]
    wi = pw_im[L - 1 - jnp.arange(L)]
    win_re = (wr[..., None] * bb_re[None] - wi[..., None] * bb_im[None])
    win_im = (wr[..., None] * bb_im[None] + wi[..., None] * bb_re[None])
    inj_op = lambda m: m.reshape(L, nblk, gpb, p, hch).transpose(1, 0, 2, 4, 3).reshape(nblk, L * LANES, p)
    winc = jnp.stack([inj_op(win_re), inj_op(win_im)], axis=1)
    read_op = lambda m: m.reshape(L, nblk, gpb, hch, p).transpose(1, 2, 4, 0, 3).reshape(nblk, gpb * p, L * hch)
    woutc = jnp.stack([read_op(cp_re[1:]), read_op(cp_im[1:])], axis=1)
    read0 = lambda m: m.reshape(nblk, gpb, hch, p).transpose(0, 1, 3, 2).reshape(nblk, gpb * p, hch)
    c0c = jnp.stack([read0(cp_re[0]), read0(cp_im[0])], axis=1)
    al = jnp.concatenate([pw_re[L].reshape(nblk, gpb * p), pw_im[L].reshape(nblk, gpb * p)], axis=1)
    al = jnp.broadcast_to(al[:, None, :], (nblk, 8, 2 * gpb * p))
    return c0c.astype(BF16), winc.astype(BF16), woutc.astype(BF16), al


def _mixers(h1, mix_norm, w_in, cmp_k, cmp_v, rel_bias, ssm, ssm_d, glu_w, glu_b, w_out, b, t):
    n, d = h1.shape
    d_ssm = glu_w.shape[0]
    o_g, o_u = D_ATTN + 6 * D_KV, D_ATTN + 6 * D_KV + 3 * N_HEADS
    gcols = 3 * HEADS_PER_KV
    wg = [jnp.pad(w_in[:, o_g + g * gcols:o_g + (g + 1) * gcols], ((0, 0), (0, LANES - gcols))) for g in range(N_KV)]
    w_perm = jnp.concatenate([w_in[:, :o_g], w_in[:, o_u:]] + wg, axis=1).astype(BF16)
    q, kvs, kcv, u, gates = _inproj(h1.reshape(b, t, d), mix_norm, w_perm, d_ssm)

    nblk = t // CMP_STRIDE
    assert nblk == LANES, "compressed-block axis is laid out on one 128-lane tile"
    pe = jnp.stack([cmp_k[0], cmp_v[0]], 0).astype(F32)
    w1 = jnp.stack([cmp_k[1], cmp_v[1]], 0).astype(BF16).reshape(2, CMP_LEN, HEAD_DIM, -1)
    b1 = jnp.stack([cmp_k[2].reshape(1, -1), cmp_v[2].reshape(1, -1)], 0).astype(F32)
    w2 = jnp.stack([cmp_k[3], cmp_v[3]], 0).astype(BF16)
    cb = _compress(kcv, pe, w1, b1, w2, nblk)

    biasc, dtiles, wb = _bias_tables(rel_bias, t)
    ovt, et = _sel_tables(t)
    a = _nsa(q, kvs, cb, gates, biasc, dtiles, wb, ovt, et)

    c0c, winc, woutc, al = _s5_tables(*ssm)
    y = _s5(u, c0c, winc, woutc, al, ssm_d)

    wo = w_out.astype(BF16)
    return _glu_out(h1, a.reshape(n, D_ATTN), y.reshape(n, d_ssm), glu_w.astype(BF16), glu_b, wo[:D_ATTN], wo[D_ATTN:])


def kernel(x, ffn1_norm, ffn1_w1, ffn1_w3, ffn1_w2, mix_norm, w_in, cmp_pe_k, cmp_w1_k, cmp_b1_k, cmp_w2_k,
           cmp_pe_v, cmp_w1_v, cmp_b1_v, cmp_w2_v, rel_bias, ssm_lam_re, ssm_lam_im, ssm_log_step, ssm_b_re,
           ssm_b_im, ssm_c_re, ssm_c_im, ssm_d, glu_w, glu_b, w_out, ffn2_norm, ffn2_w1, ffn2_w3, ffn2_w2,
           final_norm):
    b, t, d = x.shape
    depth = ffn1_w1.shape[0]
    h = x.reshape(b * t, d)
    for l in range(depth):
        last = l == depth - 1
        h = _ffn(h, ffn1_norm[l], ffn1_w1[l].astype(BF16), ffn1_w3[l].astype(BF16), ffn1_w2[l].astype(BF16))
        ssm = (ssm_lam_re[l], ssm_lam_im[l], ssm_log_step[l], ssm_b_re[l], ssm_b_im[l], ssm_c_re[l], ssm_c_im[l])
        h = _mixers(h, mix_norm[l], w_in[l],
                    (cmp_pe_k[l], cmp_w1_k[l], cmp_b1_k[l], cmp_w2_k[l]),
                    (cmp_pe_v[l], cmp_w1_v[l], cmp_b1_v[l], cmp_w2_v[l]),
                    rel_bias, ssm, ssm_d[l], glu_w[l], glu_b[l], w_out[l], b, t)
        h = _ffn(h, ffn2_norm[l], ffn2_w1[l].astype(BF16), ffn2_w3[l].astype(BF16), ffn2_w2[l].astype(BF16),
                 final_gain=final_norm if last else None)
    return h.reshape(b, t, d)
```

```python
import functools
import math

import jax
import jax.numpy as jnp
from jax import lax
from jax.experimental import pallas as pl
from jax.experimental.pallas import tpu as pltpu

F32 = jnp.float32
BF16 = jnp.bfloat16

HEAD_DIM = 128
N_KV = 2
HEADS_PER_KV = 4
N_HEADS = N_KV * HEADS_PER_KV
D_ATTN = N_HEADS * HEAD_DIM
D_KV = N_KV * HEAD_DIM
CMP_LEN = 32
CMP_STRIDE = 16
SEL_BLOCK = 64
N_SELECT = 16
WINDOW = 512
N_BUCKETS = 32
MAX_DISTANCE = 128
SSM_GROUP = 16
SSM_STATE = 64
EPS = 1e-6
NEG = -1e30

LANES = 128
VMEM_LIMIT_BYTES = 56 * 1024 * 1024
NSA_VMEM_LIMIT_BYTES = 60 * 1024 * 1024
TQ = 256
QK = TQ // LANES
ROWS4 = HEADS_PER_KV * TQ
SEL_CHUNK = 512
SSM_CHUNK = 16


def _cparams(sem, vmem_limit_bytes=VMEM_LIMIT_BYTES):
    return pltpu.CompilerParams(dimension_semantics=sem, vmem_limit_bytes=vmem_limit_bytes)


def _ffn_body(x_ref, g_ref, w1_ref, w3_ref, w2_ref, *rest, final):
    if final:
        fg_ref, o_ref, xn_ref = rest
    else:
        o_ref, xn_ref = rest
    j = pl.program_id(1)
    nj = pl.num_programs(1)

    def contribution(xn):
        a = jnp.dot(xn, w1_ref[...], preferred_element_type=F32)
        b = jnp.dot(xn, w3_ref[...], preferred_element_type=F32)
        gated = (a * jax.nn.sigmoid(a)) * b
        return jnp.dot(gated.astype(BF16), w2_ref[...], preferred_element_type=F32)

    @pl.when(j == 0)
    def _():
        x = x_ref[...]
        ms = jnp.mean(x * x, axis=-1, keepdims=True)
        xn = (x * lax.rsqrt(ms + EPS) * g_ref[...]).astype(BF16)
        xn_ref[...] = xn
        o_ref[...] = contribution(xn)

    @pl.when((j > 0) & (j < nj - 1))
    def _():
        o_ref[...] += contribution(xn_ref[...])

    @pl.when(j == nj - 1)
    def _():
        h = x_ref[...] + 0.5 * (o_ref[...] + contribution(xn_ref[...]))
        if final:
            ms = jnp.mean(h * h, axis=-1, keepdims=True)
            h = h * lax.rsqrt(ms + EPS) * fg_ref[...]
        o_ref[...] = h


def _ffn(x, gain, w1, w3, w2, final_gain=None, *, tm=512, tf=512):
    n, d = x.shape
    dff = w1.shape[1]
    nj = dff // tf
    assert nj >= 2 and nj * tf == dff
    final = final_gain is not None
    tiled = lambda w: w.astype(BF16).reshape(d, nj, tf).transpose(1, 0, 2)
    in_specs = [
        pl.BlockSpec((tm, d), lambda i, j: (i, 0)),
        pl.BlockSpec((1, d), lambda i, j: (0, 0)),
        pl.BlockSpec((None, d, tf), lambda i, j: (j, 0, 0)),
        pl.BlockSpec((None, d, tf), lambda i, j: (j, 0, 0)),
        pl.BlockSpec((tf, d), lambda i, j: (j, 0)),
    ]
    args = [x, gain.reshape(1, d), tiled(w1), tiled(w3), w2.astype(BF16)]
    if final:
        in_specs.append(pl.BlockSpec((1, d), lambda i, j: (0, 0)))
        args.append(final_gain.reshape(1, d))
    return pl.pallas_call(
        functools.partial(_ffn_body, final=final),
        grid=(n // tm, nj),
        in_specs=in_specs,
        out_specs=pl.BlockSpec((tm, d), lambda i, j: (i, 0)),
        out_shape=jax.ShapeDtypeStruct((n, d), F32),
        scratch_shapes=[pltpu.VMEM((tm, d), BF16)],
        compiler_params=_cparams(("parallel", "arbitrary")),
        name="ffn_final" if final else "ffn",
    )(*args)


def _inproj_body(x_ref, g_ref, w_ref, q_ref, kvs_ref, kcv_ref, u_ref, gate_ref):
    i = pl.program_id(1)

    @pl.when(i == 0)
    def _():
        kvs_ref[...] = jnp.zeros(kvs_ref.shape, kvs_ref.dtype)
        kcv_ref[...] = jnp.zeros(kcv_ref.shape, kcv_ref.dtype)

    @pl.when(i > 0)
    def _():
        x = x_ref[...]
        ms = jnp.mean(x * x, axis=-1, keepdims=True)
        xn = (x * lax.rsqrt(ms + EPS) * g_ref[...]).astype(BF16)
        c0 = 0
        c1 = q_ref.shape[1]
        q = jnp.dot(xn, w_ref[:, c0:c1], preferred_element_type=F32)
        q_ref[...] = (q * (HEAD_DIM ** -0.5)).astype(BF16)
        c0, c1 = c1, c1 + kcv_ref.shape[1]
        kcv_ref[...] = jnp.dot(xn, w_ref[:, c0:c1], preferred_element_type=F32)
        c0, c1 = c1, c1 + kvs_ref.shape[1]
        kvs_ref[...] = jnp.dot(xn, w_ref[:, c0:c1], preferred_element_type=F32).astype(BF16)
        c0, c1 = c1, c1 + u_ref.shape[1]
        u_ref[...] = jnp.dot(xn, w_ref[:, c0:c1], preferred_element_type=F32)
        gate_ref[...] = jax.nn.sigmoid(jnp.dot(xn, w_ref[:, c1:], preferred_element_type=F32))


def _inproj(h, gain, w_perm, d_ssm):
    b, t, d = h.shape
    tm = WINDOW
    nt = t // tm
    nq, ncv, nvs, ng = D_ATTN, 2 * D_KV, 4 * D_KV, N_KV * LANES
    ncol = w_perm.shape[1]
    assert ncol == nq + ncv + nvs + d_ssm + ng and t % tm == 0
    data = lambda bi, i: (bi, jnp.maximum(i - 1, 0), 0)
    return pl.pallas_call(
        _inproj_body,
        grid=(b, nt + 1),
        in_specs=[
            pl.BlockSpec((None, tm, d), data),
            pl.BlockSpec((1, d), lambda bi, i: (0, 0)),
            pl.BlockSpec((d, ncol), lambda bi, i: (0, 0)),
        ],
        out_specs=[
            pl.BlockSpec((None, tm, nq), data),
            pl.BlockSpec((None, tm, nvs), lambda bi, i: (bi, i, 0)),
            pl.BlockSpec((None, tm, ncv), lambda bi, i: (bi, jnp.where(i == 0, nt, i - 1), 0)),
            pl.BlockSpec((None, tm, d_ssm), data),
            pl.BlockSpec((None, tm, ng), data),
        ],
        out_shape=[
            jax.ShapeDtypeStruct((b, t, nq), BF16),
            jax.ShapeDtypeStruct((b, t + tm, nvs), BF16),
            jax.ShapeDtypeStruct((b, t + tm, ncv), F32),
            jax.ShapeDtypeStruct((b, t, d_ssm), F32),
            jax.ShapeDtypeStruct((b, t, ng), F32),
        ],
        compiler_params=_cparams(("parallel", "arbitrary")),
        name="inproj",
    )(h, gain.reshape(1, d), w_perm)


def _compress_body(x_ref, pe_ref, w1_ref, b1_ref, w2_ref, o_ref):
    nb, nblk = o_ref.shape[0], o_ref.shape[1]
    acc = jnp.broadcast_to(b1_ref[...], (nb * nblk, b1_ref.shape[1])).astype(F32)
    for l in range(CMP_LEN):
        xl = x_ref[:, pl.ds(l, nblk, stride=CMP_STRIDE), :] + pe_ref[l:l + 1, :]
        acc = acc + jnp.dot(xl.reshape(nb * nblk, xl.shape[2]).astype(BF16), w1_ref[l], preferred_element_type=F32)
    hid = jax.nn.gelu(acc, approximate=True)
    out = jnp.dot(hid.astype(BF16), w2_ref[...], preferred_element_type=F32)
    o_ref[...] = out.reshape(o_ref.shape).astype(BF16)


def _compress(kcv, pe, w1, b1, w2, nblk, *, nb=4):
    b, tpad, _ = kcv.shape
    nb = min(nb, b)
    hid, dh = w1.shape[3], w2.shape[2]
    assert (nblk - 1) * CMP_STRIDE + CMP_LEN <= tpad
    return pl.pallas_call(
        _compress_body,
        grid=(2 * N_KV, b // nb),
        in_specs=[
            pl.BlockSpec((nb, tpad, dh), lambda s, i: (i, 0, s)),
            pl.BlockSpec((None, CMP_LEN, dh), lambda s, i: (s // N_KV, 0, 0)),
            pl.BlockSpec((None, CMP_LEN, dh, hid), lambda s, i: (s // N_KV, 0, 0, 0)),
            pl.BlockSpec((None, 1, hid), lambda s, i: (s // N_KV, 0, 0)),
            pl.BlockSpec((None, hid, dh), lambda s, i: (s // N_KV, 0, 0)),
        ],
        out_specs=pl.BlockSpec((None, None, nb, nblk, dh), lambda s, i: (s // N_KV, s % N_KV, i, 0, 0)),
        out_shape=jax.ShapeDtypeStruct((2, N_KV, b, nblk, dh), BF16),
        compiler_params=_cparams(("parallel", "parallel")),
        name="compress",
    )(kcv, pe, w1, b1, w2)


def _nt_dot(a, b):
    return lax.dot_general(a, b, (((1,), (1,)), ((), ())), preferred_element_type=F32)


def _nsa_body(q_ref, ks_ref, vs_ref, kw_ref, vw_ref, kcb_ref, vcb_ref, gate_ref, biasc_ref, dt_ref, wb_ref,
              ovt_ref, et_ref, o_ref, kse_ref, s_ref, m_ref, l_ref, acc_ref):
    i = pl.program_id(1)
    t0 = i * TQ
    t = et_ref.shape[0]
    ns = t // SEL_BLOCK
    nsel = min(N_SELECT, ns)
    groups = range(N_KV)
    gl = lambda g: slice(g * HEAD_DIM, (g + 1) * HEAD_DIM)

    @pl.when(i == 0)
    def _():
        for g in groups:
            kse_ref[g, :, 0:HEAD_DIM] = ks_ref[WINDOW:WINDOW + t, gl(g)]
            kse_ref[g, :, HEAD_DIM:] = et_ref[...]

    q = q_ref[...]
    w0 = pl.multiple_of(t0, LANES)
    span = wb_ref.shape[2]

    q4 = [jnp.concatenate([q[:, (g * HEADS_PER_KV + h) * HEAD_DIM:(g * HEADS_PER_KV + h + 1) * HEAD_DIM]
                           for h in range(HEADS_PER_KV)], axis=0) for g in groups]

    sc = [_nt_dot(q4[g], kcb_ref[g]) + biasc_ref[g] for g in groups]
    row_t = t0 + (lax.broadcasted_iota(jnp.int32, (ROWS4, LANES), 0) & (TQ - 1))
    col_c = lax.broadcasted_iota(jnp.int32, (ROWS4, LANES), 1)
    valid_c = (row_t - col_c * CMP_STRIDE - (CMP_LEN - 1) >= 0) & (col_c < LANES - 1)
    mc = [jnp.max(s, axis=-1, keepdims=True) for s in sc]
    pc = [jnp.where(valid_c, jnp.exp(s - m), 0.0) for s, m in zip(sc, mc)]
    pc = [p / jnp.maximum(jnp.sum(p, axis=-1, keepdims=True), 1e-30) for p in pc]
    pcb = [p.astype(BF16) for p in pc]
    o_cmps = [jnp.dot(pcb[g], vcb_ref[g], preferred_element_type=F32) for g in groups]
    pimp = [_nt_dot(ovt_ref[...], p) for p in pcb]
    imp = [sum(p[0:ns, h * TQ:(h + 1) * TQ] for h in range(1, HEADS_PER_KV)) + p[0:ns, 0:TQ] for p in pimp]

    jrow = lax.broadcasted_iota(jnp.int32, (ns, TQ), 0)
    tpos = t0 + lax.broadcasted_iota(jnp.int32, (ns, TQ), 1)
    cur = lax.shift_right_logical(tpos, int(math.log2(SEL_BLOCK)))
    forced = (jrow == 0) | (jrow == cur) | (jrow == cur - 1)
    impm = [jnp.where(forced, 1e6, jnp.where(jrow * SEL_BLOCK <= tpos, x, -1e9)) for x in imp]
    nslab = ns // 8
    slabs = [[x[8 * v:8 * v + 8] for v in range(nslab)] for x in impm]
    cnts = [[jnp.zeros((8, TQ), F32) for _ in range(nslab)] for _ in groups]
    sub = lax.broadcasted_iota(jnp.int32, (8, TQ), 0)
    for jp in range(ns):
        v0, r0 = divmod(jp, 8)
        for g in groups:
            row = jnp.broadcast_to(slabs[g][v0][r0:r0 + 1, :], (8, TQ))
            for v in range(nslab):
                if v > v0:
                    beats = row >= slabs[g][v]
                elif v < v0:
                    beats = row > slabs[g][v]
                else:
                    beats = (row > slabs[g][v]) | ((row == slabs[g][v]) & (sub > r0))
                cnts[g][v] = cnts[g][v] + jnp.where(beats, 1.0, 0.0)
    q4s = []
    for g in groups:
        negt = [jnp.where(c < nsel, 0.0, NEG) for c in cnts[g]]
        negt = jnp.concatenate(negt + [jnp.zeros((LANES - ns, TQ), F32)], axis=0)
        neg = negt.T.astype(BF16)
        q4s.append(jnp.concatenate([q4[g], jnp.concatenate([neg] * HEADS_PER_KV, axis=0)], axis=1))

    nsub = SEL_CHUNK // LANES
    nchunks = ((i + 1) * QK - 1) // nsub + 1
    m_ref[...] = jnp.full(m_ref.shape, -3e38, F32)

    def score_chunk(ck, _):
        kb = pl.multiple_of(ck * SEL_CHUNK, SEL_CHUNK)
        s = [_nt_dot(q4s[g], kse_ref[g, pl.ds(kb, SEL_CHUNK), :]) for g in groups]
        for g in groups:
            mloc = None
            for j in range(nsub):
                sj = s[g][:, j * LANES:(j + 1) * LANES] + dt_ref[g, jnp.clip(ck * nsub + j - i * QK + 2, 0, QK + 2)]
                s_ref[g, ck, :, j * LANES:(j + 1) * LANES] = sj
                mloc = sj if mloc is None else jnp.maximum(mloc, sj)
            m_ref[g] = jnp.maximum(m_ref[g], mloc)
        return 0

    lax.fori_loop(0, nchunks, score_chunk, 0)
    for g in groups:
        m_ref[g] = jnp.broadcast_to(jnp.max(m_ref[g], axis=-1, keepdims=True), (ROWS4, LANES))
    l_ref[...] = jnp.zeros(l_ref.shape, F32)
    acc_ref[...] = jnp.zeros(acc_ref.shape, F32)

    kneg = jnp.where(t0 - WINDOW + lax.broadcasted_iota(jnp.int32, (1, span), 1) >= 0, 0.0, NEG)
    sw = [_nt_dot(q4[g], kw_ref[pl.ds(w0, span), gl(g)]) + wb_ref[g] + kneg for g in groups]
    mw = [jnp.max(s, axis=-1, keepdims=True) for s in sw]
    pw = [jnp.exp(s - m) for s, m in zip(sw, mw)]
    lw = [jnp.sum(p, axis=-1, keepdims=True) for p in pw]
    o_wins = [jnp.dot(pw[g].astype(BF16), vw_ref[pl.ds(w0, span), gl(g)], preferred_element_type=F32) for g in groups]
    o_wins = [o / jnp.maximum(l, 1e-30) for o, l in zip(o_wins, lw)]

    def pv_chunk(ck, _):
        kb = pl.multiple_of(WINDOW + ck * SEL_CHUNK, SEL_CHUNK)
        ps = [[jnp.exp(s_ref[g, ck, :, j * LANES:(j + 1) * LANES] - m_ref[g]) for j in range(nsub)] for g in groups]
        for g in groups:
            lsum = ps[g][0]
            for pj in ps[g][1:]:
                lsum = lsum + pj
            l_ref[g] += lsum
            p = jnp.concatenate([pj.astype(BF16) for pj in ps[g]], axis=1)
            acc_ref[g] += jnp.dot(p, vs_ref[pl.ds(kb, SEL_CHUNK), gl(g)], preferred_element_type=F32)
        return 0

    lax.fori_loop(0, nchunks, pv_chunk, 0)

    ls = [jnp.maximum(jnp.sum(l_ref[g], axis=-1, keepdims=True), 1e-30) for g in groups]
    o_sels = [acc_ref[g] / ls[g] for g in groups]
    gts = [gate_ref[:, g * LANES:(g + 1) * LANES] for g in groups]
    for h in range(HEADS_PER_KV):
        r = slice(h * TQ, (h + 1) * TQ)
        for g in groups:
            gt = gts[g]
            o_h = (gt[:, 3 * h:3 * h + 1] * o_cmps[g][r] + gt[:, 3 * h + 1:3 * h + 2] * o_sels[g][r]
                   + gt[:, 3 * h + 2:3 * h + 3] * o_wins[g][r])
            c0 = (g * HEADS_PER_KV + h) * HEAD_DIM
            o_ref[:, c0:c0 + HEAD_DIM] = o_h.astype(BF16)


def _nsa(q, kvs, cb, gates, biasc, dtiles, wb, ovt, et):
    b, t, _ = q.shape
    nt = t // TQ
    tp = kvs.shape[1]
    span = wb.shape[2]
    gw = HEADS_PER_KV * HEAD_DIM
    assert tp == t + WINDOW and span == TQ + WINDOW and t % SEL_CHUNK == 0 and (t // SEL_BLOCK) % 8 == 0

    def stream(j):
        return pl.BlockSpec((None, tp, N_KV * HEAD_DIM), lambda bi, i: (bi, 0, j))

    return pl.pallas_call(
        _nsa_body,
        grid=(b, nt),
        in_specs=[
            pl.BlockSpec((None, TQ, N_KV * gw), lambda bi, i: (bi, i, 0)),
            stream(0), stream(1), stream(2), stream(3),
            pl.BlockSpec((None, N_KV, None, LANES, HEAD_DIM), lambda bi, i: (0, 0, bi, 0, 0)),
            pl.BlockSpec((None, N_KV, None, LANES, HEAD_DIM), lambda bi, i: (1, 0, bi, 0, 0)),
            pl.BlockSpec((None, TQ, N_KV * LANES), lambda bi, i: (bi, i, 0)),
            pl.BlockSpec((N_KV, None, ROWS4, LANES), lambda bi, i: (0, i, 0, 0)),
            pl.BlockSpec((N_KV, QK + 3, ROWS4, LANES), lambda bi, i: (0, 0, 0, 0)),
            pl.BlockSpec((N_KV, ROWS4, span), lambda bi, i: (0, 0, 0)),
            pl.BlockSpec((LANES, LANES), lambda bi, i: (0, 0)),
            pl.BlockSpec((t, LANES), lambda bi, i: (0, 0)),
        ],
        out_specs=pl.BlockSpec((None, TQ, N_KV * gw), lambda bi, i: (bi, i, 0)),
        out_shape=jax.ShapeDtypeStruct((b, t, N_KV * gw), BF16),
        scratch_shapes=[
            pltpu.VMEM((N_KV, t, 2 * HEAD_DIM), BF16),
            pltpu.VMEM((N_KV, t // SEL_CHUNK, ROWS4, SEL_CHUNK), F32),
            pltpu.VMEM((N_KV, ROWS4, LANES), F32),
            pltpu.VMEM((N_KV, ROWS4, LANES), F32),
            pltpu.VMEM((N_KV, ROWS4, HEAD_DIM), F32),
        ],
        compiler_params=_cparams(("parallel", "arbitrary"), NSA_VMEM_LIMIT_BYTES),
        name="nsa",
    )(q, kvs, kvs, kvs, kvs, cb, cb, gates, biasc, dtiles, wb, ovt, et)


def _s5_body(u_ref, c0_ref, winc_ref, woutc_ref, rep_k_ref, rep_in_ref, rep_out_ref, m_in_ref, m_out_ref,
             al_ref, d_ref, y_ref, ucat_ref, inj_ref, xp_ref, grev_ref, win_ref, wout_ref):
    nb, t, lanes = u_ref.shape
    L = SSM_CHUNK
    nchunk = t // L
    rows = nb * nchunk
    sdim = al_ref.shape[1] // 2

    @pl.when(pl.program_id(1) == 0)
    def _():
        m_in, m_out = m_in_ref[...], m_out_ref[...]
        for half in range(2):
            for s in range(L):
                x = jnp.dot(winc_ref[half, s * lanes:(s + 1) * lanes, :], rep_in_ref[...], preferred_element_type=F32)
                win_ref[s * lanes:(s + 1) * lanes, half * sdim:(half + 1) * sdim] = (x * m_in).astype(BF16)
            for tt in range(L):
                x = jnp.dot(woutc_ref[half], rep_out_ref[:, tt * lanes:(tt + 1) * lanes], preferred_element_type=F32)
                wout_ref[half * sdim:(half + 1) * sdim, tt * lanes:(tt + 1) * lanes] = (x * m_out).astype(BF16)
        c0 = [(jnp.dot(c0_ref[half], rep_k_ref[...], preferred_element_type=F32) * m_out).astype(BF16)
              for half in range(2)]
        kx = jnp.dot(win_ref[...], jnp.concatenate(c0, axis=0), preferred_element_type=F32)
        zero = jnp.zeros((lanes, lanes), BF16)

        def kblk(tau):
            s = L - 1 - tau
            return kx[s * lanes:(s + 1) * lanes].astype(BF16) if 0 <= tau < L else zero

        for r in range(L + 1):
            grev_ref[r * lanes:(r + 1) * lanes, 0:lanes] = kblk(L - r - 1)
            grev_ref[r * lanes:(r + 1) * lanes, lanes:] = kblk(L - r)

    def u_at(s):
        return u_ref[:, pl.ds(s, nchunk, stride=L), :].reshape(rows, lanes)

    for s in range(L):
        ucat_ref[:, s * lanes:(s + 1) * lanes] = u_at(s).astype(BF16)
    inj = jnp.dot(ucat_ref[...], win_ref[...], preferred_element_type=F32)
    npl = sdim // lanes
    for k in range(2 * npl):
        inj_ref[k] = inj[:, k * lanes:(k + 1) * lanes]
    ar = [jnp.broadcast_to(al_ref[0:1, k * lanes:(k + 1) * lanes], (nb, lanes)) for k in range(npl)]
    ai = [jnp.broadcast_to(al_ref[0:1, sdim + k * lanes:sdim + (k + 1) * lanes], (nb, lanes)) for k in range(npl)]

    def step(c, carry):
        xr, xi = carry
        rsel = pl.ds(c, nb, stride=nchunk)
        nr, ni = [], []
        for k in range(npl):
            xp_ref[k, rsel, :] = xr[k]
            xp_ref[npl + k, rsel, :] = xi[k]
            nr.append(ar[k] * xr[k] - ai[k] * xi[k] + inj_ref[k, rsel, :])
            ni.append(ar[k] * xi[k] + ai[k] * xr[k] + inj_ref[npl + k, rsel, :])
        return tuple(nr), tuple(ni)

    z = tuple(jnp.zeros((nb, lanes), F32) for _ in range(npl))
    lax.fori_loop(0, nchunk, step, (z, z), unroll=4)
    xp = jnp.concatenate([xp_ref[k] for k in range(2 * npl)], axis=1).astype(BF16)
    d = d_ref[...]
    for tp in range(0, L, 2):
        res = jnp.dot(ucat_ref[:, 0:(tp + 2) * lanes], grev_ref[(L - 1 - tp) * lanes:(L + 1) * lanes, :],
                      preferred_element_type=F32)
        res = res + jnp.dot(xp, wout_ref[:, tp * lanes:(tp + 2) * lanes], preferred_element_type=F32)
        for k in range(2):
            y = res[:, k * lanes:(k + 1) * lanes] + d * u_at(tp + k)
            y_ref[:, pl.ds(tp + k, nchunk, stride=L), :] = y.reshape(nb, nchunk, lanes)


def _s5(u, c0c, winc, woutc, al, d, *, nb=4):
    b, t, dch = u.shape
    nb = min(nb, b)
    nblk = dch // LANES
    L = SSM_CHUNK
    rows = nb * (t // L)
    hch, p = c0c.shape[3], winc.shape[3]
    gpb = LANES // hch
    sdim = gpb * p
    kcat = L * LANES
    lane, col, st = jnp.arange(LANES), jnp.arange(kcat), jnp.arange(sdim)
    rep_k = (jnp.arange(hch)[:, None] == (lane % hch)[None, :]).astype(BF16)
    rep_in = (jnp.arange(p)[:, None] == (st % p)[None, :]).astype(BF16)
    rep_out = (jnp.arange(L * hch)[:, None] == (col // LANES * hch + col % hch)[None, :]).astype(BF16)
    m_in = ((lane // hch)[:, None] == (st // p)[None, :]).astype(F32)
    m_out = ((st // p)[:, None] == (lane // hch)[None, :]).astype(F32)
    const = lambda a: pl.BlockSpec(a.shape, lambda j, i: (0,) * a.ndim)
    return pl.pallas_call(
        _s5_body,
        grid=(nblk, b // nb),
        in_specs=[
            pl.BlockSpec((nb, t, LANES), lambda j, i: (i, 0, j)),
            pl.BlockSpec((None, 2, sdim, hch), lambda j, i: (j, 0, 0, 0)),
            pl.BlockSpec((None, 2, kcat, p), lambda j, i: (j, 0, 0, 0)),
            pl.BlockSpec((None, 2, sdim, L * hch), lambda j, i: (j, 0, 0, 0)),
            const(rep_k), const(rep_in), const(rep_out), const(m_in), const(m_out),
            pl.BlockSpec((None, 8, 2 * sdim), lambda j, i: (j, 0, 0)),
            pl.BlockSpec((1, LANES), lambda j, i: (0, j)),
        ],
        out_specs=pl.BlockSpec((nb, t, LANES), lambda j, i: (i, 0, j)),
        out_shape=jax.ShapeDtypeStruct((b, t, dch), F32),
        scratch_shapes=[pltpu.VMEM((rows, kcat), BF16), pltpu.VMEM((2 * sdim // LANES, rows, LANES), F32),
                        pltpu.VMEM((2 * sdim // LANES, rows, LANES), F32),
                        pltpu.VMEM(((L + 1) * LANES, 2 * LANES), BF16), pltpu.VMEM((kcat, 2 * sdim), BF16),
                        pltpu.VMEM((2 * sdim, kcat), BF16)],
        compiler_params=_cparams(("parallel", "arbitrary")),
        name="s5",
    )(u, c0c, winc, woutc, rep_k, rep_in, rep_out, m_in, m_out, al, d.reshape(1, dch))


def _glu_out_body(h_ref, a_ref, y_ref, wg_ref, bg_ref, woa_ref, wos_ref, o_ref):
    hg = jax.nn.gelu(y_ref[...], approximate=True)
    z = jnp.dot(hg.astype(BF16), wg_ref[...], preferred_element_type=F32) + bg_ref[...]
    s = hg * jax.nn.sigmoid(z)
    mix = jnp.dot(a_ref[...], woa_ref[...], preferred_element_type=F32)
    mix = mix + jnp.dot(s.astype(BF16), wos_ref[...], preferred_element_type=F32)
    o_ref[...] = h_ref[...] + mix


def _glu_out(h, a, y, wg, bg, woa, wos, *, tm=512):
    n, dm = h.shape
    da, ds = a.shape[1], y.shape[1]
    row = lambda w: pl.BlockSpec((tm, w), lambda i: (i, 0))
    full = lambda r, c: pl.BlockSpec((r, c), lambda i: (0, 0))
    return pl.pallas_call(
        _glu_out_body,
        grid=(n // tm,),
        in_specs=[row(dm), row(da), row(ds), full(ds, ds), full(1, ds), full(da, dm), full(ds, dm)],
        out_specs=row(dm),
        out_shape=jax.ShapeDtypeStruct((n, dm), F32),
        compiler_params=_cparams(("parallel",)),
        name="glu_out",
    )(h, a, y, wg, bg.reshape(1, ds), woa, wos)


def _t5_bucket(dist):
    n = jnp.maximum(dist, 0)
    max_exact = N_BUCKETS // 2
    nf = jnp.maximum(n, 1).astype(F32)
    large = max_exact + (jnp.log(nf / max_exact) / math.log(MAX_DISTANCE / max_exact)
                         * (N_BUCKETS - max_exact)).astype(jnp.int32)
    large = jnp.minimum(large, N_BUCKETS - 1)
    return jnp.where(n < max_exact, n, large)


def _t5_body(first_ref, table_ref, bc_ref, dt_ref, wb_ref, *, nc):
    g = pl.program_id(0)
    nt = bc_ref.shape[0]
    span = wb_ref.shape[1]
    heads = [g * HEADS_PER_KV + h for h in range(HEADS_PER_KV)]
    rs = 64
    c = lax.broadcasted_iota(jnp.int32, (rs, LANES), 1)
    slabs = [(a0, a0 + lax.broadcasted_iota(jnp.int32, (rs, LANES), 0)) for a0 in range(0, TQ, rs)]

    def lookup(dist, valid, store):
        bias = [jnp.full((rs, LANES), table_ref[0, hd], F32) for hd in heads]
        for k in range(1, N_BUCKETS):
            ge = dist >= first_ref[k]
            bias = [jnp.where(ge, table_ref[k, hd], bv) for hd, bv in zip(heads, bias)]
        for h, bv in enumerate(bias):
            store(h, bv if valid is None else jnp.where(valid, bv, NEG))

    for a0, a in slabs:
        def rows(h, a0=a0):
            return slice(h * TQ + a0, h * TQ + a0 + rs)

        def put_far(h, tile):
            dt_ref[0, rows(h), :] = tile

        lookup(jnp.full((rs, LANES), 2 * MAX_DISTANCE, jnp.int32), None, put_far)
        for r in range(-1, QK):
            def put_near(h, tile, r=r):
                dt_ref[r + 2, rows(h), :] = tile

            dist = a - r * LANES - c
            lookup(dist, dist >= 0, put_near)
        for j in range(span // LANES):
            def put_wb(h, tile, j=j):
                wb_ref[rows(h), j * LANES:(j + 1) * LANES] = tile

            dist = a + WINDOW - (c + j * LANES)
            lookup(dist, (dist >= 0) & (dist < WINDOW), put_wb)
    dt_ref[QK + 2] = jnp.full((ROWS4, LANES), NEG, F32)

    def cmp_tile(i, _):
        for a0, a in slabs:
            def put_bc(h, tile, a0=a0):
                bc_ref[i, h * TQ + a0:h * TQ + a0 + rs, :] = tile

            dist = i * TQ + a - (c * CMP_STRIDE + CMP_LEN - 1)
            lookup(dist, (dist >= 0) & (c < nc), put_bc)
        return 0

    lax.fori_loop(0, nt, cmp_tile, 0)


def _bias_tables(rel_bias, t):
    nt = t // TQ
    span = TQ + WINDOW
    nc = (t - CMP_LEN) // CMP_STRIDE + 1
    buckets = _t5_bucket(jnp.arange(2 * MAX_DISTANCE))
    first = jnp.sum((buckets[None, :] < jnp.arange(N_BUCKETS)[:, None]).astype(jnp.int32), axis=1)
    smem = pl.BlockSpec(memory_space=pltpu.SMEM)
    return pl.pallas_call(
        functools.partial(_t5_body, nc=nc),
        grid=(N_KV,),
        in_specs=[smem, smem],
        out_specs=[
            pl.BlockSpec((None, nt, ROWS4, LANES), lambda g: (g, 0, 0, 0)),
            pl.BlockSpec((None, QK + 3, ROWS4, LANES), lambda g: (g, 0, 0, 0)),
            pl.BlockSpec((None, ROWS4, span), lambda g: (g, 0, 0)),
        ],
        out_shape=[
            jax.ShapeDtypeStruct((N_KV, nt, ROWS4, LANES), F32),
            jax.ShapeDtypeStruct((N_KV, QK + 3, ROWS4, LANES), F32),
            jax.ShapeDtypeStruct((N_KV, ROWS4, span), F32),
        ],
        compiler_params=_cparams(("parallel",)),
        name="t5_tables",
    )(first, rel_bias.astype(F32))


def _sel_tables(t):
    ns = t // SEL_BLOCK
    nc_pad = LANES
    c_start = jnp.arange(nc_pad) * CMP_STRIDE
    j_start = jnp.arange(LANES) * SEL_BLOCK
    ov = jnp.clip(jnp.minimum(c_start[:, None] + CMP_LEN, j_start[None, :] + SEL_BLOCK)
                  - jnp.maximum(c_start[:, None], j_start[None, :]), 0, None).astype(F32) / CMP_LEN
    ov = jnp.where(jnp.arange(LANES)[None, :] < ns, ov, 0.0)
    et = jnp.arange(t)[:, None] // SEL_BLOCK == jnp.arange(LANES)[None, :]
    return ov.T.astype(BF16), et.astype(BF16)


def _s5_tables(lam_re, lam_im, log_step, b_re, b_im, c_re, c_im):
    ng, p = lam_re.shape
    hch = b_re.shape[2]
    L = SSM_CHUNK
    step = jnp.exp(log_step.astype(F32))[:, None]
    lre, lim = lam_re.astype(F32), lam_im.astype(F32)
    mag = jnp.exp(lre * step)
    ab_re, ab_im = mag * jnp.cos(lim * step), mag * jnp.sin(lim * step)
    nr, ni = ab_re - 1.0, ab_im
    den = lre * lre + lim * lim
    f_re, f_im = (nr * lre + ni * lim) / den, (ni * lre - nr * lim) / den
    br, bim = b_re.astype(F32), b_im.astype(F32)
    bb_re = f_re[..., None] * br - f_im[..., None] * bim
    bb_im = f_re[..., None] * bim + f_im[..., None] * br
    cr, ci = c_re.astype(F32), c_im.astype(F32)
    pr, pi = [jnp.ones_like(ab_re)], [jnp.zeros_like(ab_re)]
    for _ in range(L):
        pr, pi = pr + [pr[-1] * ab_re - pi[-1] * ab_im], pi + [pr[-1] * ab_im + pi[-1] * ab_re]
    pw_re, pw_im = jnp.stack(pr, 0), jnp.stack(pi, 0)
    cp_re = cr[None] * pw_re[:, :, None, :] - ci[None] * pw_im[:, :, None, :]
    cp_im = -(cr[None] * pw_im[:, :, None, :] + ci[None] * pw_re[:, :, None, :])
    gpb = LANES // hch
    nblk = ng // gpb
    wr = pw_re[L - 1 - jnp.arange(L)]
    wi = pw_im[L - 1 - jnp.arange(L)]
    win_re = (wr[..., None] * bb_re[None] - wi[..., None] * bb_im[None])
    win_im = (wr[..., None] * bb_im[None] + wi[..., None] * bb_re[None])
    inj_op = lambda m: m.reshape(L, nblk, gpb, p, hch).transpose(1, 0, 2, 4, 3).reshape(nblk, L * LANES, p)
    winc = jnp.stack([inj_op(win_re), inj_op(win_im)], axis=1)
    read_op = lambda m: m.reshape(L, nblk, gpb, hch, p).transpose(1, 2, 4, 0, 3).reshape(nblk, gpb * p, L * hch)
    woutc = jnp.stack([read_op(cp_re[1:]), read_op(cp_im[1:])], axis=1)
    read0 = lambda m: m.reshape(nblk, gpb, hch, p).transpose(0, 1, 3, 2).reshape(nblk, gpb * p, hch)
    c0c = jnp.stack([read0(cp_re[0]), read0(cp_im[0])], axis=1)
    al = jnp.concatenate([pw_re[L].reshape(nblk, gpb * p), pw_im[L].reshape(nblk, gpb * p)], axis=1)
    al = jnp.broadcast_to(al[:, None, :], (nblk, 8, 2 * gpb * p))
    return c0c.astype(BF16), winc.astype(BF16), woutc.astype(BF16), al


def _mixers(h1, mix_norm, w_in, cmp_k, cmp_v, rel_bias, ssm, ssm_d, glu_w, glu_b, w_out, b, t):
    n, d = h1.shape
    d_ssm = glu_w.shape[0]
    o_g, o_u = D_ATTN + 6 * D_KV, D_ATTN + 6 * D_KV + 3 * N_HEADS
    gcols = 3 * HEADS_PER_KV
    wg = [jnp.pad(w_in[:, o_g + g * gcols:o_g + (g + 1) * gcols], ((0, 0), (0, LANES - gcols))) for g in range(N_KV)]
    w_perm = jnp.concatenate([w_in[:, :o_g], w_in[:, o_u:]] + wg, axis=1).astype(BF16)
    q, kvs, kcv, u, gates = _inproj(h1.reshape(b, t, d), mix_norm, w_perm, d_ssm)

    nblk = t // CMP_STRIDE
    assert nblk == LANES, "compressed-block axis is laid out on one 128-lane tile"
    pe = jnp.stack([cmp_k[0], cmp_v[0]], 0).astype(F32)
    w1 = jnp.stack([cmp_k[1], cmp_v[1]], 0).astype(BF16).reshape(2, CMP_LEN, HEAD_DIM, -1)
    b1 = jnp.stack([cmp_k[2].reshape(1, -1), cmp_v[2].reshape(1, -1)], 0).astype(F32)
    w2 = jnp.stack([cmp_k[3], cmp_v[3]], 0).astype(BF16)
    cb = _compress(kcv, pe, w1, b1, w2, nblk)

    biasc, dtiles, wb = _bias_tables(rel_bias, t)
    ovt, et = _sel_tables(t)
    a = _nsa(q, kvs, cb, gates, biasc, dtiles, wb, ovt, et)

    c0c, winc, woutc, al = _s5_tables(*ssm)
    y = _s5(u, c0c, winc, woutc, al, ssm_d)

    wo = w_out.astype(BF16)
    return _glu_out(h1, a.reshape(n, D_ATTN), y.reshape(n, d_ssm), glu_w.astype(BF16), glu_b, wo[:D_ATTN], wo[D_ATTN:])


def kernel(x, ffn1_norm, ffn1_w1, ffn1_w3, ffn1_w2, mix_norm, w_in, cmp_pe_k, cmp_w1_k, cmp_b1_k, cmp_w2_k,
           cmp_pe_v, cmp_w1_v, cmp_b1_v, cmp_w2_v, rel_bias, ssm_lam_re, ssm_lam_im, ssm_log_step, ssm_b_re,
           ssm_b_im, ssm_c_re, ssm_c_im, ssm_d, glu_w, glu_b, w_out, ffn2_norm, ffn2_w1, ffn2_w3, ffn2_w2,
           final_norm):
    b, t, d = x.shape
    depth = ffn1_w1.shape[0]
    h = x.reshape(b * t, d)
    for l in range(depth):
        last = l == depth - 1
        h = _ffn(h, ffn1_norm[l], ffn1_w1[l], ffn1_w3[l], ffn1_w2[l])
        ssm = (ssm_lam_re[l], ssm_lam_im[l], ssm_log_step[l], ssm_b_re[l], ssm_b_im[l], ssm_c_re[l], ssm_c_im[l])
        h = _mixers(h, mix_norm[l], w_in[l],
                    (cmp_pe_k[l], cmp_w1_k[l], cmp_b1_k[l], cmp_w2_k[l]),
                    (cmp_pe_v[l], cmp_w1_v[l], cmp_b1_v[l], cmp_w2_v[l]),
                    rel_bias, ssm, ssm_d[l], glu_w[l], glu_b[l], w_out[l], b, t)
        h = _ffn(h, ffn2_norm[l], ffn2_w1[l], ffn2_w3[l], ffn2_w2[l], final_gain=final_norm if last else None)
    return h.reshape(b, t, d)
```

```python
import functools
import math

import jax
import jax.numpy as jnp
from jax import lax
from jax.experimental import pallas as pl
from jax.experimental.pallas import tpu as pltpu

F32 = jnp.float32
BF16 = jnp.bfloat16

HEAD_DIM = 128
N_KV = 2
HEADS_PER_KV = 4
N_HEADS = N_KV * HEADS_PER_KV
D_ATTN = N_HEADS * HEAD_DIM
D_KV = N_KV * HEAD_DIM
CMP_LEN = 32
CMP_STRIDE = 16
SEL_BLOCK = 64
N_SELECT = 16
WINDOW = 512
N_BUCKETS = 32
MAX_DISTANCE = 128
SSM_GROUP = 16
SSM_STATE = 64
EPS = 1e-6
NEG = -1e30

LANES = 128
VMEM_LIMIT_BYTES = 56 * 1024 * 1024
NSA_VMEM_LIMIT_BYTES = 60 * 1024 * 1024
TQ = 256
QK = TQ // LANES
ROWS4 = HEADS_PER_KV * TQ
SEL_CHUNK = 512
SSM_CHUNK = 16


def _cparams(sem, vmem_limit_bytes=VMEM_LIMIT_BYTES):
    return pltpu.CompilerParams(dimension_semantics=sem, vmem_limit_bytes=vmem_limit_bytes)


def _ffn_body(x_ref, g_ref, w1_ref, w3_ref, w2_ref, *rest, final):
    if final:
        fg_ref, o_ref, xn_ref = rest
    else:
        o_ref, xn_ref = rest
    j = pl.program_id(1)
    nj = pl.num_programs(1)

    def contribution(xn):
        a = jnp.dot(xn, w1_ref[...], preferred_element_type=F32)
        b = jnp.dot(xn, w3_ref[...], preferred_element_type=F32)
        gated = (a * jax.nn.sigmoid(a)) * b
        return jnp.dot(gated.astype(BF16), w2_ref[...], preferred_element_type=F32)

    @pl.when(j == 0)
    def _():
        x = x_ref[...]
        ms = jnp.mean(x * x, axis=-1, keepdims=True)
        xn = (x * lax.rsqrt(ms + EPS) * g_ref[...]).astype(BF16)
        xn_ref[...] = xn
        o_ref[...] = contribution(xn)

    @pl.when((j > 0) & (j < nj - 1))
    def _():
        o_ref[...] += contribution(xn_ref[...])

    @pl.when(j == nj - 1)
    def _():
        h = x_ref[...] + 0.5 * (o_ref[...] + contribution(xn_ref[...]))
        if final:
            ms = jnp.mean(h * h, axis=-1, keepdims=True)
            h = h * lax.rsqrt(ms + EPS) * fg_ref[...]
        o_ref[...] = h


def _ffn(x, gain, w1, w3, w2, final_gain=None, *, tm=512, tf=512):
    n, d = x.shape
    dff = w1.shape[1]
    nj = dff // tf
    assert nj >= 2 and nj * tf == dff
    final = final_gain is not None
    in_specs = [
        pl.BlockSpec((tm, d), lambda i, j: (i, 0)),
        pl.BlockSpec((1, d), lambda i, j: (0, 0)),
        pl.BlockSpec((d, tf), lambda i, j: (0, j)),
        pl.BlockSpec((d, tf), lambda i, j: (0, j)),
        pl.BlockSpec((tf, d), lambda i, j: (j, 0)),
    ]
    args = [x, gain.reshape(1, d), w1.astype(BF16), w3.astype(BF16), w2.astype(BF16)]
    if final:
        in_specs.append(pl.BlockSpec((1, d), lambda i, j: (0, 0)))
        args.append(final_gain.reshape(1, d))
    return pl.pallas_call(
        functools.partial(_ffn_body, final=final),
        grid=(n // tm, nj),
        in_specs=in_specs,
        out_specs=pl.BlockSpec((tm, d), lambda i, j: (i, 0)),
        out_shape=jax.ShapeDtypeStruct((n, d), F32),
        scratch_shapes=[pltpu.VMEM((tm, d), BF16)],
        compiler_params=_cparams(("parallel", "arbitrary")),
        name="ffn_final" if final else "ffn",
    )(*args)


def _inproj_body(x_ref, g_ref, w_ref, q_ref, kvs_ref, kcv_ref, u_ref, gate_ref):
    i = pl.program_id(1)

    @pl.when(i == 0)
    def _():
        kvs_ref[...] = jnp.zeros(kvs_ref.shape, kvs_ref.dtype)
        kcv_ref[...] = jnp.zeros(kcv_ref.shape, kcv_ref.dtype)

    @pl.when(i > 0)
    def _():
        x = x_ref[...]
        ms = jnp.mean(x * x, axis=-1, keepdims=True)
        xn = (x * lax.rsqrt(ms + EPS) * g_ref[...]).astype(BF16)
        c0 = 0
        c1 = q_ref.shape[1]
        q = jnp.dot(xn, w_ref[:, c0:c1], preferred_element_type=F32)
        q_ref[...] = (q * (HEAD_DIM ** -0.5)).astype(BF16)
        c0, c1 = c1, c1 + kcv_ref.shape[1]
        kcv_ref[...] = jnp.dot(xn, w_ref[:, c0:c1], preferred_element_type=F32)
        c0, c1 = c1, c1 + kvs_ref.shape[1]
        kvs_ref[...] = jnp.dot(xn, w_ref[:, c0:c1], preferred_element_type=F32).astype(BF16)
        c0, c1 = c1, c1 + u_ref.shape[1]
        u_ref[...] = jnp.dot(xn, w_ref[:, c0:c1], preferred_element_type=F32)
        gate_ref[...] = jax.nn.sigmoid(jnp.dot(xn, w_ref[:, c1:], preferred_element_type=F32))


def _inproj(h, gain, w_perm, d_ssm):
    b, t, d = h.shape
    tm = WINDOW
    nt = t // tm
    nq, ncv, nvs, ng = D_ATTN, 2 * D_KV, 4 * D_KV, N_KV * LANES
    ncol = w_perm.shape[1]
    assert ncol == nq + ncv + nvs + d_ssm + ng and t % tm == 0
    data = lambda bi, i: (bi, jnp.maximum(i - 1, 0), 0)
    return pl.pallas_call(
        _inproj_body,
        grid=(b, nt + 1),
        in_specs=[
            pl.BlockSpec((None, tm, d), data),
            pl.BlockSpec((1, d), lambda bi, i: (0, 0)),
            pl.BlockSpec((d, ncol), lambda bi, i: (0, 0)),
        ],
        out_specs=[
            pl.BlockSpec((None, tm, nq), data),
            pl.BlockSpec((None, tm, nvs), lambda bi, i: (bi, i, 0)),
            pl.BlockSpec((None, tm, ncv), lambda bi, i: (bi, jnp.where(i == 0, nt, i - 1), 0)),
            pl.BlockSpec((None, tm, d_ssm), data),
            pl.BlockSpec((None, tm, ng), data),
        ],
        out_shape=[
            jax.ShapeDtypeStruct((b, t, nq), BF16),
            jax.ShapeDtypeStruct((b, t + tm, nvs), BF16),
            jax.ShapeDtypeStruct((b, t + tm, ncv), F32),
            jax.ShapeDtypeStruct((b, t, d_ssm), F32),
            jax.ShapeDtypeStruct((b, t, ng), F32),
        ],
        compiler_params=_cparams(("parallel", "arbitrary")),
        name="inproj",
    )(h, gain.reshape(1, d), w_perm)


def _compress_body(x_ref, pe_ref, w1_ref, b1_ref, w2_ref, o_ref):
    nb, nblk = o_ref.shape[0], o_ref.shape[1]
    acc = jnp.broadcast_to(b1_ref[...], (nb * nblk, b1_ref.shape[1])).astype(F32)
    for l in range(CMP_LEN):
        xl = x_ref[:, pl.ds(l, nblk, stride=CMP_STRIDE), :] + pe_ref[l:l + 1, :]
        acc = acc + jnp.dot(xl.reshape(nb * nblk, xl.shape[2]).astype(BF16), w1_ref[l], preferred_element_type=F32)
    hid = jax.nn.gelu(acc, approximate=True)
    out = jnp.dot(hid.astype(BF16), w2_ref[...], preferred_element_type=F32)
    o_ref[...] = out.reshape(o_ref.shape).astype(BF16)


def _compress(kcv, pe, w1, b1, w2, nblk, *, nb=4):
    b, tpad, _ = kcv.shape
    nb = min(nb, b)
    hid, dh = w1.shape[3], w2.shape[2]
    assert (nblk - 1) * CMP_STRIDE + CMP_LEN <= tpad
    return pl.pallas_call(
        _compress_body,
        grid=(2 * N_KV, b // nb),
        in_specs=[
            pl.BlockSpec((nb, tpad, dh), lambda s, i: (i, 0, s)),
            pl.BlockSpec((None, CMP_LEN, dh), lambda s, i: (s // N_KV, 0, 0)),
            pl.BlockSpec((None, CMP_LEN, dh, hid), lambda s, i: (s // N_KV, 0, 0, 0)),
            pl.BlockSpec((None, 1, hid), lambda s, i: (s // N_KV, 0, 0)),
            pl.BlockSpec((None, hid, dh), lambda s, i: (s // N_KV, 0, 0)),
        ],
        out_specs=pl.BlockSpec((None, None, nb, nblk, dh), lambda s, i: (s // N_KV, s % N_KV, i, 0, 0)),
        out_shape=jax.ShapeDtypeStruct((2, N_KV, b, nblk, dh), BF16),
        compiler_params=_cparams(("parallel", "parallel")),
        name="compress",
    )(kcv, pe, w1, b1, w2)


def _nt_dot(a, b):
    return lax.dot_general(a, b, (((1,), (1,)), ((), ())), preferred_element_type=F32)


def _nsa_body(q_ref, ks_ref, vs_ref, kw_ref, vw_ref, kcb_ref, vcb_ref, gate_ref, biasc_ref, dt_ref, wb_ref,
              ovt_ref, et_ref, o_ref, kse_ref, s_ref, m_ref, l_ref, acc_ref):
    i = pl.program_id(1)
    t0 = i * TQ
    t = et_ref.shape[0]
    ns = t // SEL_BLOCK
    nsel = min(N_SELECT, ns)
    groups = range(N_KV)
    gl = lambda g: slice(g * HEAD_DIM, (g + 1) * HEAD_DIM)

    @pl.when(i == 0)
    def _():
        for g in groups:
            kse_ref[g, :, 0:HEAD_DIM] = ks_ref[WINDOW:WINDOW + t, gl(g)]
            kse_ref[g, :, HEAD_DIM:] = et_ref[...]

    q = q_ref[...]
    w0 = pl.multiple_of(t0, LANES)
    span = wb_ref.shape[2]

    q4 = [jnp.concatenate([q[:, (g * HEADS_PER_KV + h) * HEAD_DIM:(g * HEADS_PER_KV + h + 1) * HEAD_DIM]
                           for h in range(HEADS_PER_KV)], axis=0) for g in groups]

    sc = [_nt_dot(q4[g], kcb_ref[g]) + biasc_ref[g] for g in groups]
    row_t = t0 + (lax.broadcasted_iota(jnp.int32, (ROWS4, LANES), 0) & (TQ - 1))
    col_c = lax.broadcasted_iota(jnp.int32, (ROWS4, LANES), 1)
    valid_c = (row_t - col_c * CMP_STRIDE - (CMP_LEN - 1) >= 0) & (col_c < LANES - 1)
    mc = [jnp.max(s, axis=-1, keepdims=True) for s in sc]
    pc = [jnp.where(valid_c, jnp.exp(s - m), 0.0) for s, m in zip(sc, mc)]
    pc = [p / jnp.maximum(jnp.sum(p, axis=-1, keepdims=True), 1e-30) for p in pc]
    pcb = [p.astype(BF16) for p in pc]
    o_cmps = [jnp.dot(pcb[g], vcb_ref[g], preferred_element_type=F32) for g in groups]
    pimp = [_nt_dot(ovt_ref[...], p) for p in pcb]
    imp = [sum(p[0:ns, h * TQ:(h + 1) * TQ] for h in range(1, HEADS_PER_KV)) + p[0:ns, 0:TQ] for p in pimp]

    jrow = lax.broadcasted_iota(jnp.int32, (ns, TQ), 0)
    tpos = t0 + lax.broadcasted_iota(jnp.int32, (ns, TQ), 1)
    cur = lax.shift_right_logical(tpos, int(math.log2(SEL_BLOCK)))
    forced = (jrow == 0) | (jrow == cur) | (jrow == cur - 1)
    impm = [jnp.where(forced, 1e6, jnp.where(jrow * SEL_BLOCK <= tpos, x, -1e9)) for x in imp]
    nslab = ns // 8
    slabs = [[x[8 * v:8 * v + 8] for v in range(nslab)] for x in impm]
    cnts = [[jnp.zeros((8, TQ), F32) for _ in range(nslab)] for _ in groups]
    sub = lax.broadcasted_iota(jnp.int32, (8, TQ), 0)
    for jp in range(ns):
        v0, r0 = divmod(jp, 8)
        for g in groups:
            row = jnp.broadcast_to(slabs[g][v0][r0:r0 + 1, :], (8, TQ))
            for v in range(nslab):
                if v > v0:
                    beats = row >= slabs[g][v]
                elif v < v0:
                    beats = row > slabs[g][v]
                else:
                    beats = (row > slabs[g][v]) | ((row == slabs[g][v]) & (sub > r0))
                cnts[g][v] = cnts[g][v] + jnp.where(beats, 1.0, 0.0)
    q4s = []
    for g in groups:
        negt = [jnp.where(c < nsel, 0.0, NEG) for c in cnts[g]]
        negt = jnp.concatenate(negt + [jnp.zeros((LANES - ns, TQ), F32)], axis=0)
        neg = negt.T.astype(BF16)
        q4s.append(jnp.concatenate([q4[g], jnp.concatenate([neg] * HEADS_PER_KV, axis=0)], axis=1))

    nsub = SEL_CHUNK // LANES
    nchunks = ((i + 1) * QK - 1) // nsub + 1
    m_ref[...] = jnp.full(m_ref.shape, -3e38, F32)

    def score_chunk(ck, _):
        kb = pl.multiple_of(ck * SEL_CHUNK, SEL_CHUNK)
        s = [_nt_dot(q4s[g], kse_ref[g, pl.ds(kb, SEL_CHUNK), :]) for g in groups]
        for g in groups:
            mloc = None
            for j in range(nsub):
                sj = s[g][:, j * LANES:(j + 1) * LANES] + dt_ref[g, jnp.clip(ck * nsub + j - i * QK + 2, 0, QK + 2)]
                s_ref[g, ck, :, j * LANES:(j + 1) * LANES] = sj
                mloc = sj if mloc is None else jnp.maximum(mloc, sj)
            m_ref[g] = jnp.maximum(m_ref[g], mloc)
        return 0

    lax.fori_loop(0, nchunks, score_chunk, 0)
    for g in groups:
        m_ref[g] = jnp.broadcast_to(jnp.max(m_ref[g], axis=-1, keepdims=True), (ROWS4, LANES))
    l_ref[...] = jnp.zeros(l_ref.shape, F32)
    acc_ref[...] = jnp.zeros(acc_ref.shape, F32)

    kneg = jnp.where(t0 - WINDOW + lax.broadcasted_iota(jnp.int32, (1, span), 1) >= 0, 0.0, NEG)
    sw = [_nt_dot(q4[g], kw_ref[pl.ds(w0, span), gl(g)]) + wb_ref[g] + kneg for g in groups]
    mw = [jnp.max(s, axis=-1, keepdims=True) for s in sw]
    pw = [jnp.exp(s - m) for s, m in zip(sw, mw)]
    lw = [jnp.sum(p, axis=-1, keepdims=True) for p in pw]
    o_wins = [jnp.dot(pw[g].astype(BF16), vw_ref[pl.ds(w0, span), gl(g)], preferred_element_type=F32) for g in groups]
    o_wins = [o / jnp.maximum(l, 1e-30) for o, l in zip(o_wins, lw)]

    def pv_chunk(ck, _):
        kb = pl.multiple_of(WINDOW + ck * SEL_CHUNK, SEL_CHUNK)
        ps = [[jnp.exp(s_ref[g, ck, :, j * LANES:(j + 1) * LANES] - m_ref[g]) for j in range(nsub)] for g in groups]
        for g in groups:
            lsum = ps[g][0]
            for pj in ps[g][1:]:
                lsum = lsum + pj
            l_ref[g] += lsum
            p = jnp.concatenate([pj.astype(BF16) for pj in ps[g]], axis=1)
            acc_ref[g] += jnp.dot(p, vs_ref[pl.ds(kb, SEL_CHUNK), gl(g)], preferred_element_type=F32)
        return 0

    lax.fori_loop(0, nchunks, pv_chunk, 0)

    ls = [jnp.maximum(jnp.sum(l_ref[g], axis=-1, keepdims=True), 1e-30) for g in groups]
    o_sels = [acc_ref[g] / ls[g] for g in groups]
    gts = [gate_ref[:, g * LANES:(g + 1) * LANES] for g in groups]
    for h in range(HEADS_PER_KV):
        r = slice(h * TQ, (h + 1) * TQ)
        for g in groups:
            gt = gts[g]
            o_h = (gt[:, 3 * h:3 * h + 1] * o_cmps[g][r] + gt[:, 3 * h + 1:3 * h + 2] * o_sels[g][r]
                   + gt[:, 3 * h + 2:3 * h + 3] * o_wins[g][r])
            c0 = (g * HEADS_PER_KV + h) * HEAD_DIM
            o_ref[:, c0:c0 + HEAD_DIM] = o_h.astype(BF16)


def _nsa(q, kvs, cb, gates, biasc, dtiles, wb, ovt, et):
    b, t, _ = q.shape
    nt = t // TQ
    tp = kvs.shape[1]
    span = wb.shape[2]
    gw = HEADS_PER_KV * HEAD_DIM
    assert tp == t + WINDOW and span == TQ + WINDOW and t % SEL_CHUNK == 0 and (t // SEL_BLOCK) % 8 == 0

    def stream(j):
        return pl.BlockSpec((None, tp, N_KV * HEAD_DIM), lambda bi, i: (bi, 0, j))

    return pl.pallas_call(
        _nsa_body,
        grid=(b, nt),
        in_specs=[
            pl.BlockSpec((None, TQ, N_KV * gw), lambda bi, i: (bi, i, 0)),
            stream(0), stream(1), stream(2), stream(3),
            pl.BlockSpec((None, N_KV, None, LANES, HEAD_DIM), lambda bi, i: (0, 0, bi, 0, 0)),
            pl.BlockSpec((None, N_KV, None, LANES, HEAD_DIM), lambda bi, i: (1, 0, bi, 0, 0)),
            pl.BlockSpec((None, TQ, N_KV * LANES), lambda bi, i: (bi, i, 0)),
            pl.BlockSpec((N_KV, None, ROWS4, LANES), lambda bi, i: (0, i, 0, 0)),
            pl.BlockSpec((N_KV, QK + 3, ROWS4, LANES), lambda bi, i: (0, 0, 0, 0)),
            pl.BlockSpec((N_KV, ROWS4, span), lambda bi, i: (0, 0, 0)),
            pl.BlockSpec((LANES, LANES), lambda bi, i: (0, 0)),
            pl.BlockSpec((t, LANES), lambda bi, i: (0, 0)),
        ],
        out_specs=pl.BlockSpec((None, TQ, N_KV * gw), lambda bi, i: (bi, i, 0)),
        out_shape=jax.ShapeDtypeStruct((b, t, N_KV * gw), BF16),
        scratch_shapes=[
            pltpu.VMEM((N_KV, t, 2 * HEAD_DIM), BF16),
            pltpu.VMEM((N_KV, t // SEL_CHUNK, ROWS4, SEL_CHUNK), F32),
            pltpu.VMEM((N_KV, ROWS4, LANES), F32),
            pltpu.VMEM((N_KV, ROWS4, LANES), F32),
            pltpu.VMEM((N_KV, ROWS4, HEAD_DIM), F32),
        ],
        compiler_params=_cparams(("parallel", "arbitrary"), NSA_VMEM_LIMIT_BYTES),
        name="nsa",
    )(q, kvs, kvs, kvs, kvs, cb, cb, gates, biasc, dtiles, wb, ovt, et)


def _s5_body(u_ref, c0_ref, winc_ref, woutc_ref, rep_k_ref, rep_in_ref, rep_out_ref, m_in_ref, m_out_ref,
             al_ref, d_ref, y_ref, ucat_ref, inj_ref, xp_ref, grev_ref, win_ref, wout_ref):
    nb, t, lanes = u_ref.shape
    L = SSM_CHUNK
    nchunk = t // L
    rows = nb * nchunk
    sdim = al_ref.shape[1] // 2

    @pl.when(pl.program_id(1) == 0)
    def _():
        m_in, m_out = m_in_ref[...], m_out_ref[...]
        for half in range(2):
            for s in range(L):
                x = jnp.dot(winc_ref[half, s * lanes:(s + 1) * lanes, :], rep_in_ref[...], preferred_element_type=F32)
                win_ref[s * lanes:(s + 1) * lanes, half * sdim:(half + 1) * sdim] = (x * m_in).astype(BF16)
            for tt in range(L):
                x = jnp.dot(woutc_ref[half], rep_out_ref[:, tt * lanes:(tt + 1) * lanes], preferred_element_type=F32)
                wout_ref[half * sdim:(half + 1) * sdim, tt * lanes:(tt + 1) * lanes] = (x * m_out).astype(BF16)
        c0 = [(jnp.dot(c0_ref[half], rep_k_ref[...], preferred_element_type=F32) * m_out).astype(BF16)
              for half in range(2)]
        kx = jnp.dot(win_ref[...], jnp.concatenate(c0, axis=0), preferred_element_type=F32)
        zero = jnp.zeros((lanes, lanes), BF16)

        def kblk(tau):
            s = L - 1 - tau
            return kx[s * lanes:(s + 1) * lanes].astype(BF16) if 0 <= tau < L else zero

        for r in range(L + 1):
            grev_ref[r * lanes:(r + 1) * lanes, 0:lanes] = kblk(L - r - 1)
            grev_ref[r * lanes:(r + 1) * lanes, lanes:] = kblk(L - r)

    def u_at(s):
        return u_ref[:, pl.ds(s, nchunk, stride=L), :].reshape(rows, lanes)

    for s in range(L):
        ucat_ref[:, s * lanes:(s + 1) * lanes] = u_at(s).astype(BF16)
    inj = jnp.dot(ucat_ref[...], win_ref[...], preferred_element_type=F32)
    npl = sdim // lanes
    for k in range(2 * npl):
        inj_ref[k] = inj[:, k * lanes:(k + 1) * lanes]
    ar = [jnp.broadcast_to(al_ref[0:1, k * lanes:(k + 1) * lanes], (nb, lanes)) for k in range(npl)]
    ai = [jnp.broadcast_to(al_ref[0:1, sdim + k * lanes:sdim + (k + 1) * lanes], (nb, lanes)) for k in range(npl)]

    def step(c, carry):
        xr, xi = carry
        rsel = pl.ds(c, nb, stride=nchunk)
        nr, ni = [], []
        for k in range(npl):
            xp_ref[k, rsel, :] = xr[k]
            xp_ref[npl + k, rsel, :] = xi[k]
            nr.append(ar[k] * xr[k] - ai[k] * xi[k] + inj_ref[k, rsel, :])
            ni.append(ar[k] * xi[k] + ai[k] * xr[k] + inj_ref[npl + k, rsel, :])
        return tuple(nr), tuple(ni)

    z = tuple(jnp.zeros((nb, lanes), F32) for _ in range(npl))
    lax.fori_loop(0, nchunk, step, (z, z), unroll=4)
    xp = jnp.concatenate([xp_ref[k] for k in range(2 * npl)], axis=1).astype(BF16)
    d = d_ref[...]
    for tp in range(0, L, 2):
        res = jnp.dot(ucat_ref[:, 0:(tp + 2) * lanes], grev_ref[(L - 1 - tp) * lanes:(L + 1) * lanes, :],
                      preferred_element_type=F32)
        res = res + jnp.dot(xp, wout_ref[:, tp * lanes:(tp + 2) * lanes], preferred_element_type=F32)
        for k in range(2):
            y = res[:, k * lanes:(k + 1) * lanes] + d * u_at(tp + k)
            y_ref[:, pl.ds(tp + k, nchunk, stride=L), :] = y.reshape(nb, nchunk, lanes)


def _s5(u, c0c, winc, woutc, al, d, *, nb=4):
    b, t, dch = u.shape
    nb = min(nb, b)
    nblk = dch // LANES
    L = SSM_CHUNK
    rows = nb * (t // L)
    hch, p = c0c.shape[3], winc.shape[3]
    gpb = LANES // hch
    sdim = gpb * p
    kcat = L * LANES
    lane, col, st = jnp.arange(LANES), jnp.arange(kcat), jnp.arange(sdim)
    rep_k = (jnp.arange(hch)[:, None] == (lane % hch)[None, :]).astype(BF16)
    rep_in = (jnp.arange(p)[:, None] == (st % p)[None, :]).astype(BF16)
    rep_out = (jnp.arange(L * hch)[:, None] == (col // LANES * hch + col % hch)[None, :]).astype(BF16)
    m_in = ((lane // hch)[:, None] == (st // p)[None, :]).astype(F32)
    m_out = ((st // p)[:, None] == (lane // hch)[None, :]).astype(F32)
    const = lambda a: pl.BlockSpec(a.shape, lambda j, i: (0,) * a.ndim)
    return pl.pallas_call(
        _s5_body,
        grid=(nblk, b // nb),
        in_specs=[
            pl.BlockSpec((nb, t, LANES), lambda j, i: (i, 0, j)),
            pl.BlockSpec((None, 2, sdim, hch), lambda j, i: (j, 0, 0, 0)),
            pl.BlockSpec((None, 2, kcat, p), lambda j, i: (j, 0, 0, 0)),
            pl.BlockSpec((None, 2, sdim, L * hch), lambda j, i: (j, 0, 0, 0)),
            const(rep_k), const(rep_in), const(rep_out), const(m_in), const(m_out),
            pl.BlockSpec((None, 8, 2 * sdim), lambda j, i: (j, 0, 0)),
            pl.BlockSpec((1, LANES), lambda j, i: (0, j)),
        ],
        out_specs=pl.BlockSpec((nb, t, LANES), lambda j, i: (i, 0, j)),
        out_shape=jax.ShapeDtypeStruct((b, t, dch), F32),
        scratch_shapes=[pltpu.VMEM((rows, kcat), BF16), pltpu.VMEM((2 * sdim // LANES, rows, LANES), F32),
                        pltpu.VMEM((2 * sdim // LANES, rows, LANES), F32),
                        pltpu.VMEM(((L + 1) * LANES, 2 * LANES), BF16), pltpu.VMEM((kcat, 2 * sdim), BF16),
                        pltpu.VMEM((2 * sdim, kcat), BF16)],
        compiler_params=_cparams(("parallel", "arbitrary")),
        name="s5",
    )(u, c0c, winc, woutc, rep_k, rep_in, rep_out, m_in, m_out, al, d.reshape(1, dch))


def _glu_out_body(h_ref, a_ref, y_ref, wg_ref, bg_ref, woa_ref, wos_ref, o_ref):
    hg = jax.nn.gelu(y_ref[...], approximate=True)
    z = jnp.dot(hg.astype(BF16), wg_ref[...], preferred_element_type=F32) + bg_ref[...]
    s = hg * jax.nn.sigmoid(z)
    mix = jnp.dot(a_ref[...], woa_ref[...], preferred_element_type=F32)
    mix = mix + jnp.dot(s.astype(BF16), wos_ref[...], preferred_element_type=F32)
    o_ref[...] = h_ref[...] + mix


def _glu_out(h, a, y, wg, bg, woa, wos, *, tm=512):
    n, dm = h.shape
    da, ds = a.shape[1], y.shape[1]
    row = lambda w: pl.BlockSpec((tm, w), lambda i: (i, 0))
    full = lambda r, c: pl.BlockSpec((r, c), lambda i: (0, 0))
    return pl.pallas_call(
        _glu_out_body,
        grid=(n // tm,),
        in_specs=[row(dm), row(da), row(ds), full(ds, ds), full(1, ds), full(da, dm), full(ds, dm)],
        out_specs=row(dm),
        out_shape=jax.ShapeDtypeStruct((n, dm), F32),
        compiler_params=_cparams(("parallel",)),
        name="glu_out",
    )(h, a, y, wg, bg.reshape(1, ds), woa, wos)


def _t5_bucket(dist):
    n = jnp.maximum(dist, 0)
    max_exact = N_BUCKETS // 2
    nf = jnp.maximum(n, 1).astype(F32)
    large = max_exact + (jnp.log(nf / max_exact) / math.log(MAX_DISTANCE / max_exact)
                         * (N_BUCKETS - max_exact)).astype(jnp.int32)
    large = jnp.minimum(large, N_BUCKETS - 1)
    return jnp.where(n < max_exact, n, large)


def _t5_body(first_ref, table_ref, bc_ref, dt_ref, wb_ref, *, nc):
    g = pl.program_id(0)
    nt = bc_ref.shape[0]
    span = wb_ref.shape[1]
    a = lax.broadcasted_iota(jnp.int32, (TQ, LANES), 0)
    c = lax.broadcasted_iota(jnp.int32, (TQ, LANES), 1)
    heads = [g * HEADS_PER_KV + h for h in range(HEADS_PER_KV)]

    def lookup(dist, valid, store):
        bias = [jnp.full((TQ, LANES), table_ref[0, hd], F32) for hd in heads]
        for k in range(1, N_BUCKETS):
            ge = dist >= first_ref[k]
            bias = [jnp.where(ge, table_ref[k, hd], bv) for hd, bv in zip(heads, bias)]
        for h, bv in enumerate(bias):
            store(slice(h * TQ, (h + 1) * TQ), bv if valid is None else jnp.where(valid, bv, NEG))

    def put_far(rows, tile):
        dt_ref[0, rows, :] = tile

    lookup(jnp.full((TQ, LANES), 2 * MAX_DISTANCE, jnp.int32), None, put_far)
    for r in range(-1, QK):
        def put_near(rows, tile, r=r):
            dt_ref[r + 2, rows, :] = tile

        dist = a - r * LANES - c
        lookup(dist, dist >= 0, put_near)
    dt_ref[QK + 2] = jnp.full((ROWS4, LANES), NEG, F32)
    for j in range(span // LANES):
        def put_wb(rows, tile, j=j):
            wb_ref[rows, j * LANES:(j + 1) * LANES] = tile

        dist = a + WINDOW - (c + j * LANES)
        lookup(dist, (dist >= 0) & (dist < WINDOW), put_wb)

    def cmp_tile(i, _):
        def put_bc(rows, tile):
            bc_ref[i, rows, :] = tile

        dist = i * TQ + a - (c * CMP_STRIDE + CMP_LEN - 1)
        lookup(dist, (dist >= 0) & (c < nc), put_bc)
        return 0

    lax.fori_loop(0, nt, cmp_tile, 0)


def _bias_tables(rel_bias, t):
    nt = t // TQ
    span = TQ + WINDOW
    nc = (t - CMP_LEN) // CMP_STRIDE + 1
    buckets = _t5_bucket(jnp.arange(2 * MAX_DISTANCE))
    first = jnp.sum((buckets[None, :] < jnp.arange(N_BUCKETS)[:, None]).astype(jnp.int32), axis=1)
    smem = pl.BlockSpec(memory_space=pltpu.SMEM)
    return pl.pallas_call(
        functools.partial(_t5_body, nc=nc),
        grid=(N_KV,),
        in_specs=[smem, smem],
        out_specs=[
            pl.BlockSpec((None, nt, ROWS4, LANES), lambda g: (g, 0, 0, 0)),
            pl.BlockSpec((None, QK + 3, ROWS4, LANES), lambda g: (g, 0, 0, 0)),
            pl.BlockSpec((None, ROWS4, span), lambda g: (g, 0, 0)),
        ],
        out_shape=[
            jax.ShapeDtypeStruct((N_KV, nt, ROWS4, LANES), F32),
            jax.ShapeDtypeStruct((N_KV, QK + 3, ROWS4, LANES), F32),
            jax.ShapeDtypeStruct((N_KV, ROWS4, span), F32),
        ],
        compiler_params=_cparams(("parallel",)),
        name="t5_tables",
    )(first, rel_bias.astype(F32))


def _sel_tables(t):
    ns = t // SEL_BLOCK
    nc_pad = LANES
    c_start = jnp.arange(nc_pad) * CMP_STRIDE
    j_start = jnp.arange(LANES) * SEL_BLOCK
    ov = jnp.clip(jnp.minimum(c_start[:, None] + CMP_LEN, j_start[None, :] + SEL_BLOCK)
                  - jnp.maximum(c_start[:, None], j_start[None, :]), 0, None).astype(F32) / CMP_LEN
    ov = jnp.where(jnp.arange(LANES)[None, :] < ns, ov, 0.0)
    et = jnp.arange(t)[:, None] // SEL_BLOCK == jnp.arange(LANES)[None, :]
    return ov.T.astype(BF16), et.astype(BF16)


def _s5_tables(lam_re, lam_im, log_step, b_re, b_im, c_re, c_im):
    ng, p = lam_re.shape
    hch = b_re.shape[2]
    L = SSM_CHUNK
    step = jnp.exp(log_step.astype(F32))[:, None]
    lre, lim = lam_re.astype(F32), lam_im.astype(F32)
    mag = jnp.exp(lre * step)
    ab_re, ab_im = mag * jnp.cos(lim * step), mag * jnp.sin(lim * step)
    nr, ni = ab_re - 1.0, ab_im
    den = lre * lre + lim * lim
    f_re, f_im = (nr * lre + ni * lim) / den, (ni * lre - nr * lim) / den
    br, bim = b_re.astype(F32), b_im.astype(F32)
    bb_re = f_re[..., None] * br - f_im[..., None] * bim
    bb_im = f_re[..., None] * bim + f_im[..., None] * br
    cr, ci = c_re.astype(F32), c_im.astype(F32)
    pr, pi = [jnp.ones_like(ab_re)], [jnp.zeros_like(ab_re)]
    for _ in range(L):
        pr, pi = pr + [pr[-1] * ab_re - pi[-1] * ab_im], pi + [pr[-1] * ab_im + pi[-1] * ab_re]
    pw_re, pw_im = jnp.stack(pr, 0), jnp.stack(pi, 0)
    cp_re = cr[None] * pw_re[:, :, None, :] - ci[None] * pw_im[:, :, None, :]
    cp_im = -(cr[None] * pw_im[:, :, None, :] + ci[None] * pw_re[:, :, None, :])
    gpb = LANES // hch
    nblk = ng // gpb
    wr = pw_re[L - 1 - jnp.arange(L)]
    wi = pw_im[L - 1 - jnp.arange(L)]
    win_re = (wr[..., None] * bb_re[None] - wi[..., None] * bb_im[None])
    win_im = (wr[..., None] * bb_im[None] + wi[..., None] * bb_re[None])
    inj_op = lambda m: m.reshape(L, nblk, gpb, p, hch).transpose(1, 0, 2, 4, 3).reshape(nblk, L * LANES, p)
    winc = jnp.stack([inj_op(win_re), inj_op(win_im)], axis=1)
    read_op = lambda m: m.reshape(L, nblk, gpb, hch, p).transpose(1, 2, 4, 0, 3).reshape(nblk, gpb * p, L * hch)
    woutc = jnp.stack([read_op(cp_re[1:]), read_op(cp_im[1:])], axis=1)
    read0 = lambda m: m.reshape(nblk, gpb, hch, p).transpose(0, 1, 3, 2).reshape(nblk, gpb * p, hch)
    c0c = jnp.stack([read0(cp_re[0]), read0(cp_im[0])], axis=1)
    al = jnp.concatenate([pw_re[L].reshape(nblk, gpb * p), pw_im[L].reshape(nblk, gpb * p)], axis=1)
    al = jnp.broadcast_to(al[:, None, :], (nblk, 8, 2 * gpb * p))
    return c0c.astype(BF16), winc.astype(BF16), woutc.astype(BF16), al


def _mixers(h1, mix_norm, w_in, cmp_k, cmp_v, rel_bias, ssm, ssm_d, glu_w, glu_b, w_out, b, t):
    n, d = h1.shape
    d_ssm = glu_w.shape[0]
    o_g, o_u = D_ATTN + 6 * D_KV, D_ATTN + 6 * D_KV + 3 * N_HEADS
    gcols = 3 * HEADS_PER_KV
    wg = [jnp.pad(w_in[:, o_g + g * gcols:o_g + (g + 1) * gcols], ((0, 0), (0, LANES - gcols))) for g in range(N_KV)]
    w_perm = jnp.concatenate([w_in[:, :o_g], w_in[:, o_u:]] + wg, axis=1).astype(BF16)
    q, kvs, kcv, u, gates = _inproj(h1.reshape(b, t, d), mix_norm, w_perm, d_ssm)

    nblk = t // CMP_STRIDE
    assert nblk == LANES, "compressed-block axis is laid out on one 128-lane tile"
    pe = jnp.stack([cmp_k[0], cmp_v[0]], 0).astype(F32)
    w1 = jnp.stack([cmp_k[1], cmp_v[1]], 0).astype(BF16).reshape(2, CMP_LEN, HEAD_DIM, -1)
    b1 = jnp.stack([cmp_k[2].reshape(1, -1), cmp_v[2].reshape(1, -1)], 0).astype(F32)
    w2 = jnp.stack([cmp_k[3], cmp_v[3]], 0).astype(BF16)
    cb = _compress(kcv, pe, w1, b1, w2, nblk)

    biasc, dtiles, wb = _bias_tables(rel_bias, t)
    ovt, et = _sel_tables(t)
    a = _nsa(q, kvs, cb, gates, biasc, dtiles, wb, ovt, et)

    c0c, winc, woutc, al = _s5_tables(*ssm)
    y = _s5(u, c0c, winc, woutc, al, ssm_d)

    wo = w_out.astype(BF16)
    return _glu_out(h1, a.reshape(n, D_ATTN), y.reshape(n, d_ssm), glu_w.astype(BF16), glu_b, wo[:D_ATTN], wo[D_ATTN:])


def kernel(x, ffn1_norm, ffn1_w1, ffn1_w3, ffn1_w2, mix_norm, w_in, cmp_pe_k, cmp_w1_k, cmp_b1_k, cmp_w2_k,
           cmp_pe_v, cmp_w1_v, cmp_b1_v, cmp_w2_v, rel_bias, ssm_lam_re, ssm_lam_im, ssm_log_step, ssm_b_re,
           ssm_b_im, ssm_c_re, ssm_c_im, ssm_d, glu_w, glu_b, w_out, ffn2_norm, ffn2_w1, ffn2_w3, ffn2_w2,
           final_norm):
    b, t, d = x.shape
    depth = ffn1_w1.shape[0]
    h = x.reshape(b * t, d)
    for l in range(depth):
        last = l == depth - 1
        h = _ffn(h, ffn1_norm[l], ffn1_w1[l], ffn1_w3[l], ffn1_w2[l])
        ssm = (ssm_lam_re[l], ssm_lam_im[l], ssm_log_step[l], ssm_b_re[l], ssm_b_im[l], ssm_c_re[l], ssm_c_im[l])
        h = _mixers(h, mix_norm[l], w_in[l],
                    (cmp_pe_k[l], cmp_w1_k[l], cmp_b1_k[l], cmp_w2_k[l]),
                    (cmp_pe_v[l], cmp_w1_v[l], cmp_b1_v[l], cmp_w2_v[l]),
                    rel_bias, ssm, ssm_d[l], glu_w[l], glu_b[l], w_out[l], b, t)
        h = _ffn(h, ffn2_norm[l], ffn2_w1[l], ffn2_w3[l], ffn2_w2[l], final_gain=final_norm if last else None)
    return h.reshape(b, t, d)
```

```python
import functools
import math

import jax
import jax.numpy as jnp
from jax import lax
from jax.experimental import pallas as pl
from jax.experimental.pallas import tpu as pltpu

F32 = jnp.float32
BF16 = jnp.bfloat16

HEAD_DIM = 128
N_KV = 2
HEADS_PER_KV = 4
N_HEADS = N_KV * HEADS_PER_KV
D_ATTN = N_HEADS * HEAD_DIM
D_KV = N_KV * HEAD_DIM
CMP_LEN = 32
CMP_STRIDE = 16
SEL_BLOCK = 64
N_SELECT = 16
WINDOW = 512
N_BUCKETS = 32
MAX_DISTANCE = 128
SSM_GROUP = 16
SSM_STATE = 64
EPS = 1e-6
NEG = -1e30
LOG2E = math.log2(math.e)

LANES = 128
VMEM_LIMIT_BYTES = 56 * 1024 * 1024
NSA_VMEM_LIMIT_BYTES = 60 * 1024 * 1024
TQ = 256
QK = TQ // LANES
ROWS4 = HEADS_PER_KV * TQ
SEL_CHUNK = 512
SSM_CHUNK = 16


def _cparams(sem, vmem_limit_bytes=VMEM_LIMIT_BYTES):
    return pltpu.CompilerParams(dimension_semantics=sem, vmem_limit_bytes=vmem_limit_bytes)


def _ffn_body(x_ref, g_ref, w1_ref, w3_ref, w2_ref, *rest, final):
    if final:
        fg_ref, o_ref, xn_ref = rest
    else:
        o_ref, xn_ref = rest
    j = pl.program_id(1)
    nj = pl.num_programs(1)

    def contribution(xn):
        a = jnp.dot(xn, w1_ref[...], preferred_element_type=F32)
        b = jnp.dot(xn, w3_ref[...], preferred_element_type=F32)
        gated = (a * jax.nn.sigmoid(a)) * b
        return jnp.dot(gated.astype(BF16), w2_ref[...], preferred_element_type=F32)

    @pl.when(j == 0)
    def _():
        x = x_ref[...]
        ms = jnp.mean(x * x, axis=-1, keepdims=True)
        xn = (x * lax.rsqrt(ms + EPS) * g_ref[...]).astype(BF16)
        xn_ref[...] = xn
        o_ref[...] = contribution(xn)

    @pl.when((j > 0) & (j < nj - 1))
    def _():
        o_ref[...] += contribution(xn_ref[...])

    @pl.when(j == nj - 1)
    def _():
        h = x_ref[...] + 0.5 * (o_ref[...] + contribution(xn_ref[...]))
        if final:
            ms = jnp.mean(h * h, axis=-1, keepdims=True)
            h = h * lax.rsqrt(ms + EPS) * fg_ref[...]
        o_ref[...] = h


def _ffn(x, gain, w1, w3, w2, final_gain=None, *, tm=512, tf=512):
    n, d = x.shape
    dff = w1.shape[1]
    nj = dff // tf
    assert nj >= 2 and nj * tf == dff
    final = final_gain is not None
    in_specs = [
        pl.BlockSpec((tm, d), lambda i, j: (i, 0)),
        pl.BlockSpec((1, d), lambda i, j: (0, 0)),
        pl.BlockSpec((d, tf), lambda i, j: (0, j)),
        pl.BlockSpec((d, tf), lambda i, j: (0, j)),
        pl.BlockSpec((tf, d), lambda i, j: (j, 0)),
    ]
    args = [x, gain.reshape(1, d), w1.astype(BF16), w3.astype(BF16), w2.astype(BF16)]
    if final:
        in_specs.append(pl.BlockSpec((1, d), lambda i, j: (0, 0)))
        args.append(final_gain.reshape(1, d))
    return pl.pallas_call(
        functools.partial(_ffn_body, final=final),
        grid=(n // tm, nj),
        in_specs=in_specs,
        out_specs=pl.BlockSpec((tm, d), lambda i, j: (i, 0)),
        out_shape=jax.ShapeDtypeStruct((n, d), F32),
        scratch_shapes=[pltpu.VMEM((tm, d), BF16)],
        compiler_params=_cparams(("parallel", "arbitrary")),
        name="ffn_final" if final else "ffn",
    )(*args)


def _inproj_body(x_ref, g_ref, w_ref, q_ref, kvs_ref, kcv_ref, u_ref, gate_ref):
    i = pl.program_id(1)

    @pl.when(i == 0)
    def _():
        kvs_ref[...] = jnp.zeros(kvs_ref.shape, kvs_ref.dtype)
        kcv_ref[...] = jnp.zeros(kcv_ref.shape, kcv_ref.dtype)

    @pl.when(i > 0)
    def _():
        x = x_ref[...]
        ms = jnp.mean(x * x, axis=-1, keepdims=True)
        xn = (x * lax.rsqrt(ms + EPS) * g_ref[...]).astype(BF16)
        c0 = 0
        c1 = q_ref.shape[1]
        q = jnp.dot(xn, w_ref[:, c0:c1], preferred_element_type=F32)
        q_ref[...] = (q * (HEAD_DIM ** -0.5 * LOG2E)).astype(BF16)
        c0, c1 = c1, c1 + kcv_ref.shape[1]
        kcv_ref[...] = jnp.dot(xn, w_ref[:, c0:c1], preferred_element_type=F32)
        c0, c1 = c1, c1 + kvs_ref.shape[1]
        kvs_ref[...] = jnp.dot(xn, w_ref[:, c0:c1], preferred_element_type=F32).astype(BF16)
        c0, c1 = c1, c1 + u_ref.shape[1]
        u_ref[...] = jnp.dot(xn, w_ref[:, c0:c1], preferred_element_type=F32)
        gate_ref[...] = jax.nn.sigmoid(jnp.dot(xn, w_ref[:, c1:], preferred_element_type=F32))


def _inproj(h, gain, w_perm, d_ssm):
    b, t, d = h.shape
    tm = WINDOW
    nt = t // tm
    nq, ncv, nvs, ng = D_ATTN, 2 * D_KV, 4 * D_KV, N_KV * LANES
    ncol = w_perm.shape[1]
    assert ncol == nq + ncv + nvs + d_ssm + ng and t % tm == 0
    data = lambda bi, i: (bi, jnp.maximum(i - 1, 0), 0)
    return pl.pallas_call(
        _inproj_body,
        grid=(b, nt + 1),
        in_specs=[
            pl.BlockSpec((None, tm, d), data),
            pl.BlockSpec((1, d), lambda bi, i: (0, 0)),
            pl.BlockSpec((d, ncol), lambda bi, i: (0, 0)),
        ],
        out_specs=[
            pl.BlockSpec((None, tm, nq), data),
            pl.BlockSpec((None, tm, nvs), lambda bi, i: (bi, i, 0)),
            pl.BlockSpec((None, tm, ncv), lambda bi, i: (bi, jnp.where(i == 0, nt, i - 1), 0)),
            pl.BlockSpec((None, tm, d_ssm), data),
            pl.BlockSpec((None, tm, ng), data),
        ],
        out_shape=[
            jax.ShapeDtypeStruct((b, t, nq), BF16),
            jax.ShapeDtypeStruct((b, t + tm, nvs), BF16),
            jax.ShapeDtypeStruct((b, t + tm, ncv), F32),
            jax.ShapeDtypeStruct((b, t, d_ssm), F32),
            jax.ShapeDtypeStruct((b, t, ng), F32),
        ],
        compiler_params=_cparams(("parallel", "arbitrary")),
        name="inproj",
    )(h, gain.reshape(1, d), w_perm)


def _compress_body(x_ref, pe_ref, w1_ref, b1_ref, w2_ref, o_ref):
    nb, nblk = o_ref.shape[0], o_ref.shape[1]
    acc = jnp.broadcast_to(b1_ref[...], (nb * nblk, b1_ref.shape[1])).astype(F32)
    for l in range(CMP_LEN):
        xl = x_ref[:, pl.ds(l, nblk, stride=CMP_STRIDE), :] + pe_ref[l:l + 1, :]
        acc = acc + jnp.dot(xl.reshape(nb * nblk, xl.shape[2]).astype(BF16), w1_ref[l], preferred_element_type=F32)
    hid = jax.nn.gelu(acc, approximate=True)
    out = jnp.dot(hid.astype(BF16), w2_ref[...], preferred_element_type=F32)
    o_ref[...] = out.reshape(o_ref.shape).astype(BF16)


def _compress(kcv, pe, w1, b1, w2, nblk, *, nb=4):
    b, tpad, _ = kcv.shape
    nb = min(nb, b)
    hid, dh = w1.shape[3], w2.shape[2]
    assert (nblk - 1) * CMP_STRIDE + CMP_LEN <= tpad
    return pl.pallas_call(
        _compress_body,
        grid=(2 * N_KV, b // nb),
        in_specs=[
            pl.BlockSpec((nb, tpad, dh), lambda s, i: (i, 0, s)),
            pl.BlockSpec((None, CMP_LEN, dh), lambda s, i: (s // N_KV, 0, 0)),
            pl.BlockSpec((None, CMP_LEN, dh, hid), lambda s, i: (s // N_KV, 0, 0, 0)),
            pl.BlockSpec((None, 1, hid), lambda s, i: (s // N_KV, 0, 0)),
            pl.BlockSpec((None, hid, dh), lambda s, i: (s // N_KV, 0, 0)),
        ],
        out_specs=pl.BlockSpec((None, None, nb, nblk, dh), lambda s, i: (s // N_KV, s % N_KV, i, 0, 0)),
        out_shape=jax.ShapeDtypeStruct((2, N_KV, b, nblk, dh), BF16),
        compiler_params=_cparams(("parallel", "parallel")),
        name="compress",
    )(kcv, pe, w1, b1, w2)


def _nt_dot(a, b):
    return lax.dot_general(a, b, (((1,), (1,)), ((), ())), preferred_element_type=F32)


def _nsa_body(q_ref, ks_ref, vs_ref, kw_ref, vw_ref, kcb_ref, vcb_ref, gate_ref, biasc_ref, dt_ref, wb_ref,
              ovt_ref, et_ref, o_ref, kse_ref, s_ref, m_ref, l_ref, acc_ref):
    i = pl.program_id(1)
    t0 = i * TQ
    t = et_ref.shape[0]
    ns = t // SEL_BLOCK
    nsel = min(N_SELECT, ns)
    groups = range(N_KV)
    gl = lambda g: slice(g * HEAD_DIM, (g + 1) * HEAD_DIM)

    @pl.when(i == 0)
    def _():
        for g in groups:
            kse_ref[g, :, 0:HEAD_DIM] = ks_ref[WINDOW:WINDOW + t, gl(g)]
            kse_ref[g, :, HEAD_DIM:] = et_ref[...]

    q = q_ref[...]
    w0 = pl.multiple_of(t0, LANES)
    span = wb_ref.shape[2]

    q4 = [jnp.concatenate([q[:, (g * HEADS_PER_KV + h) * HEAD_DIM:(g * HEADS_PER_KV + h + 1) * HEAD_DIM]
                           for h in range(HEADS_PER_KV)], axis=0) for g in groups]

    sc = [_nt_dot(q4[g], kcb_ref[g]) + biasc_ref[g] for g in groups]
    row_t = t0 + (lax.broadcasted_iota(jnp.int32, (ROWS4, LANES), 0) & (TQ - 1))
    col_c = lax.broadcasted_iota(jnp.int32, (ROWS4, LANES), 1)
    valid_c = (row_t - col_c * CMP_STRIDE - (CMP_LEN - 1) >= 0) & (col_c < LANES - 1)
    mc = [jnp.max(s, axis=-1, keepdims=True) for s in sc]
    pc = [jnp.where(valid_c, jnp.exp2(s - m), 0.0) for s, m in zip(sc, mc)]
    pc = [p / jnp.maximum(jnp.sum(p, axis=-1, keepdims=True), 1e-30) for p in pc]
    pcb = [p.astype(BF16) for p in pc]
    o_cmps = [jnp.dot(pcb[g], vcb_ref[g], preferred_element_type=F32) for g in groups]
    pimp = [_nt_dot(ovt_ref[...], p) for p in pcb]
    imp = [sum(p[0:ns, h * TQ:(h + 1) * TQ] for h in range(1, HEADS_PER_KV)) + p[0:ns, 0:TQ] for p in pimp]

    jrow = lax.broadcasted_iota(jnp.int32, (ns, TQ), 0)
    tpos = t0 + lax.broadcasted_iota(jnp.int32, (ns, TQ), 1)
    cur = lax.shift_right_logical(tpos, int(math.log2(SEL_BLOCK)))
    forced = (jrow == 0) | (jrow == cur) | (jrow == cur - 1)
    impm = [jnp.where(forced, 1e6, jnp.where(jrow * SEL_BLOCK <= tpos, x, -1e9)) for x in imp]
    nslab = ns // 8
    slabs = [[x[8 * v:8 * v + 8] for v in range(nslab)] for x in impm]
    cnts = [[jnp.zeros((8, TQ), F32) for _ in range(nslab)] for _ in groups]
    sub = lax.broadcasted_iota(jnp.int32, (8, TQ), 0)
    for jp in range(ns):
        v0, r0 = divmod(jp, 8)
        for g in groups:
            row = jnp.broadcast_to(slabs[g][v0][r0:r0 + 1, :], (8, TQ))
            for v in range(nslab):
                if v > v0:
                    beats = row >= slabs[g][v]
                elif v < v0:
                    beats = row > slabs[g][v]
                else:
                    beats = (row > slabs[g][v]) | ((row == slabs[g][v]) & (sub > r0))
                cnts[g][v] = cnts[g][v] + jnp.where(beats, 1.0, 0.0)
    q4s = []
    for g in groups:
        negt = [jnp.where(c < nsel, 0.0, NEG) for c in cnts[g]]
        negt = jnp.concatenate(negt + [jnp.zeros((LANES - ns, TQ), F32)], axis=0)
        neg = negt.T.astype(BF16)
        q4s.append(jnp.concatenate([q4[g], jnp.concatenate([neg] * HEADS_PER_KV, axis=0)], axis=1))

    nsub = SEL_CHUNK // LANES
    nchunks = ((i + 1) * QK - 1) // nsub + 1
    m_ref[...] = jnp.full(m_ref.shape, -3e38, F32)

    def score_chunk(ck, _):
        kb = pl.multiple_of(ck * SEL_CHUNK, SEL_CHUNK)
        s = [_nt_dot(q4s[g], kse_ref[g, pl.ds(kb, SEL_CHUNK), :]) for g in groups]
        for g in groups:
            mloc = None
            for j in range(nsub):
                sj = s[g][:, j * LANES:(j + 1) * LANES] + dt_ref[g, jnp.clip(ck * nsub + j - i * QK + 2, 0, QK + 2)]
                s_ref[g, ck, :, j * LANES:(j + 1) * LANES] = sj
                mloc = sj if mloc is None else jnp.maximum(mloc, sj)
            m_ref[g] = jnp.maximum(m_ref[g], mloc)
        return 0

    lax.fori_loop(0, nchunks, score_chunk, 0)
    for g in groups:
        m_ref[g] = jnp.broadcast_to(jnp.max(m_ref[g], axis=-1, keepdims=True), (ROWS4, LANES))
    l_ref[...] = jnp.zeros(l_ref.shape, F32)
    acc_ref[...] = jnp.zeros(acc_ref.shape, F32)

    kneg = jnp.where(t0 - WINDOW + lax.broadcasted_iota(jnp.int32, (1, span), 1) >= 0, 0.0, NEG)
    sw = [_nt_dot(q4[g], kw_ref[pl.ds(w0, span), gl(g)]) + wb_ref[g] + kneg for g in groups]
    mw = [jnp.max(s, axis=-1, keepdims=True) for s in sw]
    pw = [jnp.exp2(s - m) for s, m in zip(sw, mw)]
    lw = [jnp.sum(p, axis=-1, keepdims=True) for p in pw]
    o_wins = [jnp.dot(pw[g].astype(BF16), vw_ref[pl.ds(w0, span), gl(g)], preferred_element_type=F32) for g in groups]
    o_wins = [o / jnp.maximum(l, 1e-30) for o, l in zip(o_wins, lw)]

    def pv_chunk(ck, _):
        kb = pl.multiple_of(WINDOW + ck * SEL_CHUNK, SEL_CHUNK)
        ps = [[jnp.exp2(s_ref[g, ck, :, j * LANES:(j + 1) * LANES] - m_ref[g]) for j in range(nsub)] for g in groups]
        for g in groups:
            lsum = ps[g][0]
            for pj in ps[g][1:]:
                lsum = lsum + pj
            l_ref[g] += lsum
            p = jnp.concatenate([pj.astype(BF16) for pj in ps[g]], axis=1)
            acc_ref[g] += jnp.dot(p, vs_ref[pl.ds(kb, SEL_CHUNK), gl(g)], preferred_element_type=F32)
        return 0

    lax.fori_loop(0, nchunks, pv_chunk, 0)

    ls = [jnp.maximum(jnp.sum(l_ref[g], axis=-1, keepdims=True), 1e-30) for g in groups]
    o_sels = [acc_ref[g] / ls[g] for g in groups]
    gts = [gate_ref[:, g * LANES:(g + 1) * LANES] for g in groups]
    for h in range(HEADS_PER_KV):
        r = slice(h * TQ, (h + 1) * TQ)
        for g in groups:
            gt = gts[g]
            o_h = (gt[:, 3 * h:3 * h + 1] * o_cmps[g][r] + gt[:, 3 * h + 1:3 * h + 2] * o_sels[g][r]
                   + gt[:, 3 * h + 2:3 * h + 3] * o_wins[g][r])
            c0 = (g * HEADS_PER_KV + h) * HEAD_DIM
            o_ref[:, c0:c0 + HEAD_DIM] = o_h.astype(BF16)


def _nsa(q, kvs, cb, gates, biasc, dtiles, wb, ovt, et):
    b, t, _ = q.shape
    nt = t // TQ
    tp = kvs.shape[1]
    span = wb.shape[2]
    gw = HEADS_PER_KV * HEAD_DIM
    assert tp == t + WINDOW and span == TQ + WINDOW and t % SEL_CHUNK == 0 and (t // SEL_BLOCK) % 8 == 0

    def stream(j):
        return pl.BlockSpec((None, tp, N_KV * HEAD_DIM), lambda bi, i: (bi, 0, j))

    return pl.pallas_call(
        _nsa_body,
        grid=(b, nt),
        in_specs=[
            pl.BlockSpec((None, TQ, N_KV * gw), lambda bi, i: (bi, i, 0)),
            stream(0), stream(1), stream(2), stream(3),
            pl.BlockSpec((None, N_KV, None, LANES, HEAD_DIM), lambda bi, i: (0, 0, bi, 0, 0)),
            pl.BlockSpec((None, N_KV, None, LANES, HEAD_DIM), lambda bi, i: (1, 0, bi, 0, 0)),
            pl.BlockSpec((None, TQ, N_KV * LANES), lambda bi, i: (bi, i, 0)),
            pl.BlockSpec((N_KV, None, ROWS4, LANES), lambda bi, i: (0, i, 0, 0)),
            pl.BlockSpec((N_KV, QK + 3, ROWS4, LANES), lambda bi, i: (0, 0, 0, 0)),
            pl.BlockSpec((N_KV, ROWS4, span), lambda bi, i: (0, 0, 0)),
            pl.BlockSpec((LANES, LANES), lambda bi, i: (0, 0)),
            pl.BlockSpec((t, LANES), lambda bi, i: (0, 0)),
        ],
        out_specs=pl.BlockSpec((None, TQ, N_KV * gw), lambda bi, i: (bi, i, 0)),
        out_shape=jax.ShapeDtypeStruct((b, t, N_KV * gw), BF16),
        scratch_shapes=[
            pltpu.VMEM((N_KV, t, 2 * HEAD_DIM), BF16),
            pltpu.VMEM((N_KV, t // SEL_CHUNK, ROWS4, SEL_CHUNK), F32),
            pltpu.VMEM((N_KV, ROWS4, LANES), F32),
            pltpu.VMEM((N_KV, ROWS4, LANES), F32),
            pltpu.VMEM((N_KV, ROWS4, HEAD_DIM), F32),
        ],
        compiler_params=_cparams(("parallel", "arbitrary"), NSA_VMEM_LIMIT_BYTES),
        name="nsa",
    )(q, kvs, kvs, kvs, kvs, cb, cb, gates, biasc, dtiles, wb, ovt, et)


def _s5_body(u_ref, c0_ref, winc_ref, woutc_ref, rep_k_ref, rep_in_ref, rep_out_ref, m_in_ref, m_out_ref,
             al_ref, d_ref, y_ref, ucat_ref, inj_ref, xp_ref, grev_ref, win_ref, wout_ref):
    nb, t, lanes = u_ref.shape
    L = SSM_CHUNK
    nchunk = t // L
    rows = nb * nchunk
    sdim = al_ref.shape[1] // 2

    @pl.when(pl.program_id(1) == 0)
    def _():
        m_in, m_out = m_in_ref[...], m_out_ref[...]
        for half in range(2):
            for s in range(L):
                x = jnp.dot(winc_ref[half, s * lanes:(s + 1) * lanes, :], rep_in_ref[...], preferred_element_type=F32)
                win_ref[s * lanes:(s + 1) * lanes, half * sdim:(half + 1) * sdim] = (x * m_in).astype(BF16)
            for tt in range(L):
                x = jnp.dot(woutc_ref[half], rep_out_ref[:, tt * lanes:(tt + 1) * lanes], preferred_element_type=F32)
                wout_ref[half * sdim:(half + 1) * sdim, tt * lanes:(tt + 1) * lanes] = (x * m_out).astype(BF16)
        c0 = [(jnp.dot(c0_ref[half], rep_k_ref[...], preferred_element_type=F32) * m_out).astype(BF16)
              for half in range(2)]
        kx = jnp.dot(win_ref[...], jnp.concatenate(c0, axis=0), preferred_element_type=F32)
        zero = jnp.zeros((lanes, lanes), BF16)

        def kblk(tau):
            s = L - 1 - tau
            return kx[s * lanes:(s + 1) * lanes].astype(BF16) if 0 <= tau < L else zero

        for r in range(L + 1):
            grev_ref[r * lanes:(r + 1) * lanes, 0:lanes] = kblk(L - r - 1)
            grev_ref[r * lanes:(r + 1) * lanes, lanes:] = kblk(L - r)

    def u_at(s):
        return u_ref[:, pl.ds(s, nchunk, stride=L), :].reshape(rows, lanes)

    for s in range(L):
        ucat_ref[:, s * lanes:(s + 1) * lanes] = u_at(s).astype(BF16)
    inj = jnp.dot(ucat_ref[...], win_ref[...], preferred_element_type=F32)
    npl = sdim // lanes
    for k in range(2 * npl):
        inj_ref[k] = inj[:, k * lanes:(k + 1) * lanes]
    ar = [jnp.broadcast_to(al_ref[0:1, k * lanes:(k + 1) * lanes], (nb, lanes)) for k in range(npl)]
    ai = [jnp.broadcast_to(al_ref[0:1, sdim + k * lanes:sdim + (k + 1) * lanes], (nb, lanes)) for k in range(npl)]

    def step(c, carry):
        xr, xi = carry
        rsel = pl.ds(c, nb, stride=nchunk)
        nr, ni = [], []
        for k in range(npl):
            xp_ref[k, rsel, :] = xr[k]
            xp_ref[npl + k, rsel, :] = xi[k]
            nr.append(ar[k] * xr[k] - ai[k] * xi[k] + inj_ref[k, rsel, :])
            ni.append(ar[k] * xi[k] + ai[k] * xr[k] + inj_ref[npl + k, rsel, :])
        return tuple(nr), tuple(ni)

    z = tuple(jnp.zeros((nb, lanes), F32) for _ in range(npl))
    lax.fori_loop(0, nchunk, step, (z, z), unroll=4)
    xp = jnp.concatenate([xp_ref[k] for k in range(2 * npl)], axis=1).astype(BF16)
    d = d_ref[...]
    for tp in range(0, L, 2):
        res = jnp.dot(ucat_ref[:, 0:(tp + 2) * lanes], grev_ref[(L - 1 - tp) * lanes:(L + 1) * lanes, :],
                      preferred_element_type=F32)
        res = res + jnp.dot(xp, wout_ref[:, tp * lanes:(tp + 2) * lanes], preferred_element_type=F32)
        for k in range(2):
            y = res[:, k * lanes:(k + 1) * lanes] + d * u_at(tp + k)
            y_ref[:, pl.ds(tp + k, nchunk, stride=L), :] = y.reshape(nb, nchunk, lanes)


def _s5(u, c0c, winc, woutc, al, d, *, nb=4):
    b, t, dch = u.shape
    nb = min(nb, b)
    nblk = dch // LANES
    L = SSM_CHUNK
    rows = nb * (t // L)
    hch, p = c0c.shape[3], winc.shape[3]
    gpb = LANES // hch
    sdim = gpb * p
    kcat = L * LANES
    lane, col, st = jnp.arange(LANES), jnp.arange(kcat), jnp.arange(sdim)
    rep_k = (jnp.arange(hch)[:, None] == (lane % hch)[None, :]).astype(BF16)
    rep_in = (jnp.arange(p)[:, None] == (st % p)[None, :]).astype(BF16)
    rep_out = (jnp.arange(L * hch)[:, None] == (col // LANES * hch + col % hch)[None, :]).astype(BF16)
    m_in = ((lane // hch)[:, None] == (st // p)[None, :]).astype(F32)
    m_out = ((st // p)[:, None] == (lane // hch)[None, :]).astype(F32)
    const = lambda a: pl.BlockSpec(a.shape, lambda j, i: (0,) * a.ndim)
    return pl.pallas_call(
        _s5_body,
        grid=(nblk, b // nb),
        in_specs=[
            pl.BlockSpec((nb, t, LANES), lambda j, i: (i, 0, j)),
            pl.BlockSpec((None, 2, sdim, hch), lambda j, i: (j, 0, 0, 0)),
            pl.BlockSpec((None, 2, kcat, p), lambda j, i: (j, 0, 0, 0)),
            pl.BlockSpec((None, 2, sdim, L * hch), lambda j, i: (j, 0, 0, 0)),
            const(rep_k), const(rep_in), const(rep_out), const(m_in), const(m_out),
            pl.BlockSpec((None, 8, 2 * sdim), lambda j, i: (j, 0, 0)),
            pl.BlockSpec((1, LANES), lambda j, i: (0, j)),
        ],
        out_specs=pl.BlockSpec((nb, t, LANES), lambda j, i: (i, 0, j)),
        out_shape=jax.ShapeDtypeStruct((b, t, dch), F32),
        scratch_shapes=[pltpu.VMEM((rows, kcat), BF16), pltpu.VMEM((2 * sdim // LANES, rows, LANES), F32),
                        pltpu.VMEM((2 * sdim // LANES, rows, LANES), F32),
                        pltpu.VMEM(((L + 1) * LANES, 2 * LANES), BF16), pltpu.VMEM((kcat, 2 * sdim), BF16),
                        pltpu.VMEM((2 * sdim, kcat), BF16)],
        compiler_params=_cparams(("parallel", "arbitrary")),
        name="s5",
    )(u, c0c, winc, woutc, rep_k, rep_in, rep_out, m_in, m_out, al, d.reshape(1, dch))


def _glu_out_body(h_ref, a_ref, y_ref, wg_ref, bg_ref, woa_ref, wos_ref, o_ref):
    hg = jax.nn.gelu(y_ref[...], approximate=True)
    z = jnp.dot(hg.astype(BF16), wg_ref[...], preferred_element_type=F32) + bg_ref[...]
    s = hg * jax.nn.sigmoid(z)
    mix = jnp.dot(a_ref[...], woa_ref[...], preferred_element_type=F32)
    mix = mix + jnp.dot(s.astype(BF16), wos_ref[...], preferred_element_type=F32)
    o_ref[...] = h_ref[...] + mix


def _glu_out(h, a, y, wg, bg, woa, wos, *, tm=512):
    n, dm = h.shape
    da, ds = a.shape[1], y.shape[1]
    row = lambda w: pl.BlockSpec((tm, w), lambda i: (i, 0))
    full = lambda r, c: pl.BlockSpec((r, c), lambda i: (0, 0))
    return pl.pallas_call(
        _glu_out_body,
        grid=(n // tm,),
        in_specs=[row(dm), row(da), row(ds), full(ds, ds), full(1, ds), full(da, dm), full(ds, dm)],
        out_specs=row(dm),
        out_shape=jax.ShapeDtypeStruct((n, dm), F32),
        compiler_params=_cparams(("parallel",)),
        name="glu_out",
    )(h, a, y, wg, bg.reshape(1, ds), woa, wos)


def _t5_bucket(dist):
    n = jnp.maximum(dist, 0)
    max_exact = N_BUCKETS // 2
    nf = jnp.maximum(n, 1).astype(F32)
    large = max_exact + (jnp.log(nf / max_exact) / math.log(MAX_DISTANCE / max_exact)
                         * (N_BUCKETS - max_exact)).astype(jnp.int32)
    large = jnp.minimum(large, N_BUCKETS - 1)
    return jnp.where(n < max_exact, n, large)


def _t5_body(first_ref, table_ref, bc_ref, dt_ref, wb_ref, *, nc):
    g = pl.program_id(0)
    nt = bc_ref.shape[0]
    span = wb_ref.shape[1]
    a = lax.broadcasted_iota(jnp.int32, (TQ, LANES), 0)
    c = lax.broadcasted_iota(jnp.int32, (TQ, LANES), 1)
    heads = [g * HEADS_PER_KV + h for h in range(HEADS_PER_KV)]

    def lookup(dist, valid, store):
        bias = [jnp.full((TQ, LANES), table_ref[0, hd] * LOG2E, F32) for hd in heads]
        for k in range(1, N_BUCKETS):
            ge = dist >= first_ref[k]
            bias = [jnp.where(ge, table_ref[k, hd] * LOG2E, bv) for hd, bv in zip(heads, bias)]
        for h, bv in enumerate(bias):
            store(slice(h * TQ, (h + 1) * TQ), bv if valid is None else jnp.where(valid, bv, NEG))

    def put_far(rows, tile):
        dt_ref[0, rows, :] = tile

    lookup(jnp.full((TQ, LANES), 2 * MAX_DISTANCE, jnp.int32), None, put_far)
    for r in range(-1, QK):
        def put_near(rows, tile, r=r):
            dt_ref[r + 2, rows, :] = tile

        dist = a - r * LANES - c
        lookup(dist, dist >= 0, put_near)
    dt_ref[QK + 2] = jnp.full((ROWS4, LANES), NEG, F32)
    for j in range(span // LANES):
        def put_wb(rows, tile, j=j):
            wb_ref[rows, j * LANES:(j + 1) * LANES] = tile

        dist = a + WINDOW - (c + j * LANES)
        lookup(dist, (dist >= 0) & (dist < WINDOW), put_wb)

    def cmp_tile(i, _):
        def put_bc(rows, tile):
            bc_ref[i, rows, :] = tile

        dist = i * TQ + a - (c * CMP_STRIDE + CMP_LEN - 1)
        lookup(dist, (dist >= 0) & (c < nc), put_bc)
        return 0

    lax.fori_loop(0, nt, cmp_tile, 0)


def _bias_tables(rel_bias, t):
    nt = t // TQ
    span = TQ + WINDOW
    nc = (t - CMP_LEN) // CMP_STRIDE + 1
    buckets = _t5_bucket(jnp.arange(2 * MAX_DISTANCE))
    first = jnp.sum((buckets[None, :] < jnp.arange(N_BUCKETS)[:, None]).astype(jnp.int32), axis=1)
    smem = pl.BlockSpec(memory_space=pltpu.SMEM)
    return pl.pallas_call(
        functools.partial(_t5_body, nc=nc),
        grid=(N_KV,),
        in_specs=[smem, smem],
        out_specs=[
            pl.BlockSpec((None, nt, ROWS4, LANES), lambda g: (g, 0, 0, 0)),
            pl.BlockSpec((None, QK + 3, ROWS4, LANES), lambda g: (g, 0, 0, 0)),
            pl.BlockSpec((None, ROWS4, span), lambda g: (g, 0, 0)),
        ],
        out_shape=[
            jax.ShapeDtypeStruct((N_KV, nt, ROWS4, LANES), F32),
            jax.ShapeDtypeStruct((N_KV, QK + 3, ROWS4, LANES), F32),
            jax.ShapeDtypeStruct((N_KV, ROWS4, span), F32),
        ],
        compiler_params=_cparams(("parallel",)),
        name="t5_tables",
    )(first, rel_bias.astype(F32))


def _sel_tables(t):
    ns = t // SEL_BLOCK
    nc_pad = LANES
    c_start = jnp.arange(nc_pad) * CMP_STRIDE
    j_start = jnp.arange(LANES) * SEL_BLOCK
    ov = jnp.clip(jnp.minimum(c_start[:, None] + CMP_LEN, j_start[None, :] + SEL_BLOCK)
                  - jnp.maximum(c_start[:, None], j_start[None, :]), 0, None).astype(F32) / CMP_LEN
    ov = jnp.where(jnp.arange(LANES)[None, :] < ns, ov, 0.0)
    et = jnp.arange(t)[:, None] // SEL_BLOCK == jnp.arange(LANES)[None, :]
    return ov.T.astype(BF16), et.astype(BF16)


def _s5_tables(lam_re, lam_im, log_step, b_re, b_im, c_re, c_im):
    ng, p = lam_re.shape
    hch = b_re.shape[2]
    L = SSM_CHUNK
    step = jnp.exp(log_step.astype(F32))[:, None]
    lre, lim = lam_re.astype(F32), lam_im.astype(F32)
    mag = jnp.exp(lre * step)
    ab_re, ab_im = mag * jnp.cos(lim * step), mag * jnp.sin(lim * step)
    nr, ni = ab_re - 1.0, ab_im
    den = lre * lre + lim * lim
    f_re, f_im = (nr * lre + ni * lim) / den, (ni * lre - nr * lim) / den
    br, bim = b_re.astype(F32), b_im.astype(F32)
    bb_re = f_re[..., None] * br - f_im[..., None] * bim
    bb_im = f_re[..., None] * bim + f_im[..., None] * br
    cr, ci = c_re.astype(F32), c_im.astype(F32)
    pr, pi = [jnp.ones_like(ab_re)], [jnp.zeros_like(ab_re)]
    for _ in range(L):
        pr, pi = pr + [pr[-1] * ab_re - pi[-1] * ab_im], pi + [pr[-1] * ab_im + pi[-1] * ab_re]
    pw_re, pw_im = jnp.stack(pr, 0), jnp.stack(pi, 0)
    cp_re = cr[None] * pw_re[:, :, None, :] - ci[None] * pw_im[:, :, None, :]
    cp_im = -(cr[None] * pw_im[:, :, None, :] + ci[None] * pw_re[:, :, None, :])
    gpb = LANES // hch
    nblk = ng // gpb
    wr = pw_re[L - 1 - jnp.arange(L)]
    wi = pw_im[L - 1 - jnp.arange(L)]
    win_re = (wr[..., None] * bb_re[None] - wi[..., None] * bb_im[None])
    win_im = (wr[..., None] * bb_im[None] + wi[..., None] * bb_re[None])
    inj_op = lambda m: m.reshape(L, nblk, gpb, p, hch).transpose(1, 0, 2, 4, 3).reshape(nblk, L * LANES, p)
    winc = jnp.stack([inj_op(win_re), inj_op(win_im)], axis=1)
    read_op = lambda m: m.reshape(L, nblk, gpb, hch, p).transpose(1, 2, 4, 0, 3).reshape(nblk, gpb * p, L * hch)
    woutc = jnp.stack([read_op(cp_re[1:]), read_op(cp_im[1:])], axis=1)
    read0 = lambda m: m.reshape(nblk, gpb, hch, p).transpose(0, 1, 3, 2).reshape(nblk, gpb * p, hch)
    c0c = jnp.stack([read0(cp_re[0]), read0(cp_im[0])], axis=1)
    al = jnp.concatenate([pw_re[L].reshape(nblk, gpb * p), pw_im[L].reshape(nblk, gpb * p)], axis=1)
    al = jnp.broadcast_to(al[:, None, :], (nblk, 8, 2 * gpb * p))
    return c0c.astype(BF16), winc.astype(BF16), woutc.astype(BF16), al


def _mixers(h1, mix_norm, w_in, cmp_k, cmp_v, rel_bias, ssm, ssm_d, glu_w, glu_b, w_out, b, t):
    n, d = h1.shape
    d_ssm = glu_w.shape[0]
    o_g, o_u = D_ATTN + 6 * D_KV, D_ATTN + 6 * D_KV + 3 * N_HEADS
    gcols = 3 * HEADS_PER_KV
    wg = [jnp.pad(w_in[:, o_g + g * gcols:o_g + (g + 1) * gcols], ((0, 0), (0, LANES - gcols))) for g in range(N_KV)]
    w_perm = jnp.concatenate([w_in[:, :o_g], w_in[:, o_u:]] + wg, axis=1).astype(BF16)
    q, kvs, kcv, u, gates = _inproj(h1.reshape(b, t, d), mix_norm, w_perm, d_ssm)

    nblk = t // CMP_STRIDE
    assert nblk == LANES, "compressed-block axis is laid out on one 128-lane tile"
    pe = jnp.stack([cmp_k[0], cmp_v[0]], 0).astype(F32)
    w1 = jnp.stack([cmp_k[1], cmp_v[1]], 0).astype(BF16).reshape(2, CMP_LEN, HEAD_DIM, -1)
    b1 = jnp.stack([cmp_k[2].reshape(1, -1), cmp_v[2].reshape(1, -1)], 0).astype(F32)
    w2 = jnp.stack([cmp_k[3], cmp_v[3]], 0).astype(BF16)
    cb = _compress(kcv, pe, w1, b1, w2, nblk)

    biasc, dtiles, wb = _bias_tables(rel_bias, t)
    ovt, et = _sel_tables(t)
    a = _nsa(q, kvs, cb, gates, biasc, dtiles, wb, ovt, et)

    c0c, winc, woutc, al = _s5_tables(*ssm)
    y = _s5(u, c0c, winc, woutc, al, ssm_d)

    wo = w_out.astype(BF16)
    return _glu_out(h1, a.reshape(n, D_ATTN), y.reshape(n, d_ssm), glu_w.astype(BF16), glu_b, wo[:D_ATTN], wo[D_ATTN:])


def kernel(x, ffn1_norm, ffn1_w1, ffn1_w3, ffn1_w2, mix_norm, w_in, cmp_pe_k, cmp_w1_k, cmp_b1_k, cmp_w2_k,
           cmp_pe_v, cmp_w1_v, cmp_b1_v, cmp_w2_v, rel_bias, ssm_lam_re, ssm_lam_im, ssm_log_step, ssm_b_re,
           ssm_b_im, ssm_c_re, ssm_c_im, ssm_d, glu_w, glu_b, w_out, ffn2_norm, ffn2_w1, ffn2_w3, ffn2_w2,
           final_norm):
    b, t, d = x.shape
    depth = ffn1_w1.shape[0]
    h = x.reshape(b * t, d)
    for l in range(depth):
        last = l == depth - 1
        h = _ffn(h, ffn1_norm[l], ffn1_w1[l], ffn1_w3[l], ffn1_w2[l])
        ssm = (ssm_lam_re[l], ssm_lam_im[l], ssm_log_step[l], ssm_b_re[l], ssm_b_im[l], ssm_c_re[l], ssm_c_im[l])
        h = _mixers(h, mix_norm[l], w_in[l],
                    (cmp_pe_k[l], cmp_w1_k[l], cmp_b1_k[l], cmp_w2_k[l]),
                    (cmp_pe_v[l], cmp_w1_v[l], cmp_b1_v[l], cmp_w2_v[l]),
                    rel_bias, ssm, ssm_d[l], glu_w[l], glu_b[l], w_out[l], b, t)
        h = _ffn(h, ffn2_norm[l], ffn2_w1[l], ffn2_w3[l], ffn2_w2[l], final_gain=final_norm if last else None)
    return h.reshape(b, t, d)
```

```python
import functools
import math

import jax
import jax.numpy as jnp
from jax import lax
from jax.experimental import pallas as pl
from jax.experimental.pallas import tpu as pltpu

F32 = jnp.float32
BF16 = jnp.bfloat16

HEAD_DIM = 128
N_KV = 2
HEADS_PER_KV = 4
N_HEADS = N_KV * HEADS_PER_KV
D_ATTN = N_HEADS * HEAD_DIM
D_KV = N_KV * HEAD_DIM
CMP_LEN = 32
CMP_STRIDE = 16
SEL_BLOCK = 64
N_SELECT = 16
WINDOW = 512
N_BUCKETS = 32
MAX_DISTANCE = 128
SSM_GROUP = 16
SSM_STATE = 64
EPS = 1e-6
NEG = -1e30
LOG2E = math.log2(math.e)

LANES = 128
VMEM_LIMIT_BYTES = 56 * 1024 * 1024
NSA_VMEM_LIMIT_BYTES = 60 * 1024 * 1024
TQ = 256
QK = TQ // LANES
ROWS4 = HEADS_PER_KV * TQ
SEL_CHUNK = 512
SSM_CHUNK = 16


def _cparams(sem, vmem_limit_bytes=VMEM_LIMIT_BYTES):
    return pltpu.CompilerParams(dimension_semantics=sem, vmem_limit_bytes=vmem_limit_bytes)


def _ffn_body(x_ref, g_ref, w1_ref, w3_ref, w2_ref, *rest, final):
    if final:
        fg_ref, o_ref, xn_ref = rest
    else:
        o_ref, xn_ref = rest
    j = pl.program_id(1)
    nj = pl.num_programs(1)

    def contribution(xn):
        a = jnp.dot(xn, w1_ref[...], preferred_element_type=F32)
        b = jnp.dot(xn, w3_ref[...], preferred_element_type=F32)
        gated = (a * jax.nn.sigmoid(a)) * b
        return jnp.dot(gated.astype(BF16), w2_ref[...].astype(BF16), preferred_element_type=F32)

    @pl.when(j == 0)
    def _():
        x = x_ref[...]
        ms = jnp.mean(x * x, axis=-1, keepdims=True)
        xn = (x * lax.rsqrt(ms + EPS) * g_ref[...]).astype(BF16)
        xn_ref[...] = xn
        o_ref[...] = contribution(xn)

    @pl.when((j > 0) & (j < nj - 1))
    def _():
        o_ref[...] += contribution(xn_ref[...])

    @pl.when(j == nj - 1)
    def _():
        h = x_ref[...] + 0.5 * (o_ref[...] + contribution(xn_ref[...]))
        if final:
            ms = jnp.mean(h * h, axis=-1, keepdims=True)
            h = h * lax.rsqrt(ms + EPS) * fg_ref[...]
        o_ref[...] = h


def _ffn(x, gain, w1, w3, w2, final_gain=None, *, tm=512, tf=512):
    n, d = x.shape
    dff = w1.shape[1]
    nj = dff // tf
    assert nj >= 2 and nj * tf == dff
    final = final_gain is not None
    in_specs = [
        pl.BlockSpec((tm, d), lambda i, j: (i, 0)),
        pl.BlockSpec((1, d), lambda i, j: (0, 0)),
        pl.BlockSpec((d, tf), lambda i, j: (0, j)),
        pl.BlockSpec((d, tf), lambda i, j: (0, j)),
        pl.BlockSpec((tf, d), lambda i, j: (j, 0)),
    ]
    args = [x, gain.reshape(1, d), w1.astype(BF16), w3.astype(BF16), w2]
    if final:
        in_specs.append(pl.BlockSpec((1, d), lambda i, j: (0, 0)))
        args.append(final_gain.reshape(1, d))
    return pl.pallas_call(
        functools.partial(_ffn_body, final=final),
        grid=(n // tm, nj),
        in_specs=in_specs,
        out_specs=pl.BlockSpec((tm, d), lambda i, j: (i, 0)),
        out_shape=jax.ShapeDtypeStruct((n, d), F32),
        scratch_shapes=[pltpu.VMEM((tm, d), BF16)],
        compiler_params=_cparams(("parallel", "arbitrary")),
        name="ffn_final" if final else "ffn",
    )(*args)


def _inproj_body(x_ref, g_ref, w_ref, q_ref, kvs_ref, kcv_ref, u_ref, gate_ref):
    i = pl.program_id(1)

    @pl.when(i == 0)
    def _():
        kvs_ref[...] = jnp.zeros(kvs_ref.shape, kvs_ref.dtype)
        kcv_ref[...] = jnp.zeros(kcv_ref.shape, kcv_ref.dtype)

    @pl.when(i > 0)
    def _():
        x = x_ref[...]
        ms = jnp.mean(x * x, axis=-1, keepdims=True)
        xn = (x * lax.rsqrt(ms + EPS) * g_ref[...]).astype(BF16)
        c0 = 0
        c1 = q_ref.shape[1]
        q = jnp.dot(xn, w_ref[:, c0:c1], preferred_element_type=F32)
        q_ref[...] = (q * (HEAD_DIM ** -0.5 * LOG2E)).astype(BF16)
        c0, c1 = c1, c1 + kcv_ref.shape[1]
        kcv_ref[...] = jnp.dot(xn, w_ref[:, c0:c1], preferred_element_type=F32)
        c0, c1 = c1, c1 + kvs_ref.shape[1]
        kvs_ref[...] = jnp.dot(xn, w_ref[:, c0:c1], preferred_element_type=F32).astype(BF16)
        c0, c1 = c1, c1 + u_ref.shape[1]
        u_ref[...] = jnp.dot(xn, w_ref[:, c0:c1], preferred_element_type=F32)
        gate_ref[...] = jax.nn.sigmoid(jnp.dot(xn, w_ref[:, c1:], preferred_element_type=F32))


def _inproj(h, gain, w_perm, d_ssm):
    b, t, d = h.shape
    tm = WINDOW
    nt = t // tm
    nq, ncv, nvs, ng = D_ATTN, 2 * D_KV, 4 * D_KV, N_KV * LANES
    ncol = w_perm.shape[1]
    assert ncol == nq + ncv + nvs + d_ssm + ng and t % tm == 0
    data = lambda bi, i: (bi, jnp.maximum(i - 1, 0), 0)
    return pl.pallas_call(
        _inproj_body,
        grid=(b, nt + 1),
        in_specs=[
            pl.BlockSpec((None, tm, d), data),
            pl.BlockSpec((1, d), lambda bi, i: (0, 0)),
            pl.BlockSpec((d, ncol), lambda bi, i: (0, 0)),
        ],
        out_specs=[
            pl.BlockSpec((None, tm, nq), data),
            pl.BlockSpec((None, tm, nvs), lambda bi, i: (bi, i, 0)),
            pl.BlockSpec((None, tm, ncv), lambda bi, i: (bi, jnp.where(i == 0, nt, i - 1), 0)),
            pl.BlockSpec((None, tm, d_ssm), data),
            pl.BlockSpec((None, tm, ng), data),
        ],
        out_shape=[
            jax.ShapeDtypeStruct((b, t, nq), BF16),
            jax.ShapeDtypeStruct((b, t + tm, nvs), BF16),
            jax.ShapeDtypeStruct((b, t + tm, ncv), F32),
            jax.ShapeDtypeStruct((b, t, d_ssm), F32),
            jax.ShapeDtypeStruct((b, t, ng), F32),
        ],
        compiler_params=_cparams(("parallel", "arbitrary")),
        name="inproj",
    )(h, gain.reshape(1, d), w_perm)


def _compress_body(x_ref, pe_ref, w1_ref, b1_ref, w2_ref, o_ref):
    nb, nblk = o_ref.shape[0], o_ref.shape[1]
    acc = jnp.broadcast_to(b1_ref[...], (nb * nblk, b1_ref.shape[1])).astype(F32)
    for l in range(CMP_LEN):
        xl = x_ref[:, pl.ds(l, nblk, stride=CMP_STRIDE), :] + pe_ref[l:l + 1, :]
        acc = acc + jnp.dot(xl.reshape(nb * nblk, xl.shape[2]).astype(BF16), w1_ref[l], preferred_element_type=F32)
    hid = jax.nn.gelu(acc, approximate=True)
    out = jnp.dot(hid.astype(BF16), w2_ref[...], preferred_element_type=F32)
    o_ref[...] = out.reshape(o_ref.shape).astype(BF16)


def _compress(kcv, pe, w1, b1, w2, nblk, *, nb=4):
    b, tpad, _ = kcv.shape
    nb = min(nb, b)
    hid, dh = w1.shape[3], w2.shape[2]
    assert (nblk - 1) * CMP_STRIDE + CMP_LEN <= tpad
    return pl.pallas_call(
        _compress_body,
        grid=(2 * N_KV, b // nb),
        in_specs=[
            pl.BlockSpec((nb, tpad, dh), lambda s, i: (i, 0, s)),
            pl.BlockSpec((None, CMP_LEN, dh), lambda s, i: (s // N_KV, 0, 0)),
            pl.BlockSpec((None, CMP_LEN, dh, hid), lambda s, i: (s // N_KV, 0, 0, 0)),
            pl.BlockSpec((None, 1, hid), lambda s, i: (s // N_KV, 0, 0)),
            pl.BlockSpec((None, hid, dh), lambda s, i: (s // N_KV, 0, 0)),
        ],
        out_specs=pl.BlockSpec((None, None, nb, nblk, dh), lambda s, i: (s // N_KV, s % N_KV, i, 0, 0)),
        out_shape=jax.ShapeDtypeStruct((2, N_KV, b, nblk, dh), BF16),
        compiler_params=_cparams(("parallel", "parallel")),
        name="compress",
    )(kcv, pe, w1, b1, w2)


def _nt_dot(a, b):
    return lax.dot_general(a, b, (((1,), (1,)), ((), ())), preferred_element_type=F32)


def _nsa_body(q_ref, ks_ref, vs_ref, kw_ref, vw_ref, kcb_ref, vcb_ref, gate_ref, biasc_ref, dt_ref, wb_ref,
              ovt_ref, et_ref, o_ref, kse_ref, s_ref, m_ref, l_ref, acc_ref):
    i = pl.program_id(1)
    t0 = i * TQ
    t = et_ref.shape[0]
    ns = t // SEL_BLOCK
    nsel = min(N_SELECT, ns)
    groups = range(N_KV)
    gl = lambda g: slice(g * HEAD_DIM, (g + 1) * HEAD_DIM)

    @pl.when(i == 0)
    def _():
        for g in groups:
            kse_ref[g, :, 0:HEAD_DIM] = ks_ref[WINDOW:WINDOW + t, gl(g)]
            kse_ref[g, :, HEAD_DIM:] = et_ref[...]

    q = q_ref[...]
    w0 = pl.multiple_of(t0, LANES)
    span = wb_ref.shape[2]

    q4 = [jnp.concatenate([q[:, (g * HEADS_PER_KV + h) * HEAD_DIM:(g * HEADS_PER_KV + h + 1) * HEAD_DIM]
                           for h in range(HEADS_PER_KV)], axis=0) for g in groups]

    sc = [_nt_dot(q4[g], kcb_ref[g]) + biasc_ref[g] for g in groups]
    row_t = t0 + (lax.broadcasted_iota(jnp.int32, (ROWS4, LANES), 0) & (TQ - 1))
    col_c = lax.broadcasted_iota(jnp.int32, (ROWS4, LANES), 1)
    valid_c = (row_t - col_c * CMP_STRIDE - (CMP_LEN - 1) >= 0) & (col_c < LANES - 1)
    mc = [jnp.max(s, axis=-1, keepdims=True) for s in sc]
    pc = [jnp.where(valid_c, jnp.exp2(s - m), 0.0) for s, m in zip(sc, mc)]
    pc = [p / jnp.maximum(jnp.sum(p, axis=-1, keepdims=True), 1e-30) for p in pc]
    pcb = [p.astype(BF16) for p in pc]
    o_cmps = [jnp.dot(pcb[g], vcb_ref[g], preferred_element_type=F32) for g in groups]
    pimp = [_nt_dot(ovt_ref[...], p) for p in pcb]
    imp = [sum(p[0:ns, h * TQ:(h + 1) * TQ] for h in range(1, HEADS_PER_KV)) + p[0:ns, 0:TQ] for p in pimp]

    jrow = lax.broadcasted_iota(jnp.int32, (ns, TQ), 0)
    tpos = t0 + lax.broadcasted_iota(jnp.int32, (ns, TQ), 1)
    cur = lax.shift_right_logical(tpos, int(math.log2(SEL_BLOCK)))
    forced = (jrow == 0) | (jrow == cur) | (jrow == cur - 1)
    impm = [jnp.where(forced, 1e6, jnp.where(jrow * SEL_BLOCK <= tpos, x, -1e9)) for x in imp]
    nslab = ns // 8
    slabs = [[x[8 * v:8 * v + 8] for v in range(nslab)] for x in impm]
    cnts = [[jnp.zeros((8, TQ), F32) for _ in range(nslab)] for _ in groups]
    sub = lax.broadcasted_iota(jnp.int32, (8, TQ), 0)
    for jp in range(ns):
        v0, r0 = divmod(jp, 8)
        for g in groups:
            row = jnp.broadcast_to(slabs[g][v0][r0:r0 + 1, :], (8, TQ))
            for v in range(nslab):
                if v > v0:
                    beats = row >= slabs[g][v]
                elif v < v0:
                    beats = row > slabs[g][v]
                else:
                    beats = (row > slabs[g][v]) | ((row == slabs[g][v]) & (sub > r0))
                cnts[g][v] = cnts[g][v] + jnp.where(beats, 1.0, 0.0)
    q4s = []
    for g in groups:
        negt = [jnp.where(c < nsel, 0.0, NEG) for c in cnts[g]]
        negt = jnp.concatenate(negt + [jnp.zeros((LANES - ns, TQ), F32)], axis=0)
        neg = negt.T.astype(BF16)
        q4s.append(jnp.concatenate([q4[g], jnp.concatenate([neg] * HEADS_PER_KV, axis=0)], axis=1))

    nsub = SEL_CHUNK // LANES
    nchunks = ((i + 1) * QK - 1) // nsub + 1
    m_ref[...] = jnp.full(m_ref.shape, -3e38, F32)

    def score_chunk(ck, _):
        kb = pl.multiple_of(ck * SEL_CHUNK, SEL_CHUNK)
        s = [_nt_dot(q4s[g], kse_ref[g, pl.ds(kb, SEL_CHUNK), :]) for g in groups]
        for g in groups:
            mloc = None
            for j in range(nsub):
                sj = s[g][:, j * LANES:(j + 1) * LANES] + dt_ref[g, jnp.clip(ck * nsub + j - i * QK + 2, 0, QK + 2)]
                s_ref[g, ck, :, j * LANES:(j + 1) * LANES] = sj
                mloc = sj if mloc is None else jnp.maximum(mloc, sj)
            m_ref[g] = jnp.maximum(m_ref[g], mloc)
        return 0

    lax.fori_loop(0, nchunks, score_chunk, 0)
    for g in groups:
        m_ref[g] = jnp.broadcast_to(jnp.max(m_ref[g], axis=-1, keepdims=True), (ROWS4, LANES))
    l_ref[...] = jnp.zeros(l_ref.shape, F32)
    acc_ref[...] = jnp.zeros(acc_ref.shape, F32)

    kneg = jnp.where(t0 - WINDOW + lax.broadcasted_iota(jnp.int32, (1, span), 1) >= 0, 0.0, NEG)
    sw = [_nt_dot(q4[g], kw_ref[pl.ds(w0, span), gl(g)]) + wb_ref[g] + kneg for g in groups]
    mw = [jnp.max(s, axis=-1, keepdims=True) for s in sw]
    pw = [jnp.exp2(s - m) for s, m in zip(sw, mw)]
    lw = [jnp.sum(p, axis=-1, keepdims=True) for p in pw]
    o_wins = [jnp.dot(pw[g].astype(BF16), vw_ref[pl.ds(w0, span), gl(g)], preferred_element_type=F32) for g in groups]
    o_wins = [o / jnp.maximum(l, 1e-30) for o, l in zip(o_wins, lw)]

    def pv_chunk(ck, _):
        kb = pl.multiple_of(WINDOW + ck * SEL_CHUNK, SEL_CHUNK)
        ps = [[jnp.exp2(s_ref[g, ck, :, j * LANES:(j + 1) * LANES] - m_ref[g]) for j in range(nsub)] for g in groups]
        for g in groups:
            lsum = ps[g][0]
            for pj in ps[g][1:]:
                lsum = lsum + pj
            l_ref[g] += lsum
            p = jnp.concatenate([pj.astype(BF16) for pj in ps[g]], axis=1)
            acc_ref[g] += jnp.dot(p, vs_ref[pl.ds(kb, SEL_CHUNK), gl(g)], preferred_element_type=F32)
        return 0

    lax.fori_loop(0, nchunks, pv_chunk, 0)

    ls = [jnp.maximum(jnp.sum(l_ref[g], axis=-1, keepdims=True), 1e-30) for g in groups]
    o_sels = [acc_ref[g] / ls[g] for g in groups]
    gts = [gate_ref[:, g * LANES:(g + 1) * LANES] for g in groups]
    for h in range(HEADS_PER_KV):
        r = slice(h * TQ, (h + 1) * TQ)
        for g in groups:
            gt = gts[g]
            o_h = (gt[:, 3 * h:3 * h + 1] * o_cmps[g][r] + gt[:, 3 * h + 1:3 * h + 2] * o_sels[g][r]
                   + gt[:, 3 * h + 2:3 * h + 3] * o_wins[g][r])
            c0 = (g * HEADS_PER_KV + h) * HEAD_DIM
            o_ref[:, c0:c0 + HEAD_DIM] = o_h.astype(BF16)


def _nsa(q, kvs, cb, gates, biasc, dtiles, wb, ovt, et):
    b, t, _ = q.shape
    nt = t // TQ
    tp = kvs.shape[1]
    span = wb.shape[2]
    gw = HEADS_PER_KV * HEAD_DIM
    assert tp == t + WINDOW and span == TQ + WINDOW and t % SEL_CHUNK == 0 and (t // SEL_BLOCK) % 8 == 0

    def stream(j):
        return pl.BlockSpec((None, tp, N_KV * HEAD_DIM), lambda bi, i: (bi, 0, j))

    return pl.pallas_call(
        _nsa_body,
        grid=(b, nt),
        in_specs=[
            pl.BlockSpec((None, TQ, N_KV * gw), lambda bi, i: (bi, i, 0)),
            stream(0), stream(1), stream(2), stream(3),
            pl.BlockSpec((None, N_KV, None, LANES, HEAD_DIM), lambda bi, i: (0, 0, bi, 0, 0)),
            pl.BlockSpec((None, N_KV, None, LANES, HEAD_DIM), lambda bi, i: (1, 0, bi, 0, 0)),
            pl.BlockSpec((None, TQ, N_KV * LANES), lambda bi, i: (bi, i, 0)),
            pl.BlockSpec((N_KV, None, ROWS4, LANES), lambda bi, i: (0, i, 0, 0)),
            pl.BlockSpec((N_KV, QK + 3, ROWS4, LANES), lambda bi, i: (0, 0, 0, 0)),
            pl.BlockSpec((N_KV, ROWS4, span), lambda bi, i: (0, 0, 0)),
            pl.BlockSpec((LANES, LANES), lambda bi, i: (0, 0)),
            pl.BlockSpec((t, LANES), lambda bi, i: (0, 0)),
        ],
        out_specs=pl.BlockSpec((None, TQ, N_KV * gw), lambda bi, i: (bi, i, 0)),
        out_shape=jax.ShapeDtypeStruct((b, t, N_KV * gw), BF16),
        scratch_shapes=[
            pltpu.VMEM((N_KV, t, 2 * HEAD_DIM), BF16),
            pltpu.VMEM((N_KV, t // SEL_CHUNK, ROWS4, SEL_CHUNK), F32),
            pltpu.VMEM((N_KV, ROWS4, LANES), F32),
            pltpu.VMEM((N_KV, ROWS4, LANES), F32),
            pltpu.VMEM((N_KV, ROWS4, HEAD_DIM), F32),
        ],
        compiler_params=_cparams(("parallel", "arbitrary"), NSA_VMEM_LIMIT_BYTES),
        name="nsa",
    )(q, kvs, kvs, kvs, kvs, cb, cb, gates, biasc, dtiles, wb, ovt, et)


def _s5_body(u_ref, c0_ref, winc_ref, woutc_ref, rep_k_ref, rep_in_ref, rep_out_ref, m_in_ref, m_out_ref,
             al_ref, d_ref, y_ref, ucat_ref, inj_ref, xp_ref, grev_ref, win_ref, wout_ref):
    nb, t, lanes = u_ref.shape
    L = SSM_CHUNK
    nchunk = t // L
    rows = nb * nchunk
    sdim = al_ref.shape[1] // 2

    @pl.when(pl.program_id(1) == 0)
    def _():
        m_in, m_out = m_in_ref[...], m_out_ref[...]
        for half in range(2):
            for s in range(L):
                x = jnp.dot(winc_ref[half, s * lanes:(s + 1) * lanes, :], rep_in_ref[...], preferred_element_type=F32)
                win_ref[s * lanes:(s + 1) * lanes, half * sdim:(half + 1) * sdim] = (x * m_in).astype(BF16)
            for tt in range(L):
                x = jnp.dot(woutc_ref[half], rep_out_ref[:, tt * lanes:(tt + 1) * lanes], preferred_element_type=F32)
                wout_ref[half * sdim:(half + 1) * sdim, tt * lanes:(tt + 1) * lanes] = (x * m_out).astype(BF16)
        c0 = [(jnp.dot(c0_ref[half], rep_k_ref[...], preferred_element_type=F32) * m_out).astype(BF16)
              for half in range(2)]
        kx = jnp.dot(win_ref[...], jnp.concatenate(c0, axis=0), preferred_element_type=F32)
        zero = jnp.zeros((lanes, lanes), BF16)

        def kblk(tau):
            s = L - 1 - tau
            return kx[s * lanes:(s + 1) * lanes].astype(BF16) if 0 <= tau < L else zero

        for r in range(L + 1):
            grev_ref[r * lanes:(r + 1) * lanes, 0:lanes] = kblk(L - r - 1)
            grev_ref[r * lanes:(r + 1) * lanes, lanes:] = kblk(L - r)

    def u_at(s):
        return u_ref[:, pl.ds(s, nchunk, stride=L), :].reshape(rows, lanes)

    for s in range(L):
        ucat_ref[:, s * lanes:(s + 1) * lanes] = u_at(s).astype(BF16)
    inj = jnp.dot(ucat_ref[...], win_ref[...], preferred_element_type=F32)
    npl = sdim // lanes
    for k in range(2 * npl):
        inj_ref[k] = inj[:, k * lanes:(k + 1) * lanes]
    ar = [jnp.broadcast_to(al_ref[0:1, k * lanes:(k + 1) * lanes], (nb, lanes)) for k in range(npl)]
    ai = [jnp.broadcast_to(al_ref[0:1, sdim + k * lanes:sdim + (k + 1) * lanes], (nb, lanes)) for k in range(npl)]

    def step(c, carry):
        xr, xi = carry
        rsel = pl.ds(c, nb, stride=nchunk)
        nr, ni = [], []
        for k in range(npl):
            xp_ref[k, rsel, :] = xr[k]
            xp_ref[npl + k, rsel, :] = xi[k]
            nr.append(ar[k] * xr[k] - ai[k] * xi[k] + inj_ref[k, rsel, :])
            ni.append(ar[k] * xi[k] + ai[k] * xr[k] + inj_ref[npl + k, rsel, :])
        return tuple(nr), tuple(ni)

    z = tuple(jnp.zeros((nb, lanes), F32) for _ in range(npl))
    lax.fori_loop(0, nchunk, step, (z, z), unroll=4)
    xp = jnp.concatenate([xp_ref[k] for k in range(2 * npl)], axis=1).astype(BF16)
    d = d_ref[...]
    for tp in range(0, L, 2):
        res = jnp.dot(ucat_ref[:, 0:(tp + 2) * lanes], grev_ref[(L - 1 - tp) * lanes:(L + 1) * lanes, :],
                      preferred_element_type=F32)
        res = res + jnp.dot(xp, wout_ref[:, tp * lanes:(tp + 2) * lanes], preferred_element_type=F32)
        for k in range(2):
            y = res[:, k * lanes:(k + 1) * lanes] + d * u_at(tp + k)
            y_ref[:, pl.ds(tp + k, nchunk, stride=L), :] = y.reshape(nb, nchunk, lanes)


def _s5(u, c0c, winc, woutc, al, d, *, nb=4):
    b, t, dch = u.shape
    nb = min(nb, b)
    nblk = dch // LANES
    L = SSM_CHUNK
    rows = nb * (t // L)
    hch, p = c0c.shape[3], winc.shape[3]
    gpb = LANES // hch
    sdim = gpb * p
    kcat = L * LANES
    lane, col, st = jnp.arange(LANES), jnp.arange(kcat), jnp.arange(sdim)
    rep_k = (jnp.arange(hch)[:, None] == (lane % hch)[None, :]).astype(BF16)
    rep_in = (jnp.arange(p)[:, None] == (st % p)[None, :]).astype(BF16)
    rep_out = (jnp.arange(L * hch)[:, None] == (col // LANES * hch + col % hch)[None, :]).astype(BF16)
    m_in = ((lane // hch)[:, None] == (st // p)[None, :]).astype(F32)
    m_out = ((st // p)[:, None] == (lane // hch)[None, :]).astype(F32)
    const = lambda a: pl.BlockSpec(a.shape, lambda j, i: (0,) * a.ndim)
    return pl.pallas_call(
        _s5_body,
        grid=(nblk, b // nb),
        in_specs=[
            pl.BlockSpec((nb, t, LANES), lambda j, i: (i, 0, j)),
            pl.BlockSpec((None, 2, sdim, hch), lambda j, i: (j, 0, 0, 0)),
            pl.BlockSpec((None, 2, kcat, p), lambda j, i: (j, 0, 0, 0)),
            pl.BlockSpec((None, 2, sdim, L * hch), lambda j, i: (j, 0, 0, 0)),
            const(rep_k), const(rep_in), const(rep_out), const(m_in), const(m_out),
            pl.BlockSpec((None, 8, 2 * sdim), lambda j, i: (j, 0, 0)),
            pl.BlockSpec((1, LANES), lambda j, i: (0, j)),
        ],
        out_specs=pl.BlockSpec((nb, t, LANES), lambda j, i: (i, 0, j)),
        out_shape=jax.ShapeDtypeStruct((b, t, dch), F32),
        scratch_shapes=[pltpu.VMEM((rows, kcat), BF16), pltpu.VMEM((2 * sdim // LANES, rows, LANES), F32),
                        pltpu.VMEM((2 * sdim // LANES, rows, LANES), F32),
                        pltpu.VMEM(((L + 1) * LANES, 2 * LANES), BF16), pltpu.VMEM((kcat, 2 * sdim), BF16),
                        pltpu.VMEM((2 * sdim, kcat), BF16)],
        compiler_params=_cparams(("parallel", "arbitrary")),
        name="s5",
    )(u, c0c, winc, woutc, rep_k, rep_in, rep_out, m_in, m_out, al, d.reshape(1, dch))


def _glu_out_body(h_ref, a_ref, y_ref, wg_ref, bg_ref, woa_ref, wos_ref, o_ref):
    hg = jax.nn.gelu(y_ref[...], approximate=True)
    z = jnp.dot(hg.astype(BF16), wg_ref[...], preferred_element_type=F32) + bg_ref[...]
    s = hg * jax.nn.sigmoid(z)
    mix = jnp.dot(a_ref[...], woa_ref[...], preferred_element_type=F32)
    mix = mix + jnp.dot(s.astype(BF16), wos_ref[...], preferred_element_type=F32)
    o_ref[...] = h_ref[...] + mix


def _glu_out(h, a, y, wg, bg, woa, wos, *, tm=512):
    n, dm = h.shape
    da, ds = a.shape[1], y.shape[1]
    row = lambda w: pl.BlockSpec((tm, w), lambda i: (i, 0))
    full = lambda r, c: pl.BlockSpec((r, c), lambda i: (0, 0))
    return pl.pallas_call(
        _glu_out_body,
        grid=(n // tm,),
        in_specs=[row(dm), row(da), row(ds), full(ds, ds), full(1, ds), full(da, dm), full(ds, dm)],
        out_specs=row(dm),
        out_shape=jax.ShapeDtypeStruct((n, dm), F32),
        compiler_params=_cparams(("parallel",)),
        name="glu_out",
    )(h, a, y, wg, bg.reshape(1, ds), woa, wos)


def _t5_bucket(dist):
    n = jnp.maximum(dist, 0)
    max_exact = N_BUCKETS // 2
    nf = jnp.maximum(n, 1).astype(F32)
    large = max_exact + (jnp.log(nf / max_exact) / math.log(MAX_DISTANCE / max_exact)
                         * (N_BUCKETS - max_exact)).astype(jnp.int32)
    large = jnp.minimum(large, N_BUCKETS - 1)
    return jnp.where(n < max_exact, n, large)


def _t5_body(first_ref, table_ref, bc_ref, dt_ref, wb_ref, *, nc):
    g = pl.program_id(0)
    nt = bc_ref.shape[0]
    span = wb_ref.shape[1]
    a = lax.broadcasted_iota(jnp.int32, (TQ, LANES), 0)
    c = lax.broadcasted_iota(jnp.int32, (TQ, LANES), 1)
    heads = [g * HEADS_PER_KV + h for h in range(HEADS_PER_KV)]

    def lookup(dist, valid, store):
        bias = [jnp.full((TQ, LANES), table_ref[0, hd] * LOG2E, F32) for hd in heads]
        for k in range(1, N_BUCKETS):
            ge = dist >= first_ref[k]
            bias = [jnp.where(ge, table_ref[k, hd] * LOG2E, bv) for hd, bv in zip(heads, bias)]
        for h, bv in enumerate(bias):
            store(slice(h * TQ, (h + 1) * TQ), bv if valid is None else jnp.where(valid, bv, NEG))

    def put_far(rows, tile):
        dt_ref[0, rows, :] = tile

    lookup(jnp.full((TQ, LANES), 2 * MAX_DISTANCE, jnp.int32), None, put_far)
    for r in range(-1, QK):
        def put_near(rows, tile, r=r):
            dt_ref[r + 2, rows, :] = tile

        dist = a - r * LANES - c
        lookup(dist, dist >= 0, put_near)
    dt_ref[QK + 2] = jnp.full((ROWS4, LANES), NEG, F32)
    for j in range(span // LANES):
        def put_wb(rows, tile, j=j):
            wb_ref[rows, j * LANES:(j + 1) * LANES] = tile

        dist = a + WINDOW - (c + j * LANES)
        lookup(dist, (dist >= 0) & (dist < WINDOW), put_wb)

    def cmp_tile(i, _):
        def put_bc(rows, tile):
            bc_ref[i, rows, :] = tile

        dist = i * TQ + a - (c * CMP_STRIDE + CMP_LEN - 1)
        lookup(dist, (dist >= 0) & (c < nc), put_bc)
        return 0

    lax.fori_loop(0, nt, cmp_tile, 0)


def _bias_tables(rel_bias, t):
    nt = t // TQ
    span = TQ + WINDOW
    nc = (t - CMP_LEN) // CMP_STRIDE + 1
    buckets = _t5_bucket(jnp.arange(2 * MAX_DISTANCE))
    first = jnp.sum((buckets[None, :] < jnp.arange(N_BUCKETS)[:, None]).astype(jnp.int32), axis=1)
    smem = pl.BlockSpec(memory_space=pltpu.SMEM)
    return pl.pallas_call(
        functools.partial(_t5_body, nc=nc),
        grid=(N_KV,),
        in_specs=[smem, smem],
        out_specs=[
            pl.BlockSpec((None, nt, ROWS4, LANES), lambda g: (g, 0, 0, 0)),
            pl.BlockSpec((None, QK + 3, ROWS4, LANES), lambda g: (g, 0, 0, 0)),
            pl.BlockSpec((None, ROWS4, span), lambda g: (g, 0, 0)),
        ],
        out_shape=[
            jax.ShapeDtypeStruct((N_KV, nt, ROWS4, LANES), F32),
            jax.ShapeDtypeStruct((N_KV, QK + 3, ROWS4, LANES), F32),
            jax.ShapeDtypeStruct((N_KV, ROWS4, span), F32),
        ],
        compiler_params=_cparams(("parallel",)),
        name="t5_tables",
    )(first, rel_bias.astype(F32))


def _sel_tables(t):
    ns = t // SEL_BLOCK
    nc_pad = LANES
    c_start = jnp.arange(nc_pad) * CMP_STRIDE
    j_start = jnp.arange(LANES) * SEL_BLOCK
    ov = jnp.clip(jnp.minimum(c_start[:, None] + CMP_LEN, j_start[None, :] + SEL_BLOCK)
                  - jnp.maximum(c_start[:, None], j_start[None, :]), 0, None).astype(F32) / CMP_LEN
    ov = jnp.where(jnp.arange(LANES)[None, :] < ns, ov, 0.0)
    et = jnp.arange(t)[:, None] // SEL_BLOCK == jnp.arange(LANES)[None, :]
    return ov.T.astype(BF16), et.astype(BF16)


def _s5_tables(lam_re, lam_im, log_step, b_re, b_im, c_re, c_im):
    ng, p = lam_re.shape
    hch = b_re.shape[2]
    L = SSM_CHUNK
    step = jnp.exp(log_step.astype(F32))[:, None]
    lre, lim = lam_re.astype(F32), lam_im.astype(F32)
    mag = jnp.exp(lre * step)
    ab_re, ab_im = mag * jnp.cos(lim * step), mag * jnp.sin(lim * step)
    nr, ni = ab_re - 1.0, ab_im
    den = lre * lre + lim * lim
    f_re, f_im = (nr * lre + ni * lim) / den, (ni * lre - nr * lim) / den
    br, bim = b_re.astype(F32), b_im.astype(F32)
    bb_re = f_re[..., None] * br - f_im[..., None] * bim
    bb_im = f_re[..., None] * bim + f_im[..., None] * br
    cr, ci = c_re.astype(F32), c_im.astype(F32)
    pr, pi = [jnp.ones_like(ab_re)], [jnp.zeros_like(ab_re)]
    for _ in range(L):
        pr, pi = pr + [pr[-1] * ab_re - pi[-1] * ab_im], pi + [pr[-1] * ab_im + pi[-1] * ab_re]
    pw_re, pw_im = jnp.stack(pr, 0), jnp.stack(pi, 0)
    cp_re = cr[None] * pw_re[:, :, None, :] - ci[None] * pw_im[:, :, None, :]
    cp_im = -(cr[None] * pw_im[:, :, None, :] + ci[None] * pw_re[:, :, None, :])
    gpb = LANES // hch
    nblk = ng // gpb
    wr = pw_re[L - 1 - jnp.arange(L)]
    wi = pw_im[L - 1 - jnp.arange(L)]
    win_re = (wr[..., None] * bb_re[None] - wi[..., None] * bb_im[None])
    win_im = (wr[..., None] * bb_im[None] + wi[..., None] * bb_re[None])
    inj_op = lambda m: m.reshape(L, nblk, gpb, p, hch).transpose(1, 0, 2, 4, 3).reshape(nblk, L * LANES, p)
    winc = jnp.stack([inj_op(win_re), inj_op(win_im)], axis=1)
    read_op = lambda m: m.reshape(L, nblk, gpb, hch, p).transpose(1, 2, 4, 0, 3).reshape(nblk, gpb * p, L * hch)
    woutc = jnp.stack([read_op(cp_re[1:]), read_op(cp_im[1:])], axis=1)
    read0 = lambda m: m.reshape(nblk, gpb, hch, p).transpose(0, 1, 3, 2).reshape(nblk, gpb * p, hch)
    c0c = jnp.stack([read0(cp_re[0]), read0(cp_im[0])], axis=1)
    al = jnp.concatenate([pw_re[L].reshape(nblk, gpb * p), pw_im[L].reshape(nblk, gpb * p)], axis=1)
    al = jnp.broadcast_to(al[:, None, :], (nblk, 8, 2 * gpb * p))
    return c0c.astype(BF16), winc.astype(BF16), woutc.astype(BF16), al


def _mixers(h1, mix_norm, w_in, cmp_k, cmp_v, rel_bias, ssm, ssm_d, glu_w, glu_b, w_out, b, t):
    n, d = h1.shape
    d_ssm = glu_w.shape[0]
    o_g, o_u = D_ATTN + 6 * D_KV, D_ATTN + 6 * D_KV + 3 * N_HEADS
    gcols = 3 * HEADS_PER_KV
    wg = [jnp.pad(w_in[:, o_g + g * gcols:o_g + (g + 1) * gcols], ((0, 0), (0, LANES - gcols))) for g in range(N_KV)]
    w_perm = jnp.concatenate([w_in[:, :o_g], w_in[:, o_u:]] + wg, axis=1).astype(BF16)
    q, kvs, kcv, u, gates = _inproj(h1.reshape(b, t, d), mix_norm, w_perm, d_ssm)

    nblk = t // CMP_STRIDE
    assert nblk == LANES, "compressed-block axis is laid out on one 128-lane tile"
    pe = jnp.stack([cmp_k[0], cmp_v[0]], 0).astype(F32)
    w1 = jnp.stack([cmp_k[1], cmp_v[1]], 0).astype(BF16).reshape(2, CMP_LEN, HEAD_DIM, -1)
    b1 = jnp.stack([cmp_k[2].reshape(1, -1), cmp_v[2].reshape(1, -1)], 0).astype(F32)
    w2 = jnp.stack([cmp_k[3], cmp_v[3]], 0).astype(BF16)
    cb = _compress(kcv, pe, w1, b1, w2, nblk)

    biasc, dtiles, wb = _bias_tables(rel_bias, t)
    ovt, et = _sel_tables(t)
    a = _nsa(q, kvs, cb, gates, biasc, dtiles, wb, ovt, et)

    c0c, winc, woutc, al = _s5_tables(*ssm)
    y = _s5(u, c0c, winc, woutc, al, ssm_d)

    wo = w_out.astype(BF16)
    return _glu_out(h1, a.reshape(n, D_ATTN), y.reshape(n, d_ssm), glu_w.astype(BF16), glu_b, wo[:D_ATTN], wo[D_ATTN:])


def kernel(x, ffn1_norm, ffn1_w1, ffn1_w3, ffn1_w2, mix_norm, w_in, cmp_pe_k, cmp_w1_k, cmp_b1_k, cmp_w2_k,
           cmp_pe_v, cmp_w1_v, cmp_b1_v, cmp_w2_v, rel_bias, ssm_lam_re, ssm_lam_im, ssm_log_step, ssm_b_re,
           ssm_b_im, ssm_c_re, ssm_c_im, ssm_d, glu_w, glu_b, w_out, ffn2_norm, ffn2_w1, ffn2_w3, ffn2_w2,
           final_norm):
    b, t, d = x.shape
    depth = ffn1_w1.shape[0]
    h = x.reshape(b * t, d)
    for l in range(depth):
        last = l == depth - 1
        h = _ffn(h, ffn1_norm[l], ffn1_w1[l], ffn1_w3[l], ffn1_w2[l])
        ssm = (ssm_lam_re[l], ssm_lam_im[l], ssm_log_step[l], ssm_b_re[l], ssm_b_im[l], ssm_c_re[l], ssm_c_im[l])
        h = _mixers(h, mix_norm[l], w_in[l],
                    (cmp_pe_k[l], cmp_w1_k[l], cmp_b1_k[l], cmp_w2_k[l]),
                    (cmp_pe_v[l], cmp_w1_v[l], cmp_b1_v[l], cmp_w2_v[l]),
                    rel_bias, ssm, ssm_d[l], glu_w[l], glu_b[l], w_out[l], b, t)
        h = _ffn(h, ffn2_norm[l], ffn2_w1[l], ffn2_w3[l], ffn2_w2[l], final_gain=final_norm if last else None)
    return h.reshape(b, t, d)
```

```python
import functools
import math

import jax
import jax.numpy as jnp
from jax import lax
from jax.experimental import pallas as pl
from jax.experimental.pallas import tpu as pltpu

F32 = jnp.float32
BF16 = jnp.bfloat16

HEAD_DIM = 128
N_KV = 2
HEADS_PER_KV = 4
N_HEADS = N_KV * HEADS_PER_KV
D_ATTN = N_HEADS * HEAD_DIM
D_KV = N_KV * HEAD_DIM
CMP_LEN = 32
CMP_STRIDE = 16
SEL_BLOCK = 64
N_SELECT = 16
WINDOW = 512
N_BUCKETS = 32
MAX_DISTANCE = 128
SSM_GROUP = 16
SSM_STATE = 64
EPS = 1e-6
NEG = -1e30
LOG2E = math.log2(math.e)

LANES = 128
VMEM_LIMIT_BYTES = 56 * 1024 * 1024
NSA_VMEM_LIMIT_BYTES = 60 * 1024 * 1024
FFN_VMEM_LIMIT_BYTES = 60 * 1024 * 1024
TQ = 256
QK = TQ // LANES
ROWS4 = HEADS_PER_KV * TQ
SEL_CHUNK = 512
SSM_CHUNK = 16


def _cparams(sem, vmem_limit_bytes=VMEM_LIMIT_BYTES):
    return pltpu.CompilerParams(dimension_semantics=sem, vmem_limit_bytes=vmem_limit_bytes)


def _ffn_body(x_ref, g_ref, w1_ref, w3_ref, w2_ref, *rest, final):
    if final:
        fg_ref, o_ref, xn_ref = rest
    else:
        o_ref, xn_ref = rest
    j = pl.program_id(1)
    nj = pl.num_programs(1)

    def contribution(xn):
        a = jnp.dot(xn, w1_ref[...], preferred_element_type=F32)
        b = jnp.dot(xn, w3_ref[...], preferred_element_type=F32)
        gated = (a * jax.nn.sigmoid(a)) * b
        return jnp.dot(gated.astype(BF16), w2_ref[...], preferred_element_type=F32)

    @pl.when(j == 0)
    def _():
        x = x_ref[...]
        ms = jnp.mean(x * x, axis=-1, keepdims=True)
        xn = (x * lax.rsqrt(ms + EPS) * g_ref[...]).astype(BF16)
        xn_ref[...] = xn
        o_ref[...] = contribution(xn)

    @pl.when((j > 0) & (j < nj - 1))
    def _():
        o_ref[...] += contribution(xn_ref[...])

    @pl.when(j == nj - 1)
    def _():
        h = x_ref[...] + 0.5 * (o_ref[...] + contribution(xn_ref[...]))
        if final:
            ms = jnp.mean(h * h, axis=-1, keepdims=True)
            h = h * lax.rsqrt(ms + EPS) * fg_ref[...]
        o_ref[...] = h


def _ffn(x, gain, w1, w3, w2, final_gain=None, *, tm=1024, tf=256):
    n, d = x.shape
    dff = w1.shape[1]
    nj = dff // tf
    assert nj >= 2 and nj * tf == dff
    final = final_gain is not None
    in_specs = [
        pl.BlockSpec((tm, d), lambda i, j: (i, 0)),
        pl.BlockSpec((1, d), lambda i, j: (0, 0)),
        pl.BlockSpec((d, tf), lambda i, j: (0, j)),
        pl.BlockSpec((d, tf), lambda i, j: (0, j)),
        pl.BlockSpec((tf, d), lambda i, j: (j, 0)),
    ]
    args = [x, gain.reshape(1, d), w1.astype(BF16), w3.astype(BF16), w2.astype(BF16)]
    if final:
        in_specs.append(pl.BlockSpec((1, d), lambda i, j: (0, 0)))
        args.append(final_gain.reshape(1, d))
    return pl.pallas_call(
        functools.partial(_ffn_body, final=final),
        grid=(n // tm, nj),
        in_specs=in_specs,
        out_specs=pl.BlockSpec((tm, d), lambda i, j: (i, 0)),
        out_shape=jax.ShapeDtypeStruct((n, d), F32),
        scratch_shapes=[pltpu.VMEM((tm, d), BF16)],
        compiler_params=_cparams(("parallel", "arbitrary"), FFN_VMEM_LIMIT_BYTES),
        name="ffn_final" if final else "ffn",
    )(*args)


def _inproj_body(x_ref, g_ref, w_ref, q_ref, kvs_ref, kcv_ref, u_ref, gate_ref):
    i = pl.program_id(1)

    @pl.when(i == 0)
    def _():
        kvs_ref[...] = jnp.zeros(kvs_ref.shape, kvs_ref.dtype)
        kcv_ref[...] = jnp.zeros(kcv_ref.shape, kcv_ref.dtype)

    @pl.when(i > 0)
    def _():
        x = x_ref[...]
        ms = jnp.mean(x * x, axis=-1, keepdims=True)
        xn = (x * lax.rsqrt(ms + EPS) * g_ref[...]).astype(BF16)
        c0 = 0
        c1 = q_ref.shape[1]
        q = jnp.dot(xn, w_ref[:, c0:c1], preferred_element_type=F32)
        q_ref[...] = (q * (HEAD_DIM ** -0.5 * LOG2E)).astype(BF16)
        c0, c1 = c1, c1 + kcv_ref.shape[1]
        kcv_ref[...] = jnp.dot(xn, w_ref[:, c0:c1], preferred_element_type=F32)
        c0, c1 = c1, c1 + kvs_ref.shape[1]
        kvs_ref[...] = jnp.dot(xn, w_ref[:, c0:c1], preferred_element_type=F32).astype(BF16)
        c0, c1 = c1, c1 + u_ref.shape[1]
        u_ref[...] = jnp.dot(xn, w_ref[:, c0:c1], preferred_element_type=F32)
        gate_ref[...] = jax.nn.sigmoid(jnp.dot(xn, w_ref[:, c1:], preferred_element_type=F32))


def _inproj(h, gain, w_perm, d_ssm):
    b, t, d = h.shape
    tm = WINDOW
    nt = t // tm
    nq, ncv, nvs, ng = D_ATTN, 2 * D_KV, 4 * D_KV, N_KV * LANES
    ncol = w_perm.shape[1]
    assert ncol == nq + ncv + nvs + d_ssm + ng and t % tm == 0
    data = lambda bi, i: (bi, jnp.maximum(i - 1, 0), 0)
    return pl.pallas_call(
        _inproj_body,
        grid=(b, nt + 1),
        in_specs=[
            pl.BlockSpec((None, tm, d), data),
            pl.BlockSpec((1, d), lambda bi, i: (0, 0)),
            pl.BlockSpec((d, ncol), lambda bi, i: (0, 0)),
        ],
        out_specs=[
            pl.BlockSpec((None, tm, nq), data),
            pl.BlockSpec((None, tm, nvs), lambda bi, i: (bi, i, 0)),
            pl.BlockSpec((None, tm, ncv), lambda bi, i: (bi, jnp.where(i == 0, nt, i - 1), 0)),
            pl.BlockSpec((None, tm, d_ssm), data),
            pl.BlockSpec((None, tm, ng), data),
        ],
        out_shape=[
            jax.ShapeDtypeStruct((b, t, nq), BF16),
            jax.ShapeDtypeStruct((b, t + tm, nvs), BF16),
            jax.ShapeDtypeStruct((b, t + tm, ncv), F32),
            jax.ShapeDtypeStruct((b, t, d_ssm), F32),
            jax.ShapeDtypeStruct((b, t, ng), F32),
        ],
        compiler_params=_cparams(("parallel", "arbitrary")),
        name="inproj",
    )(h, gain.reshape(1, d), w_perm)


def _compress_body(x_ref, pe_ref, w1_ref, b1_ref, w2_ref, o_ref):
    nb, nblk = o_ref.shape[0], o_ref.shape[1]
    acc = jnp.broadcast_to(b1_ref[...], (nb * nblk, b1_ref.shape[1])).astype(F32)
    for l in range(CMP_LEN):
        xl = x_ref[:, pl.ds(l, nblk, stride=CMP_STRIDE), :] + pe_ref[l:l + 1, :]
        acc = acc + jnp.dot(xl.reshape(nb * nblk, xl.shape[2]).astype(BF16), w1_ref[l], preferred_element_type=F32)
    hid = jax.nn.gelu(acc, approximate=True)
    out = jnp.dot(hid.astype(BF16), w2_ref[...], preferred_element_type=F32)
    o_ref[...] = out.reshape(o_ref.shape).astype(BF16)


def _compress(kcv, pe, w1, b1, w2, nblk, *, nb=4):
    b, tpad, _ = kcv.shape
    nb = min(nb, b)
    hid, dh = w1.shape[3], w2.shape[2]
    assert (nblk - 1) * CMP_STRIDE + CMP_LEN <= tpad
    return pl.pallas_call(
        _compress_body,
        grid=(2 * N_KV, b // nb),
        in_specs=[
            pl.BlockSpec((nb, tpad, dh), lambda s, i: (i, 0, s)),
            pl.BlockSpec((None, CMP_LEN, dh), lambda s, i: (s // N_KV, 0, 0)),
            pl.BlockSpec((None, CMP_LEN, dh, hid), lambda s, i: (s // N_KV, 0, 0, 0)),
            pl.BlockSpec((None, 1, hid), lambda s, i: (s // N_KV, 0, 0)),
            pl.BlockSpec((None, hid, dh), lambda s, i: (s // N_KV, 0, 0)),
        ],
        out_specs=pl.BlockSpec((None, None, nb, nblk, dh), lambda s, i: (s // N_KV, s % N_KV, i, 0, 0)),
        out_shape=jax.ShapeDtypeStruct((2, N_KV, b, nblk, dh), BF16),
        compiler_params=_cparams(("parallel", "parallel")),
        name="compress",
    )(kcv, pe, w1, b1, w2)


def _nt_dot(a, b):
    return lax.dot_general(a, b, (((1,), (1,)), ((), ())), preferred_element_type=F32)


def _nsa_body(q_ref, ks_ref, vs_ref, kw_ref, vw_ref, kcb_ref, vcb_ref, gate_ref, biasc_ref, dt_ref, wb_ref,
              ovt_ref, et_ref, o_ref, kse_ref, s_ref, m_ref, l_ref, acc_ref):
    i = pl.program_id(1)
    t0 = i * TQ
    t = et_ref.shape[0]
    ns = t // SEL_BLOCK
    nsel = min(N_SELECT, ns)
    groups = range(N_KV)
    gl = lambda g: slice(g * HEAD_DIM, (g + 1) * HEAD_DIM)

    @pl.when(i == 0)
    def _():
        for g in groups:
            kse_ref[g, :, 0:HEAD_DIM] = ks_ref[WINDOW:WINDOW + t, gl(g)]
            kse_ref[g, :, HEAD_DIM:] = et_ref[...]

    q = q_ref[...]
    w0 = pl.multiple_of(t0, LANES)
    span = wb_ref.shape[2]

    q4 = [jnp.concatenate([q[:, (g * HEADS_PER_KV + h) * HEAD_DIM:(g * HEADS_PER_KV + h + 1) * HEAD_DIM]
                           for h in range(HEADS_PER_KV)], axis=0) for g in groups]

    sc = [_nt_dot(q4[g], kcb_ref[g]) + biasc_ref[g] for g in groups]
    row_t = t0 + (lax.broadcasted_iota(jnp.int32, (ROWS4, LANES), 0) & (TQ - 1))
    col_c = lax.broadcasted_iota(jnp.int32, (ROWS4, LANES), 1)
    valid_c = (row_t - col_c * CMP_STRIDE - (CMP_LEN - 1) >= 0) & (col_c < LANES - 1)
    mc = [jnp.max(s, axis=-1, keepdims=True) for s in sc]
    pc = [jnp.where(valid_c, jnp.exp2(s - m), 0.0) for s, m in zip(sc, mc)]
    pc = [p / jnp.maximum(jnp.sum(p, axis=-1, keepdims=True), 1e-30) for p in pc]
    pcb = [p.astype(BF16) for p in pc]
    o_cmps = [jnp.dot(pcb[g], vcb_ref[g], preferred_element_type=F32) for g in groups]
    pimp = [_nt_dot(ovt_ref[...], p) for p in pcb]
    imp = [sum(p[0:ns, h * TQ:(h + 1) * TQ] for h in range(1, HEADS_PER_KV)) + p[0:ns, 0:TQ] for p in pimp]

    jrow = lax.broadcasted_iota(jnp.int32, (ns, TQ), 0)
    tpos = t0 + lax.broadcasted_iota(jnp.int32, (ns, TQ), 1)
    cur = lax.shift_right_logical(tpos, int(math.log2(SEL_BLOCK)))
    forced = (jrow == 0) | (jrow == cur) | (jrow == cur - 1)
    impm = [jnp.where(forced, 1e6, jnp.where(jrow * SEL_BLOCK <= tpos, x, -1e9)) for x in imp]
    nslab = ns // 8
    slabs = [[x[8 * v:8 * v + 8] for v in range(nslab)] for x in impm]
    cnts = [[jnp.zeros((8, TQ), F32) for _ in range(nslab)] for _ in groups]
    sub = lax.broadcasted_iota(jnp.int32, (8, TQ), 0)
    for jp in range(ns):
        v0, r0 = divmod(jp, 8)
        for g in groups:
            row = jnp.broadcast_to(slabs[g][v0][r0:r0 + 1, :], (8, TQ))
            for v in range(nslab):
                if v > v0:
                    beats = row >= slabs[g][v]
                elif v < v0:
                    beats = row > slabs[g][v]
                else:
                    beats = (row > slabs[g][v]) | ((row == slabs[g][v]) & (sub > r0))
                cnts[g][v] = cnts[g][v] + jnp.where(beats, 1.0, 0.0)
    q4s = []
    for g in groups:
        negt = [jnp.where(c < nsel, 0.0, NEG) for c in cnts[g]]
        negt = jnp.concatenate(negt + [jnp.zeros((LANES - ns, TQ), F32)], axis=0)
        neg = negt.T.astype(BF16)
        q4s.append(jnp.concatenate([q4[g], jnp.concatenate([neg] * HEADS_PER_KV, axis=0)], axis=1))

    nsub = SEL_CHUNK // LANES
    nchunks = ((i + 1) * QK - 1) // nsub + 1
    m_ref[...] = jnp.full(m_ref.shape, -3e38, F32)

    def score_chunk(ck, _):
        kb = pl.multiple_of(ck * SEL_CHUNK, SEL_CHUNK)
        s = [_nt_dot(q4s[g], kse_ref[g, pl.ds(kb, SEL_CHUNK), :]) for g in groups]
        for g in groups:
            mloc = None
            for j in range(nsub):
                sj = s[g][:, j * LANES:(j + 1) * LANES] + dt_ref[g, jnp.clip(ck * nsub + j - i * QK + 2, 0, QK + 2)]
                s_ref[g, ck, :, j * LANES:(j + 1) * LANES] = sj
                mloc = sj if mloc is None else jnp.maximum(mloc, sj)
            m_ref[g] = jnp.maximum(m_ref[g], mloc)
        return 0

    lax.fori_loop(0, nchunks, score_chunk, 0)
    for g in groups:
        m_ref[g] = jnp.broadcast_to(jnp.max(m_ref[g], axis=-1, keepdims=True), (ROWS4, LANES))
    l_ref[...] = jnp.zeros(l_ref.shape, F32)
    acc_ref[...] = jnp.zeros(acc_ref.shape, F32)

    kneg = jnp.where(t0 - WINDOW + lax.broadcasted_iota(jnp.int32, (1, span), 1) >= 0, 0.0, NEG)
    sw = [_nt_dot(q4[g], kw_ref[pl.ds(w0, span), gl(g)]) + wb_ref[g] + kneg for g in groups]
    mw = [jnp.max(s, axis=-1, keepdims=True) for s in sw]
    pw = [jnp.exp2(s - m) for s, m in zip(sw, mw)]
    lw = [jnp.sum(p, axis=-1, keepdims=True) for p in pw]
    o_wins = [jnp.dot(pw[g].astype(BF16), vw_ref[pl.ds(w0, span), gl(g)], preferred_element_type=F32) for g in groups]
    o_wins = [o / jnp.maximum(l, 1e-30) for o, l in zip(o_wins, lw)]

    def pv_chunk(ck, _):
        kb = pl.multiple_of(WINDOW + ck * SEL_CHUNK, SEL_CHUNK)
        ps = [[jnp.exp2(s_ref[g, ck, :, j * LANES:(j + 1) * LANES] - m_ref[g]) for j in range(nsub)] for g in groups]
        for g in groups:
            lsum = ps[g][0]
            for pj in ps[g][1:]:
                lsum = lsum + pj
            l_ref[g] += lsum
            p = jnp.concatenate([pj.astype(BF16) for pj in ps[g]], axis=1)
            acc_ref[g] += jnp.dot(p, vs_ref[pl.ds(kb, SEL_CHUNK), gl(g)], preferred_element_type=F32)
        return 0

    lax.fori_loop(0, nchunks, pv_chunk, 0)

    ls = [jnp.maximum(jnp.sum(l_ref[g], axis=-1, keepdims=True), 1e-30) for g in groups]
    o_sels = [acc_ref[g] / ls[g] for g in groups]
    gts = [gate_ref[:, g * LANES:(g + 1) * LANES] for g in groups]
    for h in range(HEADS_PER_KV):
        r = slice(h * TQ, (h + 1) * TQ)
        for g in groups:
            gt = gts[g]
            o_h = (gt[:, 3 * h:3 * h + 1] * o_cmps[g][r] + gt[:, 3 * h + 1:3 * h + 2] * o_sels[g][r]
                   + gt[:, 3 * h + 2:3 * h + 3] * o_wins[g][r])
            c0 = (g * HEADS_PER_KV + h) * HEAD_DIM
            o_ref[:, c0:c0 + HEAD_DIM] = o_h.astype(BF16)


def _nsa(q, kvs, cb, gates, biasc, dtiles, wb, ovt, et):
    b, t, _ = q.shape
    nt = t // TQ
    tp = kvs.shape[1]
    span = wb.shape[2]
    gw = HEADS_PER_KV * HEAD_DIM
    assert tp == t + WINDOW and span == TQ + WINDOW and t % SEL_CHUNK == 0 and (t // SEL_BLOCK) % 8 == 0

    def stream(j):
        return pl.BlockSpec((None, tp, N_KV * HEAD_DIM), lambda bi, i: (bi, 0, j))

    return pl.pallas_call(
        _nsa_body,
        grid=(b, nt),
        in_specs=[
            pl.BlockSpec((None, TQ, N_KV * gw), lambda bi, i: (bi, i, 0)),
            stream(0), stream(1), stream(2), stream(3),
            pl.BlockSpec((None, N_KV, None, LANES, HEAD_DIM), lambda bi, i: (0, 0, bi, 0, 0)),
            pl.BlockSpec((None, N_KV, None, LANES, HEAD_DIM), lambda bi, i: (1, 0, bi, 0, 0)),
            pl.BlockSpec((None, TQ, N_KV * LANES), lambda bi, i: (bi, i, 0)),
            pl.BlockSpec((N_KV, None, ROWS4, LANES), lambda bi, i: (0, i, 0, 0)),
            pl.BlockSpec((N_KV, QK + 3, ROWS4, LANES), lambda bi, i: (0, 0, 0, 0)),
            pl.BlockSpec((N_KV, ROWS4, span), lambda bi, i: (0, 0, 0)),
            pl.BlockSpec((LANES, LANES), lambda bi, i: (0, 0)),
            pl.BlockSpec((t, LANES), lambda bi, i: (0, 0)),
        ],
        out_specs=pl.BlockSpec((None, TQ, N_KV * gw), lambda bi, i: (bi, i, 0)),
        out_shape=jax.ShapeDtypeStruct((b, t, N_KV * gw), BF16),
        scratch_shapes=[
            pltpu.VMEM((N_KV, t, 2 * HEAD_DIM), BF16),
            pltpu.VMEM((N_KV, t // SEL_CHUNK, ROWS4, SEL_CHUNK), F32),
            pltpu.VMEM((N_KV, ROWS4, LANES), F32),
            pltpu.VMEM((N_KV, ROWS4, LANES), F32),
            pltpu.VMEM((N_KV, ROWS4, HEAD_DIM), F32),
        ],
        compiler_params=_cparams(("parallel", "arbitrary"), NSA_VMEM_LIMIT_BYTES),
        name="nsa",
    )(q, kvs, kvs, kvs, kvs, cb, cb, gates, biasc, dtiles, wb, ovt, et)


def _s5_body(u_ref, c0_ref, winc_ref, woutc_ref, rep_k_ref, rep_in_ref, rep_out_ref, m_in_ref, m_out_ref,
             al_ref, d_ref, y_ref, ucat_ref, inj_ref, xp_ref, grev_ref, win_ref, wout_ref):
    nb, t, lanes = u_ref.shape
    L = SSM_CHUNK
    nchunk = t // L
    rows = nb * nchunk
    sdim = al_ref.shape[1] // 2

    @pl.when(pl.program_id(1) == 0)
    def _():
        m_in, m_out = m_in_ref[...], m_out_ref[...]
        for half in range(2):
            for s in range(L):
                x = jnp.dot(winc_ref[half, s * lanes:(s + 1) * lanes, :], rep_in_ref[...], preferred_element_type=F32)
                win_ref[s * lanes:(s + 1) * lanes, half * sdim:(half + 1) * sdim] = (x * m_in).astype(BF16)
            for tt in range(L):
                x = jnp.dot(woutc_ref[half], rep_out_ref[:, tt * lanes:(tt + 1) * lanes], preferred_element_type=F32)
                wout_ref[half * sdim:(half + 1) * sdim, tt * lanes:(tt + 1) * lanes] = (x * m_out).astype(BF16)
        c0 = [(jnp.dot(c0_ref[half], rep_k_ref[...], preferred_element_type=F32) * m_out).astype(BF16)
              for half in range(2)]
        kx = jnp.dot(win_ref[...], jnp.concatenate(c0, axis=0), preferred_element_type=F32)
        zero = jnp.zeros((lanes, lanes), BF16)

        def kblk(tau):
            s = L - 1 - tau
            return kx[s * lanes:(s + 1) * lanes].astype(BF16) if 0 <= tau < L else zero

        for r in range(L + 1):
            grev_ref[r * lanes:(r + 1) * lanes, 0:lanes] = kblk(L - r - 1)
            grev_ref[r * lanes:(r + 1) * lanes, lanes:] = kblk(L - r)

    def u_at(s):
        return u_ref[:, pl.ds(s, nchunk, stride=L), :].reshape(rows, lanes)

    for s in range(L):
        ucat_ref[:, s * lanes:(s + 1) * lanes] = u_at(s).astype(BF16)
    inj = jnp.dot(ucat_ref[...], win_ref[...], preferred_element_type=F32)
    npl = sdim // lanes
    for k in range(2 * npl):
        inj_ref[k] = inj[:, k * lanes:(k + 1) * lanes]
    ar = [jnp.broadcast_to(al_ref[0:1, k * lanes:(k + 1) * lanes], (nb, lanes)) for k in range(npl)]
    ai = [jnp.broadcast_to(al_ref[0:1, sdim + k * lanes:sdim + (k + 1) * lanes], (nb, lanes)) for k in range(npl)]

    def step(c, carry):
        xr, xi = carry
        rsel = pl.ds(c, nb, stride=nchunk)
        nr, ni = [], []
        for k in range(npl):
            xp_ref[k, rsel, :] = xr[k]
            xp_ref[npl + k, rsel, :] = xi[k]
            nr.append(ar[k] * xr[k] - ai[k] * xi[k] + inj_ref[k, rsel, :])
            ni.append(ar[k] * xi[k] + ai[k] * xr[k] + inj_ref[npl + k, rsel, :])
        return tuple(nr), tuple(ni)

    z = tuple(jnp.zeros((nb, lanes), F32) for _ in range(npl))
    lax.fori_loop(0, nchunk, step, (z, z), unroll=4)
    xp = jnp.concatenate([xp_ref[k] for k in range(2 * npl)], axis=1).astype(BF16)
    d = d_ref[...]
    for tp in range(0, L, 2):
        res = jnp.dot(ucat_ref[:, 0:(tp + 2) * lanes], grev_ref[(L - 1 - tp) * lanes:(L + 1) * lanes, :],
                      preferred_element_type=F32)
        res = res + jnp.dot(xp, wout_ref[:, tp * lanes:(tp + 2) * lanes], preferred_element_type=F32)
        for k in range(2):
            y = res[:, k * lanes:(k + 1) * lanes] + d * u_at(tp + k)
            y_ref[:, pl.ds(tp + k, nchunk, stride=L), :] = y.reshape(nb, nchunk, lanes)


def _s5(u, c0c, winc, woutc, al, d, *, nb=4):
    b, t, dch = u.shape
    nb = min(nb, b)
    nblk = dch // LANES
    L = SSM_CHUNK
    rows = nb * (t // L)
    hch, p = c0c.shape[3], winc.shape[3]
    gpb = LANES // hch
    sdim = gpb * p
    kcat = L * LANES
    lane, col, st = jnp.arange(LANES), jnp.arange(kcat), jnp.arange(sdim)
    rep_k = (jnp.arange(hch)[:, None] == (lane % hch)[None, :]).astype(BF16)
    rep_in = (jnp.arange(p)[:, None] == (st % p)[None, :]).astype(BF16)
    rep_out = (jnp.arange(L * hch)[:, None] == (col // LANES * hch + col % hch)[None, :]).astype(BF16)
    m_in = ((lane // hch)[:, None] == (st // p)[None, :]).astype(F32)
    m_out = ((st // p)[:, None] == (lane // hch)[None, :]).astype(F32)
    const = lambda a: pl.BlockSpec(a.shape, lambda j, i: (0,) * a.ndim)
    return pl.pallas_call(
        _s5_body,
        grid=(nblk, b // nb),
        in_specs=[
            pl.BlockSpec((nb, t, LANES), lambda j, i: (i, 0, j)),
            pl.BlockSpec((None, 2, sdim, hch), lambda j, i: (j, 0, 0, 0)),
            pl.BlockSpec((None, 2, kcat, p), lambda j, i: (j, 0, 0, 0)),
            pl.BlockSpec((None, 2, sdim, L * hch), lambda j, i: (j, 0, 0, 0)),
            const(rep_k), const(rep_in), const(rep_out), const(m_in), const(m_out),
            pl.BlockSpec((None, 8, 2 * sdim), lambda j, i: (j, 0, 0)),
            pl.BlockSpec((1, LANES), lambda j, i: (0, j)),
        ],
        out_specs=pl.BlockSpec((nb, t, LANES), lambda j, i: (i, 0, j)),
        out_shape=jax.ShapeDtypeStruct((b, t, dch), F32),
        scratch_shapes=[pltpu.VMEM((rows, kcat), BF16), pltpu.VMEM((2 * sdim // LANES, rows, LANES), F32),
                        pltpu.VMEM((2 * sdim // LANES, rows, LANES), F32),
                        pltpu.VMEM(((L + 1) * LANES, 2 * LANES), BF16), pltpu.VMEM((kcat, 2 * sdim), BF16),
                        pltpu.VMEM((2 * sdim, kcat), BF16)],
        compiler_params=_cparams(("parallel", "arbitrary")),
        name="s5",
    )(u, c0c, winc, woutc, rep_k, rep_in, rep_out, m_in, m_out, al, d.reshape(1, dch))


def _glu_out_body(h_ref, a_ref, y_ref, wg_ref, bg_ref, woa_ref, wos_ref, o_ref):
    hg = jax.nn.gelu(y_ref[...], approximate=True)
    z = jnp.dot(hg.astype(BF16), wg_ref[...], preferred_element_type=F32) + bg_ref[...]
    s = hg * jax.nn.sigmoid(z)
    mix = jnp.dot(a_ref[...], woa_ref[...], preferred_element_type=F32)
    mix = mix + jnp.dot(s.astype(BF16), wos_ref[...], preferred_element_type=F32)
    o_ref[...] = h_ref[...] + mix


def _glu_out(h, a, y, wg, bg, woa, wos, *, tm=512):
    n, dm = h.shape
    da, ds = a.shape[1], y.shape[1]
    row = lambda w: pl.BlockSpec((tm, w), lambda i: (i, 0))
    full = lambda r, c: pl.BlockSpec((r, c), lambda i: (0, 0))
    return pl.pallas_call(
        _glu_out_body,
        grid=(n // tm,),
        in_specs=[row(dm), row(da), row(ds), full(ds, ds), full(1, ds), full(da, dm), full(ds, dm)],
        out_specs=row(dm),
        out_shape=jax.ShapeDtypeStruct((n, dm), F32),
        compiler_params=_cparams(("parallel",)),
        name="glu_out",
    )(h, a, y, wg, bg.reshape(1, ds), woa, wos)


def _t5_bucket(dist):
    n = jnp.maximum(dist, 0)
    max_exact = N_BUCKETS // 2
    nf = jnp.maximum(n, 1).astype(F32)
    large = max_exact + (jnp.log(nf / max_exact) / math.log(MAX_DISTANCE / max_exact)
                         * (N_BUCKETS - max_exact)).astype(jnp.int32)
    large = jnp.minimum(large, N_BUCKETS - 1)
    return jnp.where(n < max_exact, n, large)


def _t5_body(first_ref, table_ref, bc_ref, dt_ref, wb_ref, *, nc):
    g = pl.program_id(0)
    nt = bc_ref.shape[0]
    span = wb_ref.shape[1]
    a = lax.broadcasted_iota(jnp.int32, (TQ, LANES), 0)
    c = lax.broadcasted_iota(jnp.int32, (TQ, LANES), 1)
    heads = [g * HEADS_PER_KV + h for h in range(HEADS_PER_KV)]

    def lookup(dist, valid, store):
        bias = [jnp.full((TQ, LANES), table_ref[0, hd] * LOG2E, F32) for hd in heads]
        for k in range(1, N_BUCKETS):
            ge = dist >= first_ref[k]
            bias = [jnp.where(ge, table_ref[k, hd] * LOG2E, bv) for hd, bv in zip(heads, bias)]
        for h, bv in enumerate(bias):
            store(slice(h * TQ, (h + 1) * TQ), bv if valid is None else jnp.where(valid, bv, NEG))

    def put_far(rows, tile):
        dt_ref[0, rows, :] = tile

    lookup(jnp.full((TQ, LANES), 2 * MAX_DISTANCE, jnp.int32), None, put_far)
    for r in range(-1, QK):
        def put_near(rows, tile, r=r):
            dt_ref[r + 2, rows, :] = tile

        dist = a - r * LANES - c
        lookup(dist, dist >= 0, put_near)
    dt_ref[QK + 2] = jnp.full((ROWS4, LANES), NEG, F32)
    for j in range(span // LANES):
        def put_wb(rows, tile, j=j):
            wb_ref[rows, j * LANES:(j + 1) * LANES] = tile

        dist = a + WINDOW - (c + j * LANES)
        lookup(dist, (dist >= 0) & (dist < WINDOW), put_wb)

    def cmp_tile(i, _):
        def put_bc(rows, tile):
            bc_ref[i, rows, :] = tile

        dist = i * TQ + a - (c * CMP_STRIDE + CMP_LEN - 1)
        lookup(dist, (dist >= 0) & (c < nc), put_bc)
        return 0

    lax.fori_loop(0, nt, cmp_tile, 0)


def _bias_tables(rel_bias, t):
    nt = t // TQ
    span = TQ + WINDOW
    nc = (t - CMP_LEN) // CMP_STRIDE + 1
    buckets = _t5_bucket(jnp.arange(2 * MAX_DISTANCE))
    first = jnp.sum((buckets[None, :] < jnp.arange(N_BUCKETS)[:, None]).astype(jnp.int32), axis=1)
    smem = pl.BlockSpec(memory_space=pltpu.SMEM)
    return pl.pallas_call(
        functools.partial(_t5_body, nc=nc),
        grid=(N_KV,),
        in_specs=[smem, smem],
        out_specs=[
            pl.BlockSpec((None, nt, ROWS4, LANES), lambda g: (g, 0, 0, 0)),
            pl.BlockSpec((None, QK + 3, ROWS4, LANES), lambda g: (g, 0, 0, 0)),
            pl.BlockSpec((None, ROWS4, span), lambda g: (g, 0, 0)),
        ],
        out_shape=[
            jax.ShapeDtypeStruct((N_KV, nt, ROWS4, LANES), F32),
            jax.ShapeDtypeStruct((N_KV, QK + 3, ROWS4, LANES), F32),
            jax.ShapeDtypeStruct((N_KV, ROWS4, span), F32),
        ],
        compiler_params=_cparams(("parallel",)),
        name="t5_tables",
    )(first, rel_bias.astype(F32))


def _sel_tables(t):
    ns = t // SEL_BLOCK
    nc_pad = LANES
    c_start = jnp.arange(nc_pad) * CMP_STRIDE
    j_start = jnp.arange(LANES) * SEL_BLOCK
    ov = jnp.clip(jnp.minimum(c_start[:, None] + CMP_LEN, j_start[None, :] + SEL_BLOCK)
                  - jnp.maximum(c_start[:, None], j_start[None, :]), 0, None).astype(F32) / CMP_LEN
    ov = jnp.where(jnp.arange(LANES)[None, :] < ns, ov, 0.0)
    et = jnp.arange(t)[:, None] // SEL_BLOCK == jnp.arange(LANES)[None, :]
    return ov.T.astype(BF16), et.astype(BF16)


def _s5_tables(lam_re, lam_im, log_step, b_re, b_im, c_re, c_im):
    ng, p = lam_re.shape
    hch = b_re.shape[2]
    L = SSM_CHUNK
    step = jnp.exp(log_step.astype(F32))[:, None]
    lre, lim = lam_re.astype(F32), lam_im.astype(F32)
    mag = jnp.exp(lre * step)
    ab_re, ab_im = mag * jnp.cos(lim * step), mag * jnp.sin(lim * step)
    nr, ni = ab_re - 1.0, ab_im
    den = lre * lre + lim * lim
    f_re, f_im = (nr * lre + ni * lim) / den, (ni * lre - nr * lim) / den
    br, bim = b_re.astype(F32), b_im.astype(F32)
    bb_re = f_re[..., None] * br - f_im[..., None] * bim
    bb_im = f_re[..., None] * bim + f_im[..., None] * br
    cr, ci = c_re.astype(F32), c_im.astype(F32)
    pr, pi = [jnp.ones_like(ab_re)], [jnp.zeros_like(ab_re)]
    for _ in range(L):
        pr, pi = pr + [pr[-1] * ab_re - pi[-1] * ab_im], pi + [pr[-1] * ab_im + pi[-1] * ab_re]
    pw_re, pw_im = jnp.stack(pr, 0), jnp.stack(pi, 0)
    cp_re = cr[None] * pw_re[:, :, None, :] - ci[None] * pw_im[:, :, None, :]
    cp_im = -(cr[None] * pw_im[:, :, None, :] + ci[None] * pw_re[:, :, None, :])
    gpb = LANES // hch
    nblk = ng // gpb
    wr = pw_re[L - 1 - jnp.arange(L)]
    wi = pw_im[L - 1 - jnp.arange(L)]
    win_re = (wr[..., None] * bb_re[None] - wi[..., None] * bb_im[None])
    win_im = (wr[..., None] * bb_im[None] + wi[..., None] * bb_re[None])
    inj_op = lambda m: m.reshape(L, nblk, gpb, p, hch).transpose(1, 0, 2, 4, 3).reshape(nblk, L * LANES, p)
    winc = jnp.stack([inj_op(win_re), inj_op(win_im)], axis=1)
    read_op = lambda m: m.reshape(L, nblk, gpb, hch, p).transpose(1, 2, 4, 0, 3).reshape(nblk, gpb * p, L * hch)
    woutc = jnp.stack([read_op(cp_re[1:]), read_op(cp_im[1:])], axis=1)
    read0 = lambda m: m.reshape(nblk, gpb, hch, p).transpose(0, 1, 3, 2).reshape(nblk, gpb * p, hch)
    c0c = jnp.stack([read0(cp_re[0]), read0(cp_im[0])], axis=1)
    al = jnp.concatenate([pw_re[L].reshape(nblk, gpb * p), pw_im[L].reshape(nblk, gpb * p)], axis=1)
    al = jnp.broadcast_to(al[:, None, :], (nblk, 8, 2 * gpb * p))
    return c0c.astype(BF16), winc.astype(BF16), woutc.astype(BF16), al


def _mixers(h1, mix_norm, w_in, cmp_k, cmp_v, rel_bias, ssm, ssm_d, glu_w, glu_b, w_out, b, t):
    n, d = h1.shape
    d_ssm = glu_w.shape[0]
    o_g, o_u = D_ATTN + 6 * D_KV, D_ATTN + 6 * D_KV + 3 * N_HEADS
    gcols = 3 * HEADS_PER_KV
    wg = [jnp.pad(w_in[:, o_g + g * gcols:o_g + (g + 1) * gcols], ((0, 0), (0, LANES - gcols))) for g in range(N_KV)]
    w_perm = jnp.concatenate([w_in[:, :o_g], w_in[:, o_u:]] + wg, axis=1).astype(BF16)
    q, kvs, kcv, u, gates = _inproj(h1.reshape(b, t, d), mix_norm, w_perm, d_ssm)

    nblk = t // CMP_STRIDE
    assert nblk == LANES, "compressed-block axis is laid out on one 128-lane tile"
    pe = jnp.stack([cmp_k[0], cmp_v[0]], 0).astype(F32)
    w1 = jnp.stack([cmp_k[1], cmp_v[1]], 0).astype(BF16).reshape(2, CMP_LEN, HEAD_DIM, -1)
    b1 = jnp.stack([cmp_k[2].reshape(1, -1), cmp_v[2].reshape(1, -1)], 0).astype(F32)
    w2 = jnp.stack([cmp_k[3], cmp_v[3]], 0).astype(BF16)
    cb = _compress(kcv, pe, w1, b1, w2, nblk)

    biasc, dtiles, wb = _bias_tables(rel_bias, t)
    ovt, et = _sel_tables(t)
    a = _nsa(q, kvs, cb, gates, biasc, dtiles, wb, ovt, et)

    c0c, winc, woutc, al = _s5_tables(*ssm)
    y = _s5(u, c0c, winc, woutc, al, ssm_d)

    wo = w_out.astype(BF16)
    return _glu_out(h1, a.reshape(n, D_ATTN), y.reshape(n, d_ssm), glu_w.astype(BF16), glu_b, wo[:D_ATTN], wo[D_ATTN:])


def kernel(x, ffn1_norm, ffn1_w1, ffn1_w3, ffn1_w2, mix_norm, w_in, cmp_pe_k, cmp_w1_k, cmp_b1_k, cmp_w2_k,
           cmp_pe_v, cmp_w1_v, cmp_b1_v, cmp_w2_v, rel_bias, ssm_lam_re, ssm_lam_im, ssm_log_step, ssm_b_re,
           ssm_b_im, ssm_c_re, ssm_c_im, ssm_d, glu_w, glu_b, w_out, ffn2_norm, ffn2_w1, ffn2_w3, ffn2_w2,
           final_norm):
    b, t, d = x.shape
    depth = ffn1_w1.shape[0]
    h = x.reshape(b * t, d)
    for l in range(depth):
        last = l == depth - 1
        h = _ffn(h, ffn1_norm[l], ffn1_w1[l], ffn1_w3[l], ffn1_w2[l])
        ssm = (ssm_lam_re[l], ssm_lam_im[l], ssm_log_step[l], ssm_b_re[l], ssm_b_im[l], ssm_c_re[l], ssm_c_im[l])
        h = _mixers(h, mix_norm[l], w_in[l],
                    (cmp_pe_k[l], cmp_w1_k[l], cmp_b1_k[l], cmp_w2_k[l]),
                    (cmp_pe_v[l], cmp_w1_v[l], cmp_b1_v[l], cmp_w2_v[l]),
                    rel_bias, ssm, ssm_d[l], glu_w[l], glu_b[l], w_out[l], b, t)
        h = _ffn(h, ffn2_norm[l], ffn2_w1[l], ffn2_w3[l], ffn2_w2[l], final_gain=final_norm if last else None)
    return h.reshape(b, t, d)
```

```python
import functools
import math

import jax
import jax.numpy as jnp
from jax import lax
from jax.experimental import pallas as pl
from jax.experimental.pallas import tpu as pltpu

F32 = jnp.float32
BF16 = jnp.bfloat16

HEAD_DIM = 128
N_KV = 2
HEADS_PER_KV = 4
N_HEADS = N_KV * HEADS_PER_KV
D_ATTN = N_HEADS * HEAD_DIM
D_KV = N_KV * HEAD_DIM
CMP_LEN = 32
CMP_STRIDE = 16
SEL_BLOCK = 64
N_SELECT = 16
WINDOW = 512
N_BUCKETS = 32
MAX_DISTANCE = 128
SSM_GROUP = 16
SSM_STATE = 64
EPS = 1e-6
NEG = -1e30
LOG2E = math.log2(math.e)

LANES = 128
VMEM_LIMIT_BYTES = 56 * 1024 * 1024
NSA_VMEM_LIMIT_BYTES = 60 * 1024 * 1024
FFN_VMEM_LIMIT_BYTES = 63 * 1024 * 1024
TQ = 256
QK = TQ // LANES
ROWS4 = HEADS_PER_KV * TQ
SEL_CHUNK = 512
SSM_CHUNK = 16


def _cparams(sem, vmem_limit_bytes=VMEM_LIMIT_BYTES):
    return pltpu.CompilerParams(dimension_semantics=sem, vmem_limit_bytes=vmem_limit_bytes)


def _ffn_body(x_ref, g_ref, w1_ref, w3_ref, w2_ref, *rest, final):
    if final:
        fg_ref, o_ref, xn_ref = rest
    else:
        o_ref, xn_ref = rest
    j = pl.program_id(1)
    nj = pl.num_programs(1)

    def contribution(xn):
        a = jnp.dot(xn, w1_ref[...], preferred_element_type=F32)
        b = jnp.dot(xn, w3_ref[...], preferred_element_type=F32)
        gated = (a * jax.nn.sigmoid(a)) * b
        return jnp.dot(gated.astype(BF16), w2_ref[...], preferred_element_type=F32)

    @pl.when(j == 0)
    def _():
        x = x_ref[...]
        ms = jnp.mean(x * x, axis=-1, keepdims=True)
        xn = (x * lax.rsqrt(ms + EPS) * g_ref[...]).astype(BF16)
        xn_ref[...] = xn
        o_ref[...] = contribution(xn)

    @pl.when((j > 0) & (j < nj - 1))
    def _():
        o_ref[...] += contribution(xn_ref[...])

    @pl.when(j == nj - 1)
    def _():
        h = x_ref[...] + 0.5 * (o_ref[...] + contribution(xn_ref[...]))
        if final:
            ms = jnp.mean(h * h, axis=-1, keepdims=True)
            h = h * lax.rsqrt(ms + EPS) * fg_ref[...]
        o_ref[...] = h


def _ffn(x, gain, w1, w3, w2, final_gain=None, *, tm=1024, tf=512):
    n, d = x.shape
    dff = w1.shape[1]
    nj = dff // tf
    assert nj >= 2 and nj * tf == dff
    final = final_gain is not None
    in_specs = [
        pl.BlockSpec((tm, d), lambda i, j: (i, 0)),
        pl.BlockSpec((1, d), lambda i, j: (0, 0)),
        pl.BlockSpec((d, tf), lambda i, j: (0, j)),
        pl.BlockSpec((d, tf), lambda i, j: (0, j)),
        pl.BlockSpec((tf, d), lambda i, j: (j, 0)),
    ]
    args = [x, gain.reshape(1, d), w1.astype(BF16), w3.astype(BF16), w2.astype(BF16)]
    if final:
        in_specs.append(pl.BlockSpec((1, d), lambda i, j: (0, 0)))
        args.append(final_gain.reshape(1, d))
    return pl.pallas_call(
        functools.partial(_ffn_body, final=final),
        grid=(n // tm, nj),
        in_specs=in_specs,
        out_specs=pl.BlockSpec((tm, d), lambda i, j: (i, 0)),
        out_shape=jax.ShapeDtypeStruct((n, d), F32),
        scratch_shapes=[pltpu.VMEM((tm, d), BF16)],
        compiler_params=_cparams(("parallel", "arbitrary"), FFN_VMEM_LIMIT_BYTES),
        name="ffn_final" if final else "ffn",
    )(*args)


def _inproj_body(x_ref, g_ref, w_ref, q_ref, kvs_ref, kcv_ref, u_ref, gate_ref):
    i = pl.program_id(1)

    @pl.when(i == 0)
    def _():
        kvs_ref[...] = jnp.zeros(kvs_ref.shape, kvs_ref.dtype)
        kcv_ref[...] = jnp.zeros(kcv_ref.shape, kcv_ref.dtype)

    @pl.when(i > 0)
    def _():
        x = x_ref[...]
        ms = jnp.mean(x * x, axis=-1, keepdims=True)
        xn = (x * lax.rsqrt(ms + EPS) * g_ref[...]).astype(BF16)
        c0 = 0
        c1 = q_ref.shape[1]
        q = jnp.dot(xn, w_ref[:, c0:c1], preferred_element_type=F32)
        q_ref[...] = (q * (HEAD_DIM ** -0.5 * LOG2E)).astype(BF16)
        c0, c1 = c1, c1 + kcv_ref.shape[1]
        kcv_ref[...] = jnp.dot(xn, w_ref[:, c0:c1], preferred_element_type=F32)
        c0, c1 = c1, c1 + kvs_ref.shape[1]
        kvs_ref[...] = jnp.dot(xn, w_ref[:, c0:c1], preferred_element_type=F32).astype(BF16)
        c0, c1 = c1, c1 + u_ref.shape[1]
        u_ref[...] = jnp.dot(xn, w_ref[:, c0:c1], preferred_element_type=F32)
        gate_ref[...] = jax.nn.sigmoid(jnp.dot(xn, w_ref[:, c1:], preferred_element_type=F32))


def _inproj(h, gain, w_perm, d_ssm):
    b, t, d = h.shape
    tm = WINDOW
    nt = t // tm
    nq, ncv, nvs, ng = D_ATTN, 2 * D_KV, 4 * D_KV, N_KV * LANES
    ncol = w_perm.shape[1]
    assert ncol == nq + ncv + nvs + d_ssm + ng and t % tm == 0
    data = lambda bi, i: (bi, jnp.maximum(i - 1, 0), 0)
    return pl.pallas_call(
        _inproj_body,
        grid=(b, nt + 1),
        in_specs=[
            pl.BlockSpec((None, tm, d), data),
            pl.BlockSpec((1, d), lambda bi, i: (0, 0)),
            pl.BlockSpec((d, ncol), lambda bi, i: (0, 0)),
        ],
        out_specs=[
            pl.BlockSpec((None, tm, nq), data),
            pl.BlockSpec((None, tm, nvs), lambda bi, i: (bi, i, 0)),
            pl.BlockSpec((None, tm, ncv), lambda bi, i: (bi, jnp.where(i == 0, nt, i - 1), 0)),
            pl.BlockSpec((None, tm, d_ssm), data),
            pl.BlockSpec((None, tm, ng), data),
        ],
        out_shape=[
            jax.ShapeDtypeStruct((b, t, nq), BF16),
            jax.ShapeDtypeStruct((b, t + tm, nvs), BF16),
            jax.ShapeDtypeStruct((b, t + tm, ncv), F32),
            jax.ShapeDtypeStruct((b, t, d_ssm), F32),
            jax.ShapeDtypeStruct((b, t, ng), F32),
        ],
        compiler_params=_cparams(("parallel", "arbitrary")),
        name="inproj",
    )(h, gain.reshape(1, d), w_perm)


def _compress_body(x_ref, pe_ref, w1_ref, b1_ref, w2_ref, o_ref):
    nb, nblk = o_ref.shape[0], o_ref.shape[1]
    acc = jnp.broadcast_to(b1_ref[...], (nb * nblk, b1_ref.shape[1])).astype(F32)
    for l in range(CMP_LEN):
        xl = x_ref[:, pl.ds(l, nblk, stride=CMP_STRIDE), :] + pe_ref[l:l + 1, :]
        acc = acc + jnp.dot(xl.reshape(nb * nblk, xl.shape[2]).astype(BF16), w1_ref[l], preferred_element_type=F32)
    hid = jax.nn.gelu(acc, approximate=True)
    out = jnp.dot(hid.astype(BF16), w2_ref[...], preferred_element_type=F32)
    o_ref[...] = out.reshape(o_ref.shape).astype(BF16)


def _compress(kcv, pe, w1, b1, w2, nblk, *, nb=4):
    b, tpad, _ = kcv.shape
    nb = min(nb, b)
    hid, dh = w1.shape[3], w2.shape[2]
    assert (nblk - 1) * CMP_STRIDE + CMP_LEN <= tpad
    return pl.pallas_call(
        _compress_body,
        grid=(2 * N_KV, b // nb),
        in_specs=[
            pl.BlockSpec((nb, tpad, dh), lambda s, i: (i, 0, s)),
            pl.BlockSpec((None, CMP_LEN, dh), lambda s, i: (s // N_KV, 0, 0)),
            pl.BlockSpec((None, CMP_LEN, dh, hid), lambda s, i: (s // N_KV, 0, 0, 0)),
            pl.BlockSpec((None, 1, hid), lambda s, i: (s // N_KV, 0, 0)),
            pl.BlockSpec((None, hid, dh), lambda s, i: (s // N_KV, 0, 0)),
        ],
        out_specs=pl.BlockSpec((None, None, nb, nblk, dh), lambda s, i: (s // N_KV, s % N_KV, i, 0, 0)),
        out_shape=jax.ShapeDtypeStruct((2, N_KV, b, nblk, dh), BF16),
        compiler_params=_cparams(("parallel", "parallel")),
        name="compress",
    )(kcv, pe, w1, b1, w2)


def _nt_dot(a, b):
    return lax.dot_general(a, b, (((1,), (1,)), ((), ())), preferred_element_type=F32)


def _nsa_body(q_ref, ks_ref, vs_ref, kw_ref, vw_ref, kcb_ref, vcb_ref, gate_ref, biasc_ref, dt_ref, wb_ref,
              ovt_ref, et_ref, o_ref, kse_ref, s_ref, m_ref, l_ref, acc_ref):
    i = pl.program_id(1)
    t0 = i * TQ
    t = et_ref.shape[0]
    ns = t // SEL_BLOCK
    nsel = min(N_SELECT, ns)
    groups = range(N_KV)
    gl = lambda g: slice(g * HEAD_DIM, (g + 1) * HEAD_DIM)

    @pl.when(i == 0)
    def _():
        for g in groups:
            kse_ref[g, :, 0:HEAD_DIM] = ks_ref[WINDOW:WINDOW + t, gl(g)]
            kse_ref[g, :, HEAD_DIM:] = et_ref[...]

    q = q_ref[...]
    w0 = pl.multiple_of(t0, LANES)
    span = wb_ref.shape[2]

    q4 = [jnp.concatenate([q[:, (g * HEADS_PER_KV + h) * HEAD_DIM:(g * HEADS_PER_KV + h + 1) * HEAD_DIM]
                           for h in range(HEADS_PER_KV)], axis=0) for g in groups]

    sc = [_nt_dot(q4[g], kcb_ref[g]) + biasc_ref[g] for g in groups]
    row_t = t0 + (lax.broadcasted_iota(jnp.int32, (ROWS4, LANES), 0) & (TQ - 1))
    col_c = lax.broadcasted_iota(jnp.int32, (ROWS4, LANES), 1)
    valid_c = (row_t - col_c * CMP_STRIDE - (CMP_LEN - 1) >= 0) & (col_c < LANES - 1)
    mc = [jnp.max(s, axis=-1, keepdims=True) for s in sc]
    pc = [jnp.where(valid_c, jnp.exp2(s - m), 0.0) for s, m in zip(sc, mc)]
    pc = [p / jnp.maximum(jnp.sum(p, axis=-1, keepdims=True), 1e-30) for p in pc]
    pcb = [p.astype(BF16) for p in pc]
    o_cmps = [jnp.dot(pcb[g], vcb_ref[g], preferred_element_type=F32) for g in groups]
    pimp = [_nt_dot(ovt_ref[...], p) for p in pcb]
    imp = [sum(p[0:ns, h * TQ:(h + 1) * TQ] for h in range(1, HEADS_PER_KV)) + p[0:ns, 0:TQ] for p in pimp]

    jrow = lax.broadcasted_iota(jnp.int32, (ns, TQ), 0)
    tpos = t0 + lax.broadcasted_iota(jnp.int32, (ns, TQ), 1)
    cur = lax.shift_right_logical(tpos, int(math.log2(SEL_BLOCK)))
    forced = (jrow == 0) | (jrow == cur) | (jrow == cur - 1)
    impm = [jnp.where(forced, 1e6, jnp.where(jrow * SEL_BLOCK <= tpos, x, -1e9)) for x in imp]
    nslab = ns // 8
    slabs = [[x[8 * v:8 * v + 8] for v in range(nslab)] for x in impm]
    cnts = [[jnp.zeros((8, TQ), F32) for _ in range(nslab)] for _ in groups]
    sub = lax.broadcasted_iota(jnp.int32, (8, TQ), 0)
    for jp in range(ns):
        v0, r0 = divmod(jp, 8)
        for g in groups:
            row = jnp.broadcast_to(slabs[g][v0][r0:r0 + 1, :], (8, TQ))
            for v in range(nslab):
                if v > v0:
                    beats = row >= slabs[g][v]
                elif v < v0:
                    beats = row > slabs[g][v]
                else:
                    beats = (row > slabs[g][v]) | ((row == slabs[g][v]) & (sub > r0))
                cnts[g][v] = cnts[g][v] + jnp.where(beats, 1.0, 0.0)
    q4s = []
    for g in groups:
        negt = [jnp.where(c < nsel, 0.0, NEG) for c in cnts[g]]
        negt = jnp.concatenate(negt + [jnp.zeros((LANES - ns, TQ), F32)], axis=0)
        neg = negt.T.astype(BF16)
        q4s.append(jnp.concatenate([q4[g], jnp.concatenate([neg] * HEADS_PER_KV, axis=0)], axis=1))

    nsub = SEL_CHUNK // LANES
    nchunks = ((i + 1) * QK - 1) // nsub + 1
    m_ref[...] = jnp.full(m_ref.shape, -3e38, F32)

    def score_chunk(ck, _):
        kb = pl.multiple_of(ck * SEL_CHUNK, SEL_CHUNK)
        s = [_nt_dot(q4s[g], kse_ref[g, pl.ds(kb, SEL_CHUNK), :]) for g in groups]
        for g in groups:
            mloc = None
            for j in range(nsub):
                sj = s[g][:, j * LANES:(j + 1) * LANES] + dt_ref[g, jnp.clip(ck * nsub + j - i * QK + 2, 0, QK + 2)]
                s_ref[g, ck, :, j * LANES:(j + 1) * LANES] = sj
                mloc = sj if mloc is None else jnp.maximum(mloc, sj)
            m_ref[g] = jnp.maximum(m_ref[g], mloc)
        return 0

    lax.fori_loop(0, nchunks, score_chunk, 0)
    for g in groups:
        m_ref[g] = jnp.broadcast_to(jnp.max(m_ref[g], axis=-1, keepdims=True), (ROWS4, LANES))
    l_ref[...] = jnp.zeros(l_ref.shape, F32)
    acc_ref[...] = jnp.zeros(acc_ref.shape, F32)

    kneg = jnp.where(t0 - WINDOW + lax.broadcasted_iota(jnp.int32, (1, span), 1) >= 0, 0.0, NEG)
    sw = [_nt_dot(q4[g], kw_ref[pl.ds(w0, span), gl(g)]) + wb_ref[g] + kneg for g in groups]
    mw = [jnp.max(s, axis=-1, keepdims=True) for s in sw]
    pw = [jnp.exp2(s - m) for s, m in zip(sw, mw)]
    lw = [jnp.sum(p, axis=-1, keepdims=True) for p in pw]
    o_wins = [jnp.dot(pw[g].astype(BF16), vw_ref[pl.ds(w0, span), gl(g)], preferred_element_type=F32) for g in groups]
    o_wins = [o / jnp.maximum(l, 1e-30) for o, l in zip(o_wins, lw)]

    def pv_chunk(ck, _):
        kb = pl.multiple_of(WINDOW + ck * SEL_CHUNK, SEL_CHUNK)
        ps = [[jnp.exp2(s_ref[g, ck, :, j * LANES:(j + 1) * LANES] - m_ref[g]) for j in range(nsub)] for g in groups]
        for g in groups:
            lsum = ps[g][0]
            for pj in ps[g][1:]:
                lsum = lsum + pj
            l_ref[g] += lsum
            p = jnp.concatenate([pj.astype(BF16) for pj in ps[g]], axis=1)
            acc_ref[g] += jnp.dot(p, vs_ref[pl.ds(kb, SEL_CHUNK), gl(g)], preferred_element_type=F32)
        return 0

    lax.fori_loop(0, nchunks, pv_chunk, 0)

    ls = [jnp.maximum(jnp.sum(l_ref[g], axis=-1, keepdims=True), 1e-30) for g in groups]
    o_sels = [acc_ref[g] / ls[g] for g in groups]
    gts = [gate_ref[:, g * LANES:(g + 1) * LANES] for g in groups]
    for h in range(HEADS_PER_KV):
        r = slice(h * TQ, (h + 1) * TQ)
        for g in groups:
            gt = gts[g]
            o_h = (gt[:, 3 * h:3 * h + 1] * o_cmps[g][r] + gt[:, 3 * h + 1:3 * h + 2] * o_sels[g][r]
                   + gt[:, 3 * h + 2:3 * h + 3] * o_wins[g][r])
            c0 = (g * HEADS_PER_KV + h) * HEAD_DIM
            o_ref[:, c0:c0 + HEAD_DIM] = o_h.astype(BF16)


def _nsa(q, kvs, cb, gates, biasc, dtiles, wb, ovt, et):
    b, t, _ = q.shape
    nt = t // TQ
    tp = kvs.shape[1]
    span = wb.shape[2]
    gw = HEADS_PER_KV * HEAD_DIM
    assert tp == t + WINDOW and span == TQ + WINDOW and t % SEL_CHUNK == 0 and (t // SEL_BLOCK) % 8 == 0

    def stream(j):
        return pl.BlockSpec((None, tp, N_KV * HEAD_DIM), lambda bi, i: (bi, 0, j))

    return pl.pallas_call(
        _nsa_body,
        grid=(b, nt),
        in_specs=[
            pl.BlockSpec((None, TQ, N_KV * gw), lambda bi, i: (bi, i, 0)),
            stream(0), stream(1), stream(2), stream(3),
            pl.BlockSpec((None, N_KV, None, LANES, HEAD_DIM), lambda bi, i: (0, 0, bi, 0, 0)),
            pl.BlockSpec((None, N_KV, None, LANES, HEAD_DIM), lambda bi, i: (1, 0, bi, 0, 0)),
            pl.BlockSpec((None, TQ, N_KV * LANES), lambda bi, i: (bi, i, 0)),
            pl.BlockSpec((N_KV, None, ROWS4, LANES), lambda bi, i: (0, i, 0, 0)),
            pl.BlockSpec((N_KV, QK + 3, ROWS4, LANES), lambda bi, i: (0, 0, 0, 0)),
            pl.BlockSpec((N_KV, ROWS4, span), lambda bi, i: (0, 0, 0)),
            pl.BlockSpec((LANES, LANES), lambda bi, i: (0, 0)),
            pl.BlockSpec((t, LANES), lambda bi, i: (0, 0)),
        ],
        out_specs=pl.BlockSpec((None, TQ, N_KV * gw), lambda bi, i: (bi, i, 0)),
        out_shape=jax.ShapeDtypeStruct((b, t, N_KV * gw), BF16),
        scratch_shapes=[
            pltpu.VMEM((N_KV, t, 2 * HEAD_DIM), BF16),
            pltpu.VMEM((N_KV, t // SEL_CHUNK, ROWS4, SEL_CHUNK), F32),
            pltpu.VMEM((N_KV, ROWS4, LANES), F32),
            pltpu.VMEM((N_KV, ROWS4, LANES), F32),
            pltpu.VMEM((N_KV, ROWS4, HEAD_DIM), F32),
        ],
        compiler_params=_cparams(("parallel", "arbitrary"), NSA_VMEM_LIMIT_BYTES),
        name="nsa",
    )(q, kvs, kvs, kvs, kvs, cb, cb, gates, biasc, dtiles, wb, ovt, et)


def _s5_body(u_ref, c0_ref, winc_ref, woutc_ref, rep_k_ref, rep_in_ref, rep_out_ref, m_in_ref, m_out_ref,
             al_ref, d_ref, y_ref, ucat_ref, inj_ref, xp_ref, grev_ref, win_ref, wout_ref):
    nb, t, lanes = u_ref.shape
    L = SSM_CHUNK
    nchunk = t // L
    rows = nb * nchunk
    sdim = al_ref.shape[1] // 2

    @pl.when(pl.program_id(1) == 0)
    def _():
        m_in, m_out = m_in_ref[...], m_out_ref[...]
        for half in range(2):
            for s in range(L):
                x = jnp.dot(winc_ref[half, s * lanes:(s + 1) * lanes, :], rep_in_ref[...], preferred_element_type=F32)
                win_ref[s * lanes:(s + 1) * lanes, half * sdim:(half + 1) * sdim] = (x * m_in).astype(BF16)
            for tt in range(L):
                x = jnp.dot(woutc_ref[half], rep_out_ref[:, tt * lanes:(tt + 1) * lanes], preferred_element_type=F32)
                wout_ref[half * sdim:(half + 1) * sdim, tt * lanes:(tt + 1) * lanes] = (x * m_out).astype(BF16)
        c0 = [(jnp.dot(c0_ref[half], rep_k_ref[...], preferred_element_type=F32) * m_out).astype(BF16)
              for half in range(2)]
        kx = jnp.dot(win_ref[...], jnp.concatenate(c0, axis=0), preferred_element_type=F32)
        zero = jnp.zeros((lanes, lanes), BF16)

        def kblk(tau):
            s = L - 1 - tau
            return kx[s * lanes:(s + 1) * lanes].astype(BF16) if 0 <= tau < L else zero

        for r in range(L + 1):
            grev_ref[r * lanes:(r + 1) * lanes, 0:lanes] = kblk(L - r - 1)
            grev_ref[r * lanes:(r + 1) * lanes, lanes:] = kblk(L - r)

    def u_at(s):
        return u_ref[:, pl.ds(s, nchunk, stride=L), :].reshape(rows, lanes)

    for s in range(L):
        ucat_ref[:, s * lanes:(s + 1) * lanes] = u_at(s).astype(BF16)
    inj = jnp.dot(ucat_ref[...], win_ref[...], preferred_element_type=F32)
    npl = sdim // lanes
    for k in range(2 * npl):
        inj_ref[k] = inj[:, k * lanes:(k + 1) * lanes]
    ar = [jnp.broadcast_to(al_ref[0:1, k * lanes:(k + 1) * lanes], (nb, lanes)) for k in range(npl)]
    ai = [jnp.broadcast_to(al_ref[0:1, sdim + k * lanes:sdim + (k + 1) * lanes], (nb, lanes)) for k in range(npl)]

    def step(c, carry):
        xr, xi = carry
        rsel = pl.ds(c, nb, stride=nchunk)
        nr, ni = [], []
        for k in range(npl):
            xp_ref[k, rsel, :] = xr[k]
            xp_ref[npl + k, rsel, :] = xi[k]
            nr.append(ar[k] * xr[k] - ai[k] * xi[k] + inj_ref[k, rsel, :])
            ni.append(ar[k] * xi[k] + ai[k] * xr[k] + inj_ref[npl + k, rsel, :])
        return tuple(nr), tuple(ni)

    z = tuple(jnp.zeros((nb, lanes), F32) for _ in range(npl))
    lax.fori_loop(0, nchunk, step, (z, z), unroll=4)
    xp = jnp.concatenate([xp_ref[k] for k in range(2 * npl)], axis=1).astype(BF16)
    d = d_ref[...]
    for tp in range(0, L, 2):
        res = jnp.dot(ucat_ref[:, 0:(tp + 2) * lanes], grev_ref[(L - 1 - tp) * lanes:(L + 1) * lanes, :],
                      preferred_element_type=F32)
        res = res + jnp.dot(xp, wout_ref[:, tp * lanes:(tp + 2) * lanes], preferred_element_type=F32)
        for k in range(2):
            y = res[:, k * lanes:(k + 1) * lanes] + d * u_at(tp + k)
            y_ref[:, pl.ds(tp + k, nchunk, stride=L), :] = y.reshape(nb, nchunk, lanes)


def _s5(u, c0c, winc, woutc, al, d, *, nb=4):
    b, t, dch = u.shape
    nb = min(nb, b)
    nblk = dch // LANES
    L = SSM_CHUNK
    rows = nb * (t // L)
    hch, p = c0c.shape[3], winc.shape[3]
    gpb = LANES // hch
    sdim = gpb * p
    kcat = L * LANES
    lane, col, st = jnp.arange(LANES), jnp.arange(kcat), jnp.arange(sdim)
    rep_k = (jnp.arange(hch)[:, None] == (lane % hch)[None, :]).astype(BF16)
    rep_in = (jnp.arange(p)[:, None] == (st % p)[None, :]).astype(BF16)
    rep_out = (jnp.arange(L * hch)[:, None] == (col // LANES * hch + col % hch)[None, :]).astype(BF16)
    m_in = ((lane // hch)[:, None] == (st // p)[None, :]).astype(F32)
    m_out = ((st // p)[:, None] == (lane // hch)[None, :]).astype(F32)
    const = lambda a: pl.BlockSpec(a.shape, lambda j, i: (0,) * a.ndim)
    return pl.pallas_call(
        _s5_body,
        grid=(nblk, b // nb),
        in_specs=[
            pl.BlockSpec((nb, t, LANES), lambda j, i: (i, 0, j)),
            pl.BlockSpec((None, 2, sdim, hch), lambda j, i: (j, 0, 0, 0)),
            pl.BlockSpec((None, 2, kcat, p), lambda j, i: (j, 0, 0, 0)),
            pl.BlockSpec((None, 2, sdim, L * hch), lambda j, i: (j, 0, 0, 0)),
            const(rep_k), const(rep_in), const(rep_out), const(m_in), const(m_out),
            pl.BlockSpec((None, 8, 2 * sdim), lambda j, i: (j, 0, 0)),
            pl.BlockSpec((1, LANES), lambda j, i: (0, j)),
        ],
        out_specs=pl.BlockSpec((nb, t, LANES), lambda j, i: (i, 0, j)),
        out_shape=jax.ShapeDtypeStruct((b, t, dch), F32),
        scratch_shapes=[pltpu.VMEM((rows, kcat), BF16), pltpu.VMEM((2 * sdim // LANES, rows, LANES), F32),
                        pltpu.VMEM((2 * sdim // LANES, rows, LANES), F32),
                        pltpu.VMEM(((L + 1) * LANES, 2 * LANES), BF16), pltpu.VMEM((kcat, 2 * sdim), BF16),
                        pltpu.VMEM((2 * sdim, kcat), BF16)],
        compiler_params=_cparams(("parallel", "arbitrary")),
        name="s5",
    )(u, c0c, winc, woutc, rep_k, rep_in, rep_out, m_in, m_out, al, d.reshape(1, dch))


def _glu_out_body(h_ref, a_ref, y_ref, wg_ref, bg_ref, woa_ref, wos_ref, o_ref):
    hg = jax.nn.gelu(y_ref[...], approximate=True)
    z = jnp.dot(hg.astype(BF16), wg_ref[...], preferred_element_type=F32) + bg_ref[...]
    s = hg * jax.nn.sigmoid(z)
    mix = jnp.dot(a_ref[...], woa_ref[...], preferred_element_type=F32)
    mix = mix + jnp.dot(s.astype(BF16), wos_ref[...], preferred_element_type=F32)
    o_ref[...] = h_ref[...] + mix


def _glu_out(h, a, y, wg, bg, woa, wos, *, tm=512):
    n, dm = h.shape
    da, ds = a.shape[1], y.shape[1]
    row = lambda w: pl.BlockSpec((tm, w), lambda i: (i, 0))
    full = lambda r, c: pl.BlockSpec((r, c), lambda i: (0, 0))
    return pl.pallas_call(
        _glu_out_body,
        grid=(n // tm,),
        in_specs=[row(dm), row(da), row(ds), full(ds, ds), full(1, ds), full(da, dm), full(ds, dm)],
        out_specs=row(dm),
        out_shape=jax.ShapeDtypeStruct((n, dm), F32),
        compiler_params=_cparams(("parallel",)),
        name="glu_out",
    )(h, a, y, wg, bg.reshape(1, ds), woa, wos)


def _t5_bucket(dist):
    n = jnp.maximum(dist, 0)
    max_exact = N_BUCKETS // 2
    nf = jnp.maximum(n, 1).astype(F32)
    large = max_exact + (jnp.log(nf / max_exact) / math.log(MAX_DISTANCE / max_exact)
                         * (N_BUCKETS - max_exact)).astype(jnp.int32)
    large = jnp.minimum(large, N_BUCKETS - 1)
    return jnp.where(n < max_exact, n, large)


def _t5_body(first_ref, table_ref, bc_ref, dt_ref, wb_ref, *, nc):
    g = pl.program_id(0)
    nt = bc_ref.shape[0]
    span = wb_ref.shape[1]
    a = lax.broadcasted_iota(jnp.int32, (TQ, LANES), 0)
    c = lax.broadcasted_iota(jnp.int32, (TQ, LANES), 1)
    heads = [g * HEADS_PER_KV + h for h in range(HEADS_PER_KV)]

    def lookup(dist, valid, store):
        bias = [jnp.full((TQ, LANES), table_ref[0, hd] * LOG2E, F32) for hd in heads]
        for k in range(1, N_BUCKETS):
            ge = dist >= first_ref[k]
            bias = [jnp.where(ge, table_ref[k, hd] * LOG2E, bv) for hd, bv in zip(heads, bias)]
        for h, bv in enumerate(bias):
            store(slice(h * TQ, (h + 1) * TQ), bv if valid is None else jnp.where(valid, bv, NEG))

    def put_far(rows, tile):
        dt_ref[0, rows, :] = tile

    lookup(jnp.full((TQ, LANES), 2 * MAX_DISTANCE, jnp.int32), None, put_far)
    for r in range(-1, QK):
        def put_near(rows, tile, r=r):
            dt_ref[r + 2, rows, :] = tile

        dist = a - r * LANES - c
        lookup(dist, dist >= 0, put_near)
    dt_ref[QK + 2] = jnp.full((ROWS4, LANES), NEG, F32)
    for j in range(span // LANES):
        def put_wb(rows, tile, j=j):
            wb_ref[rows, j * LANES:(j + 1) * LANES] = tile

        dist = a + WINDOW - (c + j * LANES)
        lookup(dist, (dist >= 0) & (dist < WINDOW), put_wb)

    def cmp_tile(i, _):
        def put_bc(rows, tile):
            bc_ref[i, rows, :] = tile

        dist = i * TQ + a - (c * CMP_STRIDE + CMP_LEN - 1)
        lookup(dist, (dist >= 0) & (c < nc), put_bc)
        return 0

    lax.fori_loop(0, nt, cmp_tile, 0)


def _bias_tables(rel_bias, t):
    nt = t // TQ
    span = TQ + WINDOW
    nc = (t - CMP_LEN) // CMP_STRIDE + 1
    buckets = _t5_bucket(jnp.arange(2 * MAX_DISTANCE))
    first = jnp.sum((buckets[None, :] < jnp.arange(N_BUCKETS)[:, None]).astype(jnp.int32), axis=1)
    smem = pl.BlockSpec(memory_space=pltpu.SMEM)
    return pl.pallas_call(
        functools.partial(_t5_body, nc=nc),
        grid=(N_KV,),
        in_specs=[smem, smem],
        out_specs=[
            pl.BlockSpec((None, nt, ROWS4, LANES), lambda g: (g, 0, 0, 0)),
            pl.BlockSpec((None, QK + 3, ROWS4, LANES), lambda g: (g, 0, 0, 0)),
            pl.BlockSpec((None, ROWS4, span), lambda g: (g, 0, 0)),
        ],
        out_shape=[
            jax.ShapeDtypeStruct((N_KV, nt, ROWS4, LANES), F32),
            jax.ShapeDtypeStruct((N_KV, QK + 3, ROWS4, LANES), F32),
            jax.ShapeDtypeStruct((N_KV, ROWS4, span), F32),
        ],
        compiler_params=_cparams(("parallel",)),
        name="t5_tables",
    )(first, rel_bias.astype(F32))


def _sel_tables(t):
    ns = t // SEL_BLOCK
    nc_pad = LANES
    c_start = jnp.arange(nc_pad) * CMP_STRIDE
    j_start = jnp.arange(LANES) * SEL_BLOCK
    ov = jnp.clip(jnp.minimum(c_start[:, None] + CMP_LEN, j_start[None, :] + SEL_BLOCK)
                  - jnp.maximum(c_start[:, None], j_start[None, :]), 0, None).astype(F32) / CMP_LEN
    ov = jnp.where(jnp.arange(LANES)[None, :] < ns, ov, 0.0)
    et = jnp.arange(t)[:, None] // SEL_BLOCK == jnp.arange(LANES)[None, :]
    return ov.T.astype(BF16), et.astype(BF16)


def _s5_tables(lam_re, lam_im, log_step, b_re, b_im, c_re, c_im):
    ng, p = lam_re.shape
    hch = b_re.shape[2]
    L = SSM_CHUNK
    step = jnp.exp(log_step.astype(F32))[:, None]
    lre, lim = lam_re.astype(F32), lam_im.astype(F32)
    mag = jnp.exp(lre * step)
    ab_re, ab_im = mag * jnp.cos(lim * step), mag * jnp.sin(lim * step)
    nr, ni = ab_re - 1.0, ab_im
    den = lre * lre + lim * lim
    f_re, f_im = (nr * lre + ni * lim) / den, (ni * lre - nr * lim) / den
    br, bim = b_re.astype(F32), b_im.astype(F32)
    bb_re = f_re[..., None] * br - f_im[..., None] * bim
    bb_im = f_re[..., None] * bim + f_im[..., None] * br
    cr, ci = c_re.astype(F32), c_im.astype(F32)
    pr, pi = [jnp.ones_like(ab_re)], [jnp.zeros_like(ab_re)]
    for _ in range(L):
        pr, pi = pr + [pr[-1] * ab_re - pi[-1] * ab_im], pi + [pr[-1] * ab_im + pi[-1] * ab_re]
    pw_re, pw_im = jnp.stack(pr, 0), jnp.stack(pi, 0)
    cp_re = cr[None] * pw_re[:, :, None, :] - ci[None] * pw_im[:, :, None, :]
    cp_im = -(cr[None] * pw_im[:, :, None, :] + ci[None] * pw_re[:, :, None, :])
    gpb = LANES // hch
    nblk = ng // gpb
    wr = pw_re[L - 1 - jnp.arange(L)]
    wi = pw_im[L - 1 - jnp.arange(L)]
    win_re = (wr[..., None] * bb_re[None] - wi[..., None] * bb_im[None])
    win_im = (wr[..., None] * bb_im[None] + wi[..., None] * bb_re[None])
    inj_op = lambda m: m.reshape(L, nblk, gpb, p, hch).transpose(1, 0, 2, 4, 3).reshape(nblk, L * LANES, p)
    winc = jnp.stack([inj_op(win_re), inj_op(win_im)], axis=1)
    read_op = lambda m: m.reshape(L, nblk, gpb, hch, p).transpose(1, 2, 4, 0, 3).reshape(nblk, gpb * p, L * hch)
    woutc = jnp.stack([read_op(cp_re[1:]), read_op(cp_im[1:])], axis=1)
    read0 = lambda m: m.reshape(nblk, gpb, hch, p).transpose(0, 1, 3, 2).reshape(nblk, gpb * p, hch)
    c0c = jnp.stack([read0(cp_re[0]), read0(cp_im[0])], axis=1)
    al = jnp.concatenate([pw_re[L].reshape(nblk, gpb * p), pw_im[L].reshape(nblk, gpb * p)], axis=1)
    al = jnp.broadcast_to(al[:, None, :], (nblk, 8, 2 * gpb * p))
    return c0c.astype(BF16), winc.astype(BF16), woutc.astype(BF16), al


def _mixers(h1, mix_norm, w_in, cmp_k, cmp_v, rel_bias, ssm, ssm_d, glu_w, glu_b, w_out, b, t):
    n, d = h1.shape
    d_ssm = glu_w.shape[0]
    o_g, o_u = D_ATTN + 6 * D_KV, D_ATTN + 6 * D_KV + 3 * N_HEADS
    gcols = 3 * HEADS_PER_KV
    wg = [jnp.pad(w_in[:, o_g + g * gcols:o_g + (g + 1) * gcols], ((0, 0), (0, LANES - gcols))) for g in range(N_KV)]
    w_perm = jnp.concatenate([w_in[:, :o_g], w_in[:, o_u:]] + wg, axis=1).astype(BF16)
    q, kvs, kcv, u, gates = _inproj(h1.reshape(b, t, d), mix_norm, w_perm, d_ssm)

    nblk = t // CMP_STRIDE
    assert nblk == LANES, "compressed-block axis is laid out on one 128-lane tile"
    pe = jnp.stack([cmp_k[0], cmp_v[0]], 0).astype(F32)
    w1 = jnp.stack([cmp_k[1], cmp_v[1]], 0).astype(BF16).reshape(2, CMP_LEN, HEAD_DIM, -1)
    b1 = jnp.stack([cmp_k[2].reshape(1, -1), cmp_v[2].reshape(1, -1)], 0).astype(F32)
    w2 = jnp.stack([cmp_k[3], cmp_v[3]], 0).astype(BF16)
    cb = _compress(kcv, pe, w1, b1, w2, nblk)

    biasc, dtiles, wb = _bias_tables(rel_bias, t)
    ovt, et = _sel_tables(t)
    a = _nsa(q, kvs, cb, gates, biasc, dtiles, wb, ovt, et)

    c0c, winc, woutc, al = _s5_tables(*ssm)
    y = _s5(u, c0c, winc, woutc, al, ssm_d)

    wo = w_out.astype(BF16)
    return _glu_out(h1, a.reshape(n, D_ATTN), y.reshape(n, d_ssm), glu_w.astype(BF16), glu_b, wo[:D_ATTN], wo[D_ATTN:])


def kernel(x, ffn1_norm, ffn1_w1, ffn1_w3, ffn1_w2, mix_norm, w_in, cmp_pe_k, cmp_w1_k, cmp_b1_k, cmp_w2_k,
           cmp_pe_v, cmp_w1_v, cmp_b1_v, cmp_w2_v, rel_bias, ssm_lam_re, ssm_lam_im, ssm_log_step, ssm_b_re,
           ssm_b_im, ssm_c_re, ssm_c_im, ssm_d, glu_w, glu_b, w_out, ffn2_norm, ffn2_w1, ffn2_w3, ffn2_w2,
           final_norm):
    b, t, d = x.shape
    depth = ffn1_w1.shape[0]
    h = x.reshape(b * t, d)
    for l in range(depth):
        last = l == depth - 1
        h = _ffn(h, ffn1_norm[l], ffn1_w1[l], ffn1_w3[l], ffn1_w2[l])
        ssm = (ssm_lam_re[l], ssm_lam_im[l], ssm_log_step[l], ssm_b_re[l], ssm_b_im[l], ssm_c_re[l], ssm_c_im[l])
        h = _mixers(h, mix_norm[l], w_in[l],
                    (cmp_pe_k[l], cmp_w1_k[l], cmp_b1_k[l], cmp_w2_k[l]),
                    (cmp_pe_v[l], cmp_w1_v[l], cmp_b1_v[l], cmp_w2_v[l]),
                    rel_bias, ssm, ssm_d[l], glu_w[l], glu_b[l], w_out[l], b, t)
        h = _ffn(h, ffn2_norm[l], ffn2_w1[l], ffn2_w3[l], ffn2_w2[l], final_gain=final_norm if last else None)
    return h.reshape(b, t, d)
```

```python
import functools
import math

import jax
import jax.numpy as jnp
from jax import lax
from jax.experimental import pallas as pl
from jax.experimental.pallas import tpu as pltpu

F32 = jnp.float32
BF16 = jnp.bfloat16

HEAD_DIM = 128
N_KV = 2
HEADS_PER_KV = 4
N_HEADS = N_KV * HEADS_PER_KV
D_ATTN = N_HEADS * HEAD_DIM
D_KV = N_KV * HEAD_DIM
CMP_LEN = 32
CMP_STRIDE = 16
SEL_BLOCK = 64
N_SELECT = 16
WINDOW = 512
N_BUCKETS = 32
MAX_DISTANCE = 128
SSM_GROUP = 16
SSM_STATE = 64
EPS = 1e-6
NEG = -1e30
LOG2E = math.log2(math.e)

LANES = 128
VMEM_LIMIT_BYTES = 56 * 1024 * 1024
NSA_VMEM_LIMIT_BYTES = 60 * 1024 * 1024
FFN_VMEM_LIMIT_BYTES = 63 * 1024 * 1024
TQ = 256
QK = TQ // LANES
ROWS4 = HEADS_PER_KV * TQ
SEL_CHUNK = 512
SSM_CHUNK = 16


def _cparams(sem, vmem_limit_bytes=VMEM_LIMIT_BYTES):
    return pltpu.CompilerParams(dimension_semantics=sem, vmem_limit_bytes=vmem_limit_bytes)


def _ffn_body(x_ref, g_ref, w1_ref, w3_ref, w2_ref, *rest, final):
    if final:
        fg_ref, o_ref, xn_ref = rest
    else:
        o_ref, xn_ref = rest
    j = pl.program_id(1)
    nj = pl.num_programs(1)

    def contribution(xn):
        a = jnp.dot(xn, w1_ref[...], preferred_element_type=F32)
        b = jnp.dot(xn, w3_ref[...], preferred_element_type=F32)
        gated = (a * jax.nn.sigmoid(a)) * b
        return jnp.dot(gated.astype(BF16), w2_ref[...], preferred_element_type=F32)

    @pl.when(j == 0)
    def _():
        x = x_ref[...]
        ms = jnp.mean(x * x, axis=-1, keepdims=True)
        xn = (x * lax.rsqrt(ms + EPS) * g_ref[...]).astype(BF16)
        xn_ref[...] = xn
        o_ref[...] = contribution(xn)

    @pl.when((j > 0) & (j < nj - 1))
    def _():
        o_ref[...] += contribution(xn_ref[...])

    @pl.when(j == nj - 1)
    def _():
        h = x_ref[...] + 0.5 * (o_ref[...] + contribution(xn_ref[...]))
        if final:
            ms = jnp.mean(h * h, axis=-1, keepdims=True)
            h = h * lax.rsqrt(ms + EPS) * fg_ref[...]
        o_ref[...] = h


def _ffn(x, gain, w1, w3, w2, final_gain=None, *, tm=1024, tf=512):
    n, d = x.shape
    dff = w1.shape[1]
    nj = dff // tf
    assert nj >= 2 and nj * tf == dff
    final = final_gain is not None
    in_specs = [
        pl.BlockSpec((tm, d), lambda i, j: (i, 0)),
        pl.BlockSpec((1, d), lambda i, j: (0, 0)),
        pl.BlockSpec((d, tf), lambda i, j: (0, j)),
        pl.BlockSpec((d, tf), lambda i, j: (0, j)),
        pl.BlockSpec((tf, d), lambda i, j: (j, 0)),
    ]
    args = [x, gain.reshape(1, d), w1.astype(BF16), w3.astype(BF16), w2.astype(BF16)]
    if final:
        in_specs.append(pl.BlockSpec((1, d), lambda i, j: (0, 0)))
        args.append(final_gain.reshape(1, d))
    return pl.pallas_call(
        functools.partial(_ffn_body, final=final),
        grid=(n // tm, nj),
        in_specs=in_specs,
        out_specs=pl.BlockSpec((tm, d), lambda i, j: (i, 0)),
        out_shape=jax.ShapeDtypeStruct((n, d), F32),
        scratch_shapes=[pltpu.VMEM((tm, d), BF16)],
        compiler_params=_cparams(("parallel", "arbitrary"), FFN_VMEM_LIMIT_BYTES),
        name="ffn_final" if final else "ffn",
    )(*args)


def _inproj_body(x_ref, g_ref, w_ref, q_ref, kvs_ref, kcv_ref, u_ref, gate_ref):
    i = pl.program_id(1)

    @pl.when(i == 0)
    def _():
        kvs_ref[...] = jnp.zeros(kvs_ref.shape, kvs_ref.dtype)
        kcv_ref[...] = jnp.zeros(kcv_ref.shape, kcv_ref.dtype)

    @pl.when(i > 0)
    def _():
        x = x_ref[...]
        ms = jnp.mean(x * x, axis=-1, keepdims=True)
        xn = (x * lax.rsqrt(ms + EPS) * g_ref[...]).astype(BF16)
        c0 = 0
        c1 = q_ref.shape[1]
        q = jnp.dot(xn, w_ref[:, c0:c1], preferred_element_type=F32)
        q_ref[...] = (q * (HEAD_DIM ** -0.5 * LOG2E)).astype(BF16)
        c0, c1 = c1, c1 + kcv_ref.shape[1]
        kcv_ref[...] = jnp.dot(xn, w_ref[:, c0:c1], preferred_element_type=F32)
        c0, c1 = c1, c1 + kvs_ref.shape[1]
        kvs_ref[...] = jnp.dot(xn, w_ref[:, c0:c1], preferred_element_type=F32).astype(BF16)
        c0, c1 = c1, c1 + u_ref.shape[1]
        u_ref[...] = jnp.dot(xn, w_ref[:, c0:c1], preferred_element_type=F32)
        gate_ref[...] = jax.nn.sigmoid(jnp.dot(xn, w_ref[:, c1:], preferred_element_type=F32))


def _inproj(h, gain, w_perm, d_ssm):
    b, t, d = h.shape
    tm = WINDOW
    nt = t // tm
    nq, ncv, nvs, ng = D_ATTN, 2 * D_KV, 4 * D_KV, N_KV * LANES
    ncol = w_perm.shape[1]
    assert ncol == nq + ncv + nvs + d_ssm + ng and t % tm == 0
    data = lambda bi, i: (bi, jnp.maximum(i - 1, 0), 0)
    return pl.pallas_call(
        _inproj_body,
        grid=(b, nt + 1),
        in_specs=[
            pl.BlockSpec((None, tm, d), data),
            pl.BlockSpec((1, d), lambda bi, i: (0, 0)),
            pl.BlockSpec((d, ncol), lambda bi, i: (0, 0)),
        ],
        out_specs=[
            pl.BlockSpec((None, tm, nq), data),
            pl.BlockSpec((None, tm, nvs), lambda bi, i: (bi, i, 0)),
            pl.BlockSpec((None, tm, ncv), lambda bi, i: (bi, jnp.where(i == 0, nt, i - 1), 0)),
            pl.BlockSpec((None, tm, d_ssm), data),
            pl.BlockSpec((None, tm, ng), data),
        ],
        out_shape=[
            jax.ShapeDtypeStruct((b, t, nq), BF16),
            jax.ShapeDtypeStruct((b, t + tm, nvs), BF16),
            jax.ShapeDtypeStruct((b, t + tm, ncv), F32),
            jax.ShapeDtypeStruct((b, t, d_ssm), F32),
            jax.ShapeDtypeStruct((b, t, ng), F32),
        ],
        compiler_params=_cparams(("parallel", "arbitrary")),
        name="inproj",
    )(h, gain.reshape(1, d), w_perm)


def _compress_body(x_ref, pe_ref, w1_ref, b1_ref, w2_ref, o_ref):
    nb, nblk = o_ref.shape[0], o_ref.shape[1]
    acc = jnp.broadcast_to(b1_ref[...], (nb * nblk, b1_ref.shape[1])).astype(F32)
    for l in range(CMP_LEN):
        xl = x_ref[:, pl.ds(l, nblk, stride=CMP_STRIDE), :] + pe_ref[l:l + 1, :]
        acc = acc + jnp.dot(xl.reshape(nb * nblk, xl.shape[2]).astype(BF16), w1_ref[l], preferred_element_type=F32)
    hid = jax.nn.gelu(acc, approximate=True)
    out = jnp.dot(hid.astype(BF16), w2_ref[...], preferred_element_type=F32)
    o_ref[...] = out.reshape(o_ref.shape).astype(BF16)


def _compress(kcv, pe, w1, b1, w2, nblk, *, nb=4):
    b, tpad, _ = kcv.shape
    nb = min(nb, b)
    hid, dh = w1.shape[3], w2.shape[2]
    assert (nblk - 1) * CMP_STRIDE + CMP_LEN <= tpad
    return pl.pallas_call(
        _compress_body,
        grid=(2 * N_KV, b // nb),
        in_specs=[
            pl.BlockSpec((nb, tpad, dh), lambda s, i: (i, 0, s)),
            pl.BlockSpec((None, CMP_LEN, dh), lambda s, i: (s // N_KV, 0, 0)),
            pl.BlockSpec((None, CMP_LEN, dh, hid), lambda s, i: (s // N_KV, 0, 0, 0)),
            pl.BlockSpec((None, 1, hid), lambda s, i: (s // N_KV, 0, 0)),
            pl.BlockSpec((None, hid, dh), lambda s, i: (s // N_KV, 0, 0)),
        ],
        out_specs=pl.BlockSpec((None, None, nb, nblk, dh), lambda s, i: (s // N_KV, s % N_KV, i, 0, 0)),
        out_shape=jax.ShapeDtypeStruct((2, N_KV, b, nblk, dh), BF16),
        compiler_params=_cparams(("parallel", "parallel")),
        name="compress",
    )(kcv, pe, w1, b1, w2)


def _nt_dot(a, b):
    return lax.dot_general(a, b, (((1,), (1,)), ((), ())), preferred_element_type=F32)


def _nsa_body(q_ref, ks_ref, vs_ref, kw_ref, vw_ref, kcb_ref, vcb_ref, gate_ref, biasc_ref, dt_ref, wb_ref,
              ovt_ref, et_ref, o_ref, kse_ref, s_ref, m_ref, l_ref, acc_ref, acc2_ref):
    i = pl.program_id(1)
    t0 = i * TQ
    t = et_ref.shape[0]
    ns = t // SEL_BLOCK
    nsel = min(N_SELECT, ns)
    groups = range(N_KV)
    gl = lambda g: slice(g * HEAD_DIM, (g + 1) * HEAD_DIM)

    @pl.when(i == 0)
    def _():
        for g in groups:
            kse_ref[g, :, 0:HEAD_DIM] = ks_ref[WINDOW:WINDOW + t, gl(g)]
            kse_ref[g, :, HEAD_DIM:] = et_ref[...]

    q = q_ref[...]
    w0 = pl.multiple_of(t0, LANES)
    span = wb_ref.shape[2]

    q4 = [jnp.concatenate([q[:, (g * HEADS_PER_KV + h) * HEAD_DIM:(g * HEADS_PER_KV + h + 1) * HEAD_DIM]
                           for h in range(HEADS_PER_KV)], axis=0) for g in groups]

    sc = [_nt_dot(q4[g], kcb_ref[g]) + biasc_ref[g] for g in groups]
    row_t = t0 + (lax.broadcasted_iota(jnp.int32, (ROWS4, LANES), 0) & (TQ - 1))
    col_c = lax.broadcasted_iota(jnp.int32, (ROWS4, LANES), 1)
    valid_c = (row_t - col_c * CMP_STRIDE - (CMP_LEN - 1) >= 0) & (col_c < LANES - 1)
    mc = [jnp.max(s, axis=-1, keepdims=True) for s in sc]
    pc = [jnp.where(valid_c, jnp.exp2(s - m), 0.0) for s, m in zip(sc, mc)]
    pc = [p / jnp.maximum(jnp.sum(p, axis=-1, keepdims=True), 1e-30) for p in pc]
    pcb = [p.astype(BF16) for p in pc]
    o_cmps = [jnp.dot(pcb[g], vcb_ref[g], preferred_element_type=F32) for g in groups]
    pimp = [_nt_dot(ovt_ref[...], p) for p in pcb]
    imp = [sum(p[0:ns, h * TQ:(h + 1) * TQ] for h in range(1, HEADS_PER_KV)) + p[0:ns, 0:TQ] for p in pimp]

    jrow = lax.broadcasted_iota(jnp.int32, (ns, TQ), 0)
    tpos = t0 + lax.broadcasted_iota(jnp.int32, (ns, TQ), 1)
    cur = lax.shift_right_logical(tpos, int(math.log2(SEL_BLOCK)))
    forced = (jrow == 0) | (jrow == cur) | (jrow == cur - 1)
    impm = [jnp.where(forced, 1e6, jnp.where(jrow * SEL_BLOCK <= tpos, x, -1e9)) for x in imp]
    nslab = ns // 8
    slabs = [[x[8 * v:8 * v + 8] for v in range(nslab)] for x in impm]
    cnts = [[jnp.zeros((8, TQ), F32) for _ in range(nslab)] for _ in groups]
    sub = lax.broadcasted_iota(jnp.int32, (8, TQ), 0)
    for jp in range(ns):
        v0, r0 = divmod(jp, 8)
        for g in groups:
            row = jnp.broadcast_to(slabs[g][v0][r0:r0 + 1, :], (8, TQ))
            for v in range(nslab):
                if v > v0:
                    beats = row >= slabs[g][v]
                elif v < v0:
                    beats = row > slabs[g][v]
                else:
                    beats = (row > slabs[g][v]) | ((row == slabs[g][v]) & (sub > r0))
                cnts[g][v] = cnts[g][v] + jnp.where(beats, 1.0, 0.0)
    q4s = []
    for g in groups:
        negt = [jnp.where(c < nsel, 0.0, NEG) for c in cnts[g]]
        negt = jnp.concatenate(negt + [jnp.zeros((LANES - ns, TQ), F32)], axis=0)
        neg = negt.T.astype(BF16)
        q4s.append(jnp.concatenate([q4[g], jnp.concatenate([neg] * HEADS_PER_KV, axis=0)], axis=1))

    nsub = SEL_CHUNK // LANES
    nchunks = ((i + 1) * QK - 1) // nsub + 1
    m_ref[...] = jnp.full(m_ref.shape, -3e38, F32)

    def score_chunk(ck, _):
        kb = pl.multiple_of(ck * SEL_CHUNK, SEL_CHUNK)
        s = [_nt_dot(q4s[g], kse_ref[g, pl.ds(kb, SEL_CHUNK), :]) for g in groups]
        for g in groups:
            mloc = None
            for j in range(nsub):
                sj = s[g][:, j * LANES:(j + 1) * LANES] + dt_ref[g, jnp.clip(ck * nsub + j - i * QK + 2, 0, QK + 2)]
                s_ref[g, ck, :, j * LANES:(j + 1) * LANES] = sj
                mloc = sj if mloc is None else jnp.maximum(mloc, sj)
            m_ref[g] = jnp.maximum(m_ref[g], mloc)
        return 0

    lax.fori_loop(0, nchunks, score_chunk, 0)
    for g in groups:
        m_ref[g] = jnp.broadcast_to(jnp.max(m_ref[g], axis=-1, keepdims=True), (ROWS4, LANES))
    l_ref[...] = jnp.zeros(l_ref.shape, F32)
    acc_ref[...] = jnp.zeros(acc_ref.shape, F32)

    kneg = jnp.where(t0 - WINDOW + lax.broadcasted_iota(jnp.int32, (1, span), 1) >= 0, 0.0, NEG)
    sw = [_nt_dot(q4[g], kw_ref[pl.ds(w0, span), gl(g)]) + wb_ref[g] + kneg for g in groups]
    mw = [jnp.max(s, axis=-1, keepdims=True) for s in sw]
    pw = [jnp.exp2(s - m) for s, m in zip(sw, mw)]
    lw = [jnp.sum(p, axis=-1, keepdims=True) for p in pw]
    o_wins = [jnp.dot(pw[g].astype(BF16), vw_ref[pl.ds(w0, span), gl(g)], preferred_element_type=F32) for g in groups]
    o_wins = [o / jnp.maximum(l, 1e-30) for o, l in zip(o_wins, lw)]
    gts = [gate_ref[:, g * LANES:(g + 1) * LANES] for g in groups]
    for h in range(HEADS_PER_KV):
        r = slice(h * TQ, (h + 1) * TQ)
        for g in groups:
            acc2_ref[g, r, :] = gts[g][:, 3 * h:3 * h + 1] * o_cmps[g][r] + gts[g][:, 3 * h + 2:3 * h + 3] * o_wins[g][r]

    def pv_chunk(ck, _):
        kb = pl.multiple_of(WINDOW + ck * SEL_CHUNK, SEL_CHUNK)
        ps = [[jnp.exp2(s_ref[g, ck, :, j * LANES:(j + 1) * LANES] - m_ref[g]) for j in range(nsub)] for g in groups]
        for g in groups:
            lsum = ps[g][0]
            for pj in ps[g][1:]:
                lsum = lsum + pj
            l_ref[g] += lsum
            p = jnp.concatenate([pj.astype(BF16) for pj in ps[g]], axis=1)
            acc_ref[g] += jnp.dot(p, vs_ref[pl.ds(kb, SEL_CHUNK), gl(g)], preferred_element_type=F32)
        return 0

    lax.fori_loop(0, nchunks, pv_chunk, 0)

    ls = [jnp.maximum(jnp.sum(l_ref[g], axis=-1, keepdims=True), 1e-30) for g in groups]
    o_sels = [acc_ref[g] / ls[g] for g in groups]
    for h in range(HEADS_PER_KV):
        r = slice(h * TQ, (h + 1) * TQ)
        for g in groups:
            o_h = acc2_ref[g, r, :] + gate_ref[:, g * LANES + 3 * h + 1:g * LANES + 3 * h + 2] * o_sels[g][r]
            c0 = (g * HEADS_PER_KV + h) * HEAD_DIM
            o_ref[:, c0:c0 + HEAD_DIM] = o_h.astype(BF16)


def _nsa(q, kvs, cb, gates, biasc, dtiles, wb, ovt, et):
    b, t, _ = q.shape
    nt = t // TQ
    tp = kvs.shape[1]
    span = wb.shape[2]
    gw = HEADS_PER_KV * HEAD_DIM
    assert tp == t + WINDOW and span == TQ + WINDOW and t % SEL_CHUNK == 0 and (t // SEL_BLOCK) % 8 == 0

    def stream(j):
        return pl.BlockSpec((None, tp, N_KV * HEAD_DIM), lambda bi, i: (bi, 0, j))

    return pl.pallas_call(
        _nsa_body,
        grid=(b, nt),
        in_specs=[
            pl.BlockSpec((None, TQ, N_KV * gw), lambda bi, i: (bi, i, 0)),
            stream(0), stream(1), stream(2), stream(3),
            pl.BlockSpec((None, N_KV, None, LANES, HEAD_DIM), lambda bi, i: (0, 0, bi, 0, 0)),
            pl.BlockSpec((None, N_KV, None, LANES, HEAD_DIM), lambda bi, i: (1, 0, bi, 0, 0)),
            pl.BlockSpec((None, TQ, N_KV * LANES), lambda bi, i: (bi, i, 0)),
            pl.BlockSpec((N_KV, None, ROWS4, LANES), lambda bi, i: (0, i, 0, 0)),
            pl.BlockSpec((N_KV, QK + 3, ROWS4, LANES), lambda bi, i: (0, 0, 0, 0)),
            pl.BlockSpec((N_KV, ROWS4, span), lambda bi, i: (0, 0, 0)),
            pl.BlockSpec((LANES, LANES), lambda bi, i: (0, 0)),
            pl.BlockSpec((t, LANES), lambda bi, i: (0, 0)),
        ],
        out_specs=pl.BlockSpec((None, TQ, N_KV * gw), lambda bi, i: (bi, i, 0)),
        out_shape=jax.ShapeDtypeStruct((b, t, N_KV * gw), BF16),
        scratch_shapes=[
            pltpu.VMEM((N_KV, t, 2 * HEAD_DIM), BF16),
            pltpu.VMEM((N_KV, t // SEL_CHUNK, ROWS4, SEL_CHUNK), F32),
            pltpu.VMEM((N_KV, ROWS4, LANES), F32),
            pltpu.VMEM((N_KV, ROWS4, LANES), F32),
            pltpu.VMEM((N_KV, ROWS4, HEAD_DIM), F32),
            pltpu.VMEM((N_KV, ROWS4, HEAD_DIM), F32),
        ],
        compiler_params=_cparams(("parallel", "arbitrary"), NSA_VMEM_LIMIT_BYTES),
        name="nsa",
    )(q, kvs, kvs, kvs, kvs, cb, cb, gates, biasc, dtiles, wb, ovt, et)


def _s5_body(u_ref, c0_ref, winc_ref, woutc_ref, rep_k_ref, rep_in_ref, rep_out_ref, m_in_ref, m_out_ref,
             al_ref, d_ref, y_ref, ucat_ref, inj_ref, xp_ref, grev_ref, win_ref, wout_ref):
    nb, t, lanes = u_ref.shape
    L = SSM_CHUNK
    nchunk = t // L
    rows = nb * nchunk
    sdim = al_ref.shape[1] // 2

    @pl.when(pl.program_id(1) == 0)
    def _():
        m_in, m_out = m_in_ref[...], m_out_ref[...]
        for half in range(2):
            for s in range(L):
                x = jnp.dot(winc_ref[half, s * lanes:(s + 1) * lanes, :], rep_in_ref[...], preferred_element_type=F32)
                win_ref[s * lanes:(s + 1) * lanes, half * sdim:(half + 1) * sdim] = (x * m_in).astype(BF16)
            for tt in range(L):
                x = jnp.dot(woutc_ref[half], rep_out_ref[:, tt * lanes:(tt + 1) * lanes], preferred_element_type=F32)
                wout_ref[half * sdim:(half + 1) * sdim, tt * lanes:(tt + 1) * lanes] = (x * m_out).astype(BF16)
        c0 = [(jnp.dot(c0_ref[half], rep_k_ref[...], preferred_element_type=F32) * m_out).astype(BF16)
              for half in range(2)]
        kx = jnp.dot(win_ref[...], jnp.concatenate(c0, axis=0), preferred_element_type=F32)
        zero = jnp.zeros((lanes, lanes), BF16)

        def kblk(tau):
            s = L - 1 - tau
            return kx[s * lanes:(s + 1) * lanes].astype(BF16) if 0 <= tau < L else zero

        for r in range(L + 1):
            grev_ref[r * lanes:(r + 1) * lanes, 0:lanes] = kblk(L - r - 1)
            grev_ref[r * lanes:(r + 1) * lanes, lanes:] = kblk(L - r)

    def u_at(s):
        return u_ref[:, pl.ds(s, nchunk, stride=L), :].reshape(rows, lanes)

    for s in range(L):
        ucat_ref[:, s * lanes:(s + 1) * lanes] = u_at(s).astype(BF16)
    inj = jnp.dot(ucat_ref[...], win_ref[...], preferred_element_type=F32)
    npl = sdim // lanes
    for k in range(2 * npl):
        inj_ref[k] = inj[:, k * lanes:(k + 1) * lanes]
    ar = [jnp.broadcast_to(al_ref[0:1, k * lanes:(k + 1) * lanes], (nb, lanes)) for k in range(npl)]
    ai = [jnp.broadcast_to(al_ref[0:1, sdim + k * lanes:sdim + (k + 1) * lanes], (nb, lanes)) for k in range(npl)]

    def step(c, carry):
        xr, xi = carry
        rsel = pl.ds(c, nb, stride=nchunk)
        nr, ni = [], []
        for k in range(npl):
            xp_ref[k, rsel, :] = xr[k]
            xp_ref[npl + k, rsel, :] = xi[k]
            nr.append(ar[k] * xr[k] - ai[k] * xi[k] + inj_ref[k, rsel, :])
            ni.append(ar[k] * xi[k] + ai[k] * xr[k] + inj_ref[npl + k, rsel, :])
        return tuple(nr), tuple(ni)

    z = tuple(jnp.zeros((nb, lanes), F32) for _ in range(npl))
    lax.fori_loop(0, nchunk, step, (z, z), unroll=4)
    xp = jnp.concatenate([xp_ref[k] for k in range(2 * npl)], axis=1).astype(BF16)
    d = d_ref[...]
    for tp in range(0, L, 2):
        res = jnp.dot(ucat_ref[:, 0:(tp + 2) * lanes], grev_ref[(L - 1 - tp) * lanes:(L + 1) * lanes, :],
                      preferred_element_type=F32)
        res = res + jnp.dot(xp, wout_ref[:, tp * lanes:(tp + 2) * lanes], preferred_element_type=F32)
        for k in range(2):
            y = res[:, k * lanes:(k + 1) * lanes] + d * u_at(tp + k)
            y_ref[:, pl.ds(tp + k, nchunk, stride=L), :] = y.reshape(nb, nchunk, lanes)


def _s5(u, c0c, winc, woutc, al, d, *, nb=4):
    b, t, dch = u.shape
    nb = min(nb, b)
    nblk = dch // LANES
    L = SSM_CHUNK
    rows = nb * (t // L)
    hch, p = c0c.shape[3], winc.shape[3]
    gpb = LANES // hch
    sdim = gpb * p
    kcat = L * LANES
    lane, col, st = jnp.arange(LANES), jnp.arange(kcat), jnp.arange(sdim)
    rep_k = (jnp.arange(hch)[:, None] == (lane % hch)[None, :]).astype(BF16)
    rep_in = (jnp.arange(p)[:, None] == (st % p)[None, :]).astype(BF16)
    rep_out = (jnp.arange(L * hch)[:, None] == (col // LANES * hch + col % hch)[None, :]).astype(BF16)
    m_in = ((lane // hch)[:, None] == (st // p)[None, :]).astype(F32)
    m_out = ((st // p)[:, None] == (lane // hch)[None, :]).astype(F32)
    const = lambda a: pl.BlockSpec(a.shape, lambda j, i: (0,) * a.ndim)
    return pl.pallas_call(
        _s5_body,
        grid=(nblk, b // nb),
        in_specs=[
            pl.BlockSpec((nb, t, LANES), lambda j, i: (i, 0, j)),
            pl.BlockSpec((None, 2, sdim, hch), lambda j, i: (j, 0, 0, 0)),
            pl.BlockSpec((None, 2, kcat, p), lambda j, i: (j, 0, 0, 0)),
            pl.BlockSpec((None, 2, sdim, L * hch), lambda j, i: (j, 0, 0, 0)),
            const(rep_k), const(rep_in), const(rep_out), const(m_in), const(m_out),
            pl.BlockSpec((None, 8, 2 * sdim), lambda j, i: (j, 0, 0)),
            pl.BlockSpec((1, LANES), lambda j, i: (0, j)),
        ],
        out_specs=pl.BlockSpec((nb, t, LANES), lambda j, i: (i, 0, j)),
        out_shape=jax.ShapeDtypeStruct((b, t, dch), F32),
        scratch_shapes=[pltpu.VMEM((rows, kcat), BF16), pltpu.VMEM((2 * sdim // LANES, rows, LANES), F32),
                        pltpu.VMEM((2 * sdim // LANES, rows, LANES), F32),
                        pltpu.VMEM(((L + 1) * LANES, 2 * LANES), BF16), pltpu.VMEM((kcat, 2 * sdim), BF16),
                        pltpu.VMEM((2 * sdim, kcat), BF16)],
        compiler_params=_cparams(("parallel", "arbitrary")),
        name="s5",
    )(u, c0c, winc, woutc, rep_k, rep_in, rep_out, m_in, m_out, al, d.reshape(1, dch))


def _glu_out_body(h_ref, a_ref, y_ref, wg_ref, bg_ref, woa_ref, wos_ref, o_ref):
    hg = jax.nn.gelu(y_ref[...], approximate=True)
    z = jnp.dot(hg.astype(BF16), wg_ref[...], preferred_element_type=F32) + bg_ref[...]
    s = hg * jax.nn.sigmoid(z)
    mix = jnp.dot(a_ref[...], woa_ref[...], preferred_element_type=F32)
    mix = mix + jnp.dot(s.astype(BF16), wos_ref[...], preferred_element_type=F32)
    o_ref[...] = h_ref[...] + mix


def _glu_out(h, a, y, wg, bg, woa, wos, *, tm=512):
    n, dm = h.shape
    da, ds = a.shape[1], y.shape[1]
    row = lambda w: pl.BlockSpec((tm, w), lambda i: (i, 0))
    full = lambda r, c: pl.BlockSpec((r, c), lambda i: (0, 0))
    return pl.pallas_call(
        _glu_out_body,
        grid=(n // tm,),
        in_specs=[row(dm), row(da), row(ds), full(ds, ds), full(1, ds), full(da, dm), full(ds, dm)],
        out_specs=row(dm),
        out_shape=jax.ShapeDtypeStruct((n, dm), F32),
        compiler_params=_cparams(("parallel",)),
        name="glu_out",
    )(h, a, y, wg, bg.reshape(1, ds), woa, wos)


def _t5_bucket(dist):
    n = jnp.maximum(dist, 0)
    max_exact = N_BUCKETS // 2
    nf = jnp.maximum(n, 1).astype(F32)
    large = max_exact + (jnp.log(nf / max_exact) / math.log(MAX_DISTANCE / max_exact)
                         * (N_BUCKETS - max_exact)).astype(jnp.int32)
    large = jnp.minimum(large, N_BUCKETS - 1)
    return jnp.where(n < max_exact, n, large)


def _t5_body(first_ref, table_ref, bc_ref, dt_ref, wb_ref, *, nc):
    g = pl.program_id(0)
    nt = bc_ref.shape[0]
    span = wb_ref.shape[1]
    a = lax.broadcasted_iota(jnp.int32, (TQ, LANES), 0)
    c = lax.broadcasted_iota(jnp.int32, (TQ, LANES), 1)
    heads = [g * HEADS_PER_KV + h for h in range(HEADS_PER_KV)]

    def lookup(dist, valid, store):
        bias = [jnp.full((TQ, LANES), table_ref[0, hd] * LOG2E, F32) for hd in heads]
        for k in range(1, N_BUCKETS):
            ge = dist >= first_ref[k]
            bias = [jnp.where(ge, table_ref[k, hd] * LOG2E, bv) for hd, bv in zip(heads, bias)]
        for h, bv in enumerate(bias):
            store(slice(h * TQ, (h + 1) * TQ), bv if valid is None else jnp.where(valid, bv, NEG))

    def put_far(rows, tile):
        dt_ref[0, rows, :] = tile

    lookup(jnp.full((TQ, LANES), 2 * MAX_DISTANCE, jnp.int32), None, put_far)
    for r in range(-1, QK):
        def put_near(rows, tile, r=r):
            dt_ref[r + 2, rows, :] = tile

        dist = a - r * LANES - c
        lookup(dist, dist >= 0, put_near)
    dt_ref[QK + 2] = jnp.full((ROWS4, LANES), NEG, F32)
    for j in range(span // LANES):
        def put_wb(rows, tile, j=j):
            wb_ref[rows, j * LANES:(j + 1) * LANES] = tile

        dist = a + WINDOW - (c + j * LANES)
        lookup(dist, (dist >= 0) & (dist < WINDOW), put_wb)

    def cmp_tile(i, _):
        def put_bc(rows, tile):
            bc_ref[i, rows, :] = tile

        dist = i * TQ + a - (c * CMP_STRIDE + CMP_LEN - 1)
        lookup(dist, (dist >= 0) & (c < nc), put_bc)
        return 0

    lax.fori_loop(0, nt, cmp_tile, 0)


def _bias_tables(rel_bias, t):
    nt = t // TQ
    span = TQ + WINDOW
    nc = (t - CMP_LEN) // CMP_STRIDE + 1
    buckets = _t5_bucket(jnp.arange(2 * MAX_DISTANCE))
    first = jnp.sum((buckets[None, :] < jnp.arange(N_BUCKETS)[:, None]).astype(jnp.int32), axis=1)
    smem = pl.BlockSpec(memory_space=pltpu.SMEM)
    return pl.pallas_call(
        functools.partial(_t5_body, nc=nc),
        grid=(N_KV,),
        in_specs=[smem, smem],
        out_specs=[
            pl.BlockSpec((None, nt, ROWS4, LANES), lambda g: (g, 0, 0, 0)),
            pl.BlockSpec((None, QK + 3, ROWS4, LANES), lambda g: (g, 0, 0, 0)),
            pl.BlockSpec((None, ROWS4, span), lambda g: (g, 0, 0)),
        ],
        out_shape=[
            jax.ShapeDtypeStruct((N_KV, nt, ROWS4, LANES), F32),
            jax.ShapeDtypeStruct((N_KV, QK + 3, ROWS4, LANES), F32),
            jax.ShapeDtypeStruct((N_KV, ROWS4, span), F32),
        ],
        compiler_params=_cparams(("parallel",)),
        name="t5_tables",
    )(first, rel_bias.astype(F32))


def _sel_tables(t):
    ns = t // SEL_BLOCK
    nc_pad = LANES
    c_start = jnp.arange(nc_pad) * CMP_STRIDE
    j_start = jnp.arange(LANES) * SEL_BLOCK
    ov = jnp.clip(jnp.minimum(c_start[:, None] + CMP_LEN, j_start[None, :] + SEL_BLOCK)
                  - jnp.maximum(c_start[:, None], j_start[None, :]), 0, None).astype(F32) / CMP_LEN
    ov = jnp.where(jnp.arange(LANES)[None, :] < ns, ov, 0.0)
    et = jnp.arange(t)[:, None] // SEL_BLOCK == jnp.arange(LANES)[None, :]
    return ov.T.astype(BF16), et.astype(BF16)


def _s5_tables(lam_re, lam_im, log_step, b_re, b_im, c_re, c_im):
    ng, p = lam_re.shape
    hch = b_re.shape[2]
    L = SSM_CHUNK
    step = jnp.exp(log_step.astype(F32))[:, None]
    lre, lim = lam_re.astype(F32), lam_im.astype(F32)
    mag = jnp.exp(lre * step)
    ab_re, ab_im = mag * jnp.cos(lim * step), mag * jnp.sin(lim * step)
    nr, ni = ab_re - 1.0, ab_im
    den = lre * lre + lim * lim
    f_re, f_im = (nr * lre + ni * lim) / den, (ni * lre - nr * lim) / den
    br, bim = b_re.astype(F32), b_im.astype(F32)
    bb_re = f_re[..., None] * br - f_im[..., None] * bim
    bb_im = f_re[..., None] * bim + f_im[..., None] * br
    cr, ci = c_re.astype(F32), c_im.astype(F32)
    pr, pi = [jnp.ones_like(ab_re)], [jnp.zeros_like(ab_re)]
    for _ in range(L):
        pr, pi = pr + [pr[-1] * ab_re - pi[-1] * ab_im], pi + [pr[-1] * ab_im + pi[-1] * ab_re]
    pw_re, pw_im = jnp.stack(pr, 0), jnp.stack(pi, 0)
    cp_re = cr[None] * pw_re[:, :, None, :] - ci[None] * pw_im[:, :, None, :]
    cp_im = -(cr[None] * pw_im[:, :, None, :] + ci[None] * pw_re[:, :, None, :])
    gpb = LANES // hch
    nblk = ng // gpb
    wr = pw_re[L - 1 - jnp.arange(L)]
    wi = pw_im[L - 1 - jnp.arange(L)]
    win_re = (wr[..., None] * bb_re[None] - wi[..., None] * bb_im[None])
    win_im = (wr[..., None] * bb_im[None] + wi[..., None] * bb_re[None])
    inj_op = lambda m: m.reshape(L, nblk, gpb, p, hch).transpose(1, 0, 2, 4, 3).reshape(nblk, L * LANES, p)
    winc = jnp.stack([inj_op(win_re), inj_op(win_im)], axis=1)
    read_op = lambda m: m.reshape(L, nblk, gpb, hch, p).transpose(1, 2, 4, 0, 3).reshape(nblk, gpb * p, L * hch)
    woutc = jnp.stack([read_op(cp_re[1:]), read_op(cp_im[1:])], axis=1)
    read0 = lambda m: m.reshape(nblk, gpb, hch, p).transpose(0, 1, 3, 2).reshape(nblk, gpb * p, hch)
    c0c = jnp.stack([read0(cp_re[0]), read0(cp_im[0])], axis=1)
    al = jnp.concatenate([pw_re[L].reshape(nblk, gpb * p), pw_im[L].reshape(nblk, gpb * p)], axis=1)
    al = jnp.broadcast_to(al[:, None, :], (nblk, 8, 2 * gpb * p))
    return c0c.astype(BF16), winc.astype(BF16), woutc.astype(BF16), al


def _mixers(h1, mix_norm, w_in, cmp_k, cmp_v, rel_bias, ssm, ssm_d, glu_w, glu_b, w_out, b, t):
    n, d = h1.shape
    d_ssm = glu_w.shape[0]
    o_g, o_u = D_ATTN + 6 * D_KV, D_ATTN + 6 * D_KV + 3 * N_HEADS
    gcols = 3 * HEADS_PER_KV
    wg = [jnp.pad(w_in[:, o_g + g * gcols:o_g + (g + 1) * gcols], ((0, 0), (0, LANES - gcols))) for g in range(N_KV)]
    w_perm = jnp.concatenate([w_in[:, :o_g], w_in[:, o_u:]] + wg, axis=1).astype(BF16)
    q, kvs, kcv, u, gates = _inproj(h1.reshape(b, t, d), mix_norm, w_perm, d_ssm)

    nblk = t // CMP_STRIDE
    assert nblk == LANES, "compressed-block axis is laid out on one 128-lane tile"
    pe = jnp.stack([cmp_k[0], cmp_v[0]], 0).astype(F32)
    w1 = jnp.stack([cmp_k[1], cmp_v[1]], 0).astype(BF16).reshape(2, CMP_LEN, HEAD_DIM, -1)
    b1 = jnp.stack([cmp_k[2].reshape(1, -1), cmp_v[2].reshape(1, -1)], 0).astype(F32)
    w2 = jnp.stack([cmp_k[3], cmp_v[3]], 0).astype(BF16)
    cb = _compress(kcv, pe, w1, b1, w2, nblk)

    biasc, dtiles, wb = _bias_tables(rel_bias, t)
    ovt, et = _sel_tables(t)
    a = _nsa(q, kvs, cb, gates, biasc, dtiles, wb, ovt, et)

    c0c, winc, woutc, al = _s5_tables(*ssm)
    y = _s5(u, c0c, winc, woutc, al, ssm_d)

    wo = w_out.astype(BF16)
    return _glu_out(h1, a.reshape(n, D_ATTN), y.reshape(n, d_ssm), glu_w.astype(BF16), glu_b, wo[:D_ATTN], wo[D_ATTN:])


def kernel(x, ffn1_norm, ffn1_w1, ffn1_w3, ffn1_w2, mix_norm, w_in, cmp_pe_k, cmp_w1_k, cmp_b1_k, cmp_w2_k,
           cmp_pe_v, cmp_w1_v, cmp_b1_v, cmp_w2_v, rel_bias, ssm_lam_re, ssm_lam_im, ssm_log_step, ssm_b_re,
           ssm_b_im, ssm_c_re, ssm_c_im, ssm_d, glu_w, glu_b, w_out, ffn2_norm, ffn2_w1, ffn2_w3, ffn2_w2,
           final_norm):
    b, t, d = x.shape
    depth = ffn1_w1.shape[0]
    h = x.reshape(b * t, d)
    for l in range(depth):
        last = l == depth - 1
        h = _ffn(h, ffn1_norm[l], ffn1_w1[l], ffn1_w3[l], ffn1_w2[l])
        ssm = (ssm_lam_re[l], ssm_lam_im[l], ssm_log_step[l], ssm_b_re[l], ssm_b_im[l], ssm_c_re[l], ssm_c_im[l])
        h = _mixers(h, mix_norm[l], w_in[l],
                    (cmp_pe_k[l], cmp_w1_k[l], cmp_b1_k[l], cmp_w2_k[l]),
                    (cmp_pe_v[l], cmp_w1_v[l], cmp_b1_v[l], cmp_w2_v[l]),
                    rel_bias, ssm, ssm_d[l], glu_w[l], glu_b[l], w_out[l], b, t)
        h = _ffn(h, ffn2_norm[l], ffn2_w1[l], ffn2_w3[l], ffn2_w2[l], final_gain=final_norm if last else None)
    return h.reshape(b, t, d)
```

```python
import functools
import math

import jax
import jax.numpy as jnp
from jax import lax
from jax.experimental import pallas as pl
from jax.experimental.pallas import tpu as pltpu

F32 = jnp.float32
BF16 = jnp.bfloat16

HEAD_DIM = 128
N_KV = 2
HEADS_PER_KV = 4
N_HEADS = N_KV * HEADS_PER_KV
D_ATTN = N_HEADS * HEAD_DIM
D_KV = N_KV * HEAD_DIM
CMP_LEN = 32
CMP_STRIDE = 16
SEL_BLOCK = 64
N_SELECT = 16
WINDOW = 512
N_BUCKETS = 32
MAX_DISTANCE = 128
SSM_GROUP = 16
SSM_STATE = 64
EPS = 1e-6
NEG = -1e30
LOG2E = math.log2(math.e)

LANES = 128
VMEM_LIMIT_BYTES = 56 * 1024 * 1024
NSA_VMEM_LIMIT_BYTES = 60 * 1024 * 1024
FFN_VMEM_LIMIT_BYTES = 63 * 1024 * 1024
TQ = 256
QK = TQ // LANES
ROWS4 = HEADS_PER_KV * TQ
SEL_CHUNK = 512
SSM_CHUNK = 16


def _cparams(sem, vmem_limit_bytes=VMEM_LIMIT_BYTES):
    return pltpu.CompilerParams(dimension_semantics=sem, vmem_limit_bytes=vmem_limit_bytes)


def _ffn_body(x_ref, g_ref, w1_ref, w3_ref, w2_ref, *rest, final):
    if final:
        fg_ref, o_ref, xn_ref = rest
    else:
        o_ref, xn_ref = rest
    j = pl.program_id(1)
    nj = pl.num_programs(1)

    def contribution(xn):
        a = jnp.dot(xn, w1_ref[...], preferred_element_type=F32)
        b = jnp.dot(xn, w3_ref[...], preferred_element_type=F32)
        gated = (a * jax.nn.sigmoid(a)) * b
        return jnp.dot(gated.astype(BF16), w2_ref[...], preferred_element_type=F32)

    @pl.when(j == 0)
    def _():
        x = x_ref[...]
        ms = jnp.mean(x * x, axis=-1, keepdims=True)
        xn = (x * lax.rsqrt(ms + EPS) * g_ref[...]).astype(BF16)
        xn_ref[...] = xn
        o_ref[...] = contribution(xn)

    @pl.when((j > 0) & (j < nj - 1))
    def _():
        o_ref[...] += contribution(xn_ref[...])

    @pl.when(j == nj - 1)
    def _():
        h = x_ref[...] + 0.5 * (o_ref[...] + contribution(xn_ref[...]))
        if final:
            ms = jnp.mean(h * h, axis=-1, keepdims=True)
            h = h * lax.rsqrt(ms + EPS) * fg_ref[...]
        o_ref[...] = h


def _ffn(x, gain, w1, w3, w2, final_gain=None, *, tm=1024, tf=512):
    n, d = x.shape
    dff = w1.shape[1]
    nj = dff // tf
    assert nj >= 2 and nj * tf == dff
    final = final_gain is not None
    in_specs = [
        pl.BlockSpec((tm, d), lambda i, j: (i, 0)),
        pl.BlockSpec((1, d), lambda i, j: (0, 0)),
        pl.BlockSpec((d, tf), lambda i, j: (0, j)),
        pl.BlockSpec((d, tf), lambda i, j: (0, j)),
        pl.BlockSpec((tf, d), lambda i, j: (j, 0)),
    ]
    args = [x, gain.reshape(1, d), w1.astype(BF16), w3.astype(BF16), w2.astype(BF16)]
    if final:
        in_specs.append(pl.BlockSpec((1, d), lambda i, j: (0, 0)))
        args.append(final_gain.reshape(1, d))
    return pl.pallas_call(
        functools.partial(_ffn_body, final=final),
        grid=(n // tm, nj),
        in_specs=in_specs,
        out_specs=pl.BlockSpec((tm, d), lambda i, j: (i, 0)),
        out_shape=jax.ShapeDtypeStruct((n, d), F32),
        scratch_shapes=[pltpu.VMEM((tm, d), BF16)],
        compiler_params=_cparams(("parallel", "arbitrary"), FFN_VMEM_LIMIT_BYTES),
        name="ffn_final" if final else "ffn",
    )(*args)


def _inproj_body(x_ref, g_ref, w_ref, q_ref, kvs_ref, kcv_ref, u_ref, gate_ref):
    i = pl.program_id(1)

    @pl.when(i == 0)
    def _():
        kvs_ref[...] = jnp.zeros(kvs_ref.shape, kvs_ref.dtype)
        kcv_ref[...] = jnp.zeros(kcv_ref.shape, kcv_ref.dtype)

    @pl.when(i > 0)
    def _():
        x = x_ref[...]
        ms = jnp.mean(x * x, axis=-1, keepdims=True)
        xn = (x * lax.rsqrt(ms + EPS) * g_ref[...]).astype(BF16)
        c0 = 0
        c1 = q_ref.shape[1]
        q = jnp.dot(xn, w_ref[:, c0:c1], preferred_element_type=F32)
        q_ref[...] = (q * (HEAD_DIM ** -0.5 * LOG2E)).astype(BF16)
        c0, c1 = c1, c1 + kcv_ref.shape[1]
        kcv_ref[...] = jnp.dot(xn, w_ref[:, c0:c1], preferred_element_type=F32)
        c0, c1 = c1, c1 + kvs_ref.shape[1]
        kvs_ref[...] = jnp.dot(xn, w_ref[:, c0:c1], preferred_element_type=F32).astype(BF16)
        tail = jnp.dot(xn, w_ref[:, c1:], preferred_element_type=F32)
        ngate = N_KV * HEADS_PER_KV * 3
        u_ref[...] = tail[:, ngate:ngate + u_ref.shape[1]]
        for g in range(N_KV):
            lo = g * HEADS_PER_KV * 3
            gate_ref[:, g * LANES:(g + 1) * LANES] = jax.nn.sigmoid(tail[:, lo:lo + LANES])


def _inproj(h, gain, w_perm, d_ssm):
    b, t, d = h.shape
    tm = WINDOW
    nt = t // tm
    nq, ncv, nvs, ng = D_ATTN, 2 * D_KV, 4 * D_KV, N_KV * LANES
    ncol = w_perm.shape[1]
    ntail = N_KV * HEADS_PER_KV * 3 + d_ssm
    assert ncol == nq + ncv + nvs + pl.cdiv(ntail, LANES) * LANES and ncol - (nq + ncv + nvs) >= ng and t % tm == 0
    data = lambda bi, i: (bi, jnp.maximum(i - 1, 0), 0)
    return pl.pallas_call(
        _inproj_body,
        grid=(b, nt + 1),
        in_specs=[
            pl.BlockSpec((None, tm, d), data),
            pl.BlockSpec((1, d), lambda bi, i: (0, 0)),
            pl.BlockSpec((d, ncol), lambda bi, i: (0, 0)),
        ],
        out_specs=[
            pl.BlockSpec((None, tm, nq), data),
            pl.BlockSpec((None, tm, nvs), lambda bi, i: (bi, i, 0)),
            pl.BlockSpec((None, tm, ncv), lambda bi, i: (bi, jnp.where(i == 0, nt, i - 1), 0)),
            pl.BlockSpec((None, tm, d_ssm), data),
            pl.BlockSpec((None, tm, ng), data),
        ],
        out_shape=[
            jax.ShapeDtypeStruct((b, t, nq), BF16),
            jax.ShapeDtypeStruct((b, t + tm, nvs), BF16),
            jax.ShapeDtypeStruct((b, t + tm, ncv), F32),
            jax.ShapeDtypeStruct((b, t, d_ssm), F32),
            jax.ShapeDtypeStruct((b, t, ng), F32),
        ],
        compiler_params=_cparams(("parallel", "arbitrary")),
        name="inproj",
    )(h, gain.reshape(1, d), w_perm)


def _compress_body(x_ref, pe_ref, w1_ref, b1_ref, w2_ref, o_ref):
    nb, nblk = o_ref.shape[0], o_ref.shape[1]
    acc = jnp.broadcast_to(b1_ref[...], (nb * nblk, b1_ref.shape[1])).astype(F32)
    for l in range(CMP_LEN):
        xl = x_ref[:, pl.ds(l, nblk, stride=CMP_STRIDE), :] + pe_ref[l:l + 1, :]
        acc = acc + jnp.dot(xl.reshape(nb * nblk, xl.shape[2]).astype(BF16), w1_ref[l], preferred_element_type=F32)
    hid = jax.nn.gelu(acc, approximate=True)
    out = jnp.dot(hid.astype(BF16), w2_ref[...], preferred_element_type=F32)
    o_ref[...] = out.reshape(o_ref.shape).astype(BF16)


def _compress(kcv, pe, w1, b1, w2, nblk, *, nb=4):
    b, tpad, _ = kcv.shape
    nb = min(nb, b)
    hid, dh = w1.shape[3], w2.shape[2]
    assert (nblk - 1) * CMP_STRIDE + CMP_LEN <= tpad
    return pl.pallas_call(
        _compress_body,
        grid=(2 * N_KV, b // nb),
        in_specs=[
            pl.BlockSpec((nb, tpad, dh), lambda s, i: (i, 0, s)),
            pl.BlockSpec((None, CMP_LEN, dh), lambda s, i: (s // N_KV, 0, 0)),
            pl.BlockSpec((None, CMP_LEN, dh, hid), lambda s, i: (s // N_KV, 0, 0, 0)),
            pl.BlockSpec((None, 1, hid), lambda s, i: (s // N_KV, 0, 0)),
            pl.BlockSpec((None, hid, dh), lambda s, i: (s // N_KV, 0, 0)),
        ],
        out_specs=pl.BlockSpec((None, None, nb, nblk, dh), lambda s, i: (s // N_KV, s % N_KV, i, 0, 0)),
        out_shape=jax.ShapeDtypeStruct((2, N_KV, b, nblk, dh), BF16),
        compiler_params=_cparams(("parallel", "parallel")),
        name="compress",
    )(kcv, pe, w1, b1, w2)


def _nt_dot(a, b):
    return lax.dot_general(a, b, (((1,), (1,)), ((), ())), preferred_element_type=F32)


def _nsa_body(q_ref, ks_ref, vs_ref, kw_ref, vw_ref, kcb_ref, vcb_ref, gate_ref, biasc_ref, dt_ref, wb_ref,
              ovt_ref, et_ref, o_ref, kse_ref, s_ref, m_ref, l_ref, acc_ref, acc2_ref):
    i = pl.program_id(1)
    t0 = i * TQ
    t = et_ref.shape[0]
    ns = t // SEL_BLOCK
    nsel = min(N_SELECT, ns)
    groups = range(N_KV)
    gl = lambda g: slice(g * HEAD_DIM, (g + 1) * HEAD_DIM)

    @pl.when(i == 0)
    def _():
        for g in groups:
            kse_ref[g, :, 0:HEAD_DIM] = ks_ref[WINDOW:WINDOW + t, gl(g)]
            kse_ref[g, :, HEAD_DIM:] = et_ref[...]

    q = q_ref[...]
    w0 = pl.multiple_of(t0, LANES)
    span = wb_ref.shape[2]

    q4 = [jnp.concatenate([q[:, (g * HEADS_PER_KV + h) * HEAD_DIM:(g * HEADS_PER_KV + h + 1) * HEAD_DIM]
                           for h in range(HEADS_PER_KV)], axis=0) for g in groups]

    sc = [_nt_dot(q4[g], kcb_ref[g]) + biasc_ref[g] for g in groups]
    row_t = t0 + (lax.broadcasted_iota(jnp.int32, (ROWS4, LANES), 0) & (TQ - 1))
    col_c = lax.broadcasted_iota(jnp.int32, (ROWS4, LANES), 1)
    valid_c = (row_t - col_c * CMP_STRIDE - (CMP_LEN - 1) >= 0) & (col_c < LANES - 1)
    mc = [jnp.max(s, axis=-1, keepdims=True) for s in sc]
    pc = [jnp.where(valid_c, jnp.exp2(s - m), 0.0) for s, m in zip(sc, mc)]
    pc = [p / jnp.maximum(jnp.sum(p, axis=-1, keepdims=True), 1e-30) for p in pc]
    pcb = [p.astype(BF16) for p in pc]
    o_cmps = [jnp.dot(pcb[g], vcb_ref[g], preferred_element_type=F32) for g in groups]
    pimp = [_nt_dot(ovt_ref[...], p) for p in pcb]
    imp = [sum(p[0:ns, h * TQ:(h + 1) * TQ] for h in range(1, HEADS_PER_KV)) + p[0:ns, 0:TQ] for p in pimp]

    jrow = lax.broadcasted_iota(jnp.int32, (ns, TQ), 0)
    tpos = t0 + lax.broadcasted_iota(jnp.int32, (ns, TQ), 1)
    cur = lax.shift_right_logical(tpos, int(math.log2(SEL_BLOCK)))
    forced = (jrow == 0) | (jrow == cur) | (jrow == cur - 1)
    impm = [jnp.where(forced, 1e6, jnp.where(jrow * SEL_BLOCK <= tpos, x, -1e9)) for x in imp]
    nslab = ns // 8
    slabs = [[x[8 * v:8 * v + 8] for v in range(nslab)] for x in impm]
    cnts = [[jnp.zeros((8, TQ), F32) for _ in range(nslab)] for _ in groups]
    sub = lax.broadcasted_iota(jnp.int32, (8, TQ), 0)
    for jp in range(ns):
        v0, r0 = divmod(jp, 8)
        for g in groups:
            row = jnp.broadcast_to(slabs[g][v0][r0:r0 + 1, :], (8, TQ))
            for v in range(nslab):
                if v > v0:
                    beats = row >= slabs[g][v]
                elif v < v0:
                    beats = row > slabs[g][v]
                else:
                    beats = (row > slabs[g][v]) | ((row == slabs[g][v]) & (sub > r0))
                cnts[g][v] = cnts[g][v] + jnp.where(beats, 1.0, 0.0)
    q4s = []
    for g in groups:
        negt = [jnp.where(c < nsel, 0.0, NEG) for c in cnts[g]]
        negt = jnp.concatenate(negt + [jnp.zeros((LANES - ns, TQ), F32)], axis=0)
        neg = negt.T.astype(BF16)
        q4s.append(jnp.concatenate([q4[g], jnp.concatenate([neg] * HEADS_PER_KV, axis=0)], axis=1))

    nsub = SEL_CHUNK // LANES
    nchunks = ((i + 1) * QK - 1) // nsub + 1
    m_ref[...] = jnp.full(m_ref.shape, -3e38, F32)

    def score_chunk(ck, _):
        kb = pl.multiple_of(ck * SEL_CHUNK, SEL_CHUNK)
        s = [_nt_dot(q4s[g], kse_ref[g, pl.ds(kb, SEL_CHUNK), :]) for g in groups]
        for g in groups:
            mloc = None
            for j in range(nsub):
                sj = s[g][:, j * LANES:(j + 1) * LANES] + dt_ref[g, jnp.clip(ck * nsub + j - i * QK + 2, 0, QK + 2)]
                s_ref[g, ck, :, j * LANES:(j + 1) * LANES] = sj
                mloc = sj if mloc is None else jnp.maximum(mloc, sj)
            m_ref[g] = jnp.maximum(m_ref[g], mloc)
        return 0

    lax.fori_loop(0, nchunks, score_chunk, 0)
    for g in groups:
        m_ref[g] = jnp.broadcast_to(jnp.max(m_ref[g], axis=-1, keepdims=True), (ROWS4, LANES))
    l_ref[...] = jnp.zeros(l_ref.shape, F32)
    acc_ref[...] = jnp.zeros(acc_ref.shape, F32)

    kneg = jnp.where(t0 - WINDOW + lax.broadcasted_iota(jnp.int32, (1, span), 1) >= 0, 0.0, NEG)
    sw = [_nt_dot(q4[g], kw_ref[pl.ds(w0, span), gl(g)]) + wb_ref[g] + kneg for g in groups]
    mw = [jnp.max(s, axis=-1, keepdims=True) for s in sw]
    pw = [jnp.exp2(s - m) for s, m in zip(sw, mw)]
    lw = [jnp.sum(p, axis=-1, keepdims=True) for p in pw]
    o_wins = [jnp.dot(pw[g].astype(BF16), vw_ref[pl.ds(w0, span), gl(g)], preferred_element_type=F32) for g in groups]
    o_wins = [o / jnp.maximum(l, 1e-30) for o, l in zip(o_wins, lw)]
    gts = [gate_ref[:, g * LANES:(g + 1) * LANES] for g in groups]
    for h in range(HEADS_PER_KV):
        r = slice(h * TQ, (h + 1) * TQ)
        for g in groups:
            acc2_ref[g, r, :] = gts[g][:, 3 * h:3 * h + 1] * o_cmps[g][r] + gts[g][:, 3 * h + 2:3 * h + 3] * o_wins[g][r]

    def pv_chunk(ck, _):
        kb = pl.multiple_of(WINDOW + ck * SEL_CHUNK, SEL_CHUNK)
        ps = [[jnp.exp2(s_ref[g, ck, :, j * LANES:(j + 1) * LANES] - m_ref[g]) for j in range(nsub)] for g in groups]
        for g in groups:
            lsum = ps[g][0]
            for pj in ps[g][1:]:
                lsum = lsum + pj
            l_ref[g] += lsum
            p = jnp.concatenate([pj.astype(BF16) for pj in ps[g]], axis=1)
            acc_ref[g] += jnp.dot(p, vs_ref[pl.ds(kb, SEL_CHUNK), gl(g)], preferred_element_type=F32)
        return 0

    lax.fori_loop(0, nchunks, pv_chunk, 0)

    ls = [jnp.maximum(jnp.sum(l_ref[g], axis=-1, keepdims=True), 1e-30) for g in groups]
    o_sels = [acc_ref[g] / ls[g] for g in groups]
    for h in range(HEADS_PER_KV):
        r = slice(h * TQ, (h + 1) * TQ)
        for g in groups:
            o_h = acc2_ref[g, r, :] + gate_ref[:, g * LANES + 3 * h + 1:g * LANES + 3 * h + 2] * o_sels[g][r]
            c0 = (g * HEADS_PER_KV + h) * HEAD_DIM
            o_ref[:, c0:c0 + HEAD_DIM] = o_h.astype(BF16)


def _nsa(q, kvs, cb, gates, biasc, dtiles, wb, ovt, et):
    b, t, _ = q.shape
    nt = t // TQ
    tp = kvs.shape[1]
    span = wb.shape[2]
    gw = HEADS_PER_KV * HEAD_DIM
    assert tp == t + WINDOW and span == TQ + WINDOW and t % SEL_CHUNK == 0 and (t // SEL_BLOCK) % 8 == 0

    def stream(j):
        return pl.BlockSpec((None, tp, N_KV * HEAD_DIM), lambda bi, i: (bi, 0, j))

    return pl.pallas_call(
        _nsa_body,
        grid=(b, nt),
        in_specs=[
            pl.BlockSpec((None, TQ, N_KV * gw), lambda bi, i: (bi, i, 0)),
            stream(0), stream(1), stream(2), stream(3),
            pl.BlockSpec((None, N_KV, None, LANES, HEAD_DIM), lambda bi, i: (0, 0, bi, 0, 0)),
            pl.BlockSpec((None, N_KV, None, LANES, HEAD_DIM), lambda bi, i: (1, 0, bi, 0, 0)),
            pl.BlockSpec((None, TQ, N_KV * LANES), lambda bi, i: (bi, i, 0)),
            pl.BlockSpec((N_KV, None, ROWS4, LANES), lambda bi, i: (0, i, 0, 0)),
            pl.BlockSpec((N_KV, QK + 3, ROWS4, LANES), lambda bi, i: (0, 0, 0, 0)),
            pl.BlockSpec((N_KV, ROWS4, span), lambda bi, i: (0, 0, 0)),
            pl.BlockSpec((LANES, LANES), lambda bi, i: (0, 0)),
            pl.BlockSpec((t, LANES), lambda bi, i: (0, 0)),
        ],
        out_specs=pl.BlockSpec((None, TQ, N_KV * gw), lambda bi, i: (bi, i, 0)),
        out_shape=jax.ShapeDtypeStruct((b, t, N_KV * gw), BF16),
        scratch_shapes=[
            pltpu.VMEM((N_KV, t, 2 * HEAD_DIM), BF16),
            pltpu.VMEM((N_KV, t // SEL_CHUNK, ROWS4, SEL_CHUNK), F32),
            pltpu.VMEM((N_KV, ROWS4, LANES), F32),
            pltpu.VMEM((N_KV, ROWS4, LANES), F32),
            pltpu.VMEM((N_KV, ROWS4, HEAD_DIM), F32),
            pltpu.VMEM((N_KV, ROWS4, HEAD_DIM), F32),
        ],
        compiler_params=_cparams(("parallel", "arbitrary"), NSA_VMEM_LIMIT_BYTES),
        name="nsa",
    )(q, kvs, kvs, kvs, kvs, cb, cb, gates, biasc, dtiles, wb, ovt, et)


def _s5_body(u_ref, c0_ref, winc_ref, woutc_ref, rep_k_ref, rep_in_ref, rep_out_ref, m_in_ref, m_out_ref,
             al_ref, d_ref, y_ref, ucat_ref, inj_ref, xp_ref, grev_ref, win_ref, wout_ref):
    nb, t, lanes = u_ref.shape
    L = SSM_CHUNK
    nchunk = t // L
    rows = nb * nchunk
    sdim = al_ref.shape[1] // 2

    @pl.when(pl.program_id(1) == 0)
    def _():
        m_in, m_out = m_in_ref[...], m_out_ref[...]
        for half in range(2):
            for s in range(L):
                x = jnp.dot(winc_ref[half, s * lanes:(s + 1) * lanes, :], rep_in_ref[...], preferred_element_type=F32)
                win_ref[s * lanes:(s + 1) * lanes, half * sdim:(half + 1) * sdim] = (x * m_in).astype(BF16)
            for tt in range(L):
                x = jnp.dot(woutc_ref[half], rep_out_ref[:, tt * lanes:(tt + 1) * lanes], preferred_element_type=F32)
                wout_ref[half * sdim:(half + 1) * sdim, tt * lanes:(tt + 1) * lanes] = (x * m_out).astype(BF16)
        c0 = [(jnp.dot(c0_ref[half], rep_k_ref[...], preferred_element_type=F32) * m_out).astype(BF16)
              for half in range(2)]
        kx = jnp.dot(win_ref[...], jnp.concatenate(c0, axis=0), preferred_element_type=F32)
        zero = jnp.zeros((lanes, lanes), BF16)

        def kblk(tau):
            s = L - 1 - tau
            return kx[s * lanes:(s + 1) * lanes].astype(BF16) if 0 <= tau < L else zero

        for r in range(L + 1):
            grev_ref[r * lanes:(r + 1) * lanes, 0:lanes] = kblk(L - r - 1)
            grev_ref[r * lanes:(r + 1) * lanes, lanes:] = kblk(L - r)

    def u_at(s):
        return u_ref[:, pl.ds(s, nchunk, stride=L), :].reshape(rows, lanes)

    for s in range(L):
        ucat_ref[:, s * lanes:(s + 1) * lanes] = u_at(s).astype(BF16)
    inj = jnp.dot(ucat_ref[...], win_ref[...], preferred_element_type=F32)
    npl = sdim // lanes
    for k in range(2 * npl):
        inj_ref[k] = inj[:, k * lanes:(k + 1) * lanes]
    ar = [jnp.broadcast_to(al_ref[0:1, k * lanes:(k + 1) * lanes], (nb, lanes)) for k in range(npl)]
    ai = [jnp.broadcast_to(al_ref[0:1, sdim + k * lanes:sdim + (k + 1) * lanes], (nb, lanes)) for k in range(npl)]

    def step(c, carry):
        xr, xi = carry
        rsel = pl.ds(c, nb, stride=nchunk)
        nr, ni = [], []
        for k in range(npl):
            xp_ref[k, rsel, :] = xr[k]
            xp_ref[npl + k, rsel, :] = xi[k]
            nr.append(ar[k] * xr[k] - ai[k] * xi[k] + inj_ref[k, rsel, :])
            ni.append(ar[k] * xi[k] + ai[k] * xr[k] + inj_ref[npl + k, rsel, :])
        return tuple(nr), tuple(ni)

    z = tuple(jnp.zeros((nb, lanes), F32) for _ in range(npl))
    lax.fori_loop(0, nchunk, step, (z, z), unroll=4)
    xp = jnp.concatenate([xp_ref[k] for k in range(2 * npl)], axis=1).astype(BF16)
    d = d_ref[...]
    for tp in range(0, L, 2):
        res = jnp.dot(ucat_ref[:, 0:(tp + 2) * lanes], grev_ref[(L - 1 - tp) * lanes:(L + 1) * lanes, :],
                      preferred_element_type=F32)
        res = res + jnp.dot(xp, wout_ref[:, tp * lanes:(tp + 2) * lanes], preferred_element_type=F32)
        for k in range(2):
            y = res[:, k * lanes:(k + 1) * lanes] + d * u_at(tp + k)
            y_ref[:, pl.ds(tp + k, nchunk, stride=L), :] = y.reshape(nb, nchunk, lanes)


def _s5(u, c0c, winc, woutc, al, d, *, nb=4):
    b, t, dch = u.shape
    nb = min(nb, b)
    nblk = dch // LANES
    L = SSM_CHUNK
    rows = nb * (t // L)
    hch, p = c0c.shape[3], winc.shape[3]
    gpb = LANES // hch
    sdim = gpb * p
    kcat = L * LANES
    lane, col, st = jnp.arange(LANES), jnp.arange(kcat), jnp.arange(sdim)
    rep_k = (jnp.arange(hch)[:, None] == (lane % hch)[None, :]).astype(BF16)
    rep_in = (jnp.arange(p)[:, None] == (st % p)[None, :]).astype(BF16)
    rep_out = (jnp.arange(L * hch)[:, None] == (col // LANES * hch + col % hch)[None, :]).astype(BF16)
    m_in = ((lane // hch)[:, None] == (st // p)[None, :]).astype(F32)
    m_out = ((st // p)[:, None] == (lane // hch)[None, :]).astype(F32)
    const = lambda a: pl.BlockSpec(a.shape, lambda j, i: (0,) * a.ndim)
    return pl.pallas_call(
        _s5_body,
        grid=(nblk, b // nb),
        in_specs=[
            pl.BlockSpec((nb, t, LANES), lambda j, i: (i, 0, j)),
            pl.BlockSpec((None, 2, sdim, hch), lambda j, i: (j, 0, 0, 0)),
            pl.BlockSpec((None, 2, kcat, p), lambda j, i: (j, 0, 0, 0)),
            pl.BlockSpec((None, 2, sdim, L * hch), lambda j, i: (j, 0, 0, 0)),
            const(rep_k), const(rep_in), const(rep_out), const(m_in), const(m_out),
            pl.BlockSpec((None, 8, 2 * sdim), lambda j, i: (j, 0, 0)),
            pl.BlockSpec((1, LANES), lambda j, i: (0, j)),
        ],
        out_specs=pl.BlockSpec((nb, t, LANES), lambda j, i: (i, 0, j)),
        out_shape=jax.ShapeDtypeStruct((b, t, dch), F32),
        scratch_shapes=[pltpu.VMEM((rows, kcat), BF16), pltpu.VMEM((2 * sdim // LANES, rows, LANES), F32),
                        pltpu.VMEM((2 * sdim // LANES, rows, LANES), F32),
                        pltpu.VMEM(((L + 1) * LANES, 2 * LANES), BF16), pltpu.VMEM((kcat, 2 * sdim), BF16),
                        pltpu.VMEM((2 * sdim, kcat), BF16)],
        compiler_params=_cparams(("parallel", "arbitrary")),
        name="s5",
    )(u, c0c, winc, woutc, rep_k, rep_in, rep_out, m_in, m_out, al, d.reshape(1, dch))


def _glu_out_body(h_ref, a_ref, y_ref, wg_ref, bg_ref, woa_ref, wos_ref, o_ref):
    hg = jax.nn.gelu(y_ref[...], approximate=True)
    z = jnp.dot(hg.astype(BF16), wg_ref[...], preferred_element_type=F32) + bg_ref[...]
    s = hg * jax.nn.sigmoid(z)
    mix = jnp.dot(a_ref[...], woa_ref[...], preferred_element_type=F32)
    mix = mix + jnp.dot(s.astype(BF16), wos_ref[...], preferred_element_type=F32)
    o_ref[...] = h_ref[...] + mix


def _glu_out(h, a, y, wg, bg, woa, wos, *, tm=512):
    n, dm = h.shape
    da, ds = a.shape[1], y.shape[1]
    row = lambda w: pl.BlockSpec((tm, w), lambda i: (i, 0))
    full = lambda r, c: pl.BlockSpec((r, c), lambda i: (0, 0))
    return pl.pallas_call(
        _glu_out_body,
        grid=(n // tm,),
        in_specs=[row(dm), row(da), row(ds), full(ds, ds), full(1, ds), full(da, dm), full(ds, dm)],
        out_specs=row(dm),
        out_shape=jax.ShapeDtypeStruct((n, dm), F32),
        compiler_params=_cparams(("parallel",)),
        name="glu_out",
    )(h, a, y, wg, bg.reshape(1, ds), woa, wos)


def _t5_bucket(dist):
    n = jnp.maximum(dist, 0)
    max_exact = N_BUCKETS // 2
    nf = jnp.maximum(n, 1).astype(F32)
    large = max_exact + (jnp.log(nf / max_exact) / math.log(MAX_DISTANCE / max_exact)
                         * (N_BUCKETS - max_exact)).astype(jnp.int32)
    large = jnp.minimum(large, N_BUCKETS - 1)
    return jnp.where(n < max_exact, n, large)


def _t5_body(first_ref, table_ref, bc_ref, dt_ref, wb_ref, *, nc):
    g = pl.program_id(0)
    nt = bc_ref.shape[0]
    span = wb_ref.shape[1]
    a = lax.broadcasted_iota(jnp.int32, (TQ, LANES), 0)
    c = lax.broadcasted_iota(jnp.int32, (TQ, LANES), 1)
    heads = [g * HEADS_PER_KV + h for h in range(HEADS_PER_KV)]

    def lookup(dist, valid, store):
        bias = [jnp.full((TQ, LANES), table_ref[0, hd] * LOG2E, F32) for hd in heads]
        for k in range(1, N_BUCKETS):
            ge = dist >= first_ref[k]
            bias = [jnp.where(ge, table_ref[k, hd] * LOG2E, bv) for hd, bv in zip(heads, bias)]
        for h, bv in enumerate(bias):
            store(slice(h * TQ, (h + 1) * TQ), bv if valid is None else jnp.where(valid, bv, NEG))

    def put_far(rows, tile):
        dt_ref[0, rows, :] = tile

    lookup(jnp.full((TQ, LANES), 2 * MAX_DISTANCE, jnp.int32), None, put_far)
    for r in range(-1, QK):
        def put_near(rows, tile, r=r):
            dt_ref[r + 2, rows, :] = tile

        dist = a - r * LANES - c
        lookup(dist, dist >= 0, put_near)
    dt_ref[QK + 2] = jnp.full((ROWS4, LANES), NEG, F32)
    for j in range(span // LANES):
        def put_wb(rows, tile, j=j):
            wb_ref[rows, j * LANES:(j + 1) * LANES] = tile

        dist = a + WINDOW - (c + j * LANES)
        lookup(dist, (dist >= 0) & (dist < WINDOW), put_wb)

    def cmp_tile(i, _):
        def put_bc(rows, tile):
            bc_ref[i, rows, :] = tile

        dist = i * TQ + a - (c * CMP_STRIDE + CMP_LEN - 1)
        lookup(dist, (dist >= 0) & (c < nc), put_bc)
        return 0

    lax.fori_loop(0, nt, cmp_tile, 0)


def _bias_tables(rel_bias, t):
    nt = t // TQ
    span = TQ + WINDOW
    nc = (t - CMP_LEN) // CMP_STRIDE + 1
    buckets = _t5_bucket(jnp.arange(2 * MAX_DISTANCE))
    first = jnp.sum((buckets[None, :] < jnp.arange(N_BUCKETS)[:, None]).astype(jnp.int32), axis=1)
    smem = pl.BlockSpec(memory_space=pltpu.SMEM)
    return pl.pallas_call(
        functools.partial(_t5_body, nc=nc),
        grid=(N_KV,),
        in_specs=[smem, smem],
        out_specs=[
            pl.BlockSpec((None, nt, ROWS4, LANES), lambda g: (g, 0, 0, 0)),
            pl.BlockSpec((None, QK + 3, ROWS4, LANES), lambda g: (g, 0, 0, 0)),
            pl.BlockSpec((None, ROWS4, span), lambda g: (g, 0, 0)),
        ],
        out_shape=[
            jax.ShapeDtypeStruct((N_KV, nt, ROWS4, LANES), F32),
            jax.ShapeDtypeStruct((N_KV, QK + 3, ROWS4, LANES), F32),
            jax.ShapeDtypeStruct((N_KV, ROWS4, span), F32),
        ],
        compiler_params=_cparams(("parallel",)),
        name="t5_tables",
    )(first, rel_bias.astype(F32))


def _sel_tables(t):
    ns = t // SEL_BLOCK
    nc_pad = LANES
    c_start = jnp.arange(nc_pad) * CMP_STRIDE
    j_start = jnp.arange(LANES) * SEL_BLOCK
    ov = jnp.clip(jnp.minimum(c_start[:, None] + CMP_LEN, j_start[None, :] + SEL_BLOCK)
                  - jnp.maximum(c_start[:, None], j_start[None, :]), 0, None).astype(F32) / CMP_LEN
    ov = jnp.where(jnp.arange(LANES)[None, :] < ns, ov, 0.0)
    et = jnp.arange(t)[:, None] // SEL_BLOCK == jnp.arange(LANES)[None, :]
    return ov.T.astype(BF16), et.astype(BF16)


def _s5_tables(lam_re, lam_im, log_step, b_re, b_im, c_re, c_im):
    ng, p = lam_re.shape
    hch = b_re.shape[2]
    L = SSM_CHUNK
    step = jnp.exp(log_step.astype(F32))[:, None]
    lre, lim = lam_re.astype(F32), lam_im.astype(F32)
    mag = jnp.exp(lre * step)
    ab_re, ab_im = mag * jnp.cos(lim * step), mag * jnp.sin(lim * step)
    nr, ni = ab_re - 1.0, ab_im
    den = lre * lre + lim * lim
    f_re, f_im = (nr * lre + ni * lim) / den, (ni * lre - nr * lim) / den
    br, bim = b_re.astype(F32), b_im.astype(F32)
    bb_re = f_re[..., None] * br - f_im[..., None] * bim
    bb_im = f_re[..., None] * bim + f_im[..., None] * br
    cr, ci = c_re.astype(F32), c_im.astype(F32)
    pr, pi = [jnp.ones_like(ab_re)], [jnp.zeros_like(ab_re)]
    for _ in range(L):
        pr, pi = pr + [pr[-1] * ab_re - pi[-1] * ab_im], pi + [pr[-1] * ab_im + pi[-1] * ab_re]
    pw_re, pw_im = jnp.stack(pr, 0), jnp.stack(pi, 0)
    cp_re = cr[None] * pw_re[:, :, None, :] - ci[None] * pw_im[:, :, None, :]
    cp_im = -(cr[None] * pw_im[:, :, None, :] + ci[None] * pw_re[:, :, None, :])
    gpb = LANES // hch
    nblk = ng // gpb
    wr = pw_re[L - 1 - jnp.arange(L)]
    wi = pw_im[L - 1 - jnp.arange(L)]
    win_re = (wr[..., None] * bb_re[None] - wi[..., None] * bb_im[None])
    win_im = (wr[..., None] * bb_im[None] + wi[..., None] * bb_re[None])
    inj_op = lambda m: m.reshape(L, nblk, gpb, p, hch).transpose(1, 0, 2, 4, 3).reshape(nblk, L * LANES, p)
    winc = jnp.stack([inj_op(win_re), inj_op(win_im)], axis=1)
    read_op = lambda m: m.reshape(L, nblk, gpb, hch, p).transpose(1, 2, 4, 0, 3).reshape(nblk, gpb * p, L * hch)
    woutc = jnp.stack([read_op(cp_re[1:]), read_op(cp_im[1:])], axis=1)
    read0 = lambda m: m.reshape(nblk, gpb, hch, p).transpose(0, 1, 3, 2).reshape(nblk, gpb * p, hch)
    c0c = jnp.stack([read0(cp_re[0]), read0(cp_im[0])], axis=1)
    al = jnp.concatenate([pw_re[L].reshape(nblk, gpb * p), pw_im[L].reshape(nblk, gpb * p)], axis=1)
    al = jnp.broadcast_to(al[:, None, :], (nblk, 8, 2 * gpb * p))
    return c0c.astype(BF16), winc.astype(BF16), woutc.astype(BF16), al


def _mixers(h1, mix_norm, w_in, cmp_k, cmp_v, rel_bias, ssm, ssm_d, glu_w, glu_b, w_out, b, t):
    n, d = h1.shape
    d_ssm = glu_w.shape[0]
    ncol = w_in.shape[1]
    w_perm = jnp.pad(w_in, ((0, 0), (0, pl.cdiv(ncol, LANES) * LANES - ncol))).astype(BF16)
    q, kvs, kcv, u, gates = _inproj(h1.reshape(b, t, d), mix_norm, w_perm, d_ssm)

    nblk = t // CMP_STRIDE
    assert nblk == LANES, "compressed-block axis is laid out on one 128-lane tile"
    pe = jnp.stack([cmp_k[0], cmp_v[0]], 0).astype(F32)
    w1 = jnp.stack([cmp_k[1], cmp_v[1]], 0).astype(BF16).reshape(2, CMP_LEN, HEAD_DIM, -1)
    b1 = jnp.stack([cmp_k[2].reshape(1, -1), cmp_v[2].reshape(1, -1)], 0).astype(F32)
    w2 = jnp.stack([cmp_k[3], cmp_v[3]], 0).astype(BF16)
    cb = _compress(kcv, pe, w1, b1, w2, nblk)

    biasc, dtiles, wb = _bias_tables(rel_bias, t)
    ovt, et = _sel_tables(t)
    a = _nsa(q, kvs, cb, gates, biasc, dtiles, wb, ovt, et)

    c0c, winc, woutc, al = _s5_tables(*ssm)
    y = _s5(u, c0c, winc, woutc, al, ssm_d)

    wo = w_out.astype(BF16)
    return _glu_out(h1, a.reshape(n, D_ATTN), y.reshape(n, d_ssm), glu_w.astype(BF16), glu_b, wo[:D_ATTN], wo[D_ATTN:])


def kernel(x, ffn1_norm, ffn1_w1, ffn1_w3, ffn1_w2, mix_norm, w_in, cmp_pe_k, cmp_w1_k, cmp_b1_k, cmp_w2_k,
           cmp_pe_v, cmp_w1_v, cmp_b1_v, cmp_w2_v, rel_bias, ssm_lam_re, ssm_lam_im, ssm_log_step, ssm_b_re,
           ssm_b_im, ssm_c_re, ssm_c_im, ssm_d, glu_w, glu_b, w_out, ffn2_norm, ffn2_w1, ffn2_w3, ffn2_w2,
           final_norm):
    b, t, d = x.shape
    depth = ffn1_w1.shape[0]
    h = x.reshape(b * t, d)
    for l in range(depth):
        last = l == depth - 1
        h = _ffn(h, ffn1_norm[l], ffn1_w1[l], ffn1_w3[l], ffn1_w2[l])
        ssm = (ssm_lam_re[l], ssm_lam_im[l], ssm_log_step[l], ssm_b_re[l], ssm_b_im[l], ssm_c_re[l], ssm_c_im[l])
        h = _mixers(h, mix_norm[l], w_in[l],
                    (cmp_pe_k[l], cmp_w1_k[l], cmp_b1_k[l], cmp_w2_k[l]),
                    (cmp_pe_v[l], cmp_w1_v[l], cmp_b1_v[l], cmp_w2_v[l]),
                    rel_bias, ssm, ssm_d[l], glu_w[l], glu_b[l], w_out[l], b, t)
        h = _ffn(h, ffn2_norm[l], ffn2_w1[l], ffn2_w3[l], ffn2_w2[l], final_gain=final_norm if last else None)
    return h.reshape(b, t, d)
```

```python
import functools
import math

import jax
import jax.numpy as jnp
from jax import lax
from jax.experimental import pallas as pl
from jax.experimental.pallas import tpu as pltpu

F32 = jnp.float32
BF16 = jnp.bfloat16

HEAD_DIM = 128
N_KV = 2
HEADS_PER_KV = 4
N_HEADS = N_KV * HEADS_PER_KV
D_ATTN = N_HEADS * HEAD_DIM
D_KV = N_KV * HEAD_DIM
CMP_LEN = 32
CMP_STRIDE = 16
SEL_BLOCK = 64
N_SELECT = 16
WINDOW = 512
N_BUCKETS = 32
MAX_DISTANCE = 128
SSM_GROUP = 16
SSM_STATE = 64
EPS = 1e-6
NEG = -1e30
LOG2E = math.log2(math.e)

LANES = 128
VMEM_LIMIT_BYTES = 56 * 1024 * 1024
NSA_VMEM_LIMIT_BYTES = 60 * 1024 * 1024
FFN_VMEM_LIMIT_BYTES = 63 * 1024 * 1024
GLU_VMEM_LIMIT_BYTES = 62 * 1024 * 1024
TQ = 256
QK = TQ // LANES
ROWS4 = HEADS_PER_KV * TQ
SEL_CHUNK = 512
SSM_CHUNK = 16


def _cparams(sem, vmem_limit_bytes=VMEM_LIMIT_BYTES):
    return pltpu.CompilerParams(dimension_semantics=sem, vmem_limit_bytes=vmem_limit_bytes)


def _ffn_body(x_ref, g_ref, w1_ref, w3_ref, w2_ref, *rest, final):
    if final:
        fg_ref, o_ref, xn_ref = rest
    else:
        o_ref, xn_ref = rest
    j = pl.program_id(1)
    nj = pl.num_programs(1)

    def contribution(xn):
        a = jnp.dot(xn, w1_ref[...], preferred_element_type=F32)
        b = jnp.dot(xn, w3_ref[...], preferred_element_type=F32)
        gated = (a * jax.nn.sigmoid(a)) * b
        return jnp.dot(gated.astype(BF16), w2_ref[...], preferred_element_type=F32)

    @pl.when(j == 0)
    def _():
        x = x_ref[...]
        ms = jnp.mean(x * x, axis=-1, keepdims=True)
        xn = (x * lax.rsqrt(ms + EPS) * g_ref[...]).astype(BF16)
        xn_ref[...] = xn
        o_ref[...] = contribution(xn)

    @pl.when((j > 0) & (j < nj - 1))
    def _():
        o_ref[...] += contribution(xn_ref[...])

    @pl.when(j == nj - 1)
    def _():
        h = x_ref[...] + 0.5 * (o_ref[...] + contribution(xn_ref[...]))
        if final:
            ms = jnp.mean(h * h, axis=-1, keepdims=True)
            h = h * lax.rsqrt(ms + EPS) * fg_ref[...]
        o_ref[...] = h


def _ffn(x, gain, w1, w3, w2, final_gain=None, *, tm=1024, tf=512):
    n, d = x.shape
    dff = w1.shape[1]
    nj = dff // tf
    assert nj >= 2 and nj * tf == dff
    final = final_gain is not None
    in_specs = [
        pl.BlockSpec((tm, d), lambda i, j: (i, 0)),
        pl.BlockSpec((1, d), lambda i, j: (0, 0)),
        pl.BlockSpec((d, tf), lambda i, j: (0, j)),
        pl.BlockSpec((d, tf), lambda i, j: (0, j)),
        pl.BlockSpec((tf, d), lambda i, j: (j, 0)),
    ]
    args = [x, gain.reshape(1, d), w1.astype(BF16), w3.astype(BF16), w2.astype(BF16)]
    if final:
        in_specs.append(pl.BlockSpec((1, d), lambda i, j: (0, 0)))
        args.append(final_gain.reshape(1, d))
    return pl.pallas_call(
        functools.partial(_ffn_body, final=final),
        grid=(n // tm, nj),
        in_specs=in_specs,
        out_specs=pl.BlockSpec((tm, d), lambda i, j: (i, 0)),
        out_shape=jax.ShapeDtypeStruct((n, d), F32),
        scratch_shapes=[pltpu.VMEM((tm, d), BF16)],
        compiler_params=_cparams(("parallel", "arbitrary"), FFN_VMEM_LIMIT_BYTES),
        name="ffn_final" if final else "ffn",
    )(*args)


def _inproj_body(x_ref, g_ref, w_ref, q_ref, kvs_ref, kcv_ref, u_ref, gate_ref):
    i = pl.program_id(1)

    @pl.when(i == 0)
    def _():
        kvs_ref[...] = jnp.zeros(kvs_ref.shape, kvs_ref.dtype)
        kcv_ref[...] = jnp.zeros(kcv_ref.shape, kcv_ref.dtype)

    @pl.when(i > 0)
    def _():
        x = x_ref[...]
        ms = jnp.mean(x * x, axis=-1, keepdims=True)
        xn = (x * lax.rsqrt(ms + EPS) * g_ref[...]).astype(BF16)
        c0 = 0
        c1 = q_ref.shape[1]
        q = jnp.dot(xn, w_ref[:, c0:c1], preferred_element_type=F32)
        q_ref[...] = (q * (HEAD_DIM ** -0.5 * LOG2E)).astype(BF16)
        c0, c1 = c1, c1 + kcv_ref.shape[1]
        kcv_ref[...] = jnp.dot(xn, w_ref[:, c0:c1], preferred_element_type=F32)
        c0, c1 = c1, c1 + kvs_ref.shape[1]
        kvs_ref[...] = jnp.dot(xn, w_ref[:, c0:c1], preferred_element_type=F32).astype(BF16)
        c0, c1 = c1, c1 + u_ref.shape[1]
        u_ref[...] = jnp.dot(xn, w_ref[:, c0:c1], preferred_element_type=F32)
        gate_ref[...] = jax.nn.sigmoid(jnp.dot(xn, w_ref[:, c1:], preferred_element_type=F32))


def _inproj(h, gain, w_perm, d_ssm):
    b, t, d = h.shape
    tm = WINDOW
    nt = t // tm
    nq, ncv, nvs, ng = D_ATTN, 2 * D_KV, 4 * D_KV, N_KV * LANES
    ncol = w_perm.shape[1]
    assert ncol == nq + ncv + nvs + d_ssm + ng and t % tm == 0
    data = lambda bi, i: (bi, jnp.maximum(i - 1, 0), 0)
    return pl.pallas_call(
        _inproj_body,
        grid=(b, nt + 1),
        in_specs=[
            pl.BlockSpec((None, tm, d), data),
            pl.BlockSpec((1, d), lambda bi, i: (0, 0)),
            pl.BlockSpec((d, ncol), lambda bi, i: (0, 0)),
        ],
        out_specs=[
            pl.BlockSpec((None, tm, nq), data),
            pl.BlockSpec((None, tm, nvs), lambda bi, i: (bi, i, 0)),
            pl.BlockSpec((None, tm, ncv), lambda bi, i: (bi, jnp.where(i == 0, nt, i - 1), 0)),
            pl.BlockSpec((None, tm, d_ssm), data),
            pl.BlockSpec((None, tm, ng), data),
        ],
        out_shape=[
            jax.ShapeDtypeStruct((b, t, nq), BF16),
            jax.ShapeDtypeStruct((b, t + tm, nvs), BF16),
            jax.ShapeDtypeStruct((b, t + tm, ncv), F32),
            jax.ShapeDtypeStruct((b, t, d_ssm), F32),
            jax.ShapeDtypeStruct((b, t, ng), F32),
        ],
        compiler_params=_cparams(("parallel", "arbitrary")),
        name="inproj",
    )(h, gain.reshape(1, d), w_perm)


def _compress_body(x_ref, pe_ref, w1_ref, b1_ref, w2_ref, o_ref):
    nb, nblk = o_ref.shape[0], o_ref.shape[1]
    acc = jnp.broadcast_to(b1_ref[...], (nb * nblk, b1_ref.shape[1])).astype(F32)
    for l in range(CMP_LEN):
        xl = x_ref[:, pl.ds(l, nblk, stride=CMP_STRIDE), :] + pe_ref[l:l + 1, :]
        acc = acc + jnp.dot(xl.reshape(nb * nblk, xl.shape[2]).astype(BF16), w1_ref[l], preferred_element_type=F32)
    hid = jax.nn.gelu(acc, approximate=True)
    out = jnp.dot(hid.astype(BF16), w2_ref[...], preferred_element_type=F32)
    o_ref[...] = out.reshape(o_ref.shape).astype(BF16)


def _compress(kcv, pe, w1, b1, w2, nblk, *, nb=4):
    b, tpad, _ = kcv.shape
    nb = min(nb, b)
    hid, dh = w1.shape[3], w2.shape[2]
    assert (nblk - 1) * CMP_STRIDE + CMP_LEN <= tpad
    return pl.pallas_call(
        _compress_body,
        grid=(2 * N_KV, b // nb),
        in_specs=[
            pl.BlockSpec((nb, tpad, dh), lambda s, i: (i, 0, s)),
            pl.BlockSpec((None, CMP_LEN, dh), lambda s, i: (s // N_KV, 0, 0)),
            pl.BlockSpec((None, CMP_LEN, dh, hid), lambda s, i: (s // N_KV, 0, 0, 0)),
            pl.BlockSpec((None, 1, hid), lambda s, i: (s // N_KV, 0, 0)),
            pl.BlockSpec((None, hid, dh), lambda s, i: (s // N_KV, 0, 0)),
        ],
        out_specs=pl.BlockSpec((None, None, nb, nblk, dh), lambda s, i: (s // N_KV, s % N_KV, i, 0, 0)),
        out_shape=jax.ShapeDtypeStruct((2, N_KV, b, nblk, dh), BF16),
        compiler_params=_cparams(("parallel", "parallel")),
        name="compress",
    )(kcv, pe, w1, b1, w2)


def _nt_dot(a, b):
    return lax.dot_general(a, b, (((1,), (1,)), ((), ())), preferred_element_type=F32)


def _nsa_body(q_ref, ks_ref, vs_ref, kw_ref, vw_ref, kcb_ref, vcb_ref, gate_ref, biasc_ref, dt_ref, wb_ref,
              ovt_ref, et_ref, o_ref, kse_ref, s_ref, m_ref, l_ref, acc_ref, acc2_ref):
    i = pl.program_id(1)
    t0 = i * TQ
    t = et_ref.shape[0]
    ns = t // SEL_BLOCK
    nsel = min(N_SELECT, ns)
    groups = range(N_KV)
    gl = lambda g: slice(g * HEAD_DIM, (g + 1) * HEAD_DIM)

    @pl.when(i == 0)
    def _():
        for g in groups:
            kse_ref[g, :, 0:HEAD_DIM] = ks_ref[WINDOW:WINDOW + t, gl(g)]
            kse_ref[g, :, HEAD_DIM:] = et_ref[...]

    q = q_ref[...]
    w0 = pl.multiple_of(t0, LANES)
    span = wb_ref.shape[2]

    q4 = [jnp.concatenate([q[:, (g * HEADS_PER_KV + h) * HEAD_DIM:(g * HEADS_PER_KV + h + 1) * HEAD_DIM]
                           for h in range(HEADS_PER_KV)], axis=0) for g in groups]

    sc = [_nt_dot(q4[g], kcb_ref[g]) + biasc_ref[g] for g in groups]
    row_t = t0 + (lax.broadcasted_iota(jnp.int32, (ROWS4, LANES), 0) & (TQ - 1))
    col_c = lax.broadcasted_iota(jnp.int32, (ROWS4, LANES), 1)
    valid_c = (row_t - col_c * CMP_STRIDE - (CMP_LEN - 1) >= 0) & (col_c < LANES - 1)
    mc = [jnp.max(s, axis=-1, keepdims=True) for s in sc]
    pc = [jnp.where(valid_c, jnp.exp2(s - m), 0.0) for s, m in zip(sc, mc)]
    pc = [p / jnp.maximum(jnp.sum(p, axis=-1, keepdims=True), 1e-30) for p in pc]
    pcb = [p.astype(BF16) for p in pc]
    o_cmps = [jnp.dot(pcb[g], vcb_ref[g], preferred_element_type=F32) for g in groups]
    pimp = [_nt_dot(ovt_ref[...], p) for p in pcb]
    imp = [sum(p[0:ns, h * TQ:(h + 1) * TQ] for h in range(1, HEADS_PER_KV)) + p[0:ns, 0:TQ] for p in pimp]

    jrow = lax.broadcasted_iota(jnp.int32, (ns, TQ), 0)
    tpos = t0 + lax.broadcasted_iota(jnp.int32, (ns, TQ), 1)
    cur = lax.shift_right_logical(tpos, int(math.log2(SEL_BLOCK)))
    forced = (jrow == 0) | (jrow == cur) | (jrow == cur - 1)
    impm = [jnp.where(forced, 1e6, jnp.where(jrow * SEL_BLOCK <= tpos, x, -1e9)) for x in imp]
    nslab = ns // 8
    slabs = [[x[8 * v:8 * v + 8] for v in range(nslab)] for x in impm]
    cnts = [[jnp.zeros((8, TQ), F32) for _ in range(nslab)] for _ in groups]
    sub = lax.broadcasted_iota(jnp.int32, (8, TQ), 0)
    for jp in range(ns):
        v0, r0 = divmod(jp, 8)
        for g in groups:
            row = jnp.broadcast_to(slabs[g][v0][r0:r0 + 1, :], (8, TQ))
            for v in range(nslab):
                if v > v0:
                    beats = row >= slabs[g][v]
                elif v < v0:
                    beats = row > slabs[g][v]
                else:
                    beats = (row > slabs[g][v]) | ((row == slabs[g][v]) & (sub > r0))
                cnts[g][v] = cnts[g][v] + jnp.where(beats, 1.0, 0.0)
    q4s = []
    for g in groups:
        negt = [jnp.where(c < nsel, 0.0, NEG) for c in cnts[g]]
        negt = jnp.concatenate(negt + [jnp.zeros((LANES - ns, TQ), F32)], axis=0)
        neg = negt.T.astype(BF16)
        q4s.append(jnp.concatenate([q4[g], jnp.concatenate([neg] * HEADS_PER_KV, axis=0)], axis=1))

    nsub = SEL_CHUNK // LANES
    nchunks = ((i + 1) * QK - 1) // nsub + 1
    m_ref[...] = jnp.full(m_ref.shape, -3e38, F32)

    def score_chunk(ck, _):
        kb = pl.multiple_of(ck * SEL_CHUNK, SEL_CHUNK)
        s = [_nt_dot(q4s[g], kse_ref[g, pl.ds(kb, SEL_CHUNK), :]) for g in groups]
        for g in groups:
            mloc = None
            for j in range(nsub):
                sj = s[g][:, j * LANES:(j + 1) * LANES] + dt_ref[g, jnp.clip(ck * nsub + j - i * QK + 2, 0, QK + 2)]
                s_ref[g, ck, :, j * LANES:(j + 1) * LANES] = sj
                mloc = sj if mloc is None else jnp.maximum(mloc, sj)
            m_ref[g] = jnp.maximum(m_ref[g], mloc)
        return 0

    lax.fori_loop(0, nchunks, score_chunk, 0)
    for g in groups:
        m_ref[g] = jnp.broadcast_to(jnp.max(m_ref[g], axis=-1, keepdims=True), (ROWS4, LANES))
    l_ref[...] = jnp.zeros(l_ref.shape, F32)
    acc_ref[...] = jnp.zeros(acc_ref.shape, F32)

    kneg = jnp.where(t0 - WINDOW + lax.broadcasted_iota(jnp.int32, (1, span), 1) >= 0, 0.0, NEG)
    sw = [_nt_dot(q4[g], kw_ref[pl.ds(w0, span), gl(g)]) + wb_ref[g] + kneg for g in groups]
    mw = [jnp.max(s, axis=-1, keepdims=True) for s in sw]
    pw = [jnp.exp2(s - m) for s, m in zip(sw, mw)]
    lw = [jnp.sum(p, axis=-1, keepdims=True) for p in pw]
    o_wins = [jnp.dot(pw[g].astype(BF16), vw_ref[pl.ds(w0, span), gl(g)], preferred_element_type=F32) for g in groups]
    o_wins = [o / jnp.maximum(l, 1e-30) for o, l in zip(o_wins, lw)]
    gts = [gate_ref[:, g * LANES:(g + 1) * LANES] for g in groups]
    for h in range(HEADS_PER_KV):
        r = slice(h * TQ, (h + 1) * TQ)
        for g in groups:
            acc2_ref[g, r, :] = gts[g][:, 3 * h:3 * h + 1] * o_cmps[g][r] + gts[g][:, 3 * h + 2:3 * h + 3] * o_wins[g][r]

    def pv_chunk(ck, _):
        kb = pl.multiple_of(WINDOW + ck * SEL_CHUNK, SEL_CHUNK)
        ps = [[jnp.exp2(s_ref[g, ck, :, j * LANES:(j + 1) * LANES] - m_ref[g]) for j in range(nsub)] for g in groups]
        for g in groups:
            lsum = ps[g][0]
            for pj in ps[g][1:]:
                lsum = lsum + pj
            l_ref[g] += lsum
            p = jnp.concatenate([pj.astype(BF16) for pj in ps[g]], axis=1)
            acc_ref[g] += jnp.dot(p, vs_ref[pl.ds(kb, SEL_CHUNK), gl(g)], preferred_element_type=F32)
        return 0

    lax.fori_loop(0, nchunks, pv_chunk, 0)

    ls = [jnp.maximum(jnp.sum(l_ref[g], axis=-1, keepdims=True), 1e-30) for g in groups]
    o_sels = [acc_ref[g] / ls[g] for g in groups]
    for h in range(HEADS_PER_KV):
        r = slice(h * TQ, (h + 1) * TQ)
        for g in groups:
            o_h = acc2_ref[g, r, :] + gate_ref[:, g * LANES + 3 * h + 1:g * LANES + 3 * h + 2] * o_sels[g][r]
            c0 = (g * HEADS_PER_KV + h) * HEAD_DIM
            o_ref[:, c0:c0 + HEAD_DIM] = o_h.astype(BF16)


def _nsa(q, kvs, cb, gates, biasc, dtiles, wb, ovt, et):
    b, t, _ = q.shape
    nt = t // TQ
    tp = kvs.shape[1]
    span = wb.shape[2]
    gw = HEADS_PER_KV * HEAD_DIM
    assert tp == t + WINDOW and span == TQ + WINDOW and t % SEL_CHUNK == 0 and (t // SEL_BLOCK) % 8 == 0

    def stream(j):
        return pl.BlockSpec((None, tp, N_KV * HEAD_DIM), lambda bi, i: (bi, 0, j))

    return pl.pallas_call(
        _nsa_body,
        grid=(b, nt),
        in_specs=[
            pl.BlockSpec((None, TQ, N_KV * gw), lambda bi, i: (bi, i, 0)),
            stream(0), stream(1), stream(2), stream(3),
            pl.BlockSpec((None, N_KV, None, LANES, HEAD_DIM), lambda bi, i: (0, 0, bi, 0, 0)),
            pl.BlockSpec((None, N_KV, None, LANES, HEAD_DIM), lambda bi, i: (1, 0, bi, 0, 0)),
            pl.BlockSpec((None, TQ, N_KV * LANES), lambda bi, i: (bi, i, 0)),
            pl.BlockSpec((N_KV, None, ROWS4, LANES), lambda bi, i: (0, i, 0, 0)),
            pl.BlockSpec((N_KV, QK + 3, ROWS4, LANES), lambda bi, i: (0, 0, 0, 0)),
            pl.BlockSpec((N_KV, ROWS4, span), lambda bi, i: (0, 0, 0)),
            pl.BlockSpec((LANES, LANES), lambda bi, i: (0, 0)),
            pl.BlockSpec((t, LANES), lambda bi, i: (0, 0)),
        ],
        out_specs=pl.BlockSpec((None, TQ, N_KV * gw), lambda bi, i: (bi, i, 0)),
        out_shape=jax.ShapeDtypeStruct((b, t, N_KV * gw), BF16),
        scratch_shapes=[
            pltpu.VMEM((N_KV, t, 2 * HEAD_DIM), BF16),
            pltpu.VMEM((N_KV, t // SEL_CHUNK, ROWS4, SEL_CHUNK), F32),
            pltpu.VMEM((N_KV, ROWS4, LANES), F32),
            pltpu.VMEM((N_KV, ROWS4, LANES), F32),
            pltpu.VMEM((N_KV, ROWS4, HEAD_DIM), F32),
            pltpu.VMEM((N_KV, ROWS4, HEAD_DIM), F32),
        ],
        compiler_params=_cparams(("parallel", "arbitrary"), NSA_VMEM_LIMIT_BYTES),
        name="nsa",
    )(q, kvs, kvs, kvs, kvs, cb, cb, gates, biasc, dtiles, wb, ovt, et)


def _s5_body(u_ref, c0_ref, winc_ref, woutc_ref, rep_k_ref, rep_in_ref, rep_out_ref, m_in_ref, m_out_ref,
             al_ref, d_ref, y_ref, ucat_ref, inj_ref, xp_ref, grev_ref, win_ref, wout_ref):
    nb, t, lanes = u_ref.shape
    L = SSM_CHUNK
    nchunk = t // L
    rows = nb * nchunk
    sdim = al_ref.shape[1] // 2

    @pl.when(pl.program_id(1) == 0)
    def _():
        m_in, m_out = m_in_ref[...], m_out_ref[...]
        for half in range(2):
            for s in range(L):
                x = jnp.dot(winc_ref[half, s * lanes:(s + 1) * lanes, :], rep_in_ref[...], preferred_element_type=F32)
                win_ref[s * lanes:(s + 1) * lanes, half * sdim:(half + 1) * sdim] = (x * m_in).astype(BF16)
            for tt in range(L):
                x = jnp.dot(woutc_ref[half], rep_out_ref[:, tt * lanes:(tt + 1) * lanes], preferred_element_type=F32)
                wout_ref[half * sdim:(half + 1) * sdim, tt * lanes:(tt + 1) * lanes] = (x * m_out).astype(BF16)
        c0 = [(jnp.dot(c0_ref[half], rep_k_ref[...], preferred_element_type=F32) * m_out).astype(BF16)
              for half in range(2)]
        kx = jnp.dot(win_ref[...], jnp.concatenate(c0, axis=0), preferred_element_type=F32)
        zero = jnp.zeros((lanes, lanes), BF16)

        def kblk(tau):
            s = L - 1 - tau
            return kx[s * lanes:(s + 1) * lanes].astype(BF16) if 0 <= tau < L else zero

        for r in range(L + 1):
            grev_ref[r * lanes:(r + 1) * lanes, 0:lanes] = kblk(L - r - 1)
            grev_ref[r * lanes:(r + 1) * lanes, lanes:] = kblk(L - r)

    def u_at(s):
        return u_ref[:, pl.ds(s, nchunk, stride=L), :].reshape(rows, lanes)

    for s in range(L):
        ucat_ref[:, s * lanes:(s + 1) * lanes] = u_at(s).astype(BF16)
    inj = jnp.dot(ucat_ref[...], win_ref[...], preferred_element_type=F32)
    npl = sdim // lanes
    for k in range(2 * npl):
        inj_ref[k] = inj[:, k * lanes:(k + 1) * lanes]
    ar = [jnp.broadcast_to(al_ref[0:1, k * lanes:(k + 1) * lanes], (nb, lanes)) for k in range(npl)]
    ai = [jnp.broadcast_to(al_ref[0:1, sdim + k * lanes:sdim + (k + 1) * lanes], (nb, lanes)) for k in range(npl)]

    def step(c, carry):
        xr, xi = carry
        rsel = pl.ds(c, nb, stride=nchunk)
        nr, ni = [], []
        for k in range(npl):
            xp_ref[k, rsel, :] = xr[k]
            xp_ref[npl + k, rsel, :] = xi[k]
            nr.append(ar[k] * xr[k] - ai[k] * xi[k] + inj_ref[k, rsel, :])
            ni.append(ar[k] * xi[k] + ai[k] * xr[k] + inj_ref[npl + k, rsel, :])
        return tuple(nr), tuple(ni)

    z = tuple(jnp.zeros((nb, lanes), F32) for _ in range(npl))
    lax.fori_loop(0, nchunk, step, (z, z), unroll=4)
    xp = jnp.concatenate([xp_ref[k] for k in range(2 * npl)], axis=1).astype(BF16)
    d = d_ref[...]
    for tp in range(0, L, 2):
        res = jnp.dot(ucat_ref[:, 0:(tp + 2) * lanes], grev_ref[(L - 1 - tp) * lanes:(L + 1) * lanes, :],
                      preferred_element_type=F32)
        res = res + jnp.dot(xp, wout_ref[:, tp * lanes:(tp + 2) * lanes], preferred_element_type=F32)
        for k in range(2):
            y = res[:, k * lanes:(k + 1) * lanes] + d * u_at(tp + k)
            y_ref[:, pl.ds(tp + k, nchunk, stride=L), :] = y.reshape(nb, nchunk, lanes)


def _s5(u, c0c, winc, woutc, al, d, *, nb=4):
    b, t, dch = u.shape
    nb = min(nb, b)
    nblk = dch // LANES
    L = SSM_CHUNK
    rows = nb * (t // L)
    hch, p = c0c.shape[3], winc.shape[3]
    gpb = LANES // hch
    sdim = gpb * p
    kcat = L * LANES
    lane, col, st = jnp.arange(LANES), jnp.arange(kcat), jnp.arange(sdim)
    rep_k = (jnp.arange(hch)[:, None] == (lane % hch)[None, :]).astype(BF16)
    rep_in = (jnp.arange(p)[:, None] == (st % p)[None, :]).astype(BF16)
    rep_out = (jnp.arange(L * hch)[:, None] == (col // LANES * hch + col % hch)[None, :]).astype(BF16)
    m_in = ((lane // hch)[:, None] == (st // p)[None, :]).astype(F32)
    m_out = ((st // p)[:, None] == (lane // hch)[None, :]).astype(F32)
    const = lambda a: pl.BlockSpec(a.shape, lambda j, i: (0,) * a.ndim)
    return pl.pallas_call(
        _s5_body,
        grid=(nblk, b // nb),
        in_specs=[
            pl.BlockSpec((nb, t, LANES), lambda j, i: (i, 0, j)),
            pl.BlockSpec((None, 2, sdim, hch), lambda j, i: (j, 0, 0, 0)),
            pl.BlockSpec((None, 2, kcat, p), lambda j, i: (j, 0, 0, 0)),
            pl.BlockSpec((None, 2, sdim, L * hch), lambda j, i: (j, 0, 0, 0)),
            const(rep_k), const(rep_in), const(rep_out), const(m_in), const(m_out),
            pl.BlockSpec((None, 8, 2 * sdim), lambda j, i: (j, 0, 0)),
            pl.BlockSpec((1, LANES), lambda j, i: (0, j)),
        ],
        out_specs=pl.BlockSpec((nb, t, LANES), lambda j, i: (i, 0, j)),
        out_shape=jax.ShapeDtypeStruct((b, t, dch), F32),
        scratch_shapes=[pltpu.VMEM((rows, kcat), BF16), pltpu.VMEM((2 * sdim // LANES, rows, LANES), F32),
                        pltpu.VMEM((2 * sdim // LANES, rows, LANES), F32),
                        pltpu.VMEM(((L + 1) * LANES, 2 * LANES), BF16), pltpu.VMEM((kcat, 2 * sdim), BF16),
                        pltpu.VMEM((2 * sdim, kcat), BF16)],
        compiler_params=_cparams(("parallel", "arbitrary")),
        name="s5",
    )(u, c0c, winc, woutc, rep_k, rep_in, rep_out, m_in, m_out, al, d.reshape(1, dch))


def _glu_out_body(h_ref, a_ref, y_ref, wg_ref, bg_ref, woa_ref, wos_ref, *rest):
    ncast = (len(rest) - 1) // 2
    o_ref = rest[ncast]
    hg = jax.nn.gelu(y_ref[...], approximate=True)
    z = jnp.dot(hg.astype(BF16), wg_ref[...], preferred_element_type=F32) + bg_ref[...]
    s = hg * jax.nn.sigmoid(z)
    mix = jnp.dot(a_ref[...], woa_ref[...], preferred_element_type=F32)
    mix = mix + jnp.dot(s.astype(BF16), wos_ref[...], preferred_element_type=F32)
    o_ref[...] = h_ref[...] + mix
    for src, dst in zip(rest[:ncast], rest[ncast + 1:]):
        dst[...] = src[...].astype(BF16)


def _glu_out(h, a, y, wg, bg, woa, wos, cast_later=(), *, tm=512):
    n, dm = h.shape
    da, ds = a.shape[1], y.shape[1]
    steps = n // tm
    row = lambda w: pl.BlockSpec((tm, w), lambda i: (i, 0))
    full = lambda r, c: pl.BlockSpec((r, c), lambda i: (0, 0))
    slab = lambda w: pl.BlockSpec((w.shape[0] // steps, w.shape[1]), lambda i: (i, 0))
    assert all(w.shape[0] % (steps * 16) == 0 for w in cast_later)
    outs = pl.pallas_call(
        _glu_out_body,
        grid=(steps,),
        in_specs=[row(dm), row(da), row(ds), full(ds, ds), full(1, ds), full(da, dm), full(ds, dm)]
        + [slab(w) for w in cast_later],
        out_specs=[row(dm)] + [slab(w) for w in cast_later],
        out_shape=[jax.ShapeDtypeStruct((n, dm), F32)] + [jax.ShapeDtypeStruct(w.shape, BF16) for w in cast_later],
        compiler_params=_cparams(("parallel",), GLU_VMEM_LIMIT_BYTES),
        name="glu_out",
    )(h, a, y, wg, bg.reshape(1, ds), woa, wos, *cast_later)
    return outs[0], tuple(outs[1:])


def _t5_bucket(dist):
    n = jnp.maximum(dist, 0)
    max_exact = N_BUCKETS // 2
    nf = jnp.maximum(n, 1).astype(F32)
    large = max_exact + (jnp.log(nf / max_exact) / math.log(MAX_DISTANCE / max_exact)
                         * (N_BUCKETS - max_exact)).astype(jnp.int32)
    large = jnp.minimum(large, N_BUCKETS - 1)
    return jnp.where(n < max_exact, n, large)


def _t5_body(first_ref, table_ref, bc_ref, dt_ref, wb_ref, *, nc):
    g = pl.program_id(0)
    nt = bc_ref.shape[0]
    span = wb_ref.shape[1]
    a = lax.broadcasted_iota(jnp.int32, (TQ, LANES), 0)
    c = lax.broadcasted_iota(jnp.int32, (TQ, LANES), 1)
    heads = [g * HEADS_PER_KV + h for h in range(HEADS_PER_KV)]

    def lookup(dist, valid, store):
        bias = [jnp.full((TQ, LANES), table_ref[0, hd] * LOG2E, F32) for hd in heads]
        for k in range(1, N_BUCKETS):
            ge = dist >= first_ref[k]
            bias = [jnp.where(ge, table_ref[k, hd] * LOG2E, bv) for hd, bv in zip(heads, bias)]
        for h, bv in enumerate(bias):
            store(slice(h * TQ, (h + 1) * TQ), bv if valid is None else jnp.where(valid, bv, NEG))

    def put_far(rows, tile):
        dt_ref[0, rows, :] = tile

    lookup(jnp.full((TQ, LANES), 2 * MAX_DISTANCE, jnp.int32), None, put_far)
    for r in range(-1, QK):
        def put_near(rows, tile, r=r):
            dt_ref[r + 2, rows, :] = tile

        dist = a - r * LANES - c
        lookup(dist, dist >= 0, put_near)
    dt_ref[QK + 2] = jnp.full((ROWS4, LANES), NEG, F32)
    for j in range(span // LANES):
        def put_wb(rows, tile, j=j):
            wb_ref[rows, j * LANES:(j + 1) * LANES] = tile

        dist = a + WINDOW - (c + j * LANES)
        lookup(dist, (dist >= 0) & (dist < WINDOW), put_wb)

    def cmp_tile(i, _):
        def put_bc(rows, tile):
            bc_ref[i, rows, :] = tile

        dist = i * TQ + a - (c * CMP_STRIDE + CMP_LEN - 1)
        lookup(dist, (dist >= 0) & (c < nc), put_bc)
        return 0

    lax.fori_loop(0, nt, cmp_tile, 0)


def _bias_tables(rel_bias, t):
    nt = t // TQ
    span = TQ + WINDOW
    nc = (t - CMP_LEN) // CMP_STRIDE + 1
    buckets = _t5_bucket(jnp.arange(2 * MAX_DISTANCE))
    first = jnp.sum((buckets[None, :] < jnp.arange(N_BUCKETS)[:, None]).astype(jnp.int32), axis=1)
    smem = pl.BlockSpec(memory_space=pltpu.SMEM)
    return pl.pallas_call(
        functools.partial(_t5_body, nc=nc),
        grid=(N_KV,),
        in_specs=[smem, smem],
        out_specs=[
            pl.BlockSpec((None, nt, ROWS4, LANES), lambda g: (g, 0, 0, 0)),
            pl.BlockSpec((None, QK + 3, ROWS4, LANES), lambda g: (g, 0, 0, 0)),
            pl.BlockSpec((None, ROWS4, span), lambda g: (g, 0, 0)),
        ],
        out_shape=[
            jax.ShapeDtypeStruct((N_KV, nt, ROWS4, LANES), F32),
            jax.ShapeDtypeStruct((N_KV, QK + 3, ROWS4, LANES), F32),
            jax.ShapeDtypeStruct((N_KV, ROWS4, span), F32),
        ],
        compiler_params=_cparams(("parallel",)),
        name="t5_tables",
    )(first, rel_bias.astype(F32))


def _sel_tables(t):
    ns = t // SEL_BLOCK
    nc_pad = LANES
    c_start = jnp.arange(nc_pad) * CMP_STRIDE
    j_start = jnp.arange(LANES) * SEL_BLOCK
    ov = jnp.clip(jnp.minimum(c_start[:, None] + CMP_LEN, j_start[None, :] + SEL_BLOCK)
                  - jnp.maximum(c_start[:, None], j_start[None, :]), 0, None).astype(F32) / CMP_LEN
    ov = jnp.where(jnp.arange(LANES)[None, :] < ns, ov, 0.0)
    et = jnp.arange(t)[:, None] // SEL_BLOCK == jnp.arange(LANES)[None, :]
    return ov.T.astype(BF16), et.astype(BF16)


def _s5_tables(lam_re, lam_im, log_step, b_re, b_im, c_re, c_im):
    ng, p = lam_re.shape
    hch = b_re.shape[2]
    L = SSM_CHUNK
    step = jnp.exp(log_step.astype(F32))[:, None]
    lre, lim = lam_re.astype(F32), lam_im.astype(F32)
    mag = jnp.exp(lre * step)
    ab_re, ab_im = mag * jnp.cos(lim * step), mag * jnp.sin(lim * step)
    nr, ni = ab_re - 1.0, ab_im
    den = lre * lre + lim * lim
    f_re, f_im = (nr * lre + ni * lim) / den, (ni * lre - nr * lim) / den
    br, bim = b_re.astype(F32), b_im.astype(F32)
    bb_re = f_re[..., None] * br - f_im[..., None] * bim
    bb_im = f_re[..., None] * bim + f_im[..., None] * br
    cr, ci = c_re.astype(F32), c_im.astype(F32)
    pr, pi = [jnp.ones_like(ab_re)], [jnp.zeros_like(ab_re)]
    for _ in range(L):
        pr, pi = pr + [pr[-1] * ab_re - pi[-1] * ab_im], pi + [pr[-1] * ab_im + pi[-1] * ab_re]
    pw_re, pw_im = jnp.stack(pr, 0), jnp.stack(pi, 0)
    cp_re = cr[None] * pw_re[:, :, None, :] - ci[None] * pw_im[:, :, None, :]
    cp_im = -(cr[None] * pw_im[:, :, None, :] + ci[None] * pw_re[:, :, None, :])
    gpb = LANES // hch
    nblk = ng // gpb
    wr = pw_re[L - 1 - jnp.arange(L)]
    wi = pw_im[L - 1 - jnp.arange(L)]
    win_re = (wr[..., None] * bb_re[None] - wi[..., None] * bb_im[None])
    win_im = (wr[..., None] * bb_im[None] + wi[..., None] * bb_re[None])
    inj_op = lambda m: m.reshape(L, nblk, gpb, p, hch).transpose(1, 0, 2, 4, 3).reshape(nblk, L * LANES, p)
    winc = jnp.stack([inj_op(win_re), inj_op(win_im)], axis=1)
    read_op = lambda m: m.reshape(L, nblk, gpb, hch, p).transpose(1, 2, 4, 0, 3).reshape(nblk, gpb * p, L * hch)
    woutc = jnp.stack([read_op(cp_re[1:]), read_op(cp_im[1:])], axis=1)
    read0 = lambda m: m.reshape(nblk, gpb, hch, p).transpose(0, 1, 3, 2).reshape(nblk, gpb * p, hch)
    c0c = jnp.stack([read0(cp_re[0]), read0(cp_im[0])], axis=1)
    al = jnp.concatenate([pw_re[L].reshape(nblk, gpb * p), pw_im[L].reshape(nblk, gpb * p)], axis=1)
    al = jnp.broadcast_to(al[:, None, :], (nblk, 8, 2 * gpb * p))
    return c0c.astype(BF16), winc.astype(BF16), woutc.astype(BF16), al


def _mixers(h1, mix_norm, w_in, cmp_k, cmp_v, rel_bias, ssm, ssm_d, glu_w, glu_b, w_out, b, t, cast_later=()):
    n, d = h1.shape
    d_ssm = glu_w.shape[0]
    o_g, o_u = D_ATTN + 6 * D_KV, D_ATTN + 6 * D_KV + 3 * N_HEADS
    gcols = 3 * HEADS_PER_KV
    wg = [jnp.pad(w_in[:, o_g + g * gcols:o_g + (g + 1) * gcols], ((0, 0), (0, LANES - gcols))) for g in range(N_KV)]
    w_perm = jnp.concatenate([w_in[:, :o_g], w_in[:, o_u:]] + wg, axis=1).astype(BF16)
    q, kvs, kcv, u, gates = _inproj(h1.reshape(b, t, d), mix_norm, w_perm, d_ssm)

    nblk = t // CMP_STRIDE
    assert nblk == LANES, "compressed-block axis is laid out on one 128-lane tile"
    pe = jnp.stack([cmp_k[0], cmp_v[0]], 0).astype(F32)
    w1 = jnp.stack([cmp_k[1], cmp_v[1]], 0).astype(BF16).reshape(2, CMP_LEN, HEAD_DIM, -1)
    b1 = jnp.stack([cmp_k[2].reshape(1, -1), cmp_v[2].reshape(1, -1)], 0).astype(F32)
    w2 = jnp.stack([cmp_k[3], cmp_v[3]], 0).astype(BF16)
    cb = _compress(kcv, pe, w1, b1, w2, nblk)

    biasc, dtiles, wb = _bias_tables(rel_bias, t)
    ovt, et = _sel_tables(t)
    a = _nsa(q, kvs, cb, gates, biasc, dtiles, wb, ovt, et)

    c0c, winc, woutc, al = _s5_tables(*ssm)
    y = _s5(u, c0c, winc, woutc, al, ssm_d)

    wo = w_out.astype(BF16)
    return _glu_out(h1, a.reshape(n, D_ATTN), y.reshape(n, d_ssm), glu_w.astype(BF16), glu_b, wo[:D_ATTN], wo[D_ATTN:],
                    cast_later)


def kernel(x, ffn1_norm, ffn1_w1, ffn1_w3, ffn1_w2, mix_norm, w_in, cmp_pe_k, cmp_w1_k, cmp_b1_k, cmp_w2_k,
           cmp_pe_v, cmp_w1_v, cmp_b1_v, cmp_w2_v, rel_bias, ssm_lam_re, ssm_lam_im, ssm_log_step, ssm_b_re,
           ssm_b_im, ssm_c_re, ssm_c_im, ssm_d, glu_w, glu_b, w_out, ffn2_norm, ffn2_w1, ffn2_w3, ffn2_w2,
           final_norm):
    b, t, d = x.shape
    depth = ffn1_w1.shape[0]
    h = x.reshape(b * t, d)
    for l in range(depth):
        last = l == depth - 1
        h = _ffn(h, ffn1_norm[l], ffn1_w1[l], ffn1_w3[l], ffn1_w2[l])
        ssm = (ssm_lam_re[l], ssm_lam_im[l], ssm_log_step[l], ssm_b_re[l], ssm_b_im[l], ssm_c_re[l], ssm_c_im[l])
        h, (w1b, w3b, w2b) = _mixers(h, mix_norm[l], w_in[l],
                                     (cmp_pe_k[l], cmp_w1_k[l], cmp_b1_k[l], cmp_w2_k[l]),
                                     (cmp_pe_v[l], cmp_w1_v[l], cmp_b1_v[l], cmp_w2_v[l]),
                                     rel_bias, ssm, ssm_d[l], glu_w[l], glu_b[l], w_out[l], b, t,
                                     cast_later=(ffn2_w1[l], ffn2_w3[l], ffn2_w2[l]))
        h = _ffn(h, ffn2_norm[l], w1b, w3b, w2b, final_gain=final_norm if last else None)
    return h.reshape(b, t, d)
```
